```python
import math
import jax, jax.numpy as jnp
from jax import lax
import numpy as np

D_MODEL = 1024
BATCH = 16
SEQ = 2048
DEPTH = 4

GRID_W = 64
HEAD_DIM = 64
POOL_WIDTH = D_MODEL // 4
POOL_WINDOWS = (2, 4, 8, 16)
POOL_GROUP = POOL_WIDTH // len(POOL_WINDOWS)
RWKV_WIDTH = D_MODEL // 4
RWKV_HEADS = RWKV_WIDTH // HEAD_DIM
DECAY_LORA = 32
AAA_LORA = 32
GATE_LORA = 64
GN_EPS = 64e-5
RWKV_IN = 3 * RWKV_WIDTH + 2 * DECAY_LORA + 2 * AAA_LORA + GATE_LORA
ATTN_WIDTH = D_MODEL // 2
ATTN_HEADS = ATTN_WIDTH // HEAD_DIM
ATTN_KV_HEADS = ATTN_HEADS // 4
ATTN_KV_WIDTH = ATTN_KV_HEADS * HEAD_DIM
Q_BLOCK = 128
ROPE_THETA = 10000.0
QK_EPS = 1e-6
IN_WIDTH = POOL_WIDTH + RWKV_IN + ATTN_WIDTH + 2 * ATTN_KV_WIDTH
N_GROUPS = 4
EXPERTS_PER_GROUP = 8
N_EXPERTS = N_GROUPS * EXPERTS_PER_GROUP
TOP_K = 2
EXPERT_HIDDEN = D_MODEL // 2
MOE_BLOCK = 256
DEEPNORM_ALPHA = float((2 * DEPTH) ** 0.25)
DEEPNORM_BETA = float((8 * DEPTH) ** -0.25)
LN_EPS = 1e-5

kernel_name = "hybrid_pool_rwkv7_axialgqa_hiermoe_encoder"


def layer_norm(x, g, b):
    xf = x.astype(jnp.float32)
    mu = jnp.mean(xf, -1, keepdims=True)
    var = jnp.mean(jnp.square(xf - mu), -1, keepdims=True)
    return ((xf - mu) * lax.rsqrt(var + LN_EPS) * g + b).astype(x.dtype)


def rms_norm(x, g):
    xf = x.astype(jnp.float32)
    return (xf * lax.rsqrt(jnp.mean(xf * xf, -1, keepdims=True) + QK_EPS) * g).astype(x.dtype)


def axial_rope_tables(seq_len):
    rows = seq_len // GRID_W
    row_id = jnp.repeat(jnp.arange(rows), GRID_W).astype(jnp.float32)
    col_id = jnp.tile(jnp.arange(GRID_W), rows).astype(jnp.float32)
    half = HEAD_DIM // 2
    inv_freq = ROPE_THETA ** (-jnp.arange(0, half, 2, dtype=jnp.float32) / half)
    ang_r = row_id[:, None] * inv_freq
    ang_c = col_id[:, None] * inv_freq
    ang = jnp.concatenate([ang_r, ang_r, ang_c, ang_c], -1)
    return jnp.cos(ang), jnp.sin(ang)


def apply_axial_rope(x, cos, sin):
    half = HEAD_DIM // 2
    quarter = half // 2
    def rot(u):
        return jnp.concatenate([-u[..., quarter:], u[..., :quarter]], -1)
    xr = jnp.concatenate([rot(x[..., :half]), rot(x[..., half:])], -1)
    c = cos[None, :, None, :].astype(x.dtype)
    s = sin[None, :, None, :].astype(x.dtype)
    return x * c + xr * s


def multiscale_pool(u, pool_w, pool_scale):
    S = u.shape[1]
    uf = u.astype(jnp.float32)
    cs = jnp.pad(jnp.cumsum(uf, axis=1), ((0, 0), (1, 0), (0, 0)))
    t = jnp.arange(S)
    outs = []
    for gi, w in enumerate(POOL_WINDOWS):
        sl = slice(gi * POOL_GROUP, (gi + 1) * POOL_GROUP)
        lo = jnp.clip(t - w // 2, 0, S)
        hi = jnp.clip(t + w - w // 2, 0, S)
        c = cs[:, :, sl]
        mean = (c[:, hi] - c[:, lo]) / (hi - lo).astype(jnp.float32)[None, :, None]
        d = (mean - uf[:, :, sl]).astype(u.dtype)
        outs.append(d @ pool_w[gi])
    return jnp.concatenate(outs, -1) * pool_scale


def centred_token_shift(f, mu_prev, mu_next):
    prev = jnp.pad(f[:, :-1], ((0, 0), (1, 0), (0, 0)))
    nxt = jnp.pad(f[:, 1:], ((0, 0), (0, 1), (0, 0)))
    return f + mu_prev * (prev - f) + mu_next * (nxt - f)


def wkv7_scan(r, w, k, v, kk, a, reverse):
    B, S, H, N = r.shape
    def step(state, inp):
        r_t, w_t, k_t, v_t, kk_t, a_t = inp
        sk = jnp.einsum('bhij,bhj->bhi', state, kk_t)
        state = (state * w_t[:, :, None, :]
                 - sk[..., None] * (kk_t * a_t)[:, :, None, :]
                 + v_t[..., None] * k_t[:, :, None, :])
        return state, jnp.einsum('bhij,bhj->bhi', state, r_t)
    xs = tuple(jnp.moveaxis(z, 1, 0) for z in (r, w, k, v, kk, a))
    s0 = jnp.zeros((B, H, N, N), jnp.float32)
    _, ys = lax.scan(step, s0, xs, reverse=reverse)
    return jnp.moveaxis(ys, 0, 1)


def rwkv7_bidir(f, w0, w_up, a0, a_up, g_up, k_k, k_a, r_k, gn_g, gn_b):
    B, S, _ = f.shape
    W = RWKV_WIDTH
    r, k, v = f[..., :W], f[..., W:2 * W], f[..., 2 * W:3 * W]
    o = 3 * W
    wd = f[..., o:o + 2 * DECAY_LORA].reshape(B, S, 2, DECAY_LORA)
    o += 2 * DECAY_LORA
    ad = f[..., o:o + 2 * AAA_LORA].reshape(B, S, 2, AAA_LORA)
    o += 2 * AAA_LORA
    gd = f[..., o:o + GATE_LORA]

    def heads(z):
        return z.reshape(B, S, RWKV_HEADS, HEAD_DIM).astype(jnp.float32)

    d = (w0 + jnp.einsum('bsdl,dlc->bsdc', jnp.tanh(wd), w_up)).astype(jnp.float32)
    decay = jnp.exp(-math.exp(-0.5) * jax.nn.sigmoid(d))
    a = jax.nn.sigmoid((a0 + jnp.einsum('bsdl,dlc->bsdc', ad, a_up)).astype(jnp.float32))
    g = jax.nn.sigmoid(gd) @ g_up
    kf = k.astype(jnp.float32)
    kk = heads(kf * k_k)
    kk = kk / jnp.maximum(jnp.sqrt(jnp.sum(kk * kk, -1, keepdims=True)), 1e-12)
    rh, vh = heads(r), heads(v)
    wkv = jnp.zeros_like(rh)
    bonus = jnp.zeros_like(rh)
    for di, rev in enumerate((False, True)):
        a_d = a[:, :, di]
        kh = heads(kf * (1.0 + (a_d - 1.0) * k_a))
        wkv = wkv + wkv7_scan(rh, heads(decay[:, :, di]), kh, vh, kk, heads(a_d), rev)
        bonus = bonus + jnp.sum(rh * kh * r_k, -1, keepdims=True) * vh
    mu = jnp.mean(wkv, -1, keepdims=True)
    var = jnp.mean(jnp.square(wkv - mu), -1, keepdims=True)
    y = ((wkv - mu) * lax.rsqrt(var + GN_EPS)).reshape(B, S, W) * gn_g + gn_b
    y = y + bonus.reshape(B, S, W)
    return (y * g).astype(f.dtype)


def gqa_axial(u, q_norm, k_norm, cos, sin):
    B, S, _ = u.shape
    q = u[..., :ATTN_WIDTH].reshape(B, S, ATTN_HEADS, HEAD_DIM)
    k = u[..., ATTN_WIDTH:ATTN_WIDTH + ATTN_KV_WIDTH].reshape(B, S, ATTN_KV_HEADS, HEAD_DIM)
    v = u[..., ATTN_WIDTH + ATTN_KV_WIDTH:].reshape(B, S, ATTN_KV_HEADS, HEAD_DIM)
    q = apply_axial_rope(rms_norm(q, q_norm), cos, sin)
    k = apply_axial_rope(rms_norm(k, k_norm), cos, sin)
    G = ATTN_HEADS // ATTN_KV_HEADS
    nb = S // Q_BLOCK
    qb = q.reshape(B, nb, Q_BLOCK, ATTN_KV_HEADS, G, HEAD_DIM).transpose(1, 0, 3, 4, 2, 5)
    scale = HEAD_DIM ** -0.5

    def block(q_blk):
        s = jnp.einsum('bkgqd,bskd->bkgqs', q_blk, k, preferred_element_type=jnp.float32) * scale
        p = jax.nn.softmax(s, axis=-1).astype(v.dtype)
        return jnp.einsum('bkgqs,bskd->bkgqd', p, v)

    o = lax.map(block, qb)
    return o.transpose(1, 0, 4, 2, 3, 5).reshape(B, S, ATTN_WIDTH)


def hier_moe(x, wg, bg, we, be, w_gate, w_up, w_down):
    B, S, D = x.shape
    N = B * S
    xt = x.reshape(N, D)
    glog = (xt @ wg).astype(jnp.float32) + bg
    grp = jnp.argmax(glog, -1)
    g_w = jnp.take_along_axis(jax.nn.softmax(glog, -1), grp[:, None], 1)
    elog = ((xt @ we).astype(jnp.float32) + be).reshape(N, N_GROUPS, EXPERTS_PER_GROUP)
    elog = jnp.take_along_axis(elog, grp[:, None, None], 1)[:, 0]
    top_v, top_i = lax.top_k(elog, TOP_K)
    gates = jax.nn.softmax(top_v, -1) * g_w
    eid = grp[:, None] * EXPERTS_PER_GROUP + top_i
    M = N * TOP_K
    e_flat = eid.reshape(M)
    tok = jnp.repeat(jnp.arange(N, dtype=jnp.int32), TOP_K)
    gate_flat = gates.reshape(M)
    order = jnp.argsort(e_flat)
    e_sorted = e_flat[order]
    counts = jnp.bincount(e_flat, length=N_EXPERTS)
    padded = (counts + MOE_BLOCK - 1) // MOE_BLOCK * MOE_BLOCK
    start = jnp.cumsum(counts) - counts
    ends_p = jnp.cumsum(padded)
    pstart = ends_p - padded
    dest = pstart[e_sorted] + jnp.arange(M) - start[e_sorted]
    n_blocks = -(-(M + N_EXPERTS * (MOE_BLOCK - 1)) // MOE_BLOCK)
    P = n_blocks * MOE_BLOCK
    row_tok = jnp.full((P,), N, jnp.int32).at[dest].set(tok[order])
    row_gate = jnp.zeros((P,), jnp.float32).at[dest].set(gate_flat[order])
    block_exp = jnp.minimum(
        jnp.searchsorted(ends_p, jnp.arange(n_blocks) * MOE_BLOCK, side='right'), N_EXPERTS - 1)
    x_pad = jnp.concatenate([xt, jnp.zeros((1, D), xt.dtype)], 0)
    xs = x_pad[row_tok].reshape(n_blocks, MOE_BLOCK, D)

    def expert_block(args):
        xb, e = args
        h = jax.nn.silu(xb @ w_gate[e]) * (xb @ w_up[e])
        return h @ w_down[e]

    ys = lax.map(expert_block, (xs, block_exp)).reshape(P, D)
    ys = ys * row_gate[:, None].astype(ys.dtype)
    out = jnp.zeros((N + 1, D), ys.dtype).at[row_tok].add(ys)[:N]
    return out.reshape(B, S, D)


def setup_inputs(seed: int = 0) -> dict:
    key = jax.random.key(seed)
    ks = iter(jax.random.split(key, 40))
    L, D = DEPTH, D_MODEL

    def nrm(shape, scale):
        return jax.random.normal(next(ks), shape, jnp.float32) * scale

    def unif(shape, lo, hi):
        return jax.random.uniform(next(ks), shape, jnp.float32, minval=lo, maxval=hi)

    return {
        "x": nrm((BATCH, SEQ, D), 1.0),
        "w_in": nrm((L, D, IN_WIDTH), D ** -0.5),
        "mu_prev": unif((L, RWKV_IN), 0.0, 0.5),
        "mu_next": unif((L, RWKV_IN), 0.0, 0.5),
        "pool_w": nrm((L, len(POOL_WINDOWS), POOL_GROUP, POOL_GROUP), POOL_GROUP ** -0.5),
        "pool_scale": 1.0 + nrm((L, POOL_WIDTH), 0.1),
        "rw_w0": nrm((L, 2, RWKV_WIDTH), 1.0) - 0.5,
        "rw_w_up": nrm((L, 2, DECAY_LORA, RWKV_WIDTH), 0.1),
        "rw_a0": nrm((L, 2, RWKV_WIDTH), 0.5),
        "rw_a_up": nrm((L, 2, AAA_LORA, RWKV_WIDTH), AAA_LORA ** -0.5),
        "rw_g_up": nrm((L, GATE_LORA, RWKV_WIDTH), GATE_LORA ** -0.5),
        "rw_k_k": 0.85 + nrm((L, RWKV_WIDTH), 0.05),
        "rw_k_a": 1.0 + nrm((L, RWKV_WIDTH), 0.05),
        "rw_r_k": nrm((L, RWKV_HEADS, HEAD_DIM), 0.1),
        "rw_gn_g": 1.0 + nrm((L, RWKV_WIDTH), 0.05),
        "rw_gn_b": nrm((L, RWKV_WIDTH), 0.02),
        "q_norm": 1.0 + nrm((L, HEAD_DIM), 0.05),
        "k_norm": 1.0 + nrm((L, HEAD_DIM), 0.05),
        "w_o": nrm((L, D, D), D ** -0.5 * DEEPNORM_BETA),
        "ln1_g": 1.0 + nrm((L, D), 0.05),
        "ln1_b": nrm((L, D), 0.02),
        "router_group": nrm((L, D, N_GROUPS), D ** -0.5),
        "router_group_b": nrm((L, N_GROUPS), 0.01),
        "router_expert": nrm((L, D, N_EXPERTS), D ** -0.5),
        "router_expert_b": nrm((L, N_EXPERTS), 0.01),
        "exp_gate": nrm((L, N_EXPERTS, D, EXPERT_HIDDEN), D ** -0.5),
        "exp_up": nrm((L, N_EXPERTS, D, EXPERT_HIDDEN), D ** -0.5),
        "exp_down": nrm((L, N_EXPERTS, EXPERT_HIDDEN, D), EXPERT_HIDDEN ** -0.5 * DEEPNORM_BETA),
        "ln2_g": 1.0 + nrm((L, D), 0.05),
        "ln2_b": nrm((L, D), 0.02),
    }


def reference(x, w_in, mu_prev, mu_next, pool_w, pool_scale, rw_w0, rw_w_up, rw_a0, rw_a_up,
              rw_g_up, rw_k_k, rw_k_a, rw_r_k, rw_gn_g, rw_gn_b, q_norm, k_norm, w_o,
              ln1_g, ln1_b, router_group, router_group_b, router_expert, router_expert_b,
              exp_gate, exp_up, exp_down, ln2_g, ln2_b):
    S = x.shape[1]
    cos, sin = axial_rope_tables(S)
    a0_end = POOL_WIDTH
    b0_end = POOL_WIDTH + RWKV_IN
    for l in range(DEPTH):
        proj = x @ w_in[l]
        y_pool = multiscale_pool(proj[..., :a0_end], pool_w[l], pool_scale[l])
        rw_in = centred_token_shift(proj[..., a0_end:b0_end], mu_prev[l], mu_next[l])
        y_rwkv = rwkv7_bidir(rw_in, rw_w0[l], rw_w_up[l], rw_a0[l], rw_a_up[l], rw_g_up[l],
                             rw_k_k[l], rw_k_a[l], rw_r_k[l], rw_gn_g[l], rw_gn_b[l])
        y_attn = gqa_axial(proj[..., b0_end:], q_norm[l], k_norm[l], cos, sin)
        y = jnp.concatenate([y_pool, y_rwkv, y_attn], -1) @ w_o[l]
        x = layer_norm(DEEPNORM_ALPHA * x + y, ln1_g[l], ln1_b[l])
        m = hier_moe(x, router_group[l], router_group_b[l], router_expert[l], router_expert_b[l],
                     exp_gate[l], exp_up[l], exp_down[l])
        x = layer_norm(DEEPNORM_ALPHA * x + m, ln2_g[l], ln2_b[l])
    return x
```

```python
import functools
import math

import jax
import jax.numpy as jnp
from jax import lax
from jax.experimental import pallas as pl
from jax.experimental.pallas import tpu as pltpu

F32 = jnp.float32
BF16 = jnp.bfloat16

D_MODEL = 1024
DEPTH = 4
GRID_W = 64
HEAD_DIM = 64
POOL_WIDTH = 256
POOL_WINDOWS = (2, 4, 8, 16)
POOL_GROUP = 64
RWKV_WIDTH = 256
RWKV_HEADS = 4
DECAY_LORA = 32
AAA_LORA = 32
GATE_LORA = 64
GN_EPS = 64e-5
RWKV_IN = 960
RWKV_PAD = 1024
ATTN_WIDTH = 512
ATTN_HEADS = 8
ATTN_KV_WIDTH = 128
ATTN_IN = ATTN_WIDTH + 2 * ATTN_KV_WIDTH
ROPE_THETA = 10000.0
QK_EPS = 1e-6
N_GROUPS = 4
EXPERTS_PER_GROUP = 8
N_EXPERTS = 32
EXPERT_HIDDEN = 512
MOE_BLOCK = 256
DEEPNORM_ALPHA = float((2 * DEPTH) ** 0.25)
LN_EPS = 1e-5
PROJ_WIDTH = POOL_WIDTH + RWKV_PAD + ATTN_IN
ROUTE_LANES = 128
WKV_CHUNK = 64
VMEM_LIMIT = 48 * 1024 * 1024


def _cparams(sem):
    return pltpu.CompilerParams(dimension_semantics=sem, vmem_limit_bytes=VMEM_LIMIT)


def _dot(a, b):
    return jnp.dot(a, b, preferred_element_type=F32)


def _dot_nt(a, b):
    return lax.dot_general(a, b, (((1,), (1,)), ((), ())), preferred_element_type=F32)


def _dot_tn(a, b):
    return lax.dot_general(a, b, (((0,), (0,)), ((), ())), preferred_element_type=F32)


def _split2(x):
    hi = x.astype(BF16)
    lo = (x - hi.astype(F32)).astype(BF16)
    return hi, lo


def _split3(x):
    hi = x.astype(BF16)
    r1 = x - hi.astype(F32)
    mid = r1.astype(BF16)
    lo = (r1 - mid.astype(F32)).astype(BF16)
    return hi, mid, lo


def _headsum(x, ones_bf16):
    hi, lo = _split2(x)
    return _dot(hi, ones_bf16) + _dot(lo, ones_bf16)


def _proj_kernel(x_ref, w_ref, pool_ref, rw_ref, at_ref):
    xb = x_ref[...].astype(BF16)
    pool_ref[...] = _dot(xb, w_ref[:, 0:POOL_WIDTH])
    rw_ref[...] = _dot(xb, w_ref[:, POOL_WIDTH:POOL_WIDTH + RWKV_PAD])
    at_ref[...] = _dot(xb, w_ref[:, POOL_WIDTH + RWKV_PAD:PROJ_WIDTH])


def _proj(x2d, w, tm):
    n = x2d.shape[0]
    return pl.pallas_call(
        _proj_kernel,
        grid=(n // tm,),
        in_specs=[pl.BlockSpec((tm, D_MODEL), lambda i: (i, 0)),
                  pl.BlockSpec((D_MODEL, PROJ_WIDTH), lambda i: (0, 0))],
        out_specs=[pl.BlockSpec((tm, POOL_WIDTH), lambda i: (i, 0)),
                   pl.BlockSpec((tm, RWKV_PAD), lambda i: (i, 0)),
                   pl.BlockSpec((tm, ATTN_IN), lambda i: (i, 0))],
        out_shape=[jax.ShapeDtypeStruct((n, POOL_WIDTH), F32),
                   jax.ShapeDtypeStruct((n, RWKV_PAD), F32),
                   jax.ShapeDtypeStruct((n, ATTN_IN), F32)],
        compiler_params=_cparams(("parallel",)),
        name="proj",
    )(x2d, w)


_POOL_PAD = 16


def _pool_kernel(u_ref, w_ref, scale_ref, o_ref, pad_ref):
    s = u_ref.shape[1]
    u = u_ref[0]
    zeros = jnp.zeros((_POOL_PAD, POOL_WIDTH), F32)
    pad_ref[0:_POOL_PAD, :] = zeros
    pad_ref[_POOL_PAD + s:2 * _POOL_PAD + s, :] = zeros
    pad_ref[_POOL_PAD:_POOL_PAD + s, :] = u

    def shifted(k):
        return pad_ref[_POOL_PAD + k:_POOL_PAD + k + s, :]

    s2 = u + shifted(-1)
    s4 = s2 + shifted(-2) + shifted(1)
    s8 = s4 + shifted(-4) + shifted(-3) + shifted(2) + shifted(3)
    s16 = s8
    for k in (-8, -7, -6, -5, 4, 5, 6, 7):
        s16 = s16 + shifted(k)

    t = lax.broadcasted_iota(jnp.int32, (s, POOL_WIDTH), 0)
    grp = lax.broadcasted_iota(jnp.int32, (s, POOL_WIDTH), 1) // POOL_GROUP
    half = jnp.where(grp == 0, 1, jnp.where(grp == 1, 2, jnp.where(grp == 2, 4, 8)))
    cnt = (jnp.minimum(t + half, s) - jnp.maximum(t - half, 0)).astype(F32)
    tot = jnp.where(grp == 0, s2, jnp.where(grp == 1, s4, jnp.where(grp == 2, s8, s16)))
    d = tot / cnt - u
    o_ref[0] = _dot(d.astype(BF16), w_ref[...]) * scale_ref[...]


def _pool(u3d, w_bd, scale):
    b, s, _ = u3d.shape
    return pl.pallas_call(
        _pool_kernel,
        grid=(b,),
        in_specs=[pl.BlockSpec((1, s, POOL_WIDTH), lambda i: (i, 0, 0)),
                  pl.BlockSpec((POOL_WIDTH, POOL_WIDTH), lambda i: (0, 0)),
                  pl.BlockSpec((1, POOL_WIDTH), lambda i: (0, 0))],
        out_specs=pl.BlockSpec((1, s, POOL_WIDTH), lambda i: (i, 0, 0)),
        out_shape=jax.ShapeDtypeStruct((b, s, POOL_WIDTH), F32),
        scratch_shapes=[pltpu.VMEM((s + 2 * _POOL_PAD, POOL_WIDTH), F32)],
        compiler_params=_cparams(("parallel",)),
        name="pool",
    )(u3d, w_bd, scale)


_SHIFT_HALO = 8


def _rwkv_prep_kernel(cur_ref, prev_ref, next_ref, mup_ref, mun_ref, wl_ref, bias_ref,
                      kk_ref, ka_ref, rk_ref, ones_ref,
                      sh_ref, dir_ref, post_ref, pad_ref):
    ts = cur_ref.shape[1]
    ti = pl.program_id(1)
    nt = pl.num_programs(1)
    cur = cur_ref[0]
    pad_ref[0:_SHIFT_HALO, :] = jnp.where(ti > 0, prev_ref[0], 0.0)
    pad_ref[_SHIFT_HALO:_SHIFT_HALO + ts, :] = cur
    pad_ref[_SHIFT_HALO + ts:2 * _SHIFT_HALO + ts, :] = jnp.where(ti < nt - 1, next_ref[0], 0.0)
    prev = pad_ref[_SHIFT_HALO - 1:_SHIFT_HALO - 1 + ts, :]
    nxt = pad_ref[_SHIFT_HALO + 1:_SHIFT_HALO + 1 + ts, :]
    f = cur + mup_ref[...] * (prev - cur) + mun_ref[...] * (nxt - cur)

    w = RWKV_WIDTH
    r = f[:, 0:w]
    k = f[:, w:2 * w]
    v = f[:, 2 * w:3 * w]
    lora = f[:, 3 * w:4 * w]
    lane = lax.broadcasted_iota(jnp.int32, lora.shape, 1)
    z = jnp.where(lane < 2 * DECAY_LORA, jnp.tanh(lora),
                  jnp.where(lane < 2 * DECAY_LORA + 2 * AAA_LORA, lora,
                            jnp.where(lane < 2 * DECAY_LORA + 2 * AAA_LORA + GATE_LORA,
                                      jax.nn.sigmoid(lora), 0.0)))
    up = _dot(z.astype(BF16), wl_ref[...]) + bias_ref[...]
    ones = ones_ref[...]

    kk0 = k * kk_ref[...]
    nrm = jnp.sqrt(_headsum(kk0 * kk0, ones))
    kk = kk0 / jnp.maximum(nrm, 1e-12)
    sh_ref[0, :, 0:w] = r
    sh_ref[0, :, w:2 * w] = v
    sh_ref[0, :, 2 * w:3 * w] = kk

    ksum = jnp.zeros_like(k)
    for di in range(2):
        logw = -math.exp(-0.5) * jax.nn.sigmoid(up[:, di * w:(di + 1) * w])
        a = jax.nn.sigmoid(up[:, (2 + di) * w:(3 + di) * w])
        kh = k * (1.0 + (a - 1.0) * ka_ref[...])
        ksum = ksum + kh
        dir_ref[di, 0, :, 0:w] = logw
        dir_ref[di, 0, :, w:2 * w] = kh
        dir_ref[di, 0, :, 2 * w:3 * w] = -(a * kk)
    bonus = _headsum(r * ksum * rk_ref[...], ones) * v
    post_ref[0, :, 0:w] = bonus
    post_ref[0, :, w:2 * w] = up[:, 4 * w:5 * w]


def _rwkv_prep(rw3d, mup, mun, wl, bias, k_k, k_a, r_k, ones, ts):
    b, s, _ = rw3d.shape
    nt = s // ts
    hb = ts // _SHIFT_HALO
    nhb = s // _SHIFT_HALO
    w = RWKV_WIDTH
    vec = lambda width: pl.BlockSpec((1, width), lambda i, j: (0, 0))
    return pl.pallas_call(
        _rwkv_prep_kernel,
        grid=(b, nt),
        in_specs=[pl.BlockSpec((1, ts, RWKV_PAD), lambda i, j: (i, j, 0)),
                  pl.BlockSpec((1, _SHIFT_HALO, RWKV_PAD),
                               lambda i, j: (i, jnp.maximum(j * hb - 1, 0), 0)),
                  pl.BlockSpec((1, _SHIFT_HALO, RWKV_PAD),
                               lambda i, j: (i, jnp.minimum((j + 1) * hb, nhb - 1), 0)),
                  vec(RWKV_PAD), vec(RWKV_PAD),
                  pl.BlockSpec((w, 5 * w), lambda i, j: (0, 0)),
                  vec(5 * w), vec(w), vec(w), vec(w),
                  pl.BlockSpec((w, w), lambda i, j: (0, 0))],
        out_specs=[pl.BlockSpec((1, ts, 3 * w), lambda i, j: (i, j, 0)),
                   pl.BlockSpec((2, 1, ts, 3 * w), lambda i, j: (0, i, j, 0)),
                   pl.BlockSpec((1, ts, 2 * w), lambda i, j: (i, j, 0))],
        out_shape=[jax.ShapeDtypeStruct((b, s, 3 * w), F32),
                   jax.ShapeDtypeStruct((2, b, s, 3 * w), F32),
                   jax.ShapeDtypeStruct((b, s, 2 * w), F32)],
        scratch_shapes=[pltpu.VMEM((ts + 2 * _SHIFT_HALO, RWKV_PAD), F32)],
        compiler_params=_cparams(("parallel", "parallel")),
        name="rwkv_prep",
    )(rw3d, rw3d, rw3d, mup, mun, wl, bias, k_k, k_a, r_k, ones)


def _expand_bd(x_bf16, mask_bd):
    return jnp.where(mask_bd, jnp.concatenate([x_bf16] * RWKV_HEADS, axis=0), jnp.zeros((), BF16))


def _wkv_kernel(sh_ref, dir_ref, y_ref, h_ref):
    c = WKV_CHUNK
    w = RWKV_WIDTH
    tt = sh_ref.shape[1]
    ncs = tt // c
    rev = pl.program_id(1) == 1
    sgn = 1 - 2 * pl.program_id(1)

    @pl.when(pl.program_id(2) == 0)
    def _():
        h_ref[...] = jnp.zeros_like(h_ref)

    row_c = lax.broadcasted_iota(jnp.int32, (c, c), 0)
    col_c = lax.broadcasted_iota(jnp.int32, (c, c), 1)
    tri_incl = jnp.where((col_c - row_c) * sgn <= 0, 1.0, 0.0).astype(BF16)
    t_i = lax.broadcasted_iota(jnp.int32, (c, w), 0)
    s_i = lax.broadcasted_iota(jnp.int32, (c, w), 1) % c
    strict = (s_i - t_i) * sgn < 0
    incl = (s_i - t_i) * sgn <= 0
    eye_c = (s_i == t_i).astype(F32)
    row_w = lax.broadcasted_iota(jnp.int32, (w, w), 0)
    col_w = lax.broadcasted_iota(jnp.int32, (w, w), 1)
    mask_bd = (row_w // c) == (col_w // HEAD_DIM)
    eye_w = row_w == col_w

    def bf(x):
        return x.astype(BF16)

    for ci in range(ncs):
        start = pl.multiple_of(jnp.where(rev, (ncs - 1 - ci) * c, ci * c), c)
        rows = pl.ds(start, c)
        r = sh_ref[0, rows, 0:w]
        v = sh_ref[0, rows, w:2 * w]
        kk = sh_ref[0, rows, 2 * w:3 * w]
        lw = dir_ref[0, 0, rows, 0:w]
        kh = dir_ref[0, 0, rows, w:2 * w]
        nb = dir_ref[0, 0, rows, 2 * w:3 * w]

        l_hi, l_mid, l_lo = _split3(lw)
        cum = _dot(tri_incl, l_hi) + _dot(tri_incl, l_mid) + _dot(tri_incl, l_lo)
        tot = jnp.sum(lw, axis=0, keepdims=True)
        e_in = jnp.exp(cum)
        e_ex = jnp.exp(cum - lw)
        e_inv = jnp.exp(-cum)
        e_end = jnp.exp(tot - cum)
        a_bar = kk * e_ex
        r_bar = r * e_in
        b_til = nb * e_inv
        k_til = kh * e_inv
        b_hat = nb * e_end
        k_hat = kh * e_end
        v_bf = bf(v)
        v_bd = _expand_bd(v_bf, mask_bd)

        lhs = jnp.concatenate([bf(a_bar), bf(r_bar)], axis=0)
        rhs = jnp.concatenate([_expand_bd(bf(b_til), mask_bd),
                               _expand_bd(bf(k_til), mask_bd)], axis=0)
        sc = _dot_nt(lhs, rhs)
        a_ab = jnp.where(strict, sc[0:c, 0:w], 0.0)
        a_ak = jnp.where(strict, sc[0:c, w:2 * w], 0.0)
        m_rb = jnp.where(incl, sc[c:2 * c, 0:w], 0.0)
        m_rk = jnp.where(incl, sc[c:2 * c, w:2 * w], 0.0)

        x = a_ab
        tinv = eye_c + x
        x = _dot(bf(x), _expand_bd(bf(x), mask_bd))
        n_lv = int(math.log2(c)) - 1
        for lv in range(n_lv):
            res = _dot(jnp.concatenate([bf(x), bf(tinv)], axis=0), _expand_bd(bf(x), mask_bd))
            tinv = tinv + res[c:2 * c]
            x = res[0:c]
        tinv_bf = bf(tinv)

        akv = _dot(bf(a_ak), v_bd)
        a_pr = _dot(tinv_bf, _expand_bd(bf(a_bar), mask_bd))
        v_pr = _dot(tinv_bf, _expand_bd(bf(akv), mask_bd))
        a_pr_bf = bf(a_pr)
        v_pr_bf = bf(v_pr)
        m_rb_bf = bf(m_rb)
        r_pr = r_bar + _dot(m_rb_bf, _expand_bd(a_pr_bf, mask_bd))
        y_pr = _dot(m_rb_bf, _expand_bd(v_pr_bf, mask_bd)) + _dot(bf(m_rk), v_bd)
        p_bd = jnp.where(eye_w, jnp.exp(tot), 0.0) + jnp.where(mask_bd, _dot_tn(bf(b_hat), a_pr_bf), 0.0)
        q_bd = jnp.where(mask_bd,
                         _dot_tn(jnp.concatenate([bf(b_hat), bf(k_hat)], axis=0),
                                 jnp.concatenate([v_pr_bf, v_bf], axis=0)), 0.0)

        h_bf = bf(h_ref[...])
        y_ref[0, 0, rows, :] = _dot(bf(r_pr), h_bf) + y_pr
        h_ref[...] = _dot(bf(p_bd), h_bf) + q_bd


def _wkv(shared, dirp, tt):
    b, s, _ = shared.shape
    nt = s // tt
    w = RWKV_WIDTH

    def tile(d, j):
        return j + d * (nt - 1 - 2 * j)

    return pl.pallas_call(
        _wkv_kernel,
        grid=(b, 2, nt),
        in_specs=[pl.BlockSpec((1, tt, 3 * w), lambda i, d, j: (i, tile(d, j), 0)),
                  pl.BlockSpec((1, 1, tt, 3 * w), lambda i, d, j: (d, i, tile(d, j), 0))],
        out_specs=pl.BlockSpec((1, 1, tt, w), lambda i, d, j: (d, i, tile(d, j), 0)),
        out_shape=jax.ShapeDtypeStruct((2, b, s, w), F32),
        scratch_shapes=[pltpu.VMEM((w, w), F32)],
        compiler_params=_cparams(("parallel", "parallel", "arbitrary")),
        name="wkv",
    )(shared, dirp)


def _rope(x, cos, sin_signed):
    n = x.shape[-1]
    lane = lax.broadcasted_iota(jnp.int32, x.shape, 1)
    first = (lane // 16) % 2 == 0
    partner = jnp.where(first, pltpu.roll(x, n - 16, 1), pltpu.roll(x, 16, 1))
    return x * cos + partner * sin_signed


def _attn_prep_kernel(at_ref, cos_ref, sin_ref, qg_ref, kg_ref, onesq_ref, q_ref, k_ref, v_ref):
    q = at_ref[:, 0:ATTN_WIDTH]
    k = at_ref[:, ATTN_WIDTH:ATTN_WIDTH + ATTN_KV_WIDTH]
    v = at_ref[:, ATTN_WIDTH + ATTN_KV_WIDTH:ATTN_IN]
    ones_q = onesq_ref[...]
    ones_k = onesq_ref[0:ATTN_KV_WIDTH, 0:ATTN_KV_WIDTH]
    inv = 1.0 / HEAD_DIM
    qn = q * lax.rsqrt(_headsum(q * q, ones_q) * inv + QK_EPS) * qg_ref[...]
    kn = k * lax.rsqrt(_headsum(k * k, ones_k) * inv + QK_EPS) * kg_ref[...]
    cos = cos_ref[...]
    sin = sin_ref[...]
    q_ref[...] = (_rope(qn, cos, sin) * (HEAD_DIM ** -0.5)).astype(BF16)
    k_ref[...] = _rope(kn, cos[:, 0:ATTN_KV_WIDTH], sin[:, 0:ATTN_KV_WIDTH]).astype(BF16)
    v_ref[...] = v.astype(BF16)


def _attn_prep(at2d, cos8, sin8, qg, kg, ones_q, seq, tm):
    n = at2d.shape[0]
    nts = seq // tm
    return pl.pallas_call(
        _attn_prep_kernel,
        grid=(n // tm,),
        in_specs=[pl.BlockSpec((tm, ATTN_IN), lambda i: (i, 0)),
                  pl.BlockSpec((tm, ATTN_WIDTH), lambda i: (i % nts, 0)),
                  pl.BlockSpec((tm, ATTN_WIDTH), lambda i: (i % nts, 0)),
                  pl.BlockSpec((1, ATTN_WIDTH), lambda i: (0, 0)),
                  pl.BlockSpec((1, ATTN_KV_WIDTH), lambda i: (0, 0)),
                  pl.BlockSpec((ATTN_WIDTH, ATTN_WIDTH), lambda i: (0, 0))],
        out_specs=[pl.BlockSpec((tm, ATTN_WIDTH), lambda i: (i, 0)),
                   pl.BlockSpec((tm, ATTN_KV_WIDTH), lambda i: (i, 0)),
                   pl.BlockSpec((tm, ATTN_KV_WIDTH), lambda i: (i, 0))],
        out_shape=[jax.ShapeDtypeStruct((n, ATTN_WIDTH), BF16),
                   jax.ShapeDtypeStruct((n, ATTN_KV_WIDTH), BF16),
                   jax.ShapeDtypeStruct((n, ATTN_KV_WIDTH), BF16)],
        compiler_params=_cparams(("parallel",)),
        name="attn_prep",
    )(at2d, cos8, sin8, qg, kg, ones_q)


def _attn_kernel(q_ref, k_ref, v_ref, o_ref):
    tq = q_ref.shape[0]
    k = k_ref[...]
    v = v_ref[...]
    lane = lax.broadcasted_iota(jnp.int32, (tq, ATTN_KV_WIDTH), 1)
    low = lane < HEAD_DIM
    zero = jnp.zeros((), BF16)
    for j in range(ATTN_HEADS // 2):
        qb = q_ref[:, j * ATTN_KV_WIDTH:(j + 1) * ATTN_KV_WIDTH]
        outs = []
        for half in range(2):
            qm = jnp.where(low if half == 0 else ~low, qb, zero)
            s = _dot_nt(qm, k)
            m = jnp.max(s, axis=-1, keepdims=True)
            p = jnp.exp(s - m)
            l = jnp.sum(p, axis=-1, keepdims=True)
            outs.append(_dot(p.astype(BF16), v) / l)
        o_ref[:, j * ATTN_KV_WIDTH:(j + 1) * ATTN_KV_WIDTH] = jnp.where(low, outs[0], outs[1]).astype(BF16)


def _attn(q, k, v, seq, tq):
    n = q.shape[0]
    nq = seq // tq
    return pl.pallas_call(
        _attn_kernel,
        grid=(n // seq, nq),
        in_specs=[pl.BlockSpec((tq, ATTN_WIDTH), lambda b, i: (b * nq + i, 0)),
                  pl.BlockSpec((seq, ATTN_KV_WIDTH), lambda b, i: (b, 0)),
                  pl.BlockSpec((seq, ATTN_KV_WIDTH), lambda b, i: (b, 0))],
        out_specs=pl.BlockSpec((tq, ATTN_WIDTH), lambda b, i: (b * nq + i, 0)),
        out_shape=jax.ShapeDtypeStruct((n, ATTN_WIDTH), BF16),
        compiler_params=_cparams(("parallel", "parallel")),
        name="attn",
    )(q, k, v)


def _layer_norm(z, g, b):
    mu = jnp.mean(z, axis=-1, keepdims=True)
    zc = z - mu
    var = jnp.mean(zc * zc, axis=-1, keepdims=True)
    return zc * lax.rsqrt(var + LN_EPS) * g + b


def _outproj_kernel(x_ref, pool_ref, wkv_ref, post_ref, attn_ref, wo_ref, gng_ref, gnb_ref,
                    ones_ref, lg_ref, lb_ref, wrh_ref, wrl_ref, br_ref,
                    x1_ref, x1b_ref, route_ref):
    w = RWKV_WIDTH
    ones = ones_ref[...]
    wkv = wkv_ref[0] + wkv_ref[1]
    inv = 1.0 / HEAD_DIM
    mu = _headsum(wkv, ones) * inv
    cen = wkv - mu
    var = _headsum(cen * cen, ones) * inv
    yr = cen * lax.rsqrt(var + GN_EPS) * gng_ref[...] + gnb_ref[...] + post_ref[:, 0:w]
    yr = yr * post_ref[:, w:2 * w]
    y = (_dot(pool_ref[...].astype(BF16), wo_ref[0:POOL_WIDTH, :])
         + _dot(yr.astype(BF16), wo_ref[POOL_WIDTH:POOL_WIDTH + w, :])
         + _dot(attn_ref[...], wo_ref[POOL_WIDTH + w:D_MODEL, :]))
    x1 = _layer_norm(DEEPNORM_ALPHA * x_ref[...] + y, lg_ref[...], lb_ref[...])
    x1_ref[...] = x1
    x1b_ref[...] = x1.astype(BF16)

    xh, xl = _split2(x1)
    logits = (_dot(xh, wrh_ref[...]) + _dot(xl, wrh_ref[...]) + _dot(xh, wrl_ref[...])) + br_ref[...]
    lane = lax.broadcasted_iota(jnp.int32, logits.shape, 1)
    neg = -jnp.inf
    big = jnp.int32(1 << 20)
    gl = jnp.where(lane < N_GROUPS, logits, neg)
    gmax = jnp.max(gl, axis=-1, keepdims=True)
    grp = jnp.min(jnp.where(gl == gmax, lane, big), axis=-1, keepdims=True)
    gw = 1.0 / jnp.sum(jnp.exp(gl - gmax), axis=-1, keepdims=True)
    in_grp = (lane >= N_GROUPS) & (lane < N_GROUPS + N_EXPERTS) & \
             ((lane - N_GROUPS) // EXPERTS_PER_GROUP == grp)
    el = jnp.where(in_grp, logits, neg)
    v1 = jnp.max(el, axis=-1, keepdims=True)
    i1 = jnp.min(jnp.where(el == v1, lane, big), axis=-1, keepdims=True)
    el2 = jnp.where(lane == i1, neg, el)
    v2 = jnp.max(el2, axis=-1, keepdims=True)
    i2 = jnp.min(jnp.where(el2 == v2, lane, big), axis=-1, keepdims=True)
    e21 = jnp.exp(v2 - v1)
    g1 = gw / (1.0 + e21)
    g2 = gw * e21 / (1.0 + e21)
    route = jnp.where(lane == 0, (i1 - N_GROUPS).astype(F32),
                      jnp.where(lane == 1, (i2 - N_GROUPS).astype(F32),
                                jnp.where(lane == 2, g1, jnp.where(lane == 3, g2, 0.0))))
    route_ref[...] = route


def _outproj(x2d, pool2d, wkv3d, post2d, attn2d, wo, gng, gnb, ones, lg, lb, wrh, wrl, br, tm):
    n = x2d.shape[0]
    w = RWKV_WIDTH
    row = lambda width: pl.BlockSpec((tm, width), lambda i: (i, 0))
    vec = lambda width: pl.BlockSpec((1, width), lambda i: (0, 0))
    return pl.pallas_call(
        _outproj_kernel,
        grid=(n // tm,),
        in_specs=[row(D_MODEL), row(POOL_WIDTH),
                  pl.BlockSpec((2, tm, w), lambda i: (0, i, 0)),
                  row(2 * w), row(ATTN_WIDTH),
                  pl.BlockSpec((D_MODEL, D_MODEL), lambda i: (0, 0)),
                  vec(w), vec(w),
                  pl.BlockSpec((w, w), lambda i: (0, 0)),
                  vec(D_MODEL), vec(D_MODEL),
                  pl.BlockSpec((D_MODEL, ROUTE_LANES), lambda i: (0, 0)),
                  pl.BlockSpec((D_MODEL, ROUTE_LANES), lambda i: (0, 0)),
                  vec(ROUTE_LANES)],
        out_specs=[row(D_MODEL), row(D_MODEL), row(ROUTE_LANES)],
        out_shape=[jax.ShapeDtypeStruct((n, D_MODEL), F32),
                   jax.ShapeDtypeStruct((n, D_MODEL), BF16),
                   jax.ShapeDtypeStruct((n, ROUTE_LANES), F32)],
        compiler_params=_cparams(("parallel",)),
        name="outproj",
    )(x2d, pool2d, wkv3d, post2d, attn2d, wo, gng, gnb, ones, lg, lb, wrh, wrl, br)


def _expert_kernel(bexp_ref, nused_ref, xs_ref, wg_ref, wu_ref, wd_ref, ys_ref):
    @pl.when(pl.program_id(0) < nused_ref[0])
    def _():
        xb = xs_ref[...]
        h1 = _dot(xb, wg_ref[0])
        h2 = _dot(xb, wu_ref[0])
        h = (h1 * jax.nn.sigmoid(h1) * h2).astype(BF16)
        ys_ref[...] = _dot(h, wd_ref[0]).astype(BF16)

    @pl.when(pl.program_id(0) >= nused_ref[0])
    def _():
        ys_ref[...] = jnp.zeros_like(ys_ref)


def _experts(block_exp, n_used, xs, wg, wu, wd):
    p = xs.shape[0]
    nb = p // MOE_BLOCK
    grid_spec = pltpu.PrefetchScalarGridSpec(
        num_scalar_prefetch=2,
        grid=(nb,),
        in_specs=[pl.BlockSpec((MOE_BLOCK, D_MODEL), lambda i, be, nu: (i, 0)),
                  pl.BlockSpec((1, D_MODEL, EXPERT_HIDDEN), lambda i, be, nu: (be[i], 0, 0)),
                  pl.BlockSpec((1, D_MODEL, EXPERT_HIDDEN), lambda i, be, nu: (be[i], 0, 0)),
                  pl.BlockSpec((1, EXPERT_HIDDEN, D_MODEL), lambda i, be, nu: (be[i], 0, 0))],
        out_specs=pl.BlockSpec((MOE_BLOCK, D_MODEL), lambda i, be, nu: (i, 0)),
    )
    return pl.pallas_call(
        _expert_kernel,
        grid_spec=grid_spec,
        out_shape=jax.ShapeDtypeStruct((p, D_MODEL), BF16),
        compiler_params=_cparams(("arbitrary",)),
        name="experts",
    )(block_exp, n_used, xs, wg, wu, wd)


def _combine_kernel(x1_ref, ya_ref, yb_ref, route_ref, lg_ref, lb_ref, o_ref):
    route = route_ref[...]
    g1 = route[:, 2:3]
    g2 = route[:, 3:4]
    m = g1 * ya_ref[...].astype(F32) + g2 * yb_ref[...].astype(F32)
    o_ref[...] = _layer_norm(DEEPNORM_ALPHA * x1_ref[...] + m, lg_ref[...], lb_ref[...])


def _combine(x1, ya, yb, route, lg, lb, tm):
    n = x1.shape[0]
    row = lambda width: pl.BlockSpec((tm, width), lambda i: (i, 0))
    vec = lambda width: pl.BlockSpec((1, width), lambda i: (0, 0))
    return pl.pallas_call(
        _combine_kernel,
        grid=(n // tm,),
        in_specs=[row(D_MODEL), row(D_MODEL), row(D_MODEL), row(ROUTE_LANES),
                  vec(D_MODEL), vec(D_MODEL)],
        out_specs=row(D_MODEL),
        out_shape=jax.ShapeDtypeStruct((n, D_MODEL), F32),
        compiler_params=_cparams(("parallel",)),
        name="combine",
    )(x1, ya, yb, route, lg, lb)


def _q_perm():
    order = [h for j in range(ATTN_HEADS // 2) for h in (j, j + ATTN_HEADS // 2)]
    return jnp.concatenate([jnp.arange(h * HEAD_DIM, (h + 1) * HEAD_DIM) for h in order])


def _block_diag(blocks):
    n = len(blocks)
    r, c = blocks[0].shape
    out = jnp.zeros((n * r, n * c), blocks[0].dtype)
    for i, blk in enumerate(blocks):
        out = out.at[i * r:(i + 1) * r, i * c:(i + 1) * c].set(blk)
    return out


def _rope_tables(seq):
    rows = seq // GRID_W
    row_id = jnp.repeat(jnp.arange(rows), GRID_W).astype(F32)
    col_id = jnp.tile(jnp.arange(GRID_W), rows).astype(F32)
    half = HEAD_DIM // 2
    inv_freq = ROPE_THETA ** (-jnp.arange(0, half, 2, dtype=F32) / half)
    ang_r = row_id[:, None] * inv_freq
    ang_c = col_id[:, None] * inv_freq
    ang = jnp.concatenate([ang_r, ang_r, ang_c, ang_c], -1)
    sign = jnp.where((jnp.arange(HEAD_DIM) // 16) % 2 == 0, -1.0, 1.0).astype(F32)
    cos8 = jnp.tile(jnp.cos(ang), (1, ATTN_HEADS))
    sin8 = jnp.tile(jnp.sin(ang) * sign, (1, ATTN_HEADS))
    return cos8, sin8


def _dispatch(route, n):
    m = n * 2
    e_flat = route[:, 0:2].astype(jnp.int32).reshape(m)
    tok = jnp.repeat(jnp.arange(n, dtype=jnp.int32), 2)
    order = jnp.argsort(e_flat)
    e_sorted = e_flat[order]
    counts = jnp.bincount(e_flat, length=N_EXPERTS)
    padded = (counts + MOE_BLOCK - 1) // MOE_BLOCK * MOE_BLOCK
    start = jnp.cumsum(counts) - counts
    ends_p = jnp.cumsum(padded)
    pstart = ends_p - padded
    dest = (pstart[e_sorted] + jnp.arange(m) - start[e_sorted]).astype(jnp.int32)
    n_blocks = -(-(m + N_EXPERTS * (MOE_BLOCK - 1)) // MOE_BLOCK)
    p = n_blocks * MOE_BLOCK
    row_tok = jnp.full((p,), n, jnp.int32).at[dest].set(tok[order])
    pos = jnp.zeros((m,), jnp.int32).at[order].set(dest)
    block_exp = jnp.minimum(
        jnp.searchsorted(ends_p, jnp.arange(n_blocks) * MOE_BLOCK, side='right'),
        N_EXPERTS - 1).astype(jnp.int32)
    n_used = (ends_p[-1] // MOE_BLOCK).astype(jnp.int32).reshape(1)
    return row_tok, pos.reshape(n, 2), block_exp, n_used


def kernel(x, w_in, mu_prev, mu_next, pool_w, pool_scale, rw_w0, rw_w_up, rw_a0, rw_a_up, rw_g_up, rw_k_k, rw_k_a, rw_r_k, rw_gn_g, rw_gn_b, q_norm, k_norm, w_o, ln1_g, ln1_b, router_group, router_group_b, router_expert, router_expert_b, exp_gate, exp_up, exp_down, ln2_g, ln2_b):
    b, s, d = x.shape
    n = b * s
    w = RWKV_WIDTH
    tm = min(512, s)
    qperm = _q_perm()
    cos8, sin8 = _rope_tables(s)
    ones_q = _block_diag([jnp.ones((HEAD_DIM, HEAD_DIM), BF16)] * ATTN_HEADS)
    ones_r = ones_q[0:w, 0:w]
    a_end = POOL_WIDTH
    b_end = POOL_WIDTH + RWKV_IN

    xc = x.reshape(n, d)
    for l in range(DEPTH):
        wq = w_in[l][:, b_end:b_end + ATTN_WIDTH][:, qperm]
        w_proj = jnp.concatenate(
            [w_in[l][:, :a_end], w_in[l][:, a_end:b_end],
             jnp.zeros((d, RWKV_PAD - RWKV_IN), F32), wq, w_in[l][:, b_end + ATTN_WIDTH:]],
            axis=1).astype(BF16)
        pool_bd = _block_diag([pool_w[l, g] for g in range(len(POOL_WINDOWS))]).astype(BF16)
        pad_vec = jnp.zeros((RWKV_PAD - RWKV_IN,), F32)
        mup = jnp.concatenate([mu_prev[l], pad_vec]).reshape(1, RWKV_PAD)
        mun = jnp.concatenate([mu_next[l], pad_vec]).reshape(1, RWKV_PAD)
        w_lora = jnp.zeros((w, 5 * w), F32)
        for di in range(2):
            w_lora = w_lora.at[di * DECAY_LORA:(di + 1) * DECAY_LORA, di * w:(di + 1) * w].set(rw_w_up[l, di])
            o = 2 * DECAY_LORA
            w_lora = w_lora.at[o + di * AAA_LORA:o + (di + 1) * AAA_LORA, (2 + di) * w:(3 + di) * w].set(rw_a_up[l, di])
        o = 2 * DECAY_LORA + 2 * AAA_LORA
        w_lora = w_lora.at[o:o + GATE_LORA, 4 * w:5 * w].set(rw_g_up[l]).astype(BF16)
        lora_bias = jnp.concatenate([rw_w0[l, 0], rw_w0[l, 1], rw_a0[l, 0], rw_a0[l, 1],
                                     jnp.zeros((w,), F32)]).reshape(1, 5 * w)
        qg = jnp.tile(q_norm[l], ATTN_HEADS).reshape(1, ATTN_WIDTH)
        kg = jnp.tile(k_norm[l], 2).reshape(1, ATTN_KV_WIDTH)
        wo_attn = w_o[l][POOL_WIDTH + w:][qperm]
        wo = jnp.concatenate([w_o[l][:POOL_WIDTH + w], wo_attn], axis=0).astype(BF16)
        wr = jnp.concatenate([router_group[l], router_expert[l],
                              jnp.zeros((d, ROUTE_LANES - N_GROUPS - N_EXPERTS), F32)], axis=1)
        wrh = wr.astype(BF16)
        wrl = (wr - wrh.astype(F32)).astype(BF16)
        br = jnp.concatenate([router_group_b[l], router_expert_b[l],
                              jnp.zeros((ROUTE_LANES - N_GROUPS - N_EXPERTS,), F32)]).reshape(1, ROUTE_LANES)

        pool_in, rw_in, at_in = _proj(xc, w_proj, tm)
        y_pool = _pool(pool_in.reshape(b, s, POOL_WIDTH), pool_bd, pool_scale[l].reshape(1, POOL_WIDTH))
        shared, dirp, post = _rwkv_prep(rw_in.reshape(b, s, RWKV_PAD), mup, mun, w_lora, lora_bias,
                                        rw_k_k[l].reshape(1, w), rw_k_a[l].reshape(1, w),
                                        rw_r_k[l].reshape(1, w), ones_r, tm)
        wkv = _wkv(shared, dirp, min(256, s))
        qh, kh, vh = _attn_prep(at_in, cos8, sin8, qg, kg, ones_q, s, tm)
        y_attn = _attn(qh, kh, vh, s, min(256, s))
        x1, x1b, route = _outproj(xc, y_pool.reshape(n, POOL_WIDTH), wkv.reshape(2, n, w),
                                  post.reshape(n, 2 * w), y_attn, wo,
                                  rw_gn_g[l].reshape(1, w), rw_gn_b[l].reshape(1, w), ones_r,
                                  ln1_g[l].reshape(1, d), ln1_b[l].reshape(1, d), wrh, wrl, br, tm)

        row_tok, pos, block_exp, n_used = _dispatch(route, n)
        xs = jnp.concatenate([x1b, jnp.zeros((1, d), BF16)], axis=0)[row_tok]
        ys = _experts(block_exp, n_used, xs, exp_gate[l].astype(BF16), exp_up[l].astype(BF16),
                      exp_down[l].astype(BF16))
        xc = _combine(x1, ys[pos[:, 0]], ys[pos[:, 1]], route,
                      ln2_g[l].reshape(1, d), ln2_b[l].reshape(1, d), tm)
    return xc.reshape(b, s, d)
```

```python
import functools
import math

import jax
import jax.numpy as jnp
from jax import lax
from jax.experimental import pallas as pl
from jax.experimental.pallas import tpu as pltpu

F32 = jnp.float32
BF16 = jnp.bfloat16

D_MODEL = 1024
DEPTH = 4
GRID_W = 64
HEAD_DIM = 64
POOL_WIDTH = 256
POOL_WINDOWS = (2, 4, 8, 16)
POOL_GROUP = 64
RWKV_WIDTH = 256
RWKV_HEADS = 4
DECAY_LORA = 32
AAA_LORA = 32
GATE_LORA = 64
GN_EPS = 64e-5
RWKV_IN = 960
RWKV_PAD = 1024
ATTN_WIDTH = 512
ATTN_HEADS = 8
ATTN_KV_WIDTH = 128
ATTN_IN = ATTN_WIDTH + 2 * ATTN_KV_WIDTH
ROPE_THETA = 10000.0
QK_EPS = 1e-6
N_GROUPS = 4
EXPERTS_PER_GROUP = 8
N_EXPERTS = 32
EXPERT_HIDDEN = 512
MOE_BLOCK = 256
DEEPNORM_ALPHA = float((2 * DEPTH) ** 0.25)
LN_EPS = 1e-5
PROJ_WIDTH = POOL_WIDTH + RWKV_PAD + ATTN_IN
ROUTE_LANES = 128
WKV_CHUNK = 64
VMEM_LIMIT = 48 * 1024 * 1024


def _cparams(sem):
    return pltpu.CompilerParams(dimension_semantics=sem, vmem_limit_bytes=VMEM_LIMIT)


def _dot(a, b):
    return jnp.dot(a, b, preferred_element_type=F32)


def _dot_nt(a, b):
    return lax.dot_general(a, b, (((1,), (1,)), ((), ())), preferred_element_type=F32)


def _dot_tn(a, b):
    return lax.dot_general(a, b, (((0,), (0,)), ((), ())), preferred_element_type=F32)


def _split2(x):
    hi = x.astype(BF16)
    lo = (x - hi.astype(F32)).astype(BF16)
    return hi, lo


def _split3(x):
    hi = x.astype(BF16)
    r1 = x - hi.astype(F32)
    mid = r1.astype(BF16)
    lo = (r1 - mid.astype(F32)).astype(BF16)
    return hi, mid, lo


def _headsum(x, ones_bf16):
    hi, lo = _split2(x)
    return _dot(hi, ones_bf16) + _dot(lo, ones_bf16)


def _proj_kernel(x_ref, w_ref, pool_ref, rw_ref, at_ref):
    xb = x_ref[...].astype(BF16)
    pool_ref[...] = _dot(xb, w_ref[:, 0:POOL_WIDTH])
    rw_ref[...] = _dot(xb, w_ref[:, POOL_WIDTH:POOL_WIDTH + RWKV_PAD])
    at_ref[...] = _dot(xb, w_ref[:, POOL_WIDTH + RWKV_PAD:PROJ_WIDTH])


def _proj(x2d, w, tm):
    n = x2d.shape[0]
    return pl.pallas_call(
        _proj_kernel,
        grid=(n // tm,),
        in_specs=[pl.BlockSpec((tm, D_MODEL), lambda i: (i, 0)),
                  pl.BlockSpec((D_MODEL, PROJ_WIDTH), lambda i: (0, 0))],
        out_specs=[pl.BlockSpec((tm, POOL_WIDTH), lambda i: (i, 0)),
                   pl.BlockSpec((tm, RWKV_PAD), lambda i: (i, 0)),
                   pl.BlockSpec((tm, ATTN_IN), lambda i: (i, 0))],
        out_shape=[jax.ShapeDtypeStruct((n, POOL_WIDTH), F32),
                   jax.ShapeDtypeStruct((n, RWKV_PAD), F32),
                   jax.ShapeDtypeStruct((n, ATTN_IN), F32)],
        compiler_params=_cparams(("parallel",)),
        name="proj",
    )(x2d, w)


_POOL_PAD = 16


def _pool_kernel(u_ref, w_ref, scale_ref, o_ref, pad_ref):
    s = u_ref.shape[1]
    u = u_ref[0]
    zeros = jnp.zeros((_POOL_PAD, POOL_WIDTH), F32)
    pad_ref[0:_POOL_PAD, :] = zeros
    pad_ref[_POOL_PAD + s:2 * _POOL_PAD + s, :] = zeros
    pad_ref[_POOL_PAD:_POOL_PAD + s, :] = u

    def shifted(k):
        return pad_ref[_POOL_PAD + k:_POOL_PAD + k + s, :]

    s2 = u + shifted(-1)
    s4 = s2 + shifted(-2) + shifted(1)
    s8 = s4 + shifted(-4) + shifted(-3) + shifted(2) + shifted(3)
    s16 = s8
    for k in (-8, -7, -6, -5, 4, 5, 6, 7):
        s16 = s16 + shifted(k)

    t = lax.broadcasted_iota(jnp.int32, (s, POOL_WIDTH), 0)
    grp = lax.broadcasted_iota(jnp.int32, (s, POOL_WIDTH), 1) // POOL_GROUP
    half = jnp.where(grp == 0, 1, jnp.where(grp == 1, 2, jnp.where(grp == 2, 4, 8)))
    cnt = (jnp.minimum(t + half, s) - jnp.maximum(t - half, 0)).astype(F32)
    tot = jnp.where(grp == 0, s2, jnp.where(grp == 1, s4, jnp.where(grp == 2, s8, s16)))
    d = tot / cnt - u
    o_ref[0] = _dot(d.astype(BF16), w_ref[...]) * scale_ref[...]


def _pool(u3d, w_bd, scale):
    b, s, _ = u3d.shape
    return pl.pallas_call(
        _pool_kernel,
        grid=(b,),
        in_specs=[pl.BlockSpec((1, s, POOL_WIDTH), lambda i: (i, 0, 0)),
                  pl.BlockSpec((POOL_WIDTH, POOL_WIDTH), lambda i: (0, 0)),
                  pl.BlockSpec((1, POOL_WIDTH), lambda i: (0, 0))],
        out_specs=pl.BlockSpec((1, s, POOL_WIDTH), lambda i: (i, 0, 0)),
        out_shape=jax.ShapeDtypeStruct((b, s, POOL_WIDTH), F32),
        scratch_shapes=[pltpu.VMEM((s + 2 * _POOL_PAD, POOL_WIDTH), F32)],
        compiler_params=_cparams(("parallel",)),
        name="pool",
    )(u3d, w_bd, scale)


_SHIFT_HALO = 8


def _rwkv_prep_kernel(cur_ref, prev_ref, next_ref, mup_ref, mun_ref, wl_ref, bias_ref,
                      kk_ref, ka_ref, rk_ref, ones_ref,
                      sh_ref, dir_ref, post_ref, pad_ref):
    ts = cur_ref.shape[1]
    ti = pl.program_id(1)
    nt = pl.num_programs(1)
    cur = cur_ref[0]
    pad_ref[0:_SHIFT_HALO, :] = jnp.where(ti > 0, prev_ref[0], 0.0)
    pad_ref[_SHIFT_HALO:_SHIFT_HALO + ts, :] = cur
    pad_ref[_SHIFT_HALO + ts:2 * _SHIFT_HALO + ts, :] = jnp.where(ti < nt - 1, next_ref[0], 0.0)
    prev = pad_ref[_SHIFT_HALO - 1:_SHIFT_HALO - 1 + ts, :]
    nxt = pad_ref[_SHIFT_HALO + 1:_SHIFT_HALO + 1 + ts, :]
    f = cur + mup_ref[...] * (prev - cur) + mun_ref[...] * (nxt - cur)

    w = RWKV_WIDTH
    r = f[:, 0:w]
    k = f[:, w:2 * w]
    v = f[:, 2 * w:3 * w]
    lora = f[:, 3 * w:4 * w]
    lane = lax.broadcasted_iota(jnp.int32, lora.shape, 1)
    z = jnp.where(lane < 2 * DECAY_LORA, jnp.tanh(lora),
                  jnp.where(lane < 2 * DECAY_LORA + 2 * AAA_LORA, lora,
                            jnp.where(lane < 2 * DECAY_LORA + 2 * AAA_LORA + GATE_LORA,
                                      jax.nn.sigmoid(lora), 0.0)))
    up = _dot(z.astype(BF16), wl_ref[...]) + bias_ref[...]
    ones = ones_ref[...]

    kk0 = k * kk_ref[...]
    nrm = jnp.sqrt(_headsum(kk0 * kk0, ones))
    kk = kk0 / jnp.maximum(nrm, 1e-12)
    sh_ref[0, :, 0:w] = r
    sh_ref[0, :, w:2 * w] = v
    sh_ref[0, :, 2 * w:3 * w] = kk

    ksum = jnp.zeros_like(k)
    for di in range(2):
        logw = -math.exp(-0.5) * jax.nn.sigmoid(up[:, di * w:(di + 1) * w])
        a = jax.nn.sigmoid(up[:, (2 + di) * w:(3 + di) * w])
        kh = k * (1.0 + (a - 1.0) * ka_ref[...])
        ksum = ksum + kh
        dir_ref[di, 0, :, 0:w] = logw
        dir_ref[di, 0, :, w:2 * w] = kh
        dir_ref[di, 0, :, 2 * w:3 * w] = -(a * kk)
    bonus = _headsum(r * ksum * rk_ref[...], ones) * v
    post_ref[0, :, 0:w] = bonus
    post_ref[0, :, w:2 * w] = up[:, 4 * w:5 * w]


def _rwkv_prep(rw3d, mup, mun, wl, bias, k_k, k_a, r_k, ones, ts):
    b, s, _ = rw3d.shape
    nt = s // ts
    hb = ts // _SHIFT_HALO
    nhb = s // _SHIFT_HALO
    w = RWKV_WIDTH
    vec = lambda width: pl.BlockSpec((1, width), lambda i, j: (0, 0))
    return pl.pallas_call(
        _rwkv_prep_kernel,
        grid=(b, nt),
        in_specs=[pl.BlockSpec((1, ts, RWKV_PAD), lambda i, j: (i, j, 0)),
                  pl.BlockSpec((1, _SHIFT_HALO, RWKV_PAD),
                               lambda i, j: (i, jnp.maximum(j * hb - 1, 0), 0)),
                  pl.BlockSpec((1, _SHIFT_HALO, RWKV_PAD),
                               lambda i, j: (i, jnp.minimum((j + 1) * hb, nhb - 1), 0)),
                  vec(RWKV_PAD), vec(RWKV_PAD),
                  pl.BlockSpec((w, 5 * w), lambda i, j: (0, 0)),
                  vec(5 * w), vec(w), vec(w), vec(w),
                  pl.BlockSpec((w, w), lambda i, j: (0, 0))],
        out_specs=[pl.BlockSpec((1, ts, 3 * w), lambda i, j: (i, j, 0)),
                   pl.BlockSpec((2, 1, ts, 3 * w), lambda i, j: (0, i, j, 0)),
                   pl.BlockSpec((1, ts, 2 * w), lambda i, j: (i, j, 0))],
        out_shape=[jax.ShapeDtypeStruct((b, s, 3 * w), F32),
                   jax.ShapeDtypeStruct((2, b, s, 3 * w), F32),
                   jax.ShapeDtypeStruct((b, s, 2 * w), F32)],
        scratch_shapes=[pltpu.VMEM((ts + 2 * _SHIFT_HALO, RWKV_PAD), F32)],
        compiler_params=_cparams(("parallel", "parallel")),
        name="rwkv_prep",
    )(rw3d, rw3d, rw3d, mup, mun, wl, bias, k_k, k_a, r_k, ones)


def _expand_bd(x_bf16, mask_bd):
    return jnp.where(mask_bd, jnp.concatenate([x_bf16] * RWKV_HEADS, axis=0), jnp.zeros((), BF16))


def _wkv_kernel(shf_ref, shr_ref, df_ref, dr_ref, yf_ref, yr_ref, hf_ref, hr_ref):
    c = WKV_CHUNK
    w = RWKV_WIDTH
    tt = shf_ref.shape[1]
    ncs = tt // c

    @pl.when(pl.program_id(1) == 0)
    def _():
        hf_ref[...] = jnp.zeros_like(hf_ref)
        hr_ref[...] = jnp.zeros_like(hr_ref)

    row_c = lax.broadcasted_iota(jnp.int32, (c, c), 0)
    col_c = lax.broadcasted_iota(jnp.int32, (c, c), 1)
    t_i = lax.broadcasted_iota(jnp.int32, (c, w), 0)
    s_i = lax.broadcasted_iota(jnp.int32, (c, w), 1) % c
    eye_c = (s_i == t_i).astype(F32)
    row_w = lax.broadcasted_iota(jnp.int32, (w, w), 0)
    col_w = lax.broadcasted_iota(jnp.int32, (w, w), 1)
    mask_bd = (row_w // c) == (col_w // HEAD_DIM)
    eye_w = row_w == col_w
    tri_incl = [jnp.where(col_c <= row_c, 1.0, 0.0).astype(BF16),
                jnp.where(col_c >= row_c, 1.0, 0.0).astype(BF16)]
    strict = [s_i < t_i, s_i > t_i]
    incl = [s_i <= t_i, s_i >= t_i]

    def bf(x):
        return x.astype(BF16)

    def bd(x):
        return _expand_bd(bf(x), mask_bd)

    chunks = []
    for ci in range(ncs):
        chunks.append((0, ci * c, shf_ref, df_ref))
        chunks.append((1, (ncs - 1 - ci) * c, shr_ref, dr_ref))

    st = []
    for d, start, sh_ref, dir_ref in chunks:
        rows = pl.ds(start, c)
        st.append(dict(d=d, rows=rows,
                       r=sh_ref[0, rows, 0:w], v=sh_ref[0, rows, w:2 * w], kk=sh_ref[0, rows, 2 * w:3 * w],
                       lw=dir_ref[0, rows, 0:w], kh=dir_ref[0, rows, w:2 * w], nb=dir_ref[0, rows, 2 * w:3 * w]))

    for q in st:
        l_hi, l_mid, l_lo = _split3(q["lw"])
        tri = tri_incl[q["d"]]
        q["cum"] = _dot(tri, l_hi) + _dot(tri, l_mid) + _dot(tri, l_lo)
    for q in st:
        cum, lw = q["cum"], q["lw"]
        tot = jnp.sum(lw, axis=0, keepdims=True)
        e_inv = jnp.exp(-cum)
        e_end = jnp.exp(tot - cum)
        q["tot"] = tot
        q["a_bar"] = q["kk"] * jnp.exp(cum - lw)
        q["r_bar"] = q["r"] * jnp.exp(cum)
        q["b_hat"] = bf(q["nb"] * e_end)
        q["k_hat"] = bf(q["kh"] * e_end)
        q["v_bf"] = bf(q["v"])
        q["v_bd"] = _expand_bd(q["v_bf"], mask_bd)
        q["lhs"] = jnp.concatenate([bf(q["a_bar"]), bf(q["r_bar"])], axis=0)
        q["rhs"] = jnp.concatenate([bd(q["nb"] * e_inv), bd(q["kh"] * e_inv)], axis=0)
    for q in st:
        sc = _dot_nt(q["lhs"], q["rhs"])
        sm, im = strict[q["d"]], incl[q["d"]]
        q["x"] = jnp.where(sm, sc[0:c, 0:w], 0.0)
        q["a_ak"] = bf(jnp.where(sm, sc[0:c, w:2 * w], 0.0))
        q["m_rb"] = bf(jnp.where(im, sc[c:2 * c, 0:w], 0.0))
        q["m_rk"] = bf(jnp.where(im, sc[c:2 * c, w:2 * w], 0.0))
        q["tinv"] = eye_c + q["x"]

    for q in st:
        q["x"] = _dot(bf(q["x"]), bd(q["x"]))
    n_lv = int(math.log2(c)) - 1
    for lv in range(n_lv):
        last = lv == n_lv - 1
        for q in st:
            xb = bd(q["x"])
            if last:
                q["tinv"] = q["tinv"] + _dot(bf(q["tinv"]), xb)
            else:
                res = _dot(jnp.concatenate([bf(q["x"]), bf(q["tinv"])], axis=0), xb)
                q["tinv"] = q["tinv"] + res[c:2 * c]
                q["x"] = res[0:c]
    for q in st:
        res = _dot(jnp.concatenate([q["a_ak"], q["m_rk"]], axis=0), q["v_bd"])
        q["akv"] = res[0:c]
        q["mrkv"] = res[c:2 * c]
    for q in st:
        tinv_bf = bf(q["tinv"])
        q["a_pr"] = bf(_dot(tinv_bf, bd(q["a_bar"])))
        q["v_pr"] = bf(_dot(tinv_bf, bd(q["akv"])))
    for q in st:
        q["r_pr"] = bf(q["r_bar"] + _dot(q["m_rb"], _expand_bd(q["a_pr"], mask_bd)))
        q["y_pr"] = _dot(q["m_rb"], _expand_bd(q["v_pr"], mask_bd)) + q["mrkv"]
    for q in st:
        q["p_bd"] = bf(jnp.where(eye_w, jnp.exp(q["tot"]), 0.0)
                       + jnp.where(mask_bd, _dot_tn(q["b_hat"], q["a_pr"]), 0.0))
        q["q_bd"] = jnp.where(mask_bd,
                              _dot_tn(jnp.concatenate([q["b_hat"], q["k_hat"]], axis=0),
                                      jnp.concatenate([q["v_pr"], q["v_bf"]], axis=0)), 0.0)

    h = [hf_ref[...], hr_ref[...]]
    y_refs = [yf_ref, yr_ref]
    for q in st:
        d = q["d"]
        res = _dot(jnp.concatenate([q["r_pr"], q["p_bd"]], axis=0), bf(h[d]))
        y_refs[d][0, q["rows"], :] = res[0:c] + q["y_pr"]
        h[d] = res[c:c + w] + q["q_bd"]
    hf_ref[...] = h[0]
    hr_ref[...] = h[1]


def _wkv(shared, dirp, tt):
    b, s, _ = shared.shape
    nt = s // tt
    w = RWKV_WIDTH
    fwd = lambda i, j: (i, j, 0)
    bwd = lambda i, j: (i, nt - 1 - j, 0)
    return pl.pallas_call(
        _wkv_kernel,
        grid=(b, nt),
        in_specs=[pl.BlockSpec((1, tt, 3 * w), fwd), pl.BlockSpec((1, tt, 3 * w), bwd),
                  pl.BlockSpec((None, 1, tt, 3 * w), lambda i, j: (0, i, j, 0)),
                  pl.BlockSpec((None, 1, tt, 3 * w), lambda i, j: (1, i, nt - 1 - j, 0))],
        out_specs=[pl.BlockSpec((1, tt, w), fwd), pl.BlockSpec((1, tt, w), bwd)],
        out_shape=[jax.ShapeDtypeStruct((b, s, w), F32), jax.ShapeDtypeStruct((b, s, w), F32)],
        scratch_shapes=[pltpu.VMEM((w, w), F32), pltpu.VMEM((w, w), F32)],
        compiler_params=_cparams(("parallel", "arbitrary")),
        name="wkv",
    )(shared, shared, dirp, dirp)


def _rope(x, cos, sin_signed):
    n = x.shape[-1]
    lane = lax.broadcasted_iota(jnp.int32, x.shape, 1)
    first = (lane // 16) % 2 == 0
    partner = jnp.where(first, pltpu.roll(x, n - 16, 1), pltpu.roll(x, 16, 1))
    return x * cos + partner * sin_signed


def _attn_prep_kernel(at_ref, cos_ref, sin_ref, qg_ref, kg_ref, onesq_ref, q_ref, k_ref, v_ref):
    q = at_ref[:, 0:ATTN_WIDTH]
    k = at_ref[:, ATTN_WIDTH:ATTN_WIDTH + ATTN_KV_WIDTH]
    v = at_ref[:, ATTN_WIDTH + ATTN_KV_WIDTH:ATTN_IN]
    ones_q = onesq_ref[...]
    ones_k = onesq_ref[0:ATTN_KV_WIDTH, 0:ATTN_KV_WIDTH]
    inv = 1.0 / HEAD_DIM
    qn = q * lax.rsqrt(_headsum(q * q, ones_q) * inv + QK_EPS) * qg_ref[...]
    kn = k * lax.rsqrt(_headsum(k * k, ones_k) * inv + QK_EPS) * kg_ref[...]
    cos = cos_ref[...]
    sin = sin_ref[...]
    q_ref[...] = (_rope(qn, cos, sin) * (HEAD_DIM ** -0.5)).astype(BF16)
    k_ref[...] = _rope(kn, cos[:, 0:ATTN_KV_WIDTH], sin[:, 0:ATTN_KV_WIDTH]).astype(BF16)
    v_ref[...] = v.astype(BF16)


def _attn_prep(at2d, cos8, sin8, qg, kg, ones_q, seq, tm):
    n = at2d.shape[0]
    nts = seq // tm
    return pl.pallas_call(
        _attn_prep_kernel,
        grid=(n // tm,),
        in_specs=[pl.BlockSpec((tm, ATTN_IN), lambda i: (i, 0)),
                  pl.BlockSpec((tm, ATTN_WIDTH), lambda i: (i % nts, 0)),
                  pl.BlockSpec((tm, ATTN_WIDTH), lambda i: (i % nts, 0)),
                  pl.BlockSpec((1, ATTN_WIDTH), lambda i: (0, 0)),
                  pl.BlockSpec((1, ATTN_KV_WIDTH), lambda i: (0, 0)),
                  pl.BlockSpec((ATTN_WIDTH, ATTN_WIDTH), lambda i: (0, 0))],
        out_specs=[pl.BlockSpec((tm, ATTN_WIDTH), lambda i: (i, 0)),
                   pl.BlockSpec((tm, ATTN_KV_WIDTH), lambda i: (i, 0)),
                   pl.BlockSpec((tm, ATTN_KV_WIDTH), lambda i: (i, 0))],
        out_shape=[jax.ShapeDtypeStruct((n, ATTN_WIDTH), BF16),
                   jax.ShapeDtypeStruct((n, ATTN_KV_WIDTH), BF16),
                   jax.ShapeDtypeStruct((n, ATTN_KV_WIDTH), BF16)],
        compiler_params=_cparams(("parallel",)),
        name="attn_prep",
    )(at2d, cos8, sin8, qg, kg, ones_q)


def _attn_kernel(q_ref, k_ref, v_ref, o_ref):
    tq = q_ref.shape[0]
    k = k_ref[...]
    v = v_ref[...]
    lane = lax.broadcasted_iota(jnp.int32, (tq, ATTN_KV_WIDTH), 1)
    low = lane < HEAD_DIM
    zero = jnp.zeros((), BF16)
    for j in range(ATTN_HEADS // 2):
        qb = q_ref[:, j * ATTN_KV_WIDTH:(j + 1) * ATTN_KV_WIDTH]
        outs = []
        for half in range(2):
            qm = jnp.where(low if half == 0 else ~low, qb, zero)
            s = _dot_nt(qm, k)
            m = jnp.max(s, axis=-1, keepdims=True)
            p = jnp.exp(s - m)
            l = jnp.sum(p, axis=-1, keepdims=True)
            outs.append(_dot(p.astype(BF16), v) / l)
        o_ref[:, j * ATTN_KV_WIDTH:(j + 1) * ATTN_KV_WIDTH] = jnp.where(low, outs[0], outs[1]).astype(BF16)


def _attn(q, k, v, seq, tq):
    n = q.shape[0]
    nq = seq // tq
    return pl.pallas_call(
        _attn_kernel,
        grid=(n // seq, nq),
        in_specs=[pl.BlockSpec((tq, ATTN_WIDTH), lambda b, i: (b * nq + i, 0)),
                  pl.BlockSpec((seq, ATTN_KV_WIDTH), lambda b, i: (b, 0)),
                  pl.BlockSpec((seq, ATTN_KV_WIDTH), lambda b, i: (b, 0))],
        out_specs=pl.BlockSpec((tq, ATTN_WIDTH), lambda b, i: (b * nq + i, 0)),
        out_shape=jax.ShapeDtypeStruct((n, ATTN_WIDTH), BF16),
        compiler_params=_cparams(("parallel", "parallel")),
        name="attn",
    )(q, k, v)


def _layer_norm(z, g, b):
    mu = jnp.mean(z, axis=-1, keepdims=True)
    zc = z - mu
    var = jnp.mean(zc * zc, axis=-1, keepdims=True)
    return zc * lax.rsqrt(var + LN_EPS) * g + b


def _outproj_kernel(x_ref, pool_ref, wkvf_ref, wkvr_ref, post_ref, attn_ref, wo_ref, gng_ref, gnb_ref,
                    ones_ref, lg_ref, lb_ref, wrh_ref, wrl_ref, br_ref,
                    x1_ref, x1b_ref, route_ref):
    w = RWKV_WIDTH
    ones = ones_ref[...]
    wkv = wkvf_ref[...] + wkvr_ref[...]
    inv = 1.0 / HEAD_DIM
    mu = _headsum(wkv, ones) * inv
    cen = wkv - mu
    var = _headsum(cen * cen, ones) * inv
    yr = cen * lax.rsqrt(var + GN_EPS) * gng_ref[...] + gnb_ref[...] + post_ref[:, 0:w]
    yr = yr * post_ref[:, w:2 * w]
    y = (_dot(pool_ref[...].astype(BF16), wo_ref[0:POOL_WIDTH, :])
         + _dot(yr.astype(BF16), wo_ref[POOL_WIDTH:POOL_WIDTH + w, :])
         + _dot(attn_ref[...], wo_ref[POOL_WIDTH + w:D_MODEL, :]))
    x1 = _layer_norm(DEEPNORM_ALPHA * x_ref[...] + y, lg_ref[...], lb_ref[...])
    x1_ref[...] = x1
    x1b_ref[...] = x1.astype(BF16)

    xh, xl = _split2(x1)
    logits = (_dot(xh, wrh_ref[...]) + _dot(xl, wrh_ref[...]) + _dot(xh, wrl_ref[...])) + br_ref[...]
    lane = lax.broadcasted_iota(jnp.int32, logits.shape, 1)
    neg = -jnp.inf
    big = jnp.int32(1 << 20)
    gl = jnp.where(lane < N_GROUPS, logits, neg)
    gmax = jnp.max(gl, axis=-1, keepdims=True)
    grp = jnp.min(jnp.where(gl == gmax, lane, big), axis=-1, keepdims=True)
    gw = 1.0 / jnp.sum(jnp.exp(gl - gmax), axis=-1, keepdims=True)
    in_grp = (lane >= N_GROUPS) & (lane < N_GROUPS + N_EXPERTS) & \
             ((lane - N_GROUPS) // EXPERTS_PER_GROUP == grp)
    el = jnp.where(in_grp, logits, neg)
    v1 = jnp.max(el, axis=-1, keepdims=True)
    i1 = jnp.min(jnp.where(el == v1, lane, big), axis=-1, keepdims=True)
    el2 = jnp.where(lane == i1, neg, el)
    v2 = jnp.max(el2, axis=-1, keepdims=True)
    i2 = jnp.min(jnp.where(el2 == v2, lane, big), axis=-1, keepdims=True)
    e21 = jnp.exp(v2 - v1)
    g1 = gw / (1.0 + e21)
    g2 = gw * e21 / (1.0 + e21)
    route = jnp.where(lane == 0, (i1 - N_GROUPS).astype(F32),
                      jnp.where(lane == 1, (i2 - N_GROUPS).astype(F32),
                                jnp.where(lane == 2, g1, jnp.where(lane == 3, g2, 0.0))))
    route_ref[...] = route


def _outproj(x2d, pool2d, wkvf, wkvr, post2d, attn2d, wo, gng, gnb, ones, lg, lb, wrh, wrl, br, tm):
    n = x2d.shape[0]
    w = RWKV_WIDTH
    row = lambda width: pl.BlockSpec((tm, width), lambda i: (i, 0))
    vec = lambda width: pl.BlockSpec((1, width), lambda i: (0, 0))
    return pl.pallas_call(
        _outproj_kernel,
        grid=(n // tm,),
        in_specs=[row(D_MODEL), row(POOL_WIDTH), row(w), row(w),
                  row(2 * w), row(ATTN_WIDTH),
                  pl.BlockSpec((D_MODEL, D_MODEL), lambda i: (0, 0)),
                  vec(w), vec(w),
                  pl.BlockSpec((w, w), lambda i: (0, 0)),
                  vec(D_MODEL), vec(D_MODEL),
                  pl.BlockSpec((D_MODEL, ROUTE_LANES), lambda i: (0, 0)),
                  pl.BlockSpec((D_MODEL, ROUTE_LANES), lambda i: (0, 0)),
                  vec(ROUTE_LANES)],
        out_specs=[row(D_MODEL), row(D_MODEL), row(ROUTE_LANES)],
        out_shape=[jax.ShapeDtypeStruct((n, D_MODEL), F32),
                   jax.ShapeDtypeStruct((n, D_MODEL), BF16),
                   jax.ShapeDtypeStruct((n, ROUTE_LANES), F32)],
        compiler_params=_cparams(("parallel",)),
        name="outproj",
    )(x2d, pool2d, wkvf, wkvr, post2d, attn2d, wo, gng, gnb, ones, lg, lb, wrh, wrl, br)


def _expert_kernel(bexp_ref, nused_ref, xs_ref, wg_ref, wu_ref, wd_ref, ys_ref):
    @pl.when(pl.program_id(0) < nused_ref[0])
    def _():
        xb = xs_ref[...]
        h1 = _dot(xb, wg_ref[0].astype(BF16))
        h2 = _dot(xb, wu_ref[0].astype(BF16))
        h = (h1 * jax.nn.sigmoid(h1) * h2).astype(BF16)
        ys_ref[...] = _dot(h, wd_ref[0].astype(BF16)).astype(BF16)

    @pl.when(pl.program_id(0) >= nused_ref[0])
    def _():
        ys_ref[...] = jnp.zeros_like(ys_ref)


def _experts(block_exp, n_used, xs, wg, wu, wd):
    p = xs.shape[0]
    nb = p // MOE_BLOCK
    grid_spec = pltpu.PrefetchScalarGridSpec(
        num_scalar_prefetch=2,
        grid=(nb,),
        in_specs=[pl.BlockSpec((MOE_BLOCK, D_MODEL), lambda i, be, nu: (i, 0)),
                  pl.BlockSpec((1, D_MODEL, EXPERT_HIDDEN), lambda i, be, nu: (be[i], 0, 0)),
                  pl.BlockSpec((1, D_MODEL, EXPERT_HIDDEN), lambda i, be, nu: (be[i], 0, 0)),
                  pl.BlockSpec((1, EXPERT_HIDDEN, D_MODEL), lambda i, be, nu: (be[i], 0, 0))],
        out_specs=pl.BlockSpec((MOE_BLOCK, D_MODEL), lambda i, be, nu: (i, 0)),
    )
    return pl.pallas_call(
        _expert_kernel,
        grid_spec=grid_spec,
        out_shape=jax.ShapeDtypeStruct((p, D_MODEL), BF16),
        compiler_params=_cparams(("arbitrary",)),
        name="experts",
    )(block_exp, n_used, xs, wg, wu, wd)


def _combine_kernel(x1_ref, ya_ref, yb_ref, route_ref, lg_ref, lb_ref, o_ref):
    route = route_ref[...]
    g1 = route[:, 2:3]
    g2 = route[:, 3:4]
    m = g1 * ya_ref[...].astype(F32) + g2 * yb_ref[...].astype(F32)
    o_ref[...] = _layer_norm(DEEPNORM_ALPHA * x1_ref[...] + m, lg_ref[...], lb_ref[...])


def _combine(x1, ya, yb, route, lg, lb, tm):
    n = x1.shape[0]
    row = lambda width: pl.BlockSpec((tm, width), lambda i: (i, 0))
    vec = lambda width: pl.BlockSpec((1, width), lambda i: (0, 0))
    return pl.pallas_call(
        _combine_kernel,
        grid=(n // tm,),
        in_specs=[row(D_MODEL), row(D_MODEL), row(D_MODEL), row(ROUTE_LANES),
                  vec(D_MODEL), vec(D_MODEL)],
        out_specs=row(D_MODEL),
        out_shape=jax.ShapeDtypeStruct((n, D_MODEL), F32),
        compiler_params=_cparams(("parallel",)),
        name="combine",
    )(x1, ya, yb, route, lg, lb)


def _q_perm():
    order = [h for j in range(ATTN_HEADS // 2) for h in (j, j + ATTN_HEADS // 2)]
    return jnp.concatenate([jnp.arange(h * HEAD_DIM, (h + 1) * HEAD_DIM) for h in order])


def _block_diag(blocks):
    n = len(blocks)
    r, c = blocks[0].shape
    out = jnp.zeros((n * r, n * c), blocks[0].dtype)
    for i, blk in enumerate(blocks):
        out = out.at[i * r:(i + 1) * r, i * c:(i + 1) * c].set(blk)
    return out


def _rope_tables(seq):
    rows = seq // GRID_W
    row_id = jnp.repeat(jnp.arange(rows), GRID_W).astype(F32)
    col_id = jnp.tile(jnp.arange(GRID_W), rows).astype(F32)
    half = HEAD_DIM // 2
    inv_freq = ROPE_THETA ** (-jnp.arange(0, half, 2, dtype=F32) / half)
    ang_r = row_id[:, None] * inv_freq
    ang_c = col_id[:, None] * inv_freq
    ang = jnp.concatenate([ang_r, ang_r, ang_c, ang_c], -1)
    sign = jnp.where((jnp.arange(HEAD_DIM) // 16) % 2 == 0, -1.0, 1.0).astype(F32)
    cos8 = jnp.tile(jnp.cos(ang), (1, ATTN_HEADS))
    sin8 = jnp.tile(jnp.sin(ang) * sign, (1, ATTN_HEADS))
    return cos8, sin8


def _dispatch(route, n):
    m = n * 2
    e_flat = route[:, 0:2].astype(jnp.int32).reshape(m)
    tok = jnp.repeat(jnp.arange(n, dtype=jnp.int32), 2)
    order = jnp.argsort(e_flat)
    e_sorted = e_flat[order]
    counts = jnp.bincount(e_flat, length=N_EXPERTS)
    padded = (counts + MOE_BLOCK - 1) // MOE_BLOCK * MOE_BLOCK
    start = jnp.cumsum(counts) - counts
    ends_p = jnp.cumsum(padded)
    pstart = ends_p - padded
    dest = (pstart[e_sorted] + jnp.arange(m) - start[e_sorted]).astype(jnp.int32)
    n_blocks = -(-(m + N_EXPERTS * (MOE_BLOCK - 1)) // MOE_BLOCK)
    p = n_blocks * MOE_BLOCK
    row_tok = jnp.full((p,), n, jnp.int32).at[dest].set(tok[order])
    pos = jnp.zeros((m,), jnp.int32).at[order].set(dest)
    block_start = jnp.arange(n_blocks, dtype=ends_p.dtype) * MOE_BLOCK
    block_exp = jnp.minimum(jnp.sum(ends_p[None, :] <= block_start[:, None], axis=1),
                            N_EXPERTS - 1).astype(jnp.int32)
    n_used = (ends_p[-1] // MOE_BLOCK).astype(jnp.int32).reshape(1)
    return row_tok, pos.reshape(n, 2), block_exp, n_used


def kernel(x, w_in, mu_prev, mu_next, pool_w, pool_scale, rw_w0, rw_w_up, rw_a0, rw_a_up, rw_g_up, rw_k_k, rw_k_a, rw_r_k, rw_gn_g, rw_gn_b, q_norm, k_norm, w_o, ln1_g, ln1_b, router_group, router_group_b, router_expert, router_expert_b, exp_gate, exp_up, exp_down, ln2_g, ln2_b):
    b, s, d = x.shape
    n = b * s
    w = RWKV_WIDTH
    tm = min(512, s)
    qperm = _q_perm()
    cos8, sin8 = _rope_tables(s)
    ones_q = _block_diag([jnp.ones((HEAD_DIM, HEAD_DIM), BF16)] * ATTN_HEADS)
    ones_r = ones_q[0:w, 0:w]
    a_end = POOL_WIDTH
    b_end = POOL_WIDTH + RWKV_IN

    xc = x.reshape(n, d)
    for l in range(DEPTH):
        wq = w_in[l][:, b_end:b_end + ATTN_WIDTH][:, qperm]
        w_proj = jnp.concatenate(
            [w_in[l][:, :a_end], w_in[l][:, a_end:b_end],
             jnp.zeros((d, RWKV_PAD - RWKV_IN), F32), wq, w_in[l][:, b_end + ATTN_WIDTH:]],
            axis=1).astype(BF16)
        pool_bd = _block_diag([pool_w[l, g] for g in range(len(POOL_WINDOWS))]).astype(BF16)
        pad_vec = jnp.zeros((RWKV_PAD - RWKV_IN,), F32)
        mup = jnp.concatenate([mu_prev[l], pad_vec]).reshape(1, RWKV_PAD)
        mun = jnp.concatenate([mu_next[l], pad_vec]).reshape(1, RWKV_PAD)
        w_lora = jnp.zeros((w, 5 * w), F32)
        for di in range(2):
            w_lora = w_lora.at[di * DECAY_LORA:(di + 1) * DECAY_LORA, di * w:(di + 1) * w].set(rw_w_up[l, di])
            o = 2 * DECAY_LORA
            w_lora = w_lora.at[o + di * AAA_LORA:o + (di + 1) * AAA_LORA, (2 + di) * w:(3 + di) * w].set(rw_a_up[l, di])
        o = 2 * DECAY_LORA + 2 * AAA_LORA
        w_lora = w_lora.at[o:o + GATE_LORA, 4 * w:5 * w].set(rw_g_up[l]).astype(BF16)
        lora_bias = jnp.concatenate([rw_w0[l, 0], rw_w0[l, 1], rw_a0[l, 0], rw_a0[l, 1],
                                     jnp.zeros((w,), F32)]).reshape(1, 5 * w)
        qg = jnp.tile(q_norm[l], ATTN_HEADS).reshape(1, ATTN_WIDTH)
        kg = jnp.tile(k_norm[l], 2).reshape(1, ATTN_KV_WIDTH)
        wo_attn = w_o[l][POOL_WIDTH + w:][qperm]
        wo = jnp.concatenate([w_o[l][:POOL_WIDTH + w], wo_attn], axis=0).astype(BF16)
        wr = jnp.concatenate([router_group[l], router_expert[l],
                              jnp.zeros((d, ROUTE_LANES - N_GROUPS - N_EXPERTS), F32)], axis=1)
        wrh = wr.astype(BF16)
        wrl = (wr - wrh.astype(F32)).astype(BF16)
        br = jnp.concatenate([router_group_b[l], router_expert_b[l],
                              jnp.zeros((ROUTE_LANES - N_GROUPS - N_EXPERTS,), F32)]).reshape(1, ROUTE_LANES)

        pool_in, rw_in, at_in = _proj(xc, w_proj, tm)
        y_pool = _pool(pool_in.reshape(b, s, POOL_WIDTH), pool_bd, pool_scale[l].reshape(1, POOL_WIDTH))
        shared, dirp, post = _rwkv_prep(rw_in.reshape(b, s, RWKV_PAD), mup, mun, w_lora, lora_bias,
                                        rw_k_k[l].reshape(1, w), rw_k_a[l].reshape(1, w),
                                        rw_r_k[l].reshape(1, w), ones_r, tm)
        wkv_f, wkv_r = _wkv(shared, dirp, min(256, s))
        qh, kh, vh = _attn_prep(at_in, cos8, sin8, qg, kg, ones_q, s, tm)
        y_attn = _attn(qh, kh, vh, s, min(256, s))
        x1, x1b, route = _outproj(xc, y_pool.reshape(n, POOL_WIDTH), wkv_f.reshape(n, w), wkv_r.reshape(n, w),
                                  post.reshape(n, 2 * w), y_attn, wo,
                                  rw_gn_g[l].reshape(1, w), rw_gn_b[l].reshape(1, w), ones_r,
                                  ln1_g[l].reshape(1, d), ln1_b[l].reshape(1, d), wrh, wrl, br, tm)

        row_tok, pos, block_exp, n_used = _dispatch(route, n)
        xs = jnp.concatenate([x1b, jnp.zeros((1, d), BF16)], axis=0)[row_tok]
        ys = _experts(block_exp, n_used, xs, exp_gate[l], exp_up[l], exp_down[l])
        xc = _combine(x1, ys[pos[:, 0]], ys[pos[:, 1]], route,
                      ln2_g[l].reshape(1, d), ln2_b[l].reshape(1, d), tm)
    return xc.reshape(b, s, d)
```

```python
import functools
import math

import jax
import jax.numpy as jnp
from jax import lax
from jax.experimental import pallas as pl
from jax.experimental.pallas import tpu as pltpu
from jax.experimental.pallas import tpu_sc as plsc

F32 = jnp.float32
BF16 = jnp.bfloat16

D_MODEL = 1024
DEPTH = 4
GRID_W = 64
HEAD_DIM = 64
POOL_WIDTH = 256
POOL_WINDOWS = (2, 4, 8, 16)
POOL_GROUP = 64
RWKV_WIDTH = 256
RWKV_HEADS = 4
DECAY_LORA = 32
AAA_LORA = 32
GATE_LORA = 64
GN_EPS = 64e-5
RWKV_IN = 960
RWKV_PAD = 1024
ATTN_WIDTH = 512
ATTN_HEADS = 8
ATTN_KV_WIDTH = 128
ATTN_IN = ATTN_WIDTH + 2 * ATTN_KV_WIDTH
ROPE_THETA = 10000.0
QK_EPS = 1e-6
N_GROUPS = 4
EXPERTS_PER_GROUP = 8
N_EXPERTS = 32
EXPERT_HIDDEN = 512
MOE_BLOCK = 256
DEEPNORM_ALPHA = float((2 * DEPTH) ** 0.25)
LN_EPS = 1e-5
PROJ_WIDTH = POOL_WIDTH + RWKV_PAD + ATTN_IN
ROUTE_LANES = 128
WKV_CHUNK = 64
VMEM_LIMIT = 48 * 1024 * 1024


def _cparams(sem):
    return pltpu.CompilerParams(dimension_semantics=sem, vmem_limit_bytes=VMEM_LIMIT)


def _dot(a, b):
    return jnp.dot(a, b, preferred_element_type=F32)


def _dot_nt(a, b):
    return lax.dot_general(a, b, (((1,), (1,)), ((), ())), preferred_element_type=F32)


def _dot_tn(a, b):
    return lax.dot_general(a, b, (((0,), (0,)), ((), ())), preferred_element_type=F32)


def _split2(x):
    hi = x.astype(BF16)
    lo = (x - hi.astype(F32)).astype(BF16)
    return hi, lo


def _split3(x):
    hi = x.astype(BF16)
    r1 = x - hi.astype(F32)
    mid = r1.astype(BF16)
    lo = (r1 - mid.astype(F32)).astype(BF16)
    return hi, mid, lo


HALF_D = D_MODEL // 2
U32 = jnp.uint32


def _pack_rows(x):
    hi = lax.bitcast_convert_type(x[:, :HALF_D].astype(BF16).astype(F32), U32)
    lo = lax.bitcast_convert_type(x[:, HALF_D:].astype(BF16).astype(F32), U32)
    return hi | (lo >> 16)


def _unpack_rows(u):
    a = lax.bitcast_convert_type(u & jnp.uint32(0xFFFF0000), F32).astype(BF16)
    b = lax.bitcast_convert_type(u << 16, F32).astype(BF16)
    return a, b


def _headsum(x, ones_bf16):
    hi, lo = _split2(x)
    return _dot(hi, ones_bf16) + _dot(lo, ones_bf16)


def _proj_kernel(x_ref, w_ref, pool_ref, rw_ref, at_ref):
    xb = x_ref[...].astype(BF16)
    pool_ref[...] = _dot(xb, w_ref[:, 0:POOL_WIDTH])
    rw_ref[...] = _dot(xb, w_ref[:, POOL_WIDTH:POOL_WIDTH + RWKV_PAD])
    at_ref[...] = _dot(xb, w_ref[:, POOL_WIDTH + RWKV_PAD:PROJ_WIDTH])


def _proj(x2d, w, tm):
    n = x2d.shape[0]
    return pl.pallas_call(
        _proj_kernel,
        grid=(n // tm,),
        in_specs=[pl.BlockSpec((tm, D_MODEL), lambda i: (i, 0)),
                  pl.BlockSpec((D_MODEL, PROJ_WIDTH), lambda i: (0, 0))],
        out_specs=[pl.BlockSpec((tm, POOL_WIDTH), lambda i: (i, 0)),
                   pl.BlockSpec((tm, RWKV_PAD), lambda i: (i, 0)),
                   pl.BlockSpec((tm, ATTN_IN), lambda i: (i, 0))],
        out_shape=[jax.ShapeDtypeStruct((n, POOL_WIDTH), F32),
                   jax.ShapeDtypeStruct((n, RWKV_PAD), F32),
                   jax.ShapeDtypeStruct((n, ATTN_IN), F32)],
        compiler_params=_cparams(("parallel",)),
        name="proj",
    )(x2d, w)


_POOL_PAD = 16


def _pool_kernel(u_ref, w_ref, scale_ref, o_ref, pad_ref):
    s = u_ref.shape[1]
    u = u_ref[0]
    zeros = jnp.zeros((_POOL_PAD, POOL_WIDTH), F32)
    pad_ref[0:_POOL_PAD, :] = zeros
    pad_ref[_POOL_PAD + s:2 * _POOL_PAD + s, :] = zeros
    pad_ref[_POOL_PAD:_POOL_PAD + s, :] = u

    def shifted(k):
        return pad_ref[_POOL_PAD + k:_POOL_PAD + k + s, :]

    s2 = u + shifted(-1)
    s4 = s2 + shifted(-2) + shifted(1)
    s8 = s4 + shifted(-4) + shifted(-3) + shifted(2) + shifted(3)
    s16 = s8
    for k in (-8, -7, -6, -5, 4, 5, 6, 7):
        s16 = s16 + shifted(k)

    t = lax.broadcasted_iota(jnp.int32, (s, POOL_WIDTH), 0)
    grp = lax.broadcasted_iota(jnp.int32, (s, POOL_WIDTH), 1) // POOL_GROUP
    half = jnp.where(grp == 0, 1, jnp.where(grp == 1, 2, jnp.where(grp == 2, 4, 8)))
    cnt = (jnp.minimum(t + half, s) - jnp.maximum(t - half, 0)).astype(F32)
    tot = jnp.where(grp == 0, s2, jnp.where(grp == 1, s4, jnp.where(grp == 2, s8, s16)))
    d = tot / cnt - u
    o_ref[0] = _dot(d.astype(BF16), w_ref[...]) * scale_ref[...]


def _pool(u3d, w_bd, scale):
    b, s, _ = u3d.shape
    return pl.pallas_call(
        _pool_kernel,
        grid=(b,),
        in_specs=[pl.BlockSpec((1, s, POOL_WIDTH), lambda i: (i, 0, 0)),
                  pl.BlockSpec((POOL_WIDTH, POOL_WIDTH), lambda i: (0, 0)),
                  pl.BlockSpec((1, POOL_WIDTH), lambda i: (0, 0))],
        out_specs=pl.BlockSpec((1, s, POOL_WIDTH), lambda i: (i, 0, 0)),
        out_shape=jax.ShapeDtypeStruct((b, s, POOL_WIDTH), F32),
        scratch_shapes=[pltpu.VMEM((s + 2 * _POOL_PAD, POOL_WIDTH), F32)],
        compiler_params=_cparams(("parallel",)),
        name="pool",
    )(u3d, w_bd, scale)


_SHIFT_HALO = 8


def _rwkv_prep_kernel(cur_ref, prev_ref, next_ref, mup_ref, mun_ref, wl_ref, bias_ref,
                      kk_ref, ka_ref, rk_ref, ones_ref,
                      sh_ref, dir_ref, post_ref, pad_ref):
    ts = cur_ref.shape[1]
    ti = pl.program_id(1)
    nt = pl.num_programs(1)
    cur = cur_ref[0]
    pad_ref[0:_SHIFT_HALO, :] = jnp.where(ti > 0, prev_ref[0], 0.0)
    pad_ref[_SHIFT_HALO:_SHIFT_HALO + ts, :] = cur
    pad_ref[_SHIFT_HALO + ts:2 * _SHIFT_HALO + ts, :] = jnp.where(ti < nt - 1, next_ref[0], 0.0)
    prev = pad_ref[_SHIFT_HALO - 1:_SHIFT_HALO - 1 + ts, :]
    nxt = pad_ref[_SHIFT_HALO + 1:_SHIFT_HALO + 1 + ts, :]
    f = cur + mup_ref[...] * (prev - cur) + mun_ref[...] * (nxt - cur)

    w = RWKV_WIDTH
    r = f[:, 0:w]
    k = f[:, w:2 * w]
    v = f[:, 2 * w:3 * w]
    lora = f[:, 3 * w:4 * w]
    lane = lax.broadcasted_iota(jnp.int32, lora.shape, 1)
    z = jnp.where(lane < 2 * DECAY_LORA, jnp.tanh(lora),
                  jnp.where(lane < 2 * DECAY_LORA + 2 * AAA_LORA, lora,
                            jnp.where(lane < 2 * DECAY_LORA + 2 * AAA_LORA + GATE_LORA,
                                      jax.nn.sigmoid(lora), 0.0)))
    up = _dot(z.astype(BF16), wl_ref[...]) + bias_ref[...]
    ones = ones_ref[...]

    kk0 = k * kk_ref[...]
    nrm = jnp.sqrt(_headsum(kk0 * kk0, ones))
    kk = kk0 / jnp.maximum(nrm, 1e-12)
    sh_ref[0, :, 0:w] = r
    sh_ref[0, :, w:2 * w] = v
    sh_ref[0, :, 2 * w:3 * w] = kk

    ksum = jnp.zeros_like(k)
    for di in range(2):
        logw = -math.exp(-0.5) * jax.nn.sigmoid(up[:, di * w:(di + 1) * w])
        a = jax.nn.sigmoid(up[:, (2 + di) * w:(3 + di) * w])
        kh = k * (1.0 + (a - 1.0) * ka_ref[...])
        ksum = ksum + kh
        dir_ref[di, 0, :, 0:w] = logw
        dir_ref[di, 0, :, w:2 * w] = kh
        dir_ref[di, 0, :, 2 * w:3 * w] = -(a * kk)
    bonus = _headsum(r * ksum * rk_ref[...], ones) * v
    post_ref[0, :, 0:w] = bonus
    post_ref[0, :, w:2 * w] = up[:, 4 * w:5 * w]


def _rwkv_prep(rw3d, mup, mun, wl, bias, k_k, k_a, r_k, ones, ts):
    b, s, _ = rw3d.shape
    nt = s // ts
    hb = ts // _SHIFT_HALO
    nhb = s // _SHIFT_HALO
    w = RWKV_WIDTH
    vec = lambda width: pl.BlockSpec((1, width), lambda i, j: (0, 0))
    return pl.pallas_call(
        _rwkv_prep_kernel,
        grid=(b, nt),
        in_specs=[pl.BlockSpec((1, ts, RWKV_PAD), lambda i, j: (i, j, 0)),
                  pl.BlockSpec((1, _SHIFT_HALO, RWKV_PAD),
                               lambda i, j: (i, jnp.maximum(j * hb - 1, 0), 0)),
                  pl.BlockSpec((1, _SHIFT_HALO, RWKV_PAD),
                               lambda i, j: (i, jnp.minimum((j + 1) * hb, nhb - 1), 0)),
                  vec(RWKV_PAD), vec(RWKV_PAD),
                  pl.BlockSpec((w, 5 * w), lambda i, j: (0, 0)),
                  vec(5 * w), vec(w), vec(w), vec(w),
                  pl.BlockSpec((w, w), lambda i, j: (0, 0))],
        out_specs=[pl.BlockSpec((1, ts, 3 * w), lambda i, j: (i, j, 0)),
                   pl.BlockSpec((2, 1, ts, 3 * w), lambda i, j: (0, i, j, 0)),
                   pl.BlockSpec((1, ts, 2 * w), lambda i, j: (i, j, 0))],
        out_shape=[jax.ShapeDtypeStruct((b, s, 3 * w), F32),
                   jax.ShapeDtypeStruct((2, b, s, 3 * w), F32),
                   jax.ShapeDtypeStruct((b, s, 2 * w), F32)],
        scratch_shapes=[pltpu.VMEM((ts + 2 * _SHIFT_HALO, RWKV_PAD), F32)],
        compiler_params=_cparams(("parallel", "parallel")),
        name="rwkv_prep",
    )(rw3d, rw3d, rw3d, mup, mun, wl, bias, k_k, k_a, r_k, ones)


def _expand_bd(x_bf16, mask_bd):
    return jnp.where(mask_bd, jnp.concatenate([x_bf16] * RWKV_HEADS, axis=0), jnp.zeros((), BF16))


def _wkv_kernel(shf_ref, shr_ref, df_ref, dr_ref, yf_ref, yr_ref, hf_ref, hr_ref):
    c = WKV_CHUNK
    w = RWKV_WIDTH
    tt = shf_ref.shape[1]
    ncs = tt // c

    @pl.when(pl.program_id(1) == 0)
    def _():
        hf_ref[...] = jnp.zeros_like(hf_ref)
        hr_ref[...] = jnp.zeros_like(hr_ref)

    row_c = lax.broadcasted_iota(jnp.int32, (c, c), 0)
    col_c = lax.broadcasted_iota(jnp.int32, (c, c), 1)
    t_i = lax.broadcasted_iota(jnp.int32, (c, w), 0)
    s_i = lax.broadcasted_iota(jnp.int32, (c, w), 1) % c
    eye_c = (s_i == t_i).astype(F32)
    row_w = lax.broadcasted_iota(jnp.int32, (w, w), 0)
    col_w = lax.broadcasted_iota(jnp.int32, (w, w), 1)
    mask_bd = (row_w // c) == (col_w // HEAD_DIM)
    eye_w = row_w == col_w
    tri_incl = [jnp.where(col_c <= row_c, 1.0, 0.0).astype(BF16),
                jnp.where(col_c >= row_c, 1.0, 0.0).astype(BF16)]
    strict = [s_i < t_i, s_i > t_i]
    incl = [s_i <= t_i, s_i >= t_i]

    def bf(x):
        return x.astype(BF16)

    def bd(x):
        return _expand_bd(bf(x), mask_bd)

    chunks = []
    for ci in range(ncs):
        chunks.append((0, ci * c, shf_ref, df_ref))
        chunks.append((1, (ncs - 1 - ci) * c, shr_ref, dr_ref))

    st = []
    for d, start, sh_ref, dir_ref in chunks:
        rows = pl.ds(start, c)
        st.append(dict(d=d, rows=rows,
                       r=sh_ref[0, rows, 0:w], v=sh_ref[0, rows, w:2 * w], kk=sh_ref[0, rows, 2 * w:3 * w],
                       lw=dir_ref[0, rows, 0:w], kh=dir_ref[0, rows, w:2 * w], nb=dir_ref[0, rows, 2 * w:3 * w]))

    for q in st:
        l_hi, l_mid, l_lo = _split3(q["lw"])
        tri = tri_incl[q["d"]]
        q["cum"] = _dot(tri, l_hi) + _dot(tri, l_mid) + _dot(tri, l_lo)
    for q in st:
        cum, lw = q["cum"], q["lw"]
        tot = jnp.sum(lw, axis=0, keepdims=True)
        e_inv = jnp.exp(-cum)
        e_end = jnp.exp(tot - cum)
        q["tot"] = tot
        q["a_bar"] = q["kk"] * jnp.exp(cum - lw)
        q["r_bar"] = q["r"] * jnp.exp(cum)
        q["b_hat"] = bf(q["nb"] * e_end)
        q["k_hat"] = bf(q["kh"] * e_end)
        q["v_bf"] = bf(q["v"])
        q["v_bd"] = _expand_bd(q["v_bf"], mask_bd)
        q["lhs"] = jnp.concatenate([bf(q["a_bar"]), bf(q["r_bar"])], axis=0)
        q["rhs"] = jnp.concatenate([bd(q["nb"] * e_inv), bd(q["kh"] * e_inv)], axis=0)
    for q in st:
        sc = _dot_nt(q["lhs"], q["rhs"])
        sm, im = strict[q["d"]], incl[q["d"]]
        q["x"] = jnp.where(sm, sc[0:c, 0:w], 0.0)
        q["a_ak"] = bf(jnp.where(sm, sc[0:c, w:2 * w], 0.0))
        q["m_rb"] = bf(jnp.where(im, sc[c:2 * c, 0:w], 0.0))
        q["m_rk"] = bf(jnp.where(im, sc[c:2 * c, w:2 * w], 0.0))
        q["tinv"] = eye_c + q["x"]

    for q in st:
        q["x"] = _dot(bf(q["x"]), bd(q["x"]))
    n_lv = int(math.log2(c)) - 1
    for lv in range(n_lv):
        last = lv == n_lv - 1
        for q in st:
            xb = bd(q["x"])
            if last:
                q["tinv"] = q["tinv"] + _dot(bf(q["tinv"]), xb)
            else:
                res = _dot(jnp.concatenate([bf(q["x"]), bf(q["tinv"])], axis=0), xb)
                q["tinv"] = q["tinv"] + res[c:2 * c]
                q["x"] = res[0:c]
    for q in st:
        res = _dot(jnp.concatenate([q["a_ak"], q["m_rk"]], axis=0), q["v_bd"])
        q["akv"] = res[0:c]
        q["mrkv"] = res[c:2 * c]
    for q in st:
        tinv_bf = bf(q["tinv"])
        q["a_pr"] = bf(_dot(tinv_bf, bd(q["a_bar"])))
        q["v_pr"] = bf(_dot(tinv_bf, bd(q["akv"])))
    for q in st:
        q["r_pr"] = bf(q["r_bar"] + _dot(q["m_rb"], _expand_bd(q["a_pr"], mask_bd)))
        q["y_pr"] = _dot(q["m_rb"], _expand_bd(q["v_pr"], mask_bd)) + q["mrkv"]
    for q in st:
        q["p_bd"] = bf(jnp.where(eye_w, jnp.exp(q["tot"]), 0.0)
                       + jnp.where(mask_bd, _dot_tn(q["b_hat"], q["a_pr"]), 0.0))
        q["q_bd"] = jnp.where(mask_bd,
                              _dot_tn(jnp.concatenate([q["b_hat"], q["k_hat"]], axis=0),
                                      jnp.concatenate([q["v_pr"], q["v_bf"]], axis=0)), 0.0)

    h = [hf_ref[...], hr_ref[...]]
    y_refs = [yf_ref, yr_ref]
    for q in st:
        d = q["d"]
        res = _dot(jnp.concatenate([q["r_pr"], q["p_bd"]], axis=0), bf(h[d]))
        y_refs[d][0, q["rows"], :] = res[0:c] + q["y_pr"]
        h[d] = res[c:c + w] + q["q_bd"]
    hf_ref[...] = h[0]
    hr_ref[...] = h[1]


def _wkv(shared, dirp, tt):
    b, s, _ = shared.shape
    nt = s // tt
    w = RWKV_WIDTH
    fwd = lambda i, j: (i, j, 0)
    bwd = lambda i, j: (i, nt - 1 - j, 0)
    return pl.pallas_call(
        _wkv_kernel,
        grid=(b, nt),
        in_specs=[pl.BlockSpec((1, tt, 3 * w), fwd), pl.BlockSpec((1, tt, 3 * w), bwd),
                  pl.BlockSpec((None, 1, tt, 3 * w), lambda i, j: (0, i, j, 0)),
                  pl.BlockSpec((None, 1, tt, 3 * w), lambda i, j: (1, i, nt - 1 - j, 0))],
        out_specs=[pl.BlockSpec((1, tt, w), fwd), pl.BlockSpec((1, tt, w), bwd)],
        out_shape=[jax.ShapeDtypeStruct((b, s, w), F32), jax.ShapeDtypeStruct((b, s, w), F32)],
        scratch_shapes=[pltpu.VMEM((w, w), F32), pltpu.VMEM((w, w), F32)],
        compiler_params=_cparams(("parallel", "arbitrary")),
        name="wkv",
    )(shared, shared, dirp, dirp)


def _rope(x, cos, sin_signed):
    n = x.shape[-1]
    lane = lax.broadcasted_iota(jnp.int32, x.shape, 1)
    first = (lane // 16) % 2 == 0
    partner = jnp.where(first, pltpu.roll(x, n - 16, 1), pltpu.roll(x, 16, 1))
    return x * cos + partner * sin_signed


def _attn_prep_kernel(at_ref, cos_ref, sin_ref, qg_ref, kg_ref, onesq_ref, q_ref, k_ref, v_ref):
    q = at_ref[:, 0:ATTN_WIDTH]
    k = at_ref[:, ATTN_WIDTH:ATTN_WIDTH + ATTN_KV_WIDTH]
    v = at_ref[:, ATTN_WIDTH + ATTN_KV_WIDTH:ATTN_IN]
    ones_q = onesq_ref[...]
    ones_k = onesq_ref[0:ATTN_KV_WIDTH, 0:ATTN_KV_WIDTH]
    inv = 1.0 / HEAD_DIM
    qn = q * lax.rsqrt(_headsum(q * q, ones_q) * inv + QK_EPS) * qg_ref[...]
    kn = k * lax.rsqrt(_headsum(k * k, ones_k) * inv + QK_EPS) * kg_ref[...]
    cos = cos_ref[...]
    sin = sin_ref[...]
    q_ref[...] = (_rope(qn, cos, sin) * (HEAD_DIM ** -0.5)).astype(BF16)
    k_ref[...] = _rope(kn, cos[:, 0:ATTN_KV_WIDTH], sin[:, 0:ATTN_KV_WIDTH]).astype(BF16)
    v_ref[...] = v.astype(BF16)


def _attn_prep(at2d, cos8, sin8, qg, kg, ones_q, seq, tm):
    n = at2d.shape[0]
    nts = seq // tm
    return pl.pallas_call(
        _attn_prep_kernel,
        grid=(n // tm,),
        in_specs=[pl.BlockSpec((tm, ATTN_IN), lambda i: (i, 0)),
                  pl.BlockSpec((tm, ATTN_WIDTH), lambda i: (i % nts, 0)),
                  pl.BlockSpec((tm, ATTN_WIDTH), lambda i: (i % nts, 0)),
                  pl.BlockSpec((1, ATTN_WIDTH), lambda i: (0, 0)),
                  pl.BlockSpec((1, ATTN_KV_WIDTH), lambda i: (0, 0)),
                  pl.BlockSpec((ATTN_WIDTH, ATTN_WIDTH), lambda i: (0, 0))],
        out_specs=[pl.BlockSpec((tm, ATTN_WIDTH), lambda i: (i, 0)),
                   pl.BlockSpec((tm, ATTN_KV_WIDTH), lambda i: (i, 0)),
                   pl.BlockSpec((tm, ATTN_KV_WIDTH), lambda i: (i, 0))],
        out_shape=[jax.ShapeDtypeStruct((n, ATTN_WIDTH), BF16),
                   jax.ShapeDtypeStruct((n, ATTN_KV_WIDTH), BF16),
                   jax.ShapeDtypeStruct((n, ATTN_KV_WIDTH), BF16)],
        compiler_params=_cparams(("parallel",)),
        name="attn_prep",
    )(at2d, cos8, sin8, qg, kg, ones_q)


def _attn_kernel(q_ref, k_ref, v_ref, o_ref):
    tq = q_ref.shape[0]
    k = k_ref[...]
    v = v_ref[...]
    lane = lax.broadcasted_iota(jnp.int32, (tq, ATTN_KV_WIDTH), 1)
    low = lane < HEAD_DIM
    zero = jnp.zeros((), BF16)
    for j in range(ATTN_HEADS // 2):
        qb = q_ref[:, j * ATTN_KV_WIDTH:(j + 1) * ATTN_KV_WIDTH]
        outs = []
        for half in range(2):
            qm = jnp.where(low if half == 0 else ~low, qb, zero)
            s = _dot_nt(qm, k)
            m = jnp.max(s, axis=-1, keepdims=True)
            p = jnp.exp(s - m)
            l = jnp.sum(p, axis=-1, keepdims=True)
            outs.append(_dot(p.astype(BF16), v) / l)
        o_ref[:, j * ATTN_KV_WIDTH:(j + 1) * ATTN_KV_WIDTH] = jnp.where(low, outs[0], outs[1]).astype(BF16)


def _attn(q, k, v, seq, tq):
    n = q.shape[0]
    nq = seq // tq
    return pl.pallas_call(
        _attn_kernel,
        grid=(n // seq, nq),
        in_specs=[pl.BlockSpec((tq, ATTN_WIDTH), lambda b, i: (b * nq + i, 0)),
                  pl.BlockSpec((seq, ATTN_KV_WIDTH), lambda b, i: (b, 0)),
                  pl.BlockSpec((seq, ATTN_KV_WIDTH), lambda b, i: (b, 0))],
        out_specs=pl.BlockSpec((tq, ATTN_WIDTH), lambda b, i: (b * nq + i, 0)),
        out_shape=jax.ShapeDtypeStruct((n, ATTN_WIDTH), BF16),
        compiler_params=_cparams(("parallel", "parallel")),
        name="attn",
    )(q, k, v)


def _layer_norm(z, g, b):
    mu = jnp.mean(z, axis=-1, keepdims=True)
    zc = z - mu
    var = jnp.mean(zc * zc, axis=-1, keepdims=True)
    return zc * lax.rsqrt(var + LN_EPS) * g + b


def _outproj_kernel(x_ref, pool_ref, wkvf_ref, wkvr_ref, post_ref, attn_ref, wo_ref, gng_ref, gnb_ref,
                    ones_ref, lg_ref, lb_ref, wrh_ref, wrl_ref, br_ref,
                    x1_ref, x1p_ref, route_ref):
    w = RWKV_WIDTH
    ones = ones_ref[...]
    wkv = wkvf_ref[...] + wkvr_ref[...]
    inv = 1.0 / HEAD_DIM
    mu = _headsum(wkv, ones) * inv
    cen = wkv - mu
    var = _headsum(cen * cen, ones) * inv
    yr = cen * lax.rsqrt(var + GN_EPS) * gng_ref[...] + gnb_ref[...] + post_ref[:, 0:w]
    yr = yr * post_ref[:, w:2 * w]
    y = (_dot(pool_ref[...].astype(BF16), wo_ref[0:POOL_WIDTH, :])
         + _dot(yr.astype(BF16), wo_ref[POOL_WIDTH:POOL_WIDTH + w, :])
         + _dot(attn_ref[...], wo_ref[POOL_WIDTH + w:D_MODEL, :]))
    x1 = _layer_norm(DEEPNORM_ALPHA * x_ref[...] + y, lg_ref[...], lb_ref[...])
    x1_ref[...] = x1
    x1p_ref[...] = _pack_rows(x1)

    xh, xl = _split2(x1)
    logits = (_dot(xh, wrh_ref[...]) + _dot(xl, wrh_ref[...]) + _dot(xh, wrl_ref[...])) + br_ref[...]
    lane = lax.broadcasted_iota(jnp.int32, logits.shape, 1)
    neg = -jnp.inf
    big = jnp.int32(1 << 20)
    gl = jnp.where(lane < N_GROUPS, logits, neg)
    gmax = jnp.max(gl, axis=-1, keepdims=True)
    grp = jnp.min(jnp.where(gl == gmax, lane, big), axis=-1, keepdims=True)
    gw = 1.0 / jnp.sum(jnp.exp(gl - gmax), axis=-1, keepdims=True)
    in_grp = (lane >= N_GROUPS) & (lane < N_GROUPS + N_EXPERTS) & \
             ((lane - N_GROUPS) // EXPERTS_PER_GROUP == grp)
    el = jnp.where(in_grp, logits, neg)
    v1 = jnp.max(el, axis=-1, keepdims=True)
    i1 = jnp.min(jnp.where(el == v1, lane, big), axis=-1, keepdims=True)
    el2 = jnp.where(lane == i1, neg, el)
    v2 = jnp.max(el2, axis=-1, keepdims=True)
    i2 = jnp.min(jnp.where(el2 == v2, lane, big), axis=-1, keepdims=True)
    e21 = jnp.exp(v2 - v1)
    g1 = gw / (1.0 + e21)
    g2 = gw * e21 / (1.0 + e21)
    route = jnp.where(lane == 0, (i1 - N_GROUPS).astype(F32),
                      jnp.where(lane == 1, (i2 - N_GROUPS).astype(F32),
                                jnp.where(lane == 2, g1, jnp.where(lane == 3, g2, 0.0))))
    route_ref[...] = route


def _outproj(x2d, pool2d, wkvf, wkvr, post2d, attn2d, wo, gng, gnb, ones, lg, lb, wrh, wrl, br, tm):
    n = x2d.shape[0]
    w = RWKV_WIDTH
    row = lambda width: pl.BlockSpec((tm, width), lambda i: (i, 0))
    vec = lambda width: pl.BlockSpec((1, width), lambda i: (0, 0))
    return pl.pallas_call(
        _outproj_kernel,
        grid=(n // tm,),
        in_specs=[row(D_MODEL), row(POOL_WIDTH), row(w), row(w),
                  row(2 * w), row(ATTN_WIDTH),
                  pl.BlockSpec((D_MODEL, D_MODEL), lambda i: (0, 0)),
                  vec(w), vec(w),
                  pl.BlockSpec((w, w), lambda i: (0, 0)),
                  vec(D_MODEL), vec(D_MODEL),
                  pl.BlockSpec((D_MODEL, ROUTE_LANES), lambda i: (0, 0)),
                  pl.BlockSpec((D_MODEL, ROUTE_LANES), lambda i: (0, 0)),
                  vec(ROUTE_LANES)],
        out_specs=[row(D_MODEL), row(HALF_D), row(ROUTE_LANES)],
        out_shape=[jax.ShapeDtypeStruct((n, D_MODEL), F32),
                   jax.ShapeDtypeStruct((n, HALF_D), U32),
                   jax.ShapeDtypeStruct((n, ROUTE_LANES), F32)],
        compiler_params=_cparams(("parallel",)),
        name="outproj",
    )(x2d, pool2d, wkvf, wkvr, post2d, attn2d, wo, gng, gnb, ones, lg, lb, wrh, wrl, br)


def _expert_kernel(bexp_ref, nused_ref, xs_ref, wg_ref, wu_ref, wd_ref, ys_ref):
    @pl.when(pl.program_id(0) < nused_ref[0])
    def _():
        xa, xb = _unpack_rows(xs_ref[...])
        h1 = _dot(xa, wg_ref[0, 0:HALF_D, :].astype(BF16)) + _dot(xb, wg_ref[0, HALF_D:D_MODEL, :].astype(BF16))
        h2 = _dot(xa, wu_ref[0, 0:HALF_D, :].astype(BF16)) + _dot(xb, wu_ref[0, HALF_D:D_MODEL, :].astype(BF16))
        h = (h1 * jax.nn.sigmoid(h1) * h2).astype(BF16)
        ys_ref[...] = _pack_rows(_dot(h, wd_ref[0].astype(BF16)))

    @pl.when(pl.program_id(0) >= nused_ref[0])
    def _():
        ys_ref[...] = jnp.zeros_like(ys_ref)


def _experts(block_exp, n_used, xs, wg, wu, wd, layer):
    p = xs.shape[0]
    nb = p // MOE_BLOCK
    wmap = lambda i, be, nu: (layer, be[i], 0, 0)
    grid_spec = pltpu.PrefetchScalarGridSpec(
        num_scalar_prefetch=2,
        grid=(nb,),
        in_specs=[pl.BlockSpec((MOE_BLOCK, HALF_D), lambda i, be, nu: (i, 0)),
                  pl.BlockSpec((None, 1, D_MODEL, EXPERT_HIDDEN), wmap),
                  pl.BlockSpec((None, 1, D_MODEL, EXPERT_HIDDEN), wmap),
                  pl.BlockSpec((None, 1, EXPERT_HIDDEN, D_MODEL), wmap)],
        out_specs=pl.BlockSpec((MOE_BLOCK, HALF_D), lambda i, be, nu: (i, 0)),
    )
    return pl.pallas_call(
        _expert_kernel,
        grid_spec=grid_spec,
        out_shape=jax.ShapeDtypeStruct((p, HALF_D), U32),
        compiler_params=_cparams(("arbitrary",)),
        name="experts",
    )(block_exp, n_used, xs, wg, wu, wd)


SC_CORES = 2
SC_SUBCORES = 16
SC_GATHER_ROWS = 64


def _sc_gather(table, idx):
    v, d = table.shape
    b = idx.shape[0]
    nw = SC_CORES * SC_SUBCORES
    ch = SC_GATHER_ROWS
    assert b % (nw * ch) == 0
    b_per_w = b // nw
    nch = b_per_w // ch
    mesh = plsc.VectorSubcoreMesh(core_axis_name="c", subcore_axis_name="s")

    @functools.partial(
        pl.kernel, mesh=mesh,
        out_type=jax.ShapeDtypeStruct((b, d), table.dtype),
        scratch_types=[pltpu.VMEM((b_per_w,), jnp.int32),
                       pltpu.VMEM((ch, d), table.dtype),
                       pltpu.SemaphoreType.DMA],
    )
    def gather_kernel(table_hbm, idx_hbm, out_hbm, idx_v, rows_v, sem):
        wid = lax.axis_index("s") * SC_CORES + lax.axis_index("c")
        base = wid * b_per_w
        pltpu.sync_copy(idx_hbm.at[pl.ds(base, b_per_w)], idx_v)

        @pl.loop(0, nch)
        def _(j):
            off = pl.multiple_of(j * ch, ch)
            pltpu.async_copy(table_hbm.at[idx_v.at[pl.ds(off, ch)]], rows_v, sem).wait()
            pltpu.sync_copy(rows_v, out_hbm.at[pl.ds(base + off, ch)])

    return gather_kernel(table, idx)


def _gather_rows(table, idx):
    b = idx.shape[0]
    unit = SC_CORES * SC_SUBCORES * SC_GATHER_ROWS
    bp = -(-b // unit) * unit
    if bp == b:
        return _sc_gather(table, idx)
    return _sc_gather(table, jnp.pad(idx, (0, bp - b)))[:b]


def _combine_kernel(x1_ref, ya_ref, yb_ref, route_ref, lg_ref, lb_ref, o_ref):
    route = route_ref[...]
    g1 = route[:, 2:3]
    g2 = route[:, 3:4]
    a_hi, a_lo = _unpack_rows(ya_ref[...])
    b_hi, b_lo = _unpack_rows(yb_ref[...])
    m = jnp.concatenate([g1 * a_hi.astype(F32) + g2 * b_hi.astype(F32),
                         g1 * a_lo.astype(F32) + g2 * b_lo.astype(F32)], axis=1)
    o_ref[...] = _layer_norm(DEEPNORM_ALPHA * x1_ref[...] + m, lg_ref[...], lb_ref[...])


def _combine(x1, yab, route, lg, lb, tm):
    n = x1.shape[0]
    nt = n // tm
    row = lambda width: pl.BlockSpec((tm, width), lambda i: (i, 0))
    vec = lambda width: pl.BlockSpec((1, width), lambda i: (0, 0))
    return pl.pallas_call(
        _combine_kernel,
        grid=(nt,),
        in_specs=[row(D_MODEL), row(HALF_D), pl.BlockSpec((tm, HALF_D), lambda i: (i + nt, 0)),
                  row(ROUTE_LANES), vec(D_MODEL), vec(D_MODEL)],
        out_specs=row(D_MODEL),
        out_shape=jax.ShapeDtypeStruct((n, D_MODEL), F32),
        compiler_params=_cparams(("parallel",)),
        name="combine",
    )(x1, yab, yab, route, lg, lb)


def _q_perm():
    order = [h for j in range(ATTN_HEADS // 2) for h in (j, j + ATTN_HEADS // 2)]
    return jnp.concatenate([jnp.arange(h * HEAD_DIM, (h + 1) * HEAD_DIM) for h in order])


def _block_diag(blocks):
    n = len(blocks)
    r, c = blocks[0].shape
    out = jnp.zeros((n * r, n * c), blocks[0].dtype)
    for i, blk in enumerate(blocks):
        out = out.at[i * r:(i + 1) * r, i * c:(i + 1) * c].set(blk)
    return out


def _rope_tables(seq):
    rows = seq // GRID_W
    row_id = jnp.repeat(jnp.arange(rows), GRID_W).astype(F32)
    col_id = jnp.tile(jnp.arange(GRID_W), rows).astype(F32)
    half = HEAD_DIM // 2
    inv_freq = ROPE_THETA ** (-jnp.arange(0, half, 2, dtype=F32) / half)
    ang_r = row_id[:, None] * inv_freq
    ang_c = col_id[:, None] * inv_freq
    ang = jnp.concatenate([ang_r, ang_r, ang_c, ang_c], -1)
    sign = jnp.where((jnp.arange(HEAD_DIM) // 16) % 2 == 0, -1.0, 1.0).astype(F32)
    cos8 = jnp.tile(jnp.cos(ang), (1, ATTN_HEADS))
    sin8 = jnp.tile(jnp.sin(ang) * sign, (1, ATTN_HEADS))
    return cos8, sin8


def _dispatch(route, n):
    m = n * 2
    experts = jnp.arange(N_EXPERTS, dtype=jnp.int32)
    e_flat = route[:, 0:2].astype(jnp.int32).reshape(m)
    pair = jnp.arange(m, dtype=jnp.int32)
    e_sorted, order = lax.sort((e_flat, pair), num_keys=1, is_stable=True)
    onehot_sorted = e_sorted[:, None] == experts[None, :]
    counts = jnp.sum(onehot_sorted, axis=0, dtype=jnp.int32)
    padded = (counts + MOE_BLOCK - 1) // MOE_BLOCK * MOE_BLOCK
    start = jnp.cumsum(counts) - counts
    ends_p = jnp.cumsum(padded)
    pstart = ends_p - padded
    shift = jnp.sum(jnp.where(onehot_sorted, (pstart - start)[None, :], 0), axis=1, dtype=jnp.int32)
    dest = pair + shift
    _, pos = lax.sort((order, dest), num_keys=1)
    n_blocks = -(-(m + N_EXPERTS * (MOE_BLOCK - 1)) // MOE_BLOCK)
    block_start = jnp.arange(n_blocks, dtype=jnp.int32) * MOE_BLOCK
    block_exp = jnp.minimum(jnp.sum(ends_p[None, :] <= block_start[:, None], axis=1, dtype=jnp.int32),
                            N_EXPERTS - 1)
    first = block_start - (pstart - start)[block_exp]
    src_pair = jnp.clip(first[:, None] + jnp.arange(MOE_BLOCK, dtype=jnp.int32)[None, :], 0, m - 1)
    row_tok = order[src_pair.reshape(-1)] // 2
    n_used = (ends_p[-1] // MOE_BLOCK).astype(jnp.int32).reshape(1)
    return row_tok, pos.reshape(n, 2), block_exp, n_used


def kernel(x, w_in, mu_prev, mu_next, pool_w, pool_scale, rw_w0, rw_w_up, rw_a0, rw_a_up, rw_g_up, rw_k_k, rw_k_a, rw_r_k, rw_gn_g, rw_gn_b, q_norm, k_norm, w_o, ln1_g, ln1_b, router_group, router_group_b, router_expert, router_expert_b, exp_gate, exp_up, exp_down, ln2_g, ln2_b):
    b, s, d = x.shape
    n = b * s
    w = RWKV_WIDTH
    tm = min(512, s)
    qperm = _q_perm()
    cos8, sin8 = _rope_tables(s)
    ones_q = _block_diag([jnp.ones((HEAD_DIM, HEAD_DIM), BF16)] * ATTN_HEADS)
    ones_r = ones_q[0:w, 0:w]
    a_end = POOL_WIDTH
    b_end = POOL_WIDTH + RWKV_IN

    xc = x.reshape(n, d)
    for l in range(DEPTH):
        wq = w_in[l][:, b_end:b_end + ATTN_WIDTH][:, qperm]
        w_proj = jnp.concatenate(
            [w_in[l][:, :a_end], w_in[l][:, a_end:b_end],
             jnp.zeros((d, RWKV_PAD - RWKV_IN), F32), wq, w_in[l][:, b_end + ATTN_WIDTH:]],
            axis=1).astype(BF16)
        pool_bd = _block_diag([pool_w[l, g] for g in range(len(POOL_WINDOWS))]).astype(BF16)
        pad_vec = jnp.zeros((RWKV_PAD - RWKV_IN,), F32)
        mup = jnp.concatenate([mu_prev[l], pad_vec]).reshape(1, RWKV_PAD)
        mun = jnp.concatenate([mu_next[l], pad_vec]).reshape(1, RWKV_PAD)
        w_lora = jnp.zeros((w, 5 * w), F32)
        for di in range(2):
            w_lora = w_lora.at[di * DECAY_LORA:(di + 1) * DECAY_LORA, di * w:(di + 1) * w].set(rw_w_up[l, di])
            o = 2 * DECAY_LORA
            w_lora = w_lora.at[o + di * AAA_LORA:o + (di + 1) * AAA_LORA, (2 + di) * w:(3 + di) * w].set(rw_a_up[l, di])
        o = 2 * DECAY_LORA + 2 * AAA_LORA
        w_lora = w_lora.at[o:o + GATE_LORA, 4 * w:5 * w].set(rw_g_up[l]).astype(BF16)
        lora_bias = jnp.concatenate([rw_w0[l, 0], rw_w0[l, 1], rw_a0[l, 0], rw_a0[l, 1],
                                     jnp.zeros((w,), F32)]).reshape(1, 5 * w)
        qg = jnp.tile(q_norm[l], ATTN_HEADS).reshape(1, ATTN_WIDTH)
        kg = jnp.tile(k_norm[l], 2).reshape(1, ATTN_KV_WIDTH)
        wo_attn = w_o[l][POOL_WIDTH + w:][qperm]
        wo = jnp.concatenate([w_o[l][:POOL_WIDTH + w], wo_attn], axis=0).astype(BF16)
        wr = jnp.concatenate([router_group[l], router_expert[l],
                              jnp.zeros((d, ROUTE_LANES - N_GROUPS - N_EXPERTS), F32)], axis=1)
        wrh = wr.astype(BF16)
        wrl = (wr - wrh.astype(F32)).astype(BF16)
        br = jnp.concatenate([router_group_b[l], router_expert_b[l],
                              jnp.zeros((ROUTE_LANES - N_GROUPS - N_EXPERTS,), F32)]).reshape(1, ROUTE_LANES)

        pool_in, rw_in, at_in = _proj(xc, w_proj, tm)
        y_pool = _pool(pool_in.reshape(b, s, POOL_WIDTH), pool_bd, pool_scale[l].reshape(1, POOL_WIDTH))
        shared, dirp, post = _rwkv_prep(rw_in.reshape(b, s, RWKV_PAD), mup, mun, w_lora, lora_bias,
                                        rw_k_k[l].reshape(1, w), rw_k_a[l].reshape(1, w),
                                        rw_r_k[l].reshape(1, w), ones_r, tm)
        wkv_f, wkv_r = _wkv(shared, dirp, min(256, s))
        qh, kh, vh = _attn_prep(at_in, cos8, sin8, qg, kg, ones_q, s, tm)
        y_attn = _attn(qh, kh, vh, s, min(256, s))
        x1, x1p, route = _outproj(xc, y_pool.reshape(n, POOL_WIDTH), wkv_f.reshape(n, w), wkv_r.reshape(n, w),
                                  post.reshape(n, 2 * w), y_attn, wo,
                                  rw_gn_g[l].reshape(1, w), rw_gn_b[l].reshape(1, w), ones_r,
                                  ln1_g[l].reshape(1, d), ln1_b[l].reshape(1, d), wrh, wrl, br, tm)

        row_tok, pos, block_exp, n_used = _dispatch(route, n)
        xs = _gather_rows(x1p, row_tok)
        ys = _experts(block_exp, n_used, xs, exp_gate, exp_up, exp_down, l)
        yab = _gather_rows(ys, pos.T.reshape(2 * n))
        xc = _combine(x1, yab, route, ln2_g[l].reshape(1, d), ln2_b[l].reshape(1, d), tm)
    return xc.reshape(b, s, d)
```

```python
import functools
import math

import jax
import jax.numpy as jnp
from jax import lax
from jax.experimental import pallas as pl
from jax.experimental.pallas import tpu as pltpu
from jax.experimental.pallas import tpu_sc as plsc

F32 = jnp.float32
BF16 = jnp.bfloat16

D_MODEL = 1024
DEPTH = 4
GRID_W = 64
HEAD_DIM = 64
POOL_WIDTH = 256
POOL_WINDOWS = (2, 4, 8, 16)
POOL_GROUP = 64
RWKV_WIDTH = 256
RWKV_HEADS = 4
DECAY_LORA = 32
AAA_LORA = 32
GATE_LORA = 64
GN_EPS = 64e-5
RWKV_IN = 960
RWKV_PAD = 1024
ATTN_WIDTH = 512
ATTN_HEADS = 8
ATTN_KV_WIDTH = 128
ATTN_IN = ATTN_WIDTH + 2 * ATTN_KV_WIDTH
ROPE_THETA = 10000.0
QK_EPS = 1e-6
N_GROUPS = 4
EXPERTS_PER_GROUP = 8
N_EXPERTS = 32
EXPERT_HIDDEN = 512
MOE_BLOCK = 512
DEEPNORM_ALPHA = float((2 * DEPTH) ** 0.25)
LN_EPS = 1e-5
PROJ_WIDTH = POOL_WIDTH + RWKV_PAD + ATTN_IN
ROUTE_LANES = 128
WKV_CHUNK = 64
VMEM_LIMIT = 48 * 1024 * 1024


def _cparams(sem):
    return pltpu.CompilerParams(dimension_semantics=sem, vmem_limit_bytes=VMEM_LIMIT)


def _dot(a, b):
    return jnp.dot(a, b, preferred_element_type=F32)


def _dot_nt(a, b):
    return lax.dot_general(a, b, (((1,), (1,)), ((), ())), preferred_element_type=F32)


def _dot_tn(a, b):
    return lax.dot_general(a, b, (((0,), (0,)), ((), ())), preferred_element_type=F32)


def _split2(x):
    hi = x.astype(BF16)
    lo = (x - hi.astype(F32)).astype(BF16)
    return hi, lo


def _split3(x):
    hi = x.astype(BF16)
    r1 = x - hi.astype(F32)
    mid = r1.astype(BF16)
    lo = (r1 - mid.astype(F32)).astype(BF16)
    return hi, mid, lo


HALF_D = D_MODEL // 2
U32 = jnp.uint32


def _pack_rows(x):
    hi = lax.bitcast_convert_type(x[:, :HALF_D].astype(BF16).astype(F32), U32)
    lo = lax.bitcast_convert_type(x[:, HALF_D:].astype(BF16).astype(F32), U32)
    return hi | (lo >> 16)


def _unpack_rows(u):
    a = lax.bitcast_convert_type(u & jnp.uint32(0xFFFF0000), F32).astype(BF16)
    b = lax.bitcast_convert_type(u << 16, F32).astype(BF16)
    return a, b


def _headsum(x, ones_bf16):
    hi, lo = _split2(x)
    return _dot(hi, ones_bf16) + _dot(lo, ones_bf16)


def _rope(x, cos, sin_signed):
    n = x.shape[-1]
    lane = lax.broadcasted_iota(jnp.int32, x.shape, 1)
    first = (lane // 16) % 2 == 0
    partner = jnp.where(first, pltpu.roll(x, n - 16, 1), pltpu.roll(x, 16, 1))
    return x * cos + partner * sin_signed


def _proj_kernel(x_ref, w_ref, cos_ref, sin_ref, qg_ref, kg_ref, onesq_ref,
                 pool_ref, rw_ref, q_ref, k_ref, v_ref):
    xb = x_ref[...].astype(BF16)
    pool_ref[...] = _dot(xb, w_ref[:, 0:POOL_WIDTH])
    rw_ref[...] = _dot(xb, w_ref[:, POOL_WIDTH:POOL_WIDTH + RWKV_PAD])
    at = _dot(xb, w_ref[:, POOL_WIDTH + RWKV_PAD:PROJ_WIDTH])

    q = at[:, 0:ATTN_WIDTH]
    k = at[:, ATTN_WIDTH:ATTN_WIDTH + ATTN_KV_WIDTH]
    v = at[:, ATTN_WIDTH + ATTN_KV_WIDTH:ATTN_IN]
    ones_q = onesq_ref[...]
    ones_k = onesq_ref[0:ATTN_KV_WIDTH, 0:ATTN_KV_WIDTH]
    inv = 1.0 / HEAD_DIM
    qn = q * lax.rsqrt(_headsum(q * q, ones_q) * inv + QK_EPS) * qg_ref[...]
    kn = k * lax.rsqrt(_headsum(k * k, ones_k) * inv + QK_EPS) * kg_ref[...]
    cos = cos_ref[...]
    sin = sin_ref[...]
    q_ref[...] = (_rope(qn, cos, sin) * (HEAD_DIM ** -0.5)).astype(BF16)
    k_ref[...] = _rope(kn, cos[:, 0:ATTN_KV_WIDTH], sin[:, 0:ATTN_KV_WIDTH]).astype(BF16)
    v_ref[...] = v.astype(BF16)


def _proj(x2d, w, cos8, sin8, qg, kg, ones_q, seq, tm):
    n = x2d.shape[0]
    nts = seq // tm
    row = lambda width: pl.BlockSpec((tm, width), lambda i: (i, 0))
    const = lambda r, c: pl.BlockSpec((r, c), lambda i: (0, 0))
    return pl.pallas_call(
        _proj_kernel,
        grid=(n // tm,),
        in_specs=[row(D_MODEL), const(D_MODEL, PROJ_WIDTH),
                  pl.BlockSpec((tm, ATTN_WIDTH), lambda i: (i % nts, 0)),
                  pl.BlockSpec((tm, ATTN_WIDTH), lambda i: (i % nts, 0)),
                  const(1, ATTN_WIDTH), const(1, ATTN_KV_WIDTH), const(ATTN_WIDTH, ATTN_WIDTH)],
        out_specs=[row(POOL_WIDTH), row(RWKV_PAD), row(ATTN_WIDTH), row(ATTN_KV_WIDTH), row(ATTN_KV_WIDTH)],
        out_shape=[jax.ShapeDtypeStruct((n, POOL_WIDTH), F32),
                   jax.ShapeDtypeStruct((n, RWKV_PAD), F32),
                   jax.ShapeDtypeStruct((n, ATTN_WIDTH), BF16),
                   jax.ShapeDtypeStruct((n, ATTN_KV_WIDTH), BF16),
                   jax.ShapeDtypeStruct((n, ATTN_KV_WIDTH), BF16)],
        compiler_params=_cparams(("parallel",)),
        name="proj",
    )(x2d, w, cos8, sin8, qg, kg, ones_q)


_POOL_PAD = 16


def _pool_kernel(u_ref, w_ref, scale_ref, o_ref, pad_ref):
    s = u_ref.shape[1]
    u = u_ref[0]
    zeros = jnp.zeros((_POOL_PAD, POOL_WIDTH), F32)
    pad_ref[0:_POOL_PAD, :] = zeros
    pad_ref[_POOL_PAD + s:2 * _POOL_PAD + s, :] = zeros
    pad_ref[_POOL_PAD:_POOL_PAD + s, :] = u

    def shifted(k):
        return pad_ref[_POOL_PAD + k:_POOL_PAD + k + s, :]

    s2 = u + shifted(-1)
    s4 = s2 + shifted(-2) + shifted(1)
    s8 = s4 + shifted(-4) + shifted(-3) + shifted(2) + shifted(3)
    s16 = s8
    for k in (-8, -7, -6, -5, 4, 5, 6, 7):
        s16 = s16 + shifted(k)

    t = lax.broadcasted_iota(jnp.int32, (s, POOL_WIDTH), 0)
    grp = lax.broadcasted_iota(jnp.int32, (s, POOL_WIDTH), 1) // POOL_GROUP
    half = jnp.where(grp == 0, 1, jnp.where(grp == 1, 2, jnp.where(grp == 2, 4, 8)))
    cnt = (jnp.minimum(t + half, s) - jnp.maximum(t - half, 0)).astype(F32)
    tot = jnp.where(grp == 0, s2, jnp.where(grp == 1, s4, jnp.where(grp == 2, s8, s16)))
    d = tot / cnt - u
    o_ref[0] = _dot(d.astype(BF16), w_ref[...]) * scale_ref[...]


def _pool(u3d, w_bd, scale):
    b, s, _ = u3d.shape
    return pl.pallas_call(
        _pool_kernel,
        grid=(b,),
        in_specs=[pl.BlockSpec((1, s, POOL_WIDTH), lambda i: (i, 0, 0)),
                  pl.BlockSpec((POOL_WIDTH, POOL_WIDTH), lambda i: (0, 0)),
                  pl.BlockSpec((1, POOL_WIDTH), lambda i: (0, 0))],
        out_specs=pl.BlockSpec((1, s, POOL_WIDTH), lambda i: (i, 0, 0)),
        out_shape=jax.ShapeDtypeStruct((b, s, POOL_WIDTH), F32),
        scratch_shapes=[pltpu.VMEM((s + 2 * _POOL_PAD, POOL_WIDTH), F32)],
        compiler_params=_cparams(("parallel",)),
        name="pool",
    )(u3d, w_bd, scale)


_SHIFT_HALO = 8


def _rwkv_prep_kernel(cur_ref, prev_ref, next_ref, mup_ref, mun_ref, wl_ref, bias_ref,
                      kk_ref, ka_ref, rk_ref, ones_ref,
                      sh_ref, dir_ref, post_ref, pad_ref):
    ts = cur_ref.shape[1]
    ti = pl.program_id(1)
    nt = pl.num_programs(1)
    cur = cur_ref[0]
    pad_ref[0:_SHIFT_HALO, :] = jnp.where(ti > 0, prev_ref[0], 0.0)
    pad_ref[_SHIFT_HALO:_SHIFT_HALO + ts, :] = cur
    pad_ref[_SHIFT_HALO + ts:2 * _SHIFT_HALO + ts, :] = jnp.where(ti < nt - 1, next_ref[0], 0.0)
    prev = pad_ref[_SHIFT_HALO - 1:_SHIFT_HALO - 1 + ts, :]
    nxt = pad_ref[_SHIFT_HALO + 1:_SHIFT_HALO + 1 + ts, :]
    f = cur + mup_ref[...] * (prev - cur) + mun_ref[...] * (nxt - cur)

    w = RWKV_WIDTH
    r = f[:, 0:w]
    k = f[:, w:2 * w]
    v = f[:, 2 * w:3 * w]
    lora = f[:, 3 * w:4 * w]
    lane = lax.broadcasted_iota(jnp.int32, lora.shape, 1)
    z = jnp.where(lane < 2 * DECAY_LORA, jnp.tanh(lora),
                  jnp.where(lane < 2 * DECAY_LORA + 2 * AAA_LORA, lora,
                            jnp.where(lane < 2 * DECAY_LORA + 2 * AAA_LORA + GATE_LORA,
                                      jax.nn.sigmoid(lora), 0.0)))
    up = _dot(z.astype(BF16), wl_ref[...]) + bias_ref[...]
    ones = ones_ref[...]

    kk0 = k * kk_ref[...]
    nrm = jnp.sqrt(_headsum(kk0 * kk0, ones))
    kk = kk0 / jnp.maximum(nrm, 1e-12)
    sh_ref[0, :, 0:w] = r
    sh_ref[0, :, w:2 * w] = v
    sh_ref[0, :, 2 * w:3 * w] = kk

    ksum = jnp.zeros_like(k)
    for di in range(2):
        logw = -math.exp(-0.5) * jax.nn.sigmoid(up[:, di * w:(di + 1) * w])
        a = jax.nn.sigmoid(up[:, (2 + di) * w:(3 + di) * w])
        kh = k * (1.0 + (a - 1.0) * ka_ref[...])
        ksum = ksum + kh
        dir_ref[di, 0, :, 0:w] = logw
        dir_ref[di, 0, :, w:2 * w] = kh
        dir_ref[di, 0, :, 2 * w:3 * w] = -(a * kk)
    bonus = _headsum(r * ksum * rk_ref[...], ones) * v
    post_ref[0, :, 0:w] = bonus
    post_ref[0, :, w:2 * w] = up[:, 4 * w:5 * w]


def _rwkv_prep(rw3d, mup, mun, wl, bias, k_k, k_a, r_k, ones, ts):
    b, s, _ = rw3d.shape
    nt = s // ts
    hb = ts // _SHIFT_HALO
    nhb = s // _SHIFT_HALO
    w = RWKV_WIDTH
    vec = lambda width: pl.BlockSpec((1, width), lambda i, j: (0, 0))
    return pl.pallas_call(
        _rwkv_prep_kernel,
        grid=(b, nt),
        in_specs=[pl.BlockSpec((1, ts, RWKV_PAD), lambda i, j: (i, j, 0)),
                  pl.BlockSpec((1, _SHIFT_HALO, RWKV_PAD),
                               lambda i, j: (i, jnp.maximum(j * hb - 1, 0), 0)),
                  pl.BlockSpec((1, _SHIFT_HALO, RWKV_PAD),
                               lambda i, j: (i, jnp.minimum((j + 1) * hb, nhb - 1), 0)),
                  vec(RWKV_PAD), vec(RWKV_PAD),
                  pl.BlockSpec((w, 5 * w), lambda i, j: (0, 0)),
                  vec(5 * w), vec(w), vec(w), vec(w),
                  pl.BlockSpec((w, w), lambda i, j: (0, 0))],
        out_specs=[pl.BlockSpec((1, ts, 3 * w), lambda i, j: (i, j, 0)),
                   pl.BlockSpec((2, 1, ts, 3 * w), lambda i, j: (0, i, j, 0)),
                   pl.BlockSpec((1, ts, 2 * w), lambda i, j: (i, j, 0))],
        out_shape=[jax.ShapeDtypeStruct((b, s, 3 * w), F32),
                   jax.ShapeDtypeStruct((2, b, s, 3 * w), F32),
                   jax.ShapeDtypeStruct((b, s, 2 * w), F32)],
        scratch_shapes=[pltpu.VMEM((ts + 2 * _SHIFT_HALO, RWKV_PAD), F32)],
        compiler_params=_cparams(("parallel", "parallel")),
        name="rwkv_prep",
    )(rw3d, rw3d, rw3d, mup, mun, wl, bias, k_k, k_a, r_k, ones)


def _expand_bd(x_bf16, mask_bd):
    return jnp.where(mask_bd, jnp.concatenate([x_bf16] * RWKV_HEADS, axis=0), jnp.zeros((), BF16))


def _wkv_kernel(shf_ref, shr_ref, df_ref, dr_ref, yf_ref, yr_ref, hf_ref, hr_ref):
    c = WKV_CHUNK
    w = RWKV_WIDTH
    tt = shf_ref.shape[1]
    ncs = tt // c

    @pl.when(pl.program_id(1) == 0)
    def _():
        hf_ref[...] = jnp.zeros_like(hf_ref)
        hr_ref[...] = jnp.zeros_like(hr_ref)

    row_c = lax.broadcasted_iota(jnp.int32, (c, c), 0)
    col_c = lax.broadcasted_iota(jnp.int32, (c, c), 1)
    t_i = lax.broadcasted_iota(jnp.int32, (c, w), 0)
    s_i = lax.broadcasted_iota(jnp.int32, (c, w), 1) % c
    eye_c = (s_i == t_i).astype(F32)
    row_w = lax.broadcasted_iota(jnp.int32, (w, w), 0)
    col_w = lax.broadcasted_iota(jnp.int32, (w, w), 1)
    mask_bd = (row_w // c) == (col_w // HEAD_DIM)
    eye_w = row_w == col_w
    tri_incl = [jnp.where(col_c <= row_c, 1.0, 0.0).astype(BF16),
                jnp.where(col_c >= row_c, 1.0, 0.0).astype(BF16)]
    strict = [s_i < t_i, s_i > t_i]
    incl = [s_i <= t_i, s_i >= t_i]

    def bf(x):
        return x.astype(BF16)

    def bd(x):
        return _expand_bd(bf(x), mask_bd)

    chunks = []
    for ci in range(ncs):
        chunks.append((0, ci * c, shf_ref, df_ref))
        chunks.append((1, (ncs - 1 - ci) * c, shr_ref, dr_ref))

    st = []
    for d, start, sh_ref, dir_ref in chunks:
        rows = pl.ds(start, c)
        st.append(dict(d=d, rows=rows,
                       r=sh_ref[0, rows, 0:w], v=sh_ref[0, rows, w:2 * w], kk=sh_ref[0, rows, 2 * w:3 * w],
                       lw=dir_ref[0, rows, 0:w], kh=dir_ref[0, rows, w:2 * w], nb=dir_ref[0, rows, 2 * w:3 * w]))

    for q in st:
        l_hi, l_mid, l_lo = _split3(q["lw"])
        tri = tri_incl[q["d"]]
        q["cum"] = _dot(tri, l_hi) + _dot(tri, l_mid) + _dot(tri, l_lo)
    for q in st:
        cum, lw = q["cum"], q["lw"]
        tot = jnp.sum(lw, axis=0, keepdims=True)
        e_inv = jnp.exp(-cum)
        e_end = jnp.exp(tot - cum)
        q["tot"] = tot
        q["a_bar"] = q["kk"] * jnp.exp(cum - lw)
        q["r_bar"] = q["r"] * jnp.exp(cum)
        q["b_hat"] = bf(q["nb"] * e_end)
        q["k_hat"] = bf(q["kh"] * e_end)
        q["v_bf"] = bf(q["v"])
        q["v_bd"] = _expand_bd(q["v_bf"], mask_bd)
        q["lhs"] = jnp.concatenate([bf(q["a_bar"]), bf(q["r_bar"])], axis=0)
        q["rhs"] = jnp.concatenate([bd(q["nb"] * e_inv), bd(q["kh"] * e_inv)], axis=0)
    for q in st:
        sc = _dot_nt(q["lhs"], q["rhs"])
        sm, im = strict[q["d"]], incl[q["d"]]
        q["x"] = jnp.where(sm, sc[0:c, 0:w], 0.0)
        q["a_ak"] = bf(jnp.where(sm, sc[0:c, w:2 * w], 0.0))
        q["m_rb"] = bf(jnp.where(im, sc[c:2 * c, 0:w], 0.0))
        q["m_rk"] = bf(jnp.where(im, sc[c:2 * c, w:2 * w], 0.0))
        q["tinv"] = eye_c + q["x"]

    for q in st:
        q["x"] = _dot(bf(q["x"]), bd(q["x"]))
    n_lv = int(math.log2(c)) - 1
    for lv in range(n_lv):
        last = lv == n_lv - 1
        for q in st:
            xb = bd(q["x"])
            if last:
                q["tinv"] = q["tinv"] + _dot(bf(q["tinv"]), xb)
            else:
                res = _dot(jnp.concatenate([bf(q["x"]), bf(q["tinv"])], axis=0), xb)
                q["tinv"] = q["tinv"] + res[c:2 * c]
                q["x"] = res[0:c]
    for q in st:
        res = _dot(jnp.concatenate([q["a_ak"], q["m_rk"]], axis=0), q["v_bd"])
        q["akv"] = res[0:c]
        q["mrkv"] = res[c:2 * c]
    for q in st:
        tinv_bf = bf(q["tinv"])
        q["a_pr"] = bf(_dot(tinv_bf, bd(q["a_bar"])))
        q["v_pr"] = bf(_dot(tinv_bf, bd(q["akv"])))
    for q in st:
        q["r_pr"] = bf(q["r_bar"] + _dot(q["m_rb"], _expand_bd(q["a_pr"], mask_bd)))
        q["y_pr"] = _dot(q["m_rb"], _expand_bd(q["v_pr"], mask_bd)) + q["mrkv"]
    for q in st:
        q["p_bd"] = bf(jnp.where(eye_w, jnp.exp(q["tot"]), 0.0)
                       + jnp.where(mask_bd, _dot_tn(q["b_hat"], q["a_pr"]), 0.0))
        q["q_bd"] = jnp.where(mask_bd,
                              _dot_tn(jnp.concatenate([q["b_hat"], q["k_hat"]], axis=0),
                                      jnp.concatenate([q["v_pr"], q["v_bf"]], axis=0)), 0.0)

    h = [hf_ref[...], hr_ref[...]]
    y_refs = [yf_ref, yr_ref]
    for q in st:
        d = q["d"]
        res = _dot(jnp.concatenate([q["r_pr"], q["p_bd"]], axis=0), bf(h[d]))
        y_refs[d][0, q["rows"], :] = res[0:c] + q["y_pr"]
        h[d] = res[c:c + w] + q["q_bd"]
    hf_ref[...] = h[0]
    hr_ref[...] = h[1]


def _wkv(shared, dirp, tt):
    b, s, _ = shared.shape
    nt = s // tt
    w = RWKV_WIDTH
    fwd = lambda i, j: (i, j, 0)
    bwd = lambda i, j: (i, nt - 1 - j, 0)
    return pl.pallas_call(
        _wkv_kernel,
        grid=(b, nt),
        in_specs=[pl.BlockSpec((1, tt, 3 * w), fwd), pl.BlockSpec((1, tt, 3 * w), bwd),
                  pl.BlockSpec((None, 1, tt, 3 * w), lambda i, j: (0, i, j, 0)),
                  pl.BlockSpec((None, 1, tt, 3 * w), lambda i, j: (1, i, nt - 1 - j, 0))],
        out_specs=[pl.BlockSpec((1, tt, w), fwd), pl.BlockSpec((1, tt, w), bwd)],
        out_shape=[jax.ShapeDtypeStruct((b, s, w), F32), jax.ShapeDtypeStruct((b, s, w), F32)],
        scratch_shapes=[pltpu.VMEM((w, w), F32), pltpu.VMEM((w, w), F32)],
        compiler_params=_cparams(("parallel", "arbitrary")),
        name="wkv",
    )(shared, shared, dirp, dirp)


def _attn_kernel(q_ref, k_ref, v_ref, o_ref):
    tq = q_ref.shape[0]
    k = k_ref[...]
    v = v_ref[...]
    lane = lax.broadcasted_iota(jnp.int32, (tq, ATTN_KV_WIDTH), 1)
    low = lane < HEAD_DIM
    zero = jnp.zeros((), BF16)
    for j in range(ATTN_HEADS // 2):
        qb = q_ref[:, j * ATTN_KV_WIDTH:(j + 1) * ATTN_KV_WIDTH]
        outs = []
        for half in range(2):
            qm = jnp.where(low if half == 0 else ~low, qb, zero)
            s = _dot_nt(qm, k)
            m = jnp.max(s, axis=-1, keepdims=True)
            p = jnp.exp(s - m)
            l = jnp.sum(p, axis=-1, keepdims=True)
            outs.append(_dot(p.astype(BF16), v) / l)
        o_ref[:, j * ATTN_KV_WIDTH:(j + 1) * ATTN_KV_WIDTH] = jnp.where(low, outs[0], outs[1]).astype(BF16)


def _attn(q, k, v, seq, tq):
    n = q.shape[0]
    nq = seq // tq
    return pl.pallas_call(
        _attn_kernel,
        grid=(n // seq, nq),
        in_specs=[pl.BlockSpec((tq, ATTN_WIDTH), lambda b, i: (b * nq + i, 0)),
                  pl.BlockSpec((seq, ATTN_KV_WIDTH), lambda b, i: (b, 0)),
                  pl.BlockSpec((seq, ATTN_KV_WIDTH), lambda b, i: (b, 0))],
        out_specs=pl.BlockSpec((tq, ATTN_WIDTH), lambda b, i: (b * nq + i, 0)),
        out_shape=jax.ShapeDtypeStruct((n, ATTN_WIDTH), BF16),
        compiler_params=_cparams(("parallel", "parallel")),
        name="attn",
    )(q, k, v)


def _layer_norm(z, g, b):
    mu = jnp.mean(z, axis=-1, keepdims=True)
    zc = z - mu
    var = jnp.mean(zc * zc, axis=-1, keepdims=True)
    return zc * lax.rsqrt(var + LN_EPS) * g + b


def _outproj_kernel(x_ref, pool_ref, wkvf_ref, wkvr_ref, post_ref, attn_ref, wo_ref, gng_ref, gnb_ref,
                    ones_ref, lg_ref, lb_ref, wrh_ref, wrl_ref, br_ref,
                    x1_ref, x1p_ref, route_ref):
    w = RWKV_WIDTH
    ones = ones_ref[...]
    wkv = wkvf_ref[...] + wkvr_ref[...]
    inv = 1.0 / HEAD_DIM
    mu = _headsum(wkv, ones) * inv
    cen = wkv - mu
    var = _headsum(cen * cen, ones) * inv
    yr = cen * lax.rsqrt(var + GN_EPS) * gng_ref[...] + gnb_ref[...] + post_ref[:, 0:w]
    yr = yr * post_ref[:, w:2 * w]
    y = (_dot(pool_ref[...].astype(BF16), wo_ref[0:POOL_WIDTH, :])
         + _dot(yr.astype(BF16), wo_ref[POOL_WIDTH:POOL_WIDTH + w, :])
         + _dot(attn_ref[...], wo_ref[POOL_WIDTH + w:D_MODEL, :]))
    x1 = _layer_norm(DEEPNORM_ALPHA * x_ref[...] + y, lg_ref[...], lb_ref[...])
    x1_ref[...] = x1
    x1p_ref[...] = _pack_rows(x1)

    xh, xl = _split2(x1)
    logits = (_dot(xh, wrh_ref[...]) + _dot(xl, wrh_ref[...]) + _dot(xh, wrl_ref[...])) + br_ref[...]
    lane = lax.broadcasted_iota(jnp.int32, logits.shape, 1)
    neg = -jnp.inf
    big = jnp.int32(1 << 20)
    gl = jnp.where(lane < N_GROUPS, logits, neg)
    gmax = jnp.max(gl, axis=-1, keepdims=True)
    grp = jnp.min(jnp.where(gl == gmax, lane, big), axis=-1, keepdims=True)
    gw = 1.0 / jnp.sum(jnp.exp(gl - gmax), axis=-1, keepdims=True)
    in_grp = (lane >= N_GROUPS) & (lane < N_GROUPS + N_EXPERTS) & \
             ((lane - N_GROUPS) // EXPERTS_PER_GROUP == grp)
    el = jnp.where(in_grp, logits, neg)
    v1 = jnp.max(el, axis=-1, keepdims=True)
    i1 = jnp.min(jnp.where(el == v1, lane, big), axis=-1, keepdims=True)
    el2 = jnp.where(lane == i1, neg, el)
    v2 = jnp.max(el2, axis=-1, keepdims=True)
    i2 = jnp.min(jnp.where(el2 == v2, lane, big), axis=-1, keepdims=True)
    e21 = jnp.exp(v2 - v1)
    g1 = gw / (1.0 + e21)
    g2 = gw * e21 / (1.0 + e21)
    route = jnp.where(lane == 0, (i1 - N_GROUPS).astype(F32),
                      jnp.where(lane == 1, (i2 - N_GROUPS).astype(F32),
                                jnp.where(lane == 2, g1, jnp.where(lane == 3, g2, 0.0))))
    route_ref[...] = route


def _outproj(x2d, pool2d, wkvf, wkvr, post2d, attn2d, wo, gng, gnb, ones, lg, lb, wrh, wrl, br, tm):
    n = x2d.shape[0]
    w = RWKV_WIDTH
    row = lambda width: pl.BlockSpec((tm, width), lambda i: (i, 0))
    vec = lambda width: pl.BlockSpec((1, width), lambda i: (0, 0))
    return pl.pallas_call(
        _outproj_kernel,
        grid=(n // tm,),
        in_specs=[row(D_MODEL), row(POOL_WIDTH), row(w), row(w),
                  row(2 * w), row(ATTN_WIDTH),
                  pl.BlockSpec((D_MODEL, D_MODEL), lambda i: (0, 0)),
                  vec(w), vec(w),
                  pl.BlockSpec((w, w), lambda i: (0, 0)),
                  vec(D_MODEL), vec(D_MODEL),
                  pl.BlockSpec((D_MODEL, ROUTE_LANES), lambda i: (0, 0)),
                  pl.BlockSpec((D_MODEL, ROUTE_LANES), lambda i: (0, 0)),
                  vec(ROUTE_LANES)],
        out_specs=[row(D_MODEL), row(HALF_D), row(ROUTE_LANES)],
        out_shape=[jax.ShapeDtypeStruct((n, D_MODEL), F32),
                   jax.ShapeDtypeStruct((n, HALF_D), U32),
                   jax.ShapeDtypeStruct((n, ROUTE_LANES), F32)],
        compiler_params=_cparams(("parallel",)),
        name="outproj",
    )(x2d, pool2d, wkvf, wkvr, post2d, attn2d, wo, gng, gnb, ones, lg, lb, wrh, wrl, br)


def _expert_kernel(bexp_ref, nused_ref, xs_ref, wg_ref, wu_ref, wd_ref, ys_ref, wgb_ref, wub_ref, wdb_ref):
    i = pl.program_id(0)
    used = i < nused_ref[0]

    @pl.when(used & ((i == 0) | (bexp_ref[i] != bexp_ref[jnp.maximum(i - 1, 0)])))
    def _():
        wgb_ref[...] = wg_ref[0].astype(BF16)
        wub_ref[...] = wu_ref[0].astype(BF16)
        wdb_ref[...] = wd_ref[0].astype(BF16)

    @pl.when(used)
    def _():
        xa, xb = _unpack_rows(xs_ref[...])
        h1 = _dot(xa, wgb_ref[0:HALF_D, :]) + _dot(xb, wgb_ref[HALF_D:D_MODEL, :])
        h2 = _dot(xa, wub_ref[0:HALF_D, :]) + _dot(xb, wub_ref[HALF_D:D_MODEL, :])
        h = (h1 * jax.nn.sigmoid(h1) * h2).astype(BF16)
        ys_ref[...] = _pack_rows(_dot(h, wdb_ref[...]))

    @pl.when(jnp.logical_not(used))
    def _():
        ys_ref[...] = jnp.zeros_like(ys_ref)


def _experts(block_exp, n_used, xs, wg, wu, wd, layer):
    p = xs.shape[0]
    nb = p // MOE_BLOCK
    wmap = lambda i, be, nu: (layer, be[i], 0, 0)
    grid_spec = pltpu.PrefetchScalarGridSpec(
        num_scalar_prefetch=2,
        grid=(nb,),
        in_specs=[pl.BlockSpec((MOE_BLOCK, HALF_D), lambda i, be, nu: (i, 0)),
                  pl.BlockSpec((None, 1, D_MODEL, EXPERT_HIDDEN), wmap),
                  pl.BlockSpec((None, 1, D_MODEL, EXPERT_HIDDEN), wmap),
                  pl.BlockSpec((None, 1, EXPERT_HIDDEN, D_MODEL), wmap)],
        out_specs=pl.BlockSpec((MOE_BLOCK, HALF_D), lambda i, be, nu: (i, 0)),
        scratch_shapes=[pltpu.VMEM((D_MODEL, EXPERT_HIDDEN), BF16),
                        pltpu.VMEM((D_MODEL, EXPERT_HIDDEN), BF16),
                        pltpu.VMEM((EXPERT_HIDDEN, D_MODEL), BF16)],
    )
    return pl.pallas_call(
        _expert_kernel,
        grid_spec=grid_spec,
        out_shape=jax.ShapeDtypeStruct((p, HALF_D), U32),
        compiler_params=_cparams(("arbitrary",)),
        name="experts",
    )(block_exp, n_used, xs, wg, wu, wd)


SC_CORES = 2
SC_SUBCORES = 16
SC_GATHER_ROWS = 64


def _sc_gather(table, idx):
    v, d = table.shape
    b = idx.shape[0]
    nw = SC_CORES * SC_SUBCORES
    ch = SC_GATHER_ROWS
    assert b % (nw * ch * 2) == 0
    b_per_w = b // nw
    nch = b_per_w // ch
    mesh = plsc.VectorSubcoreMesh(core_axis_name="c", subcore_axis_name="s")

    @functools.partial(
        pl.kernel, mesh=mesh,
        out_type=jax.ShapeDtypeStruct((b, d), table.dtype),
        scratch_types=[pltpu.VMEM((b_per_w,), jnp.int32),
                       pltpu.VMEM((2, ch, d), table.dtype),
                       pltpu.SemaphoreType.DMA, pltpu.SemaphoreType.DMA,
                       pltpu.SemaphoreType.DMA, pltpu.SemaphoreType.DMA],
    )
    def gather_kernel(table_hbm, idx_hbm, out_hbm, idx_v, rows_v, gsem0, gsem1, wsem0, wsem1):
        wid = lax.axis_index("s") * SC_CORES + lax.axis_index("c")
        base = wid * b_per_w
        pltpu.sync_copy(idx_hbm.at[pl.ds(base, b_per_w)], idx_v)
        gsem = (gsem0, gsem1)
        wsem = (wsem0, wsem1)

        def gather(jj, slot):
            off = pl.multiple_of(jj * ch, ch)
            return pltpu.make_async_copy(table_hbm.at[idx_v.at[pl.ds(off, ch)]], rows_v.at[slot], gsem[slot])

        def write(jj, slot):
            off = pl.multiple_of(jj * ch, ch)
            return pltpu.make_async_copy(rows_v.at[slot], out_hbm.at[pl.ds(base + off, ch)], wsem[slot])

        gather(0, 0).start()

        @pl.loop(0, nch, step=2)
        def _(j):
            for slot in range(2):
                jj = j + slot

                gather(jj, slot).wait()

                @pl.when(jj >= 1)
                def _():
                    write(jj - 1, 1 - slot).wait()

                @pl.when(jj + 1 < nch)
                def _():
                    gather(jj + 1, 1 - slot).start()

                write(jj, slot).start()

        write(nch - 1, (nch - 1) % 2).wait()

    return gather_kernel(table, idx)


def _gather_rows(table, idx):
    b = idx.shape[0]
    unit = SC_CORES * SC_SUBCORES * SC_GATHER_ROWS * 2
    bp = -(-b // unit) * unit
    if bp == b:
        return _sc_gather(table, idx)
    return _sc_gather(table, jnp.pad(idx, (0, bp - b)))[:b]


def _combine_kernel(x1_ref, ya_ref, yb_ref, route_ref, lg_ref, lb_ref, o_ref):
    route = route_ref[...]
    g1 = route[:, 2:3]
    g2 = route[:, 3:4]
    a_hi, a_lo = _unpack_rows(ya_ref[...])
    b_hi, b_lo = _unpack_rows(yb_ref[...])
    m = jnp.concatenate([g1 * a_hi.astype(F32) + g2 * b_hi.astype(F32),
                         g1 * a_lo.astype(F32) + g2 * b_lo.astype(F32)], axis=1)
    o_ref[...] = _layer_norm(DEEPNORM_ALPHA * x1_ref[...] + m, lg_ref[...], lb_ref[...])


def _combine(x1, yab, route, lg, lb, tm):
    n = x1.shape[0]
    nt = n // tm
    row = lambda width: pl.BlockSpec((tm, width), lambda i: (i, 0))
    vec = lambda width: pl.BlockSpec((1, width), lambda i: (0, 0))
    return pl.pallas_call(
        _combine_kernel,
        grid=(nt,),
        in_specs=[row(D_MODEL), row(HALF_D), pl.BlockSpec((tm, HALF_D), lambda i: (i + nt, 0)),
                  row(ROUTE_LANES), vec(D_MODEL), vec(D_MODEL)],
        out_specs=row(D_MODEL),
        out_shape=jax.ShapeDtypeStruct((n, D_MODEL), F32),
        compiler_params=_cparams(("parallel",)),
        name="combine",
    )(x1, yab, yab, route, lg, lb)


def _q_perm():
    order = [h for j in range(ATTN_HEADS // 2) for h in (j, j + ATTN_HEADS // 2)]
    return jnp.concatenate([jnp.arange(h * HEAD_DIM, (h + 1) * HEAD_DIM) for h in order])


def _block_diag(blocks):
    n = len(blocks)
    r, c = blocks[0].shape
    out = jnp.zeros((n * r, n * c), blocks[0].dtype)
    for i, blk in enumerate(blocks):
        out = out.at[i * r:(i + 1) * r, i * c:(i + 1) * c].set(blk)
    return out


def _rope_tables(seq):
    rows = seq // GRID_W
    row_id = jnp.repeat(jnp.arange(rows), GRID_W).astype(F32)
    col_id = jnp.tile(jnp.arange(GRID_W), rows).astype(F32)
    half = HEAD_DIM // 2
    inv_freq = ROPE_THETA ** (-jnp.arange(0, half, 2, dtype=F32) / half)
    ang_r = row_id[:, None] * inv_freq
    ang_c = col_id[:, None] * inv_freq
    ang = jnp.concatenate([ang_r, ang_r, ang_c, ang_c], -1)
    sign = jnp.where((jnp.arange(HEAD_DIM) // 16) % 2 == 0, -1.0, 1.0).astype(F32)
    cos8 = jnp.tile(jnp.cos(ang), (1, ATTN_HEADS))
    sin8 = jnp.tile(jnp.sin(ang) * sign, (1, ATTN_HEADS))
    return cos8, sin8


def _dispatch(route, n):
    m = n * 2
    experts = jnp.arange(N_EXPERTS, dtype=jnp.int32)
    e_flat = route[:, 0:2].astype(jnp.int32).reshape(m)
    pair = jnp.arange(m, dtype=jnp.int32)
    e_sorted, order = lax.sort((e_flat, pair), num_keys=1, is_stable=True)
    onehot_sorted = e_sorted[:, None] == experts[None, :]
    counts = jnp.sum(onehot_sorted, axis=0, dtype=jnp.int32)
    padded = (counts + MOE_BLOCK - 1) // MOE_BLOCK * MOE_BLOCK
    start = jnp.cumsum(counts) - counts
    ends_p = jnp.cumsum(padded)
    pstart = ends_p - padded
    shift = jnp.sum(jnp.where(onehot_sorted, (pstart - start)[None, :], 0), axis=1, dtype=jnp.int32)
    dest = pair + shift
    _, pos = lax.sort((order, dest), num_keys=1)
    n_blocks = -(-(m + N_EXPERTS * (MOE_BLOCK - 1)) // MOE_BLOCK)
    block_start = jnp.arange(n_blocks, dtype=jnp.int32) * MOE_BLOCK
    block_exp = jnp.minimum(jnp.sum(ends_p[None, :] <= block_start[:, None], axis=1, dtype=jnp.int32),
                            N_EXPERTS - 1)
    first = block_start - (pstart - start)[block_exp]
    src_pair = jnp.clip(first[:, None] + jnp.arange(MOE_BLOCK, dtype=jnp.int32)[None, :], 0, m - 1)
    row_tok = order[src_pair.reshape(-1)] // 2
    n_used = (ends_p[-1] // MOE_BLOCK).astype(jnp.int32).reshape(1)
    return row_tok, pos.reshape(n, 2), block_exp, n_used


def kernel(x, w_in, mu_prev, mu_next, pool_w, pool_scale, rw_w0, rw_w_up, rw_a0, rw_a_up, rw_g_up, rw_k_k, rw_k_a, rw_r_k, rw_gn_g, rw_gn_b, q_norm, k_norm, w_o, ln1_g, ln1_b, router_group, router_group_b, router_expert, router_expert_b, exp_gate, exp_up, exp_down, ln2_g, ln2_b):
    b, s, d = x.shape
    n = b * s
    w = RWKV_WIDTH
    tm = min(512, s)
    qperm = _q_perm()
    cos8, sin8 = _rope_tables(s)
    ones_q = _block_diag([jnp.ones((HEAD_DIM, HEAD_DIM), BF16)] * ATTN_HEADS)
    ones_r = ones_q[0:w, 0:w]
    a_end = POOL_WIDTH
    b_end = POOL_WIDTH + RWKV_IN

    xc = x.reshape(n, d)
    for l in range(DEPTH):
        wq = w_in[l][:, b_end:b_end + ATTN_WIDTH][:, qperm]
        w_proj = jnp.concatenate(
            [w_in[l][:, :a_end], w_in[l][:, a_end:b_end],
             jnp.zeros((d, RWKV_PAD - RWKV_IN), F32), wq, w_in[l][:, b_end + ATTN_WIDTH:]],
            axis=1).astype(BF16)
        pool_bd = _block_diag([pool_w[l, g] for g in range(len(POOL_WINDOWS))]).astype(BF16)
        pad_vec = jnp.zeros((RWKV_PAD - RWKV_IN,), F32)
        mup = jnp.concatenate([mu_prev[l], pad_vec]).reshape(1, RWKV_PAD)
        mun = jnp.concatenate([mu_next[l], pad_vec]).reshape(1, RWKV_PAD)
        w_lora = jnp.zeros((w, 5 * w), F32)
        for di in range(2):
            w_lora = w_lora.at[di * DECAY_LORA:(di + 1) * DECAY_LORA, di * w:(di + 1) * w].set(rw_w_up[l, di])
            o = 2 * DECAY_LORA
            w_lora = w_lora.at[o + di * AAA_LORA:o + (di + 1) * AAA_LORA, (2 + di) * w:(3 + di) * w].set(rw_a_up[l, di])
        o = 2 * DECAY_LORA + 2 * AAA_LORA
        w_lora = w_lora.at[o:o + GATE_LORA, 4 * w:5 * w].set(rw_g_up[l]).astype(BF16)
        lora_bias = jnp.concatenate([rw_w0[l, 0], rw_w0[l, 1], rw_a0[l, 0], rw_a0[l, 1],
                                     jnp.zeros((w,), F32)]).reshape(1, 5 * w)
        qg = jnp.tile(q_norm[l], ATTN_HEADS).reshape(1, ATTN_WIDTH)
        kg = jnp.tile(k_norm[l], 2).reshape(1, ATTN_KV_WIDTH)
        wo_attn = w_o[l][POOL_WIDTH + w:][qperm]
        wo = jnp.concatenate([w_o[l][:POOL_WIDTH + w], wo_attn], axis=0).astype(BF16)
        wr = jnp.concatenate([router_group[l], router_expert[l],
                              jnp.zeros((d, ROUTE_LANES - N_GROUPS - N_EXPERTS), F32)], axis=1)
        wrh = wr.astype(BF16)
        wrl = (wr - wrh.astype(F32)).astype(BF16)
        br = jnp.concatenate([router_group_b[l], router_expert_b[l],
                              jnp.zeros((ROUTE_LANES - N_GROUPS - N_EXPERTS,), F32)]).reshape(1, ROUTE_LANES)

        pool_in, rw_in, qh, kh, vh = _proj(xc, w_proj, cos8, sin8, qg, kg, ones_q, s, tm)
        y_pool = _pool(pool_in.reshape(b, s, POOL_WIDTH), pool_bd, pool_scale[l].reshape(1, POOL_WIDTH))
        shared, dirp, post = _rwkv_prep(rw_in.reshape(b, s, RWKV_PAD), mup, mun, w_lora, lora_bias,
                                        rw_k_k[l].reshape(1, w), rw_k_a[l].reshape(1, w),
                                        rw_r_k[l].reshape(1, w), ones_r, tm)
        wkv_f, wkv_r = _wkv(shared, dirp, min(256, s))
        y_attn = _attn(qh, kh, vh, s, min(256, s))
        x1, x1p, route = _outproj(xc, y_pool.reshape(n, POOL_WIDTH), wkv_f.reshape(n, w), wkv_r.reshape(n, w),
                                  post.reshape(n, 2 * w), y_attn, wo,
                                  rw_gn_g[l].reshape(1, w), rw_gn_b[l].reshape(1, w), ones_r,
                                  ln1_g[l].reshape(1, d), ln1_b[l].reshape(1, d), wrh, wrl, br, tm)

        row_tok, pos, block_exp, n_used = _dispatch(route, n)
        xs = _gather_rows(x1p, row_tok)
        ys = _experts(block_exp, n_used, xs, exp_gate, exp_up, exp_down, l)
        yab = _gather_rows(ys, pos.T.reshape(2 * n))
        xc = _combine(x1, yab, route, ln2_g[l].reshape(1, d), ln2_b[l].reshape(1, d), tm)
    return xc.reshape(b, s, d)
```

```python
import functools
import math

import jax
import jax.numpy as jnp
from jax import lax
from jax.experimental import pallas as pl
from jax.experimental.pallas import tpu as pltpu
from jax.experimental.pallas import tpu_sc as plsc

F32 = jnp.float32
BF16 = jnp.bfloat16

D_MODEL = 1024
DEPTH = 4
GRID_W = 64
HEAD_DIM = 64
POOL_WIDTH = 256
POOL_WINDOWS = (2, 4, 8, 16)
POOL_GROUP = 64
RWKV_WIDTH = 256
RWKV_HEADS = 4
DECAY_LORA = 32
AAA_LORA = 32
GATE_LORA = 64
GN_EPS = 64e-5
RWKV_IN = 960
RWKV_PAD = 1024
ATTN_WIDTH = 512
ATTN_HEADS = 8
ATTN_KV_WIDTH = 128
ATTN_IN = ATTN_WIDTH + 2 * ATTN_KV_WIDTH
ROPE_THETA = 10000.0
QK_EPS = 1e-6
N_GROUPS = 4
EXPERTS_PER_GROUP = 8
N_EXPERTS = 32
EXPERT_HIDDEN = 512
MOE_BLOCK = 512
DEEPNORM_ALPHA = float((2 * DEPTH) ** 0.25)
LN_EPS = 1e-5
PROJ_WIDTH = POOL_WIDTH + RWKV_PAD + ATTN_IN
ROUTE_LANES = 128
WKV_CHUNK = 64
VMEM_LIMIT = 48 * 1024 * 1024


def _cparams(sem):
    return pltpu.CompilerParams(dimension_semantics=sem, vmem_limit_bytes=VMEM_LIMIT)


def _dot(a, b):
    return jnp.dot(a, b, preferred_element_type=F32)


def _dot_nt(a, b):
    return lax.dot_general(a, b, (((1,), (1,)), ((), ())), preferred_element_type=F32)


def _dot_tn(a, b):
    return lax.dot_general(a, b, (((0,), (0,)), ((), ())), preferred_element_type=F32)


def _split2(x):
    hi = x.astype(BF16)
    lo = (x - hi.astype(F32)).astype(BF16)
    return hi, lo


def _split3(x):
    hi = x.astype(BF16)
    r1 = x - hi.astype(F32)
    mid = r1.astype(BF16)
    lo = (r1 - mid.astype(F32)).astype(BF16)
    return hi, mid, lo


HALF_D = D_MODEL // 2
U32 = jnp.uint32


def _pack_rows(x):
    hi = lax.bitcast_convert_type(x[:, :HALF_D].astype(BF16).astype(F32), U32)
    lo = lax.bitcast_convert_type(x[:, HALF_D:].astype(BF16).astype(F32), U32)
    return hi | (lo >> 16)


def _unpack_rows(u):
    a = lax.bitcast_convert_type(u & jnp.uint32(0xFFFF0000), F32).astype(BF16)
    b = lax.bitcast_convert_type(u << 16, F32).astype(BF16)
    return a, b


def _headsum(x, ones_bf16):
    hi, lo = _split2(x)
    return _dot(hi, ones_bf16) + _dot(lo, ones_bf16)


def _rope(x, cos, sin_signed):
    n = x.shape[-1]
    lane = lax.broadcasted_iota(jnp.int32, x.shape, 1)
    first = (lane // 16) % 2 == 0
    partner = jnp.where(first, pltpu.roll(x, n - 16, 1), pltpu.roll(x, 16, 1))
    return x * cos + partner * sin_signed


def _proj_kernel(x_ref, w_ref, cos_ref, sin_ref, qg_ref, kg_ref, onesq_ref,
                 pool_ref, rw_ref, q_ref, k_ref, v_ref):
    xb = x_ref[...].astype(BF16)
    pool_ref[...] = _dot(xb, w_ref[:, 0:POOL_WIDTH])
    rw_ref[...] = _dot(xb, w_ref[:, POOL_WIDTH:POOL_WIDTH + RWKV_PAD])
    at = _dot(xb, w_ref[:, POOL_WIDTH + RWKV_PAD:PROJ_WIDTH])

    q = at[:, 0:ATTN_WIDTH]
    k = at[:, ATTN_WIDTH:ATTN_WIDTH + ATTN_KV_WIDTH]
    v = at[:, ATTN_WIDTH + ATTN_KV_WIDTH:ATTN_IN]
    ones_q = onesq_ref[...]
    ones_k = onesq_ref[0:ATTN_KV_WIDTH, 0:ATTN_KV_WIDTH]
    inv = 1.0 / HEAD_DIM
    qn = q * lax.rsqrt(_headsum(q * q, ones_q) * inv + QK_EPS) * qg_ref[...]
    kn = k * lax.rsqrt(_headsum(k * k, ones_k) * inv + QK_EPS) * kg_ref[...]
    cos = cos_ref[...]
    sin = sin_ref[...]
    q_ref[...] = (_rope(qn, cos, sin) * (HEAD_DIM ** -0.5 * math.log2(math.e))).astype(BF16)
    k_ref[...] = _rope(kn, cos[:, 0:ATTN_KV_WIDTH], sin[:, 0:ATTN_KV_WIDTH]).astype(BF16)
    v_ref[...] = v.astype(BF16)


def _proj(x2d, w, cos8, sin8, qg, kg, ones_q, seq, tm):
    n = x2d.shape[0]
    nts = seq // tm
    row = lambda width: pl.BlockSpec((tm, width), lambda i: (i, 0))
    const = lambda r, c: pl.BlockSpec((r, c), lambda i: (0, 0))
    return pl.pallas_call(
        _proj_kernel,
        grid=(n // tm,),
        in_specs=[row(D_MODEL), const(D_MODEL, PROJ_WIDTH),
                  pl.BlockSpec((tm, ATTN_WIDTH), lambda i: (i % nts, 0)),
                  pl.BlockSpec((tm, ATTN_WIDTH), lambda i: (i % nts, 0)),
                  const(1, ATTN_WIDTH), const(1, ATTN_KV_WIDTH), const(ATTN_WIDTH, ATTN_WIDTH)],
        out_specs=[row(POOL_WIDTH), row(RWKV_PAD), row(ATTN_WIDTH), row(ATTN_KV_WIDTH), row(ATTN_KV_WIDTH)],
        out_shape=[jax.ShapeDtypeStruct((n, POOL_WIDTH), F32),
                   jax.ShapeDtypeStruct((n, RWKV_PAD), F32),
                   jax.ShapeDtypeStruct((n, ATTN_WIDTH), BF16),
                   jax.ShapeDtypeStruct((n, ATTN_KV_WIDTH), BF16),
                   jax.ShapeDtypeStruct((n, ATTN_KV_WIDTH), BF16)],
        compiler_params=_cparams(("parallel",)),
        name="proj",
    )(x2d, w, cos8, sin8, qg, kg, ones_q)


_POOL_PAD = 16


def _pool_kernel(u_ref, w_ref, scale_ref, o_ref, pad_ref):
    s = u_ref.shape[1]
    u = u_ref[0]
    zeros = jnp.zeros((_POOL_PAD, POOL_WIDTH), F32)
    pad_ref[0:_POOL_PAD, :] = zeros
    pad_ref[_POOL_PAD + s:2 * _POOL_PAD + s, :] = zeros
    pad_ref[_POOL_PAD:_POOL_PAD + s, :] = u

    def shifted(k):
        return pad_ref[_POOL_PAD + k:_POOL_PAD + k + s, :]

    s2 = u + shifted(-1)
    s4 = s2 + shifted(-2) + shifted(1)
    s8 = s4 + shifted(-4) + shifted(-3) + shifted(2) + shifted(3)
    s16 = s8
    for k in (-8, -7, -6, -5, 4, 5, 6, 7):
        s16 = s16 + shifted(k)

    t = lax.broadcasted_iota(jnp.int32, (s, POOL_WIDTH), 0)
    grp = lax.broadcasted_iota(jnp.int32, (s, POOL_WIDTH), 1) // POOL_GROUP
    half = jnp.where(grp == 0, 1, jnp.where(grp == 1, 2, jnp.where(grp == 2, 4, 8)))
    cnt = (jnp.minimum(t + half, s) - jnp.maximum(t - half, 0)).astype(F32)
    tot = jnp.where(grp == 0, s2, jnp.where(grp == 1, s4, jnp.where(grp == 2, s8, s16)))
    d = tot / cnt - u
    o_ref[0] = _dot(d.astype(BF16), w_ref[...]) * scale_ref[...]


def _pool(u3d, w_bd, scale):
    b, s, _ = u3d.shape
    return pl.pallas_call(
        _pool_kernel,
        grid=(b,),
        in_specs=[pl.BlockSpec((1, s, POOL_WIDTH), lambda i: (i, 0, 0)),
                  pl.BlockSpec((POOL_WIDTH, POOL_WIDTH), lambda i: (0, 0)),
                  pl.BlockSpec((1, POOL_WIDTH), lambda i: (0, 0))],
        out_specs=pl.BlockSpec((1, s, POOL_WIDTH), lambda i: (i, 0, 0)),
        out_shape=jax.ShapeDtypeStruct((b, s, POOL_WIDTH), F32),
        scratch_shapes=[pltpu.VMEM((s + 2 * _POOL_PAD, POOL_WIDTH), F32)],
        compiler_params=_cparams(("parallel",)),
        name="pool",
    )(u3d, w_bd, scale)


_SHIFT_HALO = 8


def _rwkv_prep_kernel(cur_ref, prev_ref, next_ref, mup_ref, mun_ref, wl_ref, bias_ref,
                      kk_ref, ka_ref, rk_ref, ones_ref,
                      sh_ref, dir_ref, post_ref, pad_ref):
    ts = cur_ref.shape[1]
    ti = pl.program_id(1)
    nt = pl.num_programs(1)
    cur = cur_ref[0]
    pad_ref[0:_SHIFT_HALO, :] = jnp.where(ti > 0, prev_ref[0], 0.0)
    pad_ref[_SHIFT_HALO:_SHIFT_HALO + ts, :] = cur
    pad_ref[_SHIFT_HALO + ts:2 * _SHIFT_HALO + ts, :] = jnp.where(ti < nt - 1, next_ref[0], 0.0)
    prev = pad_ref[_SHIFT_HALO - 1:_SHIFT_HALO - 1 + ts, :]
    nxt = pad_ref[_SHIFT_HALO + 1:_SHIFT_HALO + 1 + ts, :]
    f = cur + mup_ref[...] * (prev - cur) + mun_ref[...] * (nxt - cur)

    w = RWKV_WIDTH
    r = f[:, 0:w]
    k = f[:, w:2 * w]
    v = f[:, 2 * w:3 * w]
    lora = f[:, 3 * w:4 * w]
    lane = lax.broadcasted_iota(jnp.int32, lora.shape, 1)
    z = jnp.where(lane < 2 * DECAY_LORA, jnp.tanh(lora),
                  jnp.where(lane < 2 * DECAY_LORA + 2 * AAA_LORA, lora,
                            jnp.where(lane < 2 * DECAY_LORA + 2 * AAA_LORA + GATE_LORA,
                                      jax.nn.sigmoid(lora), 0.0)))
    up = _dot(z.astype(BF16), wl_ref[...]) + bias_ref[...]
    ones = ones_ref[...]

    kk0 = k * kk_ref[...]
    nrm = jnp.sqrt(_headsum(kk0 * kk0, ones))
    kk = kk0 / jnp.maximum(nrm, 1e-12)
    sh_ref[0, :, 0:w] = r
    sh_ref[0, :, w:2 * w] = v
    sh_ref[0, :, 2 * w:3 * w] = kk

    ksum = jnp.zeros_like(k)
    for di in range(2):
        logw = -math.exp(-0.5) * jax.nn.sigmoid(up[:, di * w:(di + 1) * w])
        a = jax.nn.sigmoid(up[:, (2 + di) * w:(3 + di) * w])
        kh = k * (1.0 + (a - 1.0) * ka_ref[...])
        ksum = ksum + kh
        dir_ref[di, 0, :, 0:w] = logw
        dir_ref[di, 0, :, w:2 * w] = kh
        dir_ref[di, 0, :, 2 * w:3 * w] = -(a * kk)
    bonus = _headsum(r * ksum * rk_ref[...], ones) * v
    post_ref[0, :, 0:w] = bonus
    post_ref[0, :, w:2 * w] = up[:, 4 * w:5 * w]


def _rwkv_prep(rw3d, mup, mun, wl, bias, k_k, k_a, r_k, ones, ts):
    b, s, _ = rw3d.shape
    nt = s // ts
    hb = ts // _SHIFT_HALO
    nhb = s // _SHIFT_HALO
    w = RWKV_WIDTH
    vec = lambda width: pl.BlockSpec((1, width), lambda i, j: (0, 0))
    return pl.pallas_call(
        _rwkv_prep_kernel,
        grid=(b, nt),
        in_specs=[pl.BlockSpec((1, ts, RWKV_PAD), lambda i, j: (i, j, 0)),
                  pl.BlockSpec((1, _SHIFT_HALO, RWKV_PAD),
                               lambda i, j: (i, jnp.maximum(j * hb - 1, 0), 0)),
                  pl.BlockSpec((1, _SHIFT_HALO, RWKV_PAD),
                               lambda i, j: (i, jnp.minimum((j + 1) * hb, nhb - 1), 0)),
                  vec(RWKV_PAD), vec(RWKV_PAD),
                  pl.BlockSpec((w, 5 * w), lambda i, j: (0, 0)),
                  vec(5 * w), vec(w), vec(w), vec(w),
                  pl.BlockSpec((w, w), lambda i, j: (0, 0))],
        out_specs=[pl.BlockSpec((1, ts, 3 * w), lambda i, j: (i, j, 0)),
                   pl.BlockSpec((2, 1, ts, 3 * w), lambda i, j: (0, i, j, 0)),
                   pl.BlockSpec((1, ts, 2 * w), lambda i, j: (i, j, 0))],
        out_shape=[jax.ShapeDtypeStruct((b, s, 3 * w), F32),
                   jax.ShapeDtypeStruct((2, b, s, 3 * w), F32),
                   jax.ShapeDtypeStruct((b, s, 2 * w), F32)],
        scratch_shapes=[pltpu.VMEM((ts + 2 * _SHIFT_HALO, RWKV_PAD), F32)],
        compiler_params=_cparams(("parallel", "parallel")),
        name="rwkv_prep",
    )(rw3d, rw3d, rw3d, mup, mun, wl, bias, k_k, k_a, r_k, ones)


def _expand_bd(x_bf16, mask_bd):
    return jnp.where(mask_bd, jnp.concatenate([x_bf16] * RWKV_HEADS, axis=0), jnp.zeros((), BF16))


def _wkv_kernel(shf_ref, shr_ref, df_ref, dr_ref, yf_ref, yr_ref, hf_ref, hr_ref):
    c = WKV_CHUNK
    w = RWKV_WIDTH
    tt = shf_ref.shape[1]
    ncs = tt // c

    @pl.when(pl.program_id(1) == 0)
    def _():
        hf_ref[...] = jnp.zeros_like(hf_ref)
        hr_ref[...] = jnp.zeros_like(hr_ref)

    row_c = lax.broadcasted_iota(jnp.int32, (c, c), 0)
    col_c = lax.broadcasted_iota(jnp.int32, (c, c), 1)
    t_i = lax.broadcasted_iota(jnp.int32, (c, w), 0)
    s_i = lax.broadcasted_iota(jnp.int32, (c, w), 1) % c
    eye_c = (s_i == t_i).astype(F32)
    row_w = lax.broadcasted_iota(jnp.int32, (w, w), 0)
    col_w = lax.broadcasted_iota(jnp.int32, (w, w), 1)
    mask_bd = (row_w // c) == (col_w // HEAD_DIM)
    eye_w = row_w == col_w
    tri_incl = [jnp.where(col_c <= row_c, 1.0, 0.0).astype(BF16),
                jnp.where(col_c >= row_c, 1.0, 0.0).astype(BF16)]
    strict = [s_i < t_i, s_i > t_i]
    incl = [s_i <= t_i, s_i >= t_i]

    def bf(x):
        return x.astype(BF16)

    def bd(x):
        return _expand_bd(bf(x), mask_bd)

    chunks = []
    for ci in range(ncs):
        chunks.append((0, ci * c, shf_ref, df_ref))
        chunks.append((1, (ncs - 1 - ci) * c, shr_ref, dr_ref))

    st = []
    for d, start, sh_ref, dir_ref in chunks:
        rows = pl.ds(start, c)
        st.append(dict(d=d, rows=rows,
                       r=sh_ref[0, rows, 0:w], v=sh_ref[0, rows, w:2 * w], kk=sh_ref[0, rows, 2 * w:3 * w],
                       lw=dir_ref[0, rows, 0:w], kh=dir_ref[0, rows, w:2 * w], nb=dir_ref[0, rows, 2 * w:3 * w]))

    for q in st:
        l_hi, l_mid, l_lo = _split3(q["lw"])
        tri = tri_incl[q["d"]]
        q["cum"] = _dot(tri, l_hi) + _dot(tri, l_mid) + _dot(tri, l_lo)
    for q in st:
        cum, lw = q["cum"], q["lw"]
        tot = jnp.sum(lw, axis=0, keepdims=True)
        e_inv = jnp.exp(-cum)
        e_end = jnp.exp(tot - cum)
        q["tot"] = tot
        q["a_bar"] = q["kk"] * jnp.exp(cum - lw)
        q["r_bar"] = q["r"] * jnp.exp(cum)
        q["b_hat"] = bf(q["nb"] * e_end)
        q["k_hat"] = bf(q["kh"] * e_end)
        q["v_bf"] = bf(q["v"])
        q["v_bd"] = _expand_bd(q["v_bf"], mask_bd)
        q["lhs"] = jnp.concatenate([bf(q["a_bar"]), bf(q["r_bar"])], axis=0)
        q["rhs"] = jnp.concatenate([bd(q["nb"] * e_inv), bd(q["kh"] * e_inv)], axis=0)
    for q in st:
        sc = _dot_nt(q["lhs"], q["rhs"])
        sm, im = strict[q["d"]], incl[q["d"]]
        q["x"] = jnp.where(sm, sc[0:c, 0:w], 0.0)
        q["a_ak"] = bf(jnp.where(sm, sc[0:c, w:2 * w], 0.0))
        q["m_rb"] = bf(jnp.where(im, sc[c:2 * c, 0:w], 0.0))
        q["m_rk"] = bf(jnp.where(im, sc[c:2 * c, w:2 * w], 0.0))
        q["tinv"] = eye_c + q["x"]

    for q in st:
        q["x"] = _dot(bf(q["x"]), bd(q["x"]))
    n_lv = int(math.log2(c)) - 1
    for lv in range(n_lv):
        last = lv == n_lv - 1
        for q in st:
            xb = bd(q["x"])
            if last:
                q["tinv"] = q["tinv"] + _dot(bf(q["tinv"]), xb)
            else:
                res = _dot(jnp.concatenate([bf(q["x"]), bf(q["tinv"])], axis=0), xb)
                q["tinv"] = q["tinv"] + res[c:2 * c]
                q["x"] = res[0:c]
    for q in st:
        res = _dot(jnp.concatenate([q["a_ak"], q["m_rk"]], axis=0), q["v_bd"])
        q["akv"] = res[0:c]
        q["mrkv"] = res[c:2 * c]
    for q in st:
        tinv_bf = bf(q["tinv"])
        q["a_pr"] = bf(_dot(tinv_bf, bd(q["a_bar"])))
        q["v_pr"] = bf(_dot(tinv_bf, bd(q["akv"])))
    for q in st:
        q["r_pr"] = bf(q["r_bar"] + _dot(q["m_rb"], _expand_bd(q["a_pr"], mask_bd)))
        q["y_pr"] = _dot(q["m_rb"], _expand_bd(q["v_pr"], mask_bd)) + q["mrkv"]
    for q in st:
        q["p_bd"] = bf(jnp.where(eye_w, jnp.exp(q["tot"]), 0.0)
                       + jnp.where(mask_bd, _dot_tn(q["b_hat"], q["a_pr"]), 0.0))
        q["q_bd"] = jnp.where(mask_bd,
                              _dot_tn(jnp.concatenate([q["b_hat"], q["k_hat"]], axis=0),
                                      jnp.concatenate([q["v_pr"], q["v_bf"]], axis=0)), 0.0)

    h = [hf_ref[...], hr_ref[...]]
    y_refs = [yf_ref, yr_ref]
    for q in st:
        d = q["d"]
        res = _dot(jnp.concatenate([q["r_pr"], q["p_bd"]], axis=0), bf(h[d]))
        y_refs[d][0, q["rows"], :] = res[0:c] + q["y_pr"]
        h[d] = res[c:c + w] + q["q_bd"]
    hf_ref[...] = h[0]
    hr_ref[...] = h[1]


def _wkv(shared, dirp, tt):
    b, s, _ = shared.shape
    nt = s // tt
    w = RWKV_WIDTH
    fwd = lambda i, j: (i, j, 0)
    bwd = lambda i, j: (i, nt - 1 - j, 0)
    return pl.pallas_call(
        _wkv_kernel,
        grid=(b, nt),
        in_specs=[pl.BlockSpec((1, tt, 3 * w), fwd), pl.BlockSpec((1, tt, 3 * w), bwd),
                  pl.BlockSpec((None, 1, tt, 3 * w), lambda i, j: (0, i, j, 0)),
                  pl.BlockSpec((None, 1, tt, 3 * w), lambda i, j: (1, i, nt - 1 - j, 0))],
        out_specs=[pl.BlockSpec((1, tt, w), fwd), pl.BlockSpec((1, tt, w), bwd)],
        out_shape=[jax.ShapeDtypeStruct((b, s, w), F32), jax.ShapeDtypeStruct((b, s, w), F32)],
        scratch_shapes=[pltpu.VMEM((w, w), F32), pltpu.VMEM((w, w), F32)],
        compiler_params=_cparams(("parallel", "arbitrary")),
        name="wkv",
    )(shared, shared, dirp, dirp)


def _attn_kernel(q_ref, k_ref, v_ref, o_ref):
    tq = q_ref.shape[0]
    k = k_ref[...]
    v = v_ref[...]
    lane = lax.broadcasted_iota(jnp.int32, (tq, ATTN_KV_WIDTH), 1)
    low = lane < HEAD_DIM
    zero = jnp.zeros((), BF16)
    for j in range(ATTN_HEADS // 2):
        qb = q_ref[:, j * ATTN_KV_WIDTH:(j + 1) * ATTN_KV_WIDTH]
        outs = []
        for half in range(2):
            qm = jnp.where(low if half == 0 else ~low, qb, zero)
            s = _dot_nt(qm, k)
            m = jnp.max(s, axis=-1, keepdims=True)
            p = jnp.exp2(s - m)
            l = jnp.sum(p, axis=-1, keepdims=True)
            outs.append(_dot(p.astype(BF16), v) / l)
        o_ref[:, j * ATTN_KV_WIDTH:(j + 1) * ATTN_KV_WIDTH] = jnp.where(low, outs[0], outs[1]).astype(BF16)


def _attn(q, k, v, seq, tq):
    n = q.shape[0]
    nq = seq // tq
    return pl.pallas_call(
        _attn_kernel,
        grid=(n // seq, nq),
        in_specs=[pl.BlockSpec((tq, ATTN_WIDTH), lambda b, i: (b * nq + i, 0)),
                  pl.BlockSpec((seq, ATTN_KV_WIDTH), lambda b, i: (b, 0)),
                  pl.BlockSpec((seq, ATTN_KV_WIDTH), lambda b, i: (b, 0))],
        out_specs=pl.BlockSpec((tq, ATTN_WIDTH), lambda b, i: (b * nq + i, 0)),
        out_shape=jax.ShapeDtypeStruct((n, ATTN_WIDTH), BF16),
        compiler_params=_cparams(("parallel", "parallel")),
        name="attn",
    )(q, k, v)


def _layer_norm(z, g, b):
    mu = jnp.mean(z, axis=-1, keepdims=True)
    zc = z - mu
    var = jnp.mean(zc * zc, axis=-1, keepdims=True)
    return zc * lax.rsqrt(var + LN_EPS) * g + b


def _outproj_kernel(x_ref, pool_ref, wkvf_ref, wkvr_ref, post_ref, attn_ref, wo_ref, gng_ref, gnb_ref,
                    ones_ref, lg_ref, lb_ref, wrh_ref, wrl_ref, br_ref,
                    x1_ref, x1p_ref, route_ref):
    w = RWKV_WIDTH
    ones = ones_ref[...]
    wkv = wkvf_ref[...] + wkvr_ref[...]
    inv = 1.0 / HEAD_DIM
    mu = _headsum(wkv, ones) * inv
    cen = wkv - mu
    var = _headsum(cen * cen, ones) * inv
    yr = cen * lax.rsqrt(var + GN_EPS) * gng_ref[...] + gnb_ref[...] + post_ref[:, 0:w]
    yr = yr * post_ref[:, w:2 * w]
    y = (_dot(pool_ref[...].astype(BF16), wo_ref[0:POOL_WIDTH, :])
         + _dot(yr.astype(BF16), wo_ref[POOL_WIDTH:POOL_WIDTH + w, :])
         + _dot(attn_ref[...], wo_ref[POOL_WIDTH + w:D_MODEL, :]))
    x1 = _layer_norm(DEEPNORM_ALPHA * x_ref[...] + y, lg_ref[...], lb_ref[...])
    x1_ref[...] = x1
    x1p_ref[...] = _pack_rows(x1)

    xh, xl = _split2(x1)
    logits = (_dot(xh, wrh_ref[...]) + _dot(xl, wrh_ref[...]) + _dot(xh, wrl_ref[...])) + br_ref[...]
    lane = lax.broadcasted_iota(jnp.int32, logits.shape, 1)
    neg = -jnp.inf
    big = jnp.int32(1 << 20)
    gl = jnp.where(lane < N_GROUPS, logits, neg)
    gmax = jnp.max(gl, axis=-1, keepdims=True)
    grp = jnp.min(jnp.where(gl == gmax, lane, big), axis=-1, keepdims=True)
    gw = 1.0 / jnp.sum(jnp.exp(gl - gmax), axis=-1, keepdims=True)
    in_grp = (lane >= N_GROUPS) & (lane < N_GROUPS + N_EXPERTS) & \
             ((lane - N_GROUPS) // EXPERTS_PER_GROUP == grp)
    el = jnp.where(in_grp, logits, neg)
    v1 = jnp.max(el, axis=-1, keepdims=True)
    i1 = jnp.min(jnp.where(el == v1, lane, big), axis=-1, keepdims=True)
    el2 = jnp.where(lane == i1, neg, el)
    v2 = jnp.max(el2, axis=-1, keepdims=True)
    i2 = jnp.min(jnp.where(el2 == v2, lane, big), axis=-1, keepdims=True)
    e21 = jnp.exp(v2 - v1)
    g1 = gw / (1.0 + e21)
    g2 = gw * e21 / (1.0 + e21)
    route = jnp.where(lane == 0, (i1 - N_GROUPS).astype(F32),
                      jnp.where(lane == 1, (i2 - N_GROUPS).astype(F32),
                                jnp.where(lane == 2, g1, jnp.where(lane == 3, g2, 0.0))))
    route_ref[...] = route


def _outproj(x2d, pool2d, wkvf, wkvr, post2d, attn2d, wo, gng, gnb, ones, lg, lb, wrh, wrl, br, tm):
    n = x2d.shape[0]
    w = RWKV_WIDTH
    row = lambda width: pl.BlockSpec((tm, width), lambda i: (i, 0))
    vec = lambda width: pl.BlockSpec((1, width), lambda i: (0, 0))
    return pl.pallas_call(
        _outproj_kernel,
        grid=(n // tm,),
        in_specs=[row(D_MODEL), row(POOL_WIDTH), row(w), row(w),
                  row(2 * w), row(ATTN_WIDTH),
                  pl.BlockSpec((D_MODEL, D_MODEL), lambda i: (0, 0)),
                  vec(w), vec(w),
                  pl.BlockSpec((w, w), lambda i: (0, 0)),
                  vec(D_MODEL), vec(D_MODEL),
                  pl.BlockSpec((D_MODEL, ROUTE_LANES), lambda i: (0, 0)),
                  pl.BlockSpec((D_MODEL, ROUTE_LANES), lambda i: (0, 0)),
                  vec(ROUTE_LANES)],
        out_specs=[row(D_MODEL), row(HALF_D), row(ROUTE_LANES)],
        out_shape=[jax.ShapeDtypeStruct((n, D_MODEL), F32),
                   jax.ShapeDtypeStruct((n, HALF_D), U32),
                   jax.ShapeDtypeStruct((n, ROUTE_LANES), F32)],
        compiler_params=_cparams(("parallel",)),
        name="outproj",
    )(x2d, pool2d, wkvf, wkvr, post2d, attn2d, wo, gng, gnb, ones, lg, lb, wrh, wrl, br)


def _expert_kernel(bexp_ref, nused_ref, xs_ref, wg_ref, wu_ref, wd_ref, ys_ref, wgb_ref, wub_ref, wdb_ref):
    i = pl.program_id(0)
    used = i < nused_ref[0]

    @pl.when(used & ((i == 0) | (bexp_ref[i] != bexp_ref[jnp.maximum(i - 1, 0)])))
    def _():
        wgb_ref[...] = wg_ref[0].astype(BF16)
        wub_ref[...] = wu_ref[0].astype(BF16)
        wdb_ref[...] = wd_ref[0].astype(BF16)

    @pl.when(used)
    def _():
        xa, xb = _unpack_rows(xs_ref[...])
        h1 = _dot(xa, wgb_ref[0:HALF_D, :]) + _dot(xb, wgb_ref[HALF_D:D_MODEL, :])
        h2 = _dot(xa, wub_ref[0:HALF_D, :]) + _dot(xb, wub_ref[HALF_D:D_MODEL, :])
        h = (h1 * jax.nn.sigmoid(h1) * h2).astype(BF16)
        ys_ref[...] = _pack_rows(_dot(h, wdb_ref[...]))

    @pl.when(jnp.logical_not(used))
    def _():
        ys_ref[...] = jnp.zeros_like(ys_ref)


def _experts(block_exp, n_used, xs, wg, wu, wd, layer):
    p = xs.shape[0]
    nb = p // MOE_BLOCK
    wmap = lambda i, be, nu: (layer, be[i], 0, 0)
    grid_spec = pltpu.PrefetchScalarGridSpec(
        num_scalar_prefetch=2,
        grid=(nb,),
        in_specs=[pl.BlockSpec((MOE_BLOCK, HALF_D), lambda i, be, nu: (i, 0)),
                  pl.BlockSpec((None, 1, D_MODEL, EXPERT_HIDDEN), wmap),
                  pl.BlockSpec((None, 1, D_MODEL, EXPERT_HIDDEN), wmap),
                  pl.BlockSpec((None, 1, EXPERT_HIDDEN, D_MODEL), wmap)],
        out_specs=pl.BlockSpec((MOE_BLOCK, HALF_D), lambda i, be, nu: (i, 0)),
        scratch_shapes=[pltpu.VMEM((D_MODEL, EXPERT_HIDDEN), BF16),
                        pltpu.VMEM((D_MODEL, EXPERT_HIDDEN), BF16),
                        pltpu.VMEM((EXPERT_HIDDEN, D_MODEL), BF16)],
    )
    return pl.pallas_call(
        _expert_kernel,
        grid_spec=grid_spec,
        out_shape=jax.ShapeDtypeStruct((p, HALF_D), U32),
        compiler_params=_cparams(("arbitrary",)),
        name="experts",
    )(block_exp, n_used, xs, wg, wu, wd)


SC_CORES = 2
SC_SUBCORES = 16
SC_GATHER_ROWS = 64


def _sc_gather(table, idx):
    v, d = table.shape
    b = idx.shape[0]
    nw = SC_CORES * SC_SUBCORES
    ch = SC_GATHER_ROWS
    assert b % (nw * ch * 2) == 0
    b_per_w = b // nw
    nch = b_per_w // ch
    mesh = plsc.VectorSubcoreMesh(core_axis_name="c", subcore_axis_name="s")

    @functools.partial(
        pl.kernel, mesh=mesh,
        out_type=jax.ShapeDtypeStruct((b, d), table.dtype),
        scratch_types=[pltpu.VMEM((b_per_w,), jnp.int32),
                       pltpu.VMEM((2, ch, d), table.dtype),
                       pltpu.SemaphoreType.DMA, pltpu.SemaphoreType.DMA,
                       pltpu.SemaphoreType.DMA, pltpu.SemaphoreType.DMA],
    )
    def gather_kernel(table_hbm, idx_hbm, out_hbm, idx_v, rows_v, gsem0, gsem1, wsem0, wsem1):
        wid = lax.axis_index("s") * SC_CORES + lax.axis_index("c")
        base = wid * b_per_w
        pltpu.sync_copy(idx_hbm.at[pl.ds(base, b_per_w)], idx_v)
        gsem = (gsem0, gsem1)
        wsem = (wsem0, wsem1)

        def gather(jj, slot):
            off = pl.multiple_of(jj * ch, ch)
            return pltpu.make_async_copy(table_hbm.at[idx_v.at[pl.ds(off, ch)]], rows_v.at[slot], gsem[slot])

        def write(jj, slot):
            off = pl.multiple_of(jj * ch, ch)
            return pltpu.make_async_copy(rows_v.at[slot], out_hbm.at[pl.ds(base + off, ch)], wsem[slot])

        gather(0, 0).start()

        @pl.loop(0, nch, step=2)
        def _(j):
            for slot in range(2):
                jj = j + slot

                gather(jj, slot).wait()

                @pl.when(jj >= 1)
                def _():
                    write(jj - 1, 1 - slot).wait()

                @pl.when(jj + 1 < nch)
                def _():
                    gather(jj + 1, 1 - slot).start()

                write(jj, slot).start()

        write(nch - 1, (nch - 1) % 2).wait()

    return gather_kernel(table, idx)


def _gather_rows(table, idx):
    b = idx.shape[0]
    unit = SC_CORES * SC_SUBCORES * SC_GATHER_ROWS * 2
    bp = -(-b // unit) * unit
    if bp == b:
        return _sc_gather(table, idx)
    return _sc_gather(table, jnp.pad(idx, (0, bp - b)))[:b]


def _combine_kernel(x1_ref, ya_ref, yb_ref, route_ref, lg_ref, lb_ref, o_ref):
    route = route_ref[...]
    g1 = route[:, 2:3]
    g2 = route[:, 3:4]
    a_hi, a_lo = _unpack_rows(ya_ref[...])
    b_hi, b_lo = _unpack_rows(yb_ref[...])
    m = jnp.concatenate([g1 * a_hi.astype(F32) + g2 * b_hi.astype(F32),
                         g1 * a_lo.astype(F32) + g2 * b_lo.astype(F32)], axis=1)
    o_ref[...] = _layer_norm(DEEPNORM_ALPHA * x1_ref[...] + m, lg_ref[...], lb_ref[...])


def _combine(x1, yab, route, lg, lb, tm):
    n = x1.shape[0]
    nt = n // tm
    row = lambda width: pl.BlockSpec((tm, width), lambda i: (i, 0))
    vec = lambda width: pl.BlockSpec((1, width), lambda i: (0, 0))
    return pl.pallas_call(
        _combine_kernel,
        grid=(nt,),
        in_specs=[row(D_MODEL), row(HALF_D), pl.BlockSpec((tm, HALF_D), lambda i: (i + nt, 0)),
                  row(ROUTE_LANES), vec(D_MODEL), vec(D_MODEL)],
        out_specs=row(D_MODEL),
        out_shape=jax.ShapeDtypeStruct((n, D_MODEL), F32),
        compiler_params=_cparams(("parallel",)),
        name="combine",
    )(x1, yab, yab, route, lg, lb)


def _q_perm():
    order = [h for j in range(ATTN_HEADS // 2) for h in (j, j + ATTN_HEADS // 2)]
    return jnp.concatenate([jnp.arange(h * HEAD_DIM, (h + 1) * HEAD_DIM) for h in order])


def _block_diag(blocks):
    n = len(blocks)
    r, c = blocks[0].shape
    out = jnp.zeros((n * r, n * c), blocks[0].dtype)
    for i, blk in enumerate(blocks):
        out = out.at[i * r:(i + 1) * r, i * c:(i + 1) * c].set(blk)
    return out


def _rope_tables(seq):
    rows = seq // GRID_W
    row_id = jnp.repeat(jnp.arange(rows), GRID_W).astype(F32)
    col_id = jnp.tile(jnp.arange(GRID_W), rows).astype(F32)
    half = HEAD_DIM // 2
    inv_freq = ROPE_THETA ** (-jnp.arange(0, half, 2, dtype=F32) / half)
    ang_r = row_id[:, None] * inv_freq
    ang_c = col_id[:, None] * inv_freq
    ang = jnp.concatenate([ang_r, ang_r, ang_c, ang_c], -1)
    sign = jnp.where((jnp.arange(HEAD_DIM) // 16) % 2 == 0, -1.0, 1.0).astype(F32)
    cos8 = jnp.tile(jnp.cos(ang), (1, ATTN_HEADS))
    sin8 = jnp.tile(jnp.sin(ang) * sign, (1, ATTN_HEADS))
    return cos8, sin8


def _dispatch(route, n):
    m = n * 2
    experts = jnp.arange(N_EXPERTS, dtype=jnp.int32)
    e_flat = route[:, 0:2].astype(jnp.int32).reshape(m)
    pair = jnp.arange(m, dtype=jnp.int32)
    e_sorted, order = lax.sort((e_flat, pair), num_keys=1, is_stable=True)
    onehot_sorted = e_sorted[:, None] == experts[None, :]
    counts = jnp.sum(onehot_sorted, axis=0, dtype=jnp.int32)
    padded = (counts + MOE_BLOCK - 1) // MOE_BLOCK * MOE_BLOCK
    start = jnp.cumsum(counts) - counts
    ends_p = jnp.cumsum(padded)
    pstart = ends_p - padded
    shift = jnp.sum(jnp.where(onehot_sorted, (pstart - start)[None, :], 0), axis=1, dtype=jnp.int32)
    dest = pair + shift
    _, pos = lax.sort((order, dest), num_keys=1)
    n_blocks = -(-(m + N_EXPERTS * (MOE_BLOCK - 1)) // MOE_BLOCK)
    block_start = jnp.arange(n_blocks, dtype=jnp.int32) * MOE_BLOCK
    block_exp = jnp.minimum(jnp.sum(ends_p[None, :] <= block_start[:, None], axis=1, dtype=jnp.int32),
                            N_EXPERTS - 1)
    lane = jnp.arange(MOE_BLOCK, dtype=jnp.int32)[None, :]
    in_expert = (block_start - pstart[block_exp])[:, None] + lane
    valid = in_expert < counts[block_exp][:, None]
    src_pair = jnp.clip((block_start - (pstart - start)[block_exp])[:, None] + lane, 0, m - 1)
    filler = (block_start[:, None] + lane) % n
    row_tok = jnp.where(valid, order[src_pair.reshape(-1)].reshape(n_blocks, MOE_BLOCK) // 2,
                        filler).reshape(-1)
    n_used = (ends_p[-1] // MOE_BLOCK).astype(jnp.int32).reshape(1)
    return row_tok, pos.reshape(n, 2), block_exp, n_used


def kernel(x, w_in, mu_prev, mu_next, pool_w, pool_scale, rw_w0, rw_w_up, rw_a0, rw_a_up, rw_g_up, rw_k_k, rw_k_a, rw_r_k, rw_gn_g, rw_gn_b, q_norm, k_norm, w_o, ln1_g, ln1_b, router_group, router_group_b, router_expert, router_expert_b, exp_gate, exp_up, exp_down, ln2_g, ln2_b):
    b, s, d = x.shape
    n = b * s
    w = RWKV_WIDTH
    tm = min(512, s)
    qperm = _q_perm()
    cos8, sin8 = _rope_tables(s)
    ones_q = _block_diag([jnp.ones((HEAD_DIM, HEAD_DIM), BF16)] * ATTN_HEADS)
    ones_r = ones_q[0:w, 0:w]
    a_end = POOL_WIDTH
    b_end = POOL_WIDTH + RWKV_IN

    xc = x.reshape(n, d)
    for l in range(DEPTH):
        wq = w_in[l][:, b_end:b_end + ATTN_WIDTH][:, qperm]
        w_proj = jnp.concatenate(
            [w_in[l][:, :a_end], w_in[l][:, a_end:b_end],
             jnp.zeros((d, RWKV_PAD - RWKV_IN), F32), wq, w_in[l][:, b_end + ATTN_WIDTH:]],
            axis=1).astype(BF16)
        pool_bd = _block_diag([pool_w[l, g] for g in range(len(POOL_WINDOWS))]).astype(BF16)
        pad_vec = jnp.zeros((RWKV_PAD - RWKV_IN,), F32)
        mup = jnp.concatenate([mu_prev[l], pad_vec]).reshape(1, RWKV_PAD)
        mun = jnp.concatenate([mu_next[l], pad_vec]).reshape(1, RWKV_PAD)
        w_lora = jnp.zeros((w, 5 * w), F32)
        for di in range(2):
            w_lora = w_lora.at[di * DECAY_LORA:(di + 1) * DECAY_LORA, di * w:(di + 1) * w].set(rw_w_up[l, di])
            o = 2 * DECAY_LORA
            w_lora = w_lora.at[o + di * AAA_LORA:o + (di + 1) * AAA_LORA, (2 + di) * w:(3 + di) * w].set(rw_a_up[l, di])
        o = 2 * DECAY_LORA + 2 * AAA_LORA
        w_lora = w_lora.at[o:o + GATE_LORA, 4 * w:5 * w].set(rw_g_up[l]).astype(BF16)
        lora_bias = jnp.concatenate([rw_w0[l, 0], rw_w0[l, 1], rw_a0[l, 0], rw_a0[l, 1],
                                     jnp.zeros((w,), F32)]).reshape(1, 5 * w)
        qg = jnp.tile(q_norm[l], ATTN_HEADS).reshape(1, ATTN_WIDTH)
        kg = jnp.tile(k_norm[l], 2).reshape(1, ATTN_KV_WIDTH)
        wo_attn = w_o[l][POOL_WIDTH + w:][qperm]
        wo = jnp.concatenate([w_o[l][:POOL_WIDTH + w], wo_attn], axis=0).astype(BF16)
        wr = jnp.concatenate([router_group[l], router_expert[l],
                              jnp.zeros((d, ROUTE_LANES - N_GROUPS - N_EXPERTS), F32)], axis=1)
        wrh = wr.astype(BF16)
        wrl = (wr - wrh.astype(F32)).astype(BF16)
        br = jnp.concatenate([router_group_b[l], router_expert_b[l],
                              jnp.zeros((ROUTE_LANES - N_GROUPS - N_EXPERTS,), F32)]).reshape(1, ROUTE_LANES)

        pool_in, rw_in, qh, kh, vh = _proj(xc, w_proj, cos8, sin8, qg, kg, ones_q, s, tm)
        y_pool = _pool(pool_in.reshape(b, s, POOL_WIDTH), pool_bd, pool_scale[l].reshape(1, POOL_WIDTH))
        shared, dirp, post = _rwkv_prep(rw_in.reshape(b, s, RWKV_PAD), mup, mun, w_lora, lora_bias,
                                        rw_k_k[l].reshape(1, w), rw_k_a[l].reshape(1, w),
                                        rw_r_k[l].reshape(1, w), ones_r, tm)
        wkv_f, wkv_r = _wkv(shared, dirp, min(256, s))
        y_attn = _attn(qh, kh, vh, s, min(512, s))
        x1, x1p, route = _outproj(xc, y_pool.reshape(n, POOL_WIDTH), wkv_f.reshape(n, w), wkv_r.reshape(n, w),
                                  post.reshape(n, 2 * w), y_attn, wo,
                                  rw_gn_g[l].reshape(1, w), rw_gn_b[l].reshape(1, w), ones_r,
                                  ln1_g[l].reshape(1, d), ln1_b[l].reshape(1, d), wrh, wrl, br, tm)

        row_tok, pos, block_exp, n_used = _dispatch(route, n)
        xs = _gather_rows(x1p, row_tok)
        ys = _experts(block_exp, n_used, xs, exp_gate, exp_up, exp_down, l)
        yab = _gather_rows(ys, pos.T.reshape(2 * n))
        xc = _combine(x1, yab, route, ln2_g[l].reshape(1, d), ln2_b[l].reshape(1, d), tm)
    return xc.reshape(b, s, d)
```

```python
import functools
import math

import jax
import jax.numpy as jnp
from jax import lax
from jax.experimental import pallas as pl
from jax.experimental.pallas import tpu as pltpu
from jax.experimental.pallas import tpu_sc as plsc

F32 = jnp.float32
BF16 = jnp.bfloat16

D_MODEL = 1024
DEPTH = 4
GRID_W = 64
HEAD_DIM = 64
POOL_WIDTH = 256
POOL_WINDOWS = (2, 4, 8, 16)
POOL_GROUP = 64
RWKV_WIDTH = 256
RWKV_HEADS = 4
DECAY_LORA = 32
AAA_LORA = 32
GATE_LORA = 64
GN_EPS = 64e-5
RWKV_IN = 960
RWKV_PAD = 1024
ATTN_WIDTH = 512
ATTN_HEADS = 8
ATTN_KV_WIDTH = 128
ATTN_IN = ATTN_WIDTH + 2 * ATTN_KV_WIDTH
ROPE_THETA = 10000.0
QK_EPS = 1e-6
N_GROUPS = 4
EXPERTS_PER_GROUP = 8
N_EXPERTS = 32
EXPERT_HIDDEN = 512
MOE_BLOCK = 512
DEEPNORM_ALPHA = float((2 * DEPTH) ** 0.25)
LN_EPS = 1e-5
PROJ_WIDTH = POOL_WIDTH + RWKV_PAD + ATTN_IN
ROUTE_LANES = 128
WKV_CHUNK = 64
VMEM_LIMIT = 48 * 1024 * 1024


def _cparams(sem):
    return pltpu.CompilerParams(dimension_semantics=sem, vmem_limit_bytes=VMEM_LIMIT)


def _dot(a, b):
    return jnp.dot(a, b, preferred_element_type=F32)


def _dot_nt(a, b):
    return lax.dot_general(a, b, (((1,), (1,)), ((), ())), preferred_element_type=F32)


def _dot_tn(a, b):
    return lax.dot_general(a, b, (((0,), (0,)), ((), ())), preferred_element_type=F32)


def _split2(x):
    hi = x.astype(BF16)
    lo = (x - hi.astype(F32)).astype(BF16)
    return hi, lo


def _split3(x):
    hi = x.astype(BF16)
    r1 = x - hi.astype(F32)
    mid = r1.astype(BF16)
    lo = (r1 - mid.astype(F32)).astype(BF16)
    return hi, mid, lo


HALF_D = D_MODEL // 2
U32 = jnp.uint32


def _pack_rows(x):
    hi = lax.bitcast_convert_type(x[:, :HALF_D].astype(BF16).astype(F32), U32)
    lo = lax.bitcast_convert_type(x[:, HALF_D:].astype(BF16).astype(F32), U32)
    return hi | (lo >> 16)


def _unpack_rows(u):
    a = lax.bitcast_convert_type(u & jnp.uint32(0xFFFF0000), F32).astype(BF16)
    b = lax.bitcast_convert_type(u << 16, F32).astype(BF16)
    return a, b


def _headsum(x, ones_bf16):
    hi, lo = _split2(x)
    return _dot(hi, ones_bf16) + _dot(lo, ones_bf16)


def _rope(x, cos, sin_signed):
    n = x.shape[-1]
    lane = lax.broadcasted_iota(jnp.int32, x.shape, 1)
    first = (lane // 16) % 2 == 0
    partner = jnp.where(first, pltpu.roll(x, n - 16, 1), pltpu.roll(x, 16, 1))
    return x * cos + partner * sin_signed


def _layer_norm(z, g, b):
    mu = jnp.mean(z, axis=-1, keepdims=True)
    zc = z - mu
    var = jnp.mean(zc * zc, axis=-1, keepdims=True)
    return zc * lax.rsqrt(var + LN_EPS) * g + b


def _combine_rows(x1, ya, yb, route, g, b):
    g1 = route[:, 2:3]
    g2 = route[:, 3:4]
    a_hi, a_lo = _unpack_rows(ya)
    b_hi, b_lo = _unpack_rows(yb)
    m = jnp.concatenate([g1 * a_hi.astype(F32) + g2 * b_hi.astype(F32),
                         g1 * a_lo.astype(F32) + g2 * b_lo.astype(F32)], axis=1)
    return _layer_norm(DEEPNORM_ALPHA * x1 + m, g, b)


def _proj_kernel(x_ref, w_ref, cos_ref, sin_ref, qg_ref, kg_ref, onesq_ref,
                 pool_ref, rw_ref, q_ref, k_ref, v_ref):
    _proj_rows(x_ref[...], w_ref, cos_ref, sin_ref, qg_ref, kg_ref, onesq_ref,
               pool_ref, rw_ref, q_ref, k_ref, v_ref)


def _combine_proj_kernel(x1_ref, ya_ref, yb_ref, route_ref, lg_ref, lb_ref,
                         w_ref, cos_ref, sin_ref, qg_ref, kg_ref, onesq_ref,
                         x_ref, pool_ref, rw_ref, q_ref, k_ref, v_ref):
    x = _combine_rows(x1_ref[...], ya_ref[...], yb_ref[...], route_ref[...], lg_ref[...], lb_ref[...])
    x_ref[...] = x
    _proj_rows(x, w_ref, cos_ref, sin_ref, qg_ref, kg_ref, onesq_ref, pool_ref, rw_ref, q_ref, k_ref, v_ref)


def _proj_rows(x, w_ref, cos_ref, sin_ref, qg_ref, kg_ref, onesq_ref, pool_ref, rw_ref, q_ref, k_ref, v_ref):
    xb = x.astype(BF16)
    pool_ref[...] = _dot(xb, w_ref[:, 0:POOL_WIDTH])
    rw_ref[...] = _dot(xb, w_ref[:, POOL_WIDTH:POOL_WIDTH + RWKV_PAD])
    at = _dot(xb, w_ref[:, POOL_WIDTH + RWKV_PAD:PROJ_WIDTH])

    q = at[:, 0:ATTN_WIDTH]
    k = at[:, ATTN_WIDTH:ATTN_WIDTH + ATTN_KV_WIDTH]
    v = at[:, ATTN_WIDTH + ATTN_KV_WIDTH:ATTN_IN]
    ones_q = onesq_ref[...]
    ones_k = onesq_ref[0:ATTN_KV_WIDTH, 0:ATTN_KV_WIDTH]
    inv = 1.0 / HEAD_DIM
    qn = q * lax.rsqrt(_headsum(q * q, ones_q) * inv + QK_EPS) * qg_ref[...]
    kn = k * lax.rsqrt(_headsum(k * k, ones_k) * inv + QK_EPS) * kg_ref[...]
    cos = cos_ref[...]
    sin = sin_ref[...]
    q_ref[...] = (_rope(qn, cos, sin) * (HEAD_DIM ** -0.5 * math.log2(math.e))).astype(BF16)
    k_ref[...] = _rope(kn, cos[:, 0:ATTN_KV_WIDTH], sin[:, 0:ATTN_KV_WIDTH]).astype(BF16)
    v_ref[...] = v.astype(BF16)


def _proj(x2d, w, cos8, sin8, qg, kg, ones_q, seq, tm, prev=None):
    n = (x2d if prev is None else prev[0]).shape[0]
    nts = seq // tm
    nt = n // tm
    row = lambda width: pl.BlockSpec((tm, width), lambda i: (i, 0))
    const = lambda r, c: pl.BlockSpec((r, c), lambda i: (0, 0))
    proj_in = [const(D_MODEL, PROJ_WIDTH),
               pl.BlockSpec((tm, ATTN_WIDTH), lambda i: (i % nts, 0)),
               pl.BlockSpec((tm, ATTN_WIDTH), lambda i: (i % nts, 0)),
               const(1, ATTN_WIDTH), const(1, ATTN_KV_WIDTH), const(ATTN_WIDTH, ATTN_WIDTH)]
    proj_out = [row(POOL_WIDTH), row(RWKV_PAD), row(ATTN_WIDTH), row(ATTN_KV_WIDTH), row(ATTN_KV_WIDTH)]
    proj_shape = [jax.ShapeDtypeStruct((n, POOL_WIDTH), F32),
                  jax.ShapeDtypeStruct((n, RWKV_PAD), F32),
                  jax.ShapeDtypeStruct((n, ATTN_WIDTH), BF16),
                  jax.ShapeDtypeStruct((n, ATTN_KV_WIDTH), BF16),
                  jax.ShapeDtypeStruct((n, ATTN_KV_WIDTH), BF16)]
    if prev is None:
        return pl.pallas_call(
            _proj_kernel,
            grid=(nt,),
            in_specs=[row(D_MODEL)] + proj_in,
            out_specs=proj_out,
            out_shape=proj_shape,
            compiler_params=_cparams(("parallel",)),
            name="proj",
        )(x2d, w, cos8, sin8, qg, kg, ones_q)
    x1, yab, route, lg, lb = prev
    return pl.pallas_call(
        _combine_proj_kernel,
        grid=(nt,),
        in_specs=[row(D_MODEL), row(HALF_D), pl.BlockSpec((tm, HALF_D), lambda i: (i + nt, 0)),
                  row(ROUTE_LANES), const(1, D_MODEL), const(1, D_MODEL)] + proj_in,
        out_specs=[row(D_MODEL)] + proj_out,
        out_shape=[jax.ShapeDtypeStruct((n, D_MODEL), F32)] + proj_shape,
        compiler_params=_cparams(("parallel",)),
        name="combine_proj",
    )(x1, yab, yab, route, lg, lb, w, cos8, sin8, qg, kg, ones_q)


def _pool_kernel(u_ref, w_ref, scale_ref, o_ref):
    s = u_ref.shape[1]
    u = u_ref[0]
    t = lax.broadcasted_iota(jnp.int32, (s, POOL_WIDTH), 0)

    def down(x, k):
        return jnp.where(t >= k, pltpu.roll(x, k, 0), 0.0)

    def up(x, k):
        return jnp.where(t < s - k, pltpu.roll(x, s - k, 0), 0.0)

    left = [down(u, 1)]
    right = [u]
    for k in (1, 2, 4):
        left.append(left[-1] + down(left[-1], k))
        right.append(right[-1] + up(right[-1], k))
    sums = [l + r for l, r in zip(left, right)]

    grp = lax.broadcasted_iota(jnp.int32, (s, POOL_WIDTH), 1) // POOL_GROUP
    half = jnp.where(grp == 0, 1, jnp.where(grp == 1, 2, jnp.where(grp == 2, 4, 8)))
    cnt = (jnp.minimum(t + half, s) - jnp.maximum(t - half, 0)).astype(F32)
    tot = jnp.where(grp == 0, sums[0], jnp.where(grp == 1, sums[1], jnp.where(grp == 2, sums[2], sums[3])))
    d = tot / cnt - u
    o_ref[0] = _dot(d.astype(BF16), w_ref[...]) * scale_ref[...]


def _pool(u3d, w_bd, scale):
    b, s, _ = u3d.shape
    return pl.pallas_call(
        _pool_kernel,
        grid=(b,),
        in_specs=[pl.BlockSpec((1, s, POOL_WIDTH), lambda i: (i, 0, 0)),
                  pl.BlockSpec((POOL_WIDTH, POOL_WIDTH), lambda i: (0, 0)),
                  pl.BlockSpec((1, POOL_WIDTH), lambda i: (0, 0))],
        out_specs=pl.BlockSpec((1, s, POOL_WIDTH), lambda i: (i, 0, 0)),
        out_shape=jax.ShapeDtypeStruct((b, s, POOL_WIDTH), F32),
        compiler_params=_cparams(("parallel",)),
        name="pool",
    )(u3d, w_bd, scale)


_SHIFT_HALO = 8


def _rwkv_prep_kernel(cur_ref, prev_ref, next_ref, mup_ref, mun_ref, wl_ref, bias_ref,
                      kk_ref, ka_ref, rk_ref, ones_ref,
                      sh_ref, dir_ref, post_ref):
    ts = cur_ref.shape[1]
    ti = pl.program_id(1)
    nt = pl.num_programs(1)
    cur = cur_ref[0]
    row = lax.broadcasted_iota(jnp.int32, cur.shape, 0)
    before = jnp.where(ti > 0, prev_ref[0, _SHIFT_HALO - 1:_SHIFT_HALO, :], 0.0)
    after = jnp.where(ti < nt - 1, next_ref[0, 0:1, :], 0.0)
    prev = jnp.where(row == 0, before, pltpu.roll(cur, 1, 0))
    nxt = jnp.where(row == ts - 1, after, pltpu.roll(cur, ts - 1, 0))
    f = cur + mup_ref[...] * (prev - cur) + mun_ref[...] * (nxt - cur)

    w = RWKV_WIDTH
    r = f[:, 0:w]
    k = f[:, w:2 * w]
    v = f[:, 2 * w:3 * w]
    lora = f[:, 3 * w:4 * w]
    lane = lax.broadcasted_iota(jnp.int32, lora.shape, 1)
    z = jnp.where(lane < 2 * DECAY_LORA, jnp.tanh(lora),
                  jnp.where(lane < 2 * DECAY_LORA + 2 * AAA_LORA, lora,
                            jnp.where(lane < 2 * DECAY_LORA + 2 * AAA_LORA + GATE_LORA,
                                      jax.nn.sigmoid(lora), 0.0)))
    up = _dot(z.astype(BF16), wl_ref[...]) + bias_ref[...]
    ones = ones_ref[...]

    kk0 = k * kk_ref[...]
    nrm = jnp.sqrt(_headsum(kk0 * kk0, ones))
    kk = kk0 / jnp.maximum(nrm, 1e-12)
    sh_ref[0, :, 0:w] = r
    sh_ref[0, :, w:2 * w] = v
    sh_ref[0, :, 2 * w:3 * w] = kk

    ksum = jnp.zeros_like(k)
    for di in range(2):
        logw = -math.exp(-0.5) * jax.nn.sigmoid(up[:, di * w:(di + 1) * w])
        a = jax.nn.sigmoid(up[:, (2 + di) * w:(3 + di) * w])
        kh = k * (1.0 + (a - 1.0) * ka_ref[...])
        ksum = ksum + kh
        dir_ref[di, 0, :, 0:w] = logw
        dir_ref[di, 0, :, w:2 * w] = kh
        dir_ref[di, 0, :, 2 * w:3 * w] = -(a * kk)
    bonus = _headsum(r * ksum * rk_ref[...], ones) * v
    post_ref[0, :, 0:w] = bonus
    post_ref[0, :, w:2 * w] = up[:, 4 * w:5 * w]


def _rwkv_prep(rw3d, mup, mun, wl, bias, k_k, k_a, r_k, ones, ts):
    b, s, _ = rw3d.shape
    nt = s // ts
    hb = ts // _SHIFT_HALO
    nhb = s // _SHIFT_HALO
    w = RWKV_WIDTH
    vec = lambda width: pl.BlockSpec((1, width), lambda i, j: (0, 0))
    return pl.pallas_call(
        _rwkv_prep_kernel,
        grid=(b, nt),
        in_specs=[pl.BlockSpec((1, ts, RWKV_PAD), lambda i, j: (i, j, 0)),
                  pl.BlockSpec((1, _SHIFT_HALO, RWKV_PAD),
                               lambda i, j: (i, jnp.maximum(j * hb - 1, 0), 0)),
                  pl.BlockSpec((1, _SHIFT_HALO, RWKV_PAD),
                               lambda i, j: (i, jnp.minimum((j + 1) * hb, nhb - 1), 0)),
                  vec(RWKV_PAD), vec(RWKV_PAD),
                  pl.BlockSpec((w, 5 * w), lambda i, j: (0, 0)),
                  vec(5 * w), vec(w), vec(w), vec(w),
                  pl.BlockSpec((w, w), lambda i, j: (0, 0))],
        out_specs=[pl.BlockSpec((1, ts, 3 * w), lambda i, j: (i, j, 0)),
                   pl.BlockSpec((2, 1, ts, 3 * w), lambda i, j: (0, i, j, 0)),
                   pl.BlockSpec((1, ts, 2 * w), lambda i, j: (i, j, 0))],
        out_shape=[jax.ShapeDtypeStruct((b, s, 3 * w), F32),
                   jax.ShapeDtypeStruct((2, b, s, 3 * w), F32),
                   jax.ShapeDtypeStruct((b, s, 2 * w), F32)],
        compiler_params=_cparams(("parallel", "parallel")),
        name="rwkv_prep",
    )(rw3d, rw3d, rw3d, mup, mun, wl, bias, k_k, k_a, r_k, ones)


def _expand_bd(x_bf16, mask_bd):
    return jnp.where(mask_bd, jnp.concatenate([x_bf16] * RWKV_HEADS, axis=0), jnp.zeros((), BF16))


def _wkv_kernel(shf_ref, shr_ref, df_ref, dr_ref, yf_ref, yr_ref, hf_ref, hr_ref):
    c = WKV_CHUNK
    w = RWKV_WIDTH
    tt = shf_ref.shape[1]
    ncs = tt // c

    @pl.when(pl.program_id(1) == 0)
    def _():
        hf_ref[...] = jnp.zeros_like(hf_ref)
        hr_ref[...] = jnp.zeros_like(hr_ref)

    row_c = lax.broadcasted_iota(jnp.int32, (c, c), 0)
    col_c = lax.broadcasted_iota(jnp.int32, (c, c), 1)
    t_i = lax.broadcasted_iota(jnp.int32, (c, w), 0)
    s_i = lax.broadcasted_iota(jnp.int32, (c, w), 1) % c
    eye_c = (s_i == t_i).astype(F32)
    row_w = lax.broadcasted_iota(jnp.int32, (w, w), 0)
    col_w = lax.broadcasted_iota(jnp.int32, (w, w), 1)
    mask_bd = (row_w // c) == (col_w // HEAD_DIM)
    eye_w = row_w == col_w
    tri_incl = [jnp.where(col_c <= row_c, 1.0, 0.0).astype(BF16),
                jnp.where(col_c >= row_c, 1.0, 0.0).astype(BF16)]
    strict = [s_i < t_i, s_i > t_i]
    incl = [s_i <= t_i, s_i >= t_i]

    def bf(x):
        return x.astype(BF16)

    def bd(x):
        return _expand_bd(bf(x), mask_bd)

    chunks = []
    for ci in range(ncs):
        chunks.append((0, ci * c, shf_ref, df_ref))
        chunks.append((1, (ncs - 1 - ci) * c, shr_ref, dr_ref))

    st = []
    for d, start, sh_ref, dir_ref in chunks:
        rows = pl.ds(start, c)
        st.append(dict(d=d, rows=rows,
                       r=sh_ref[0, rows, 0:w], v=sh_ref[0, rows, w:2 * w], kk=sh_ref[0, rows, 2 * w:3 * w],
                       lw=dir_ref[0, rows, 0:w], kh=dir_ref[0, rows, w:2 * w], nb=dir_ref[0, rows, 2 * w:3 * w]))

    for q in st:
        l_hi, l_mid, l_lo = _split3(q["lw"])
        tri = tri_incl[q["d"]]
        q["cum"] = _dot(tri, l_hi) + _dot(tri, l_mid) + _dot(tri, l_lo)
    for q in st:
        cum, lw = q["cum"], q["lw"]
        tot = jnp.sum(lw, axis=0, keepdims=True)
        e_inv = jnp.exp(-cum)
        e_end = jnp.exp(tot - cum)
        q["tot"] = tot
        q["a_bar"] = q["kk"] * jnp.exp(cum - lw)
        q["r_bar"] = q["r"] * jnp.exp(cum)
        q["b_hat"] = bf(q["nb"] * e_end)
        q["k_hat"] = bf(q["kh"] * e_end)
        q["v_bf"] = bf(q["v"])
        q["v_bd"] = _expand_bd(q["v_bf"], mask_bd)
        q["lhs"] = jnp.concatenate([bf(q["a_bar"]), bf(q["r_bar"])], axis=0)
        q["rhs"] = jnp.concatenate([bd(q["nb"] * e_inv), bd(q["kh"] * e_inv)], axis=0)
    for q in st:
        sc = _dot_nt(q["lhs"], q["rhs"])
        sm, im = strict[q["d"]], incl[q["d"]]
        q["x"] = jnp.where(sm, sc[0:c, 0:w], 0.0)
        q["a_ak"] = bf(jnp.where(sm, sc[0:c, w:2 * w], 0.0))
        q["m_rb"] = bf(jnp.where(im, sc[c:2 * c, 0:w], 0.0))
        q["m_rk"] = bf(jnp.where(im, sc[c:2 * c, w:2 * w], 0.0))
        q["tinv"] = eye_c + q["x"]

    for q in st:
        q["x"] = _dot(bf(q["x"]), bd(q["x"]))
    n_lv = int(math.log2(c)) - 1
    for lv in range(n_lv):
        last = lv == n_lv - 1
        for q in st:
            xb = bd(q["x"])
            if last:
                q["tinv"] = q["tinv"] + _dot(bf(q["tinv"]), xb)
            else:
                res = _dot(jnp.concatenate([bf(q["x"]), bf(q["tinv"])], axis=0), xb)
                q["tinv"] = q["tinv"] + res[c:2 * c]
                q["x"] = res[0:c]
    for q in st:
        res = _dot(jnp.concatenate([q["a_ak"], q["m_rk"]], axis=0), q["v_bd"])
        q["akv"] = res[0:c]
        q["mrkv"] = res[c:2 * c]
    for q in st:
        tinv_bf = bf(q["tinv"])
        q["a_pr"] = bf(_dot(tinv_bf, bd(q["a_bar"])))
        q["v_pr"] = bf(_dot(tinv_bf, bd(q["akv"])))
    for q in st:
        q["r_pr"] = bf(q["r_bar"] + _dot(q["m_rb"], _expand_bd(q["a_pr"], mask_bd)))
        q["y_pr"] = _dot(q["m_rb"], _expand_bd(q["v_pr"], mask_bd)) + q["mrkv"]
    for q in st:
        q["p_bd"] = bf(jnp.where(eye_w, jnp.exp(q["tot"]), 0.0)
                       + jnp.where(mask_bd, _dot_tn(q["b_hat"], q["a_pr"]), 0.0))
        q["q_bd"] = jnp.where(mask_bd,
                              _dot_tn(jnp.concatenate([q["b_hat"], q["k_hat"]], axis=0),
                                      jnp.concatenate([q["v_pr"], q["v_bf"]], axis=0)), 0.0)

    h = [hf_ref[...], hr_ref[...]]
    y_refs = [yf_ref, yr_ref]
    for q in st:
        d = q["d"]
        res = _dot(jnp.concatenate([q["r_pr"], q["p_bd"]], axis=0), bf(h[d]))
        y_refs[d][0, q["rows"], :] = res[0:c] + q["y_pr"]
        h[d] = res[c:c + w] + q["q_bd"]
    hf_ref[...] = h[0]
    hr_ref[...] = h[1]


def _wkv(shared, dirp, tt):
    b, s, _ = shared.shape
    nt = s // tt
    w = RWKV_WIDTH
    fwd = lambda i, j: (i, j, 0)
    bwd = lambda i, j: (i, nt - 1 - j, 0)
    return pl.pallas_call(
        _wkv_kernel,
        grid=(b, nt),
        in_specs=[pl.BlockSpec((1, tt, 3 * w), fwd), pl.BlockSpec((1, tt, 3 * w), bwd),
                  pl.BlockSpec((None, 1, tt, 3 * w), lambda i, j: (0, i, j, 0)),
                  pl.BlockSpec((None, 1, tt, 3 * w), lambda i, j: (1, i, nt - 1 - j, 0))],
        out_specs=[pl.BlockSpec((1, tt, w), fwd), pl.BlockSpec((1, tt, w), bwd)],
        out_shape=[jax.ShapeDtypeStruct((b, s, w), F32), jax.ShapeDtypeStruct((b, s, w), F32)],
        scratch_shapes=[pltpu.VMEM((w, w), F32), pltpu.VMEM((w, w), F32)],
        compiler_params=_cparams(("parallel", "arbitrary")),
        name="wkv",
    )(shared, shared, dirp, dirp)


def _attn_kernel(q_ref, k_ref, v_ref, o_ref):
    tq = q_ref.shape[0]
    k = k_ref[...]
    v = v_ref[...]
    lane = lax.broadcasted_iota(jnp.int32, (tq, ATTN_KV_WIDTH), 1)
    low = lane < HEAD_DIM
    zero = jnp.zeros((), BF16)
    for j in range(ATTN_HEADS // 2):
        qb = q_ref[:, j * ATTN_KV_WIDTH:(j + 1) * ATTN_KV_WIDTH]
        outs = []
        for half in range(2):
            qm = jnp.where(low if half == 0 else ~low, qb, zero)
            s = _dot_nt(qm, k)
            m = jnp.max(s, axis=-1, keepdims=True)
            p = jnp.exp2(s - m)
            l = jnp.sum(p, axis=-1, keepdims=True)
            outs.append(_dot(p.astype(BF16), v) / l)
        o_ref[:, j * ATTN_KV_WIDTH:(j + 1) * ATTN_KV_WIDTH] = jnp.where(low, outs[0], outs[1]).astype(BF16)


def _attn(q, k, v, seq, tq):
    n = q.shape[0]
    nq = seq // tq
    return pl.pallas_call(
        _attn_kernel,
        grid=(n // seq, nq),
        in_specs=[pl.BlockSpec((tq, ATTN_WIDTH), lambda b, i: (b * nq + i, 0)),
                  pl.BlockSpec((seq, ATTN_KV_WIDTH), lambda b, i: (b, 0)),
                  pl.BlockSpec((seq, ATTN_KV_WIDTH), lambda b, i: (b, 0))],
        out_specs=pl.BlockSpec((tq, ATTN_WIDTH), lambda b, i: (b * nq + i, 0)),
        out_shape=jax.ShapeDtypeStruct((n, ATTN_WIDTH), BF16),
        compiler_params=_cparams(("parallel", "parallel")),
        name="attn",
    )(q, k, v)


def _outproj_kernel(x_ref, pool_ref, wkvf_ref, wkvr_ref, post_ref, attn_ref, wo_ref, gng_ref, gnb_ref,
                    ones_ref, lg_ref, lb_ref, wrh_ref, wrl_ref, br_ref,
                    x1_ref, x1p_ref, route_ref):
    w = RWKV_WIDTH
    ones = ones_ref[...]
    wkv = wkvf_ref[...] + wkvr_ref[...]
    inv = 1.0 / HEAD_DIM
    mu = _headsum(wkv, ones) * inv
    cen = wkv - mu
    var = _headsum(cen * cen, ones) * inv
    yr = cen * lax.rsqrt(var + GN_EPS) * gng_ref[...] + gnb_ref[...] + post_ref[:, 0:w]
    yr = yr * post_ref[:, w:2 * w]
    y = (_dot(pool_ref[...].astype(BF16), wo_ref[0:POOL_WIDTH, :])
         + _dot(yr.astype(BF16), wo_ref[POOL_WIDTH:POOL_WIDTH + w, :])
         + _dot(attn_ref[...], wo_ref[POOL_WIDTH + w:D_MODEL, :]))
    x1 = _layer_norm(DEEPNORM_ALPHA * x_ref[...] + y, lg_ref[...], lb_ref[...])
    x1_ref[...] = x1
    x1p_ref[...] = _pack_rows(x1)

    xh, xl = _split2(x1)
    logits = (_dot(xh, wrh_ref[...]) + _dot(xl, wrh_ref[...]) + _dot(xh, wrl_ref[...])) + br_ref[...]
    lane = lax.broadcasted_iota(jnp.int32, logits.shape, 1)
    lane_f = lane.astype(F32)
    lane_grp_f = ((lane - N_GROUPS) // EXPERTS_PER_GROUP).astype(F32)
    neg = -jnp.inf
    big = jnp.float32(1 << 20)
    gl = jnp.where(lane < N_GROUPS, logits, neg)
    gmax = jnp.max(gl, axis=-1, keepdims=True)
    grp = jnp.min(jnp.where(gl == gmax, lane_f, big), axis=-1, keepdims=True)
    gw = 1.0 / jnp.sum(jnp.exp(gl - gmax), axis=-1, keepdims=True)
    in_grp = (lane >= N_GROUPS) & (lane < N_GROUPS + N_EXPERTS) & (lane_grp_f == grp)
    el = jnp.where(in_grp, logits, neg)
    v1 = jnp.max(el, axis=-1, keepdims=True)
    i1 = jnp.min(jnp.where(el == v1, lane_f, big), axis=-1, keepdims=True)
    el2 = jnp.where(lane_f == i1, neg, el)
    v2 = jnp.max(el2, axis=-1, keepdims=True)
    i2 = jnp.min(jnp.where(el2 == v2, lane_f, big), axis=-1, keepdims=True)
    e21 = jnp.exp(v2 - v1)
    g1 = gw / (1.0 + e21)
    g2 = gw * e21 / (1.0 + e21)
    route = jnp.where(lane == 0, i1 - N_GROUPS,
                      jnp.where(lane == 1, i2 - N_GROUPS,
                                jnp.where(lane == 2, g1, jnp.where(lane == 3, g2, 0.0))))
    route_ref[...] = route


def _outproj(x2d, pool2d, wkvf, wkvr, post2d, attn2d, wo, gng, gnb, ones, lg, lb, wrh, wrl, br, tm):
    n = x2d.shape[0]
    w = RWKV_WIDTH
    row = lambda width: pl.BlockSpec((tm, width), lambda i: (i, 0))
    vec = lambda width: pl.BlockSpec((1, width), lambda i: (0, 0))
    return pl.pallas_call(
        _outproj_kernel,
        grid=(n // tm,),
        in_specs=[row(D_MODEL), row(POOL_WIDTH), row(w), row(w),
                  row(2 * w), row(ATTN_WIDTH),
                  pl.BlockSpec((D_MODEL, D_MODEL), lambda i: (0, 0)),
                  vec(w), vec(w),
                  pl.BlockSpec((w, w), lambda i: (0, 0)),
                  vec(D_MODEL), vec(D_MODEL),
                  pl.BlockSpec((D_MODEL, ROUTE_LANES), lambda i: (0, 0)),
                  pl.BlockSpec((D_MODEL, ROUTE_LANES), lambda i: (0, 0)),
                  vec(ROUTE_LANES)],
        out_specs=[row(D_MODEL), row(HALF_D), row(ROUTE_LANES)],
        out_shape=[jax.ShapeDtypeStruct((n, D_MODEL), F32),
                   jax.ShapeDtypeStruct((n, HALF_D), U32),
                   jax.ShapeDtypeStruct((n, ROUTE_LANES), F32)],
        compiler_params=_cparams(("parallel",)),
        name="outproj",
    )(x2d, pool2d, wkvf, wkvr, post2d, attn2d, wo, gng, gnb, ones, lg, lb, wrh, wrl, br)


def _expert_kernel(bexp_ref, nused_ref, xs_ref, wg_ref, wu_ref, wd_ref, ys_ref, wgb_ref, wub_ref, wdb_ref):
    i = pl.program_id(0)
    used = i < nused_ref[0]

    @pl.when(used & ((i == 0) | (bexp_ref[i] != bexp_ref[jnp.maximum(i - 1, 0)])))
    def _():
        wgb_ref[...] = wg_ref[0].astype(BF16)
        wub_ref[...] = wu_ref[0].astype(BF16)
        wdb_ref[...] = wd_ref[0].astype(BF16)

    @pl.when(used)
    def _():
        xa, xb = _unpack_rows(xs_ref[...])
        h1 = _dot(xa, wgb_ref[0:HALF_D, :]) + _dot(xb, wgb_ref[HALF_D:D_MODEL, :])
        h2 = _dot(xa, wub_ref[0:HALF_D, :]) + _dot(xb, wub_ref[HALF_D:D_MODEL, :])
        h = (h1 * jax.nn.sigmoid(h1) * h2).astype(BF16)
        ys_ref[...] = _pack_rows(_dot(h, wdb_ref[...]))

    @pl.when(jnp.logical_not(used))
    def _():
        ys_ref[...] = jnp.zeros_like(ys_ref)


def _experts(block_exp, n_used, xs, wg, wu, wd, layer):
    p = xs.shape[0]
    nb = p // MOE_BLOCK
    wmap = lambda i, be, nu: (layer, be[i], 0, 0)
    grid_spec = pltpu.PrefetchScalarGridSpec(
        num_scalar_prefetch=2,
        grid=(nb,),
        in_specs=[pl.BlockSpec((MOE_BLOCK, HALF_D), lambda i, be, nu: (i, 0)),
                  pl.BlockSpec((None, 1, D_MODEL, EXPERT_HIDDEN), wmap),
                  pl.BlockSpec((None, 1, D_MODEL, EXPERT_HIDDEN), wmap),
                  pl.BlockSpec((None, 1, EXPERT_HIDDEN, D_MODEL), wmap)],
        out_specs=pl.BlockSpec((MOE_BLOCK, HALF_D), lambda i, be, nu: (i, 0)),
        scratch_shapes=[pltpu.VMEM((D_MODEL, EXPERT_HIDDEN), BF16),
                        pltpu.VMEM((D_MODEL, EXPERT_HIDDEN), BF16),
                        pltpu.VMEM((EXPERT_HIDDEN, D_MODEL), BF16)],
    )
    return pl.pallas_call(
        _expert_kernel,
        grid_spec=grid_spec,
        out_shape=jax.ShapeDtypeStruct((p, HALF_D), U32),
        compiler_params=_cparams(("arbitrary",)),
        name="experts",
    )(block_exp, n_used, xs, wg, wu, wd)


SC_CORES = 2
SC_SUBCORES = 16
SC_GATHER_ROWS = 64


def _sc_gather(table, idx):
    v, d = table.shape
    b = idx.shape[0]
    nw = SC_CORES * SC_SUBCORES
    ch = SC_GATHER_ROWS
    assert b % (nw * ch * 2) == 0
    b_per_w = b // nw
    nch = b_per_w // ch
    mesh = plsc.VectorSubcoreMesh(core_axis_name="c", subcore_axis_name="s")

    @functools.partial(
        pl.kernel, mesh=mesh,
        out_type=jax.ShapeDtypeStruct((b, d), table.dtype),
        scratch_types=[pltpu.VMEM((b_per_w,), jnp.int32),
                       pltpu.VMEM((2, ch, d), table.dtype),
                       pltpu.SemaphoreType.DMA, pltpu.SemaphoreType.DMA,
                       pltpu.SemaphoreType.DMA, pltpu.SemaphoreType.DMA],
    )
    def gather_kernel(table_hbm, idx_hbm, out_hbm, idx_v, rows_v, gsem0, gsem1, wsem0, wsem1):
        wid = lax.axis_index("s") * SC_CORES + lax.axis_index("c")
        base = wid * b_per_w
        pltpu.sync_copy(idx_hbm.at[pl.ds(base, b_per_w)], idx_v)
        gsem = (gsem0, gsem1)
        wsem = (wsem0, wsem1)

        def gather(jj, slot):
            off = pl.multiple_of(jj * ch, ch)
            return pltpu.make_async_copy(table_hbm.at[idx_v.at[pl.ds(off, ch)]], rows_v.at[slot], gsem[slot])

        def write(jj, slot):
            off = pl.multiple_of(jj * ch, ch)
            return pltpu.make_async_copy(rows_v.at[slot], out_hbm.at[pl.ds(base + off, ch)], wsem[slot])

        gather(0, 0).start()

        @pl.loop(0, nch, step=2)
        def _(j):
            for slot in range(2):
                jj = j + slot

                gather(jj, slot).wait()

                @pl.when(jj >= 1)
                def _():
                    write(jj - 1, 1 - slot).wait()

                @pl.when(jj + 1 < nch)
                def _():
                    gather(jj + 1, 1 - slot).start()

                write(jj, slot).start()

        write(nch - 1, (nch - 1) % 2).wait()

    return gather_kernel(table, idx)


def _gather_rows(table, idx):
    b = idx.shape[0]
    unit = SC_CORES * SC_SUBCORES * SC_GATHER_ROWS * 2
    bp = -(-b // unit) * unit
    if bp == b:
        return _sc_gather(table, idx)
    return _sc_gather(table, jnp.pad(idx, (0, bp - b)))[:b]


def _combine_kernel(x1_ref, ya_ref, yb_ref, route_ref, lg_ref, lb_ref, o_ref):
    o_ref[...] = _combine_rows(x1_ref[...], ya_ref[...], yb_ref[...], route_ref[...], lg_ref[...], lb_ref[...])


def _combine(x1, yab, route, lg, lb, tm):
    n = x1.shape[0]
    nt = n // tm
    row = lambda width: pl.BlockSpec((tm, width), lambda i: (i, 0))
    vec = lambda width: pl.BlockSpec((1, width), lambda i: (0, 0))
    return pl.pallas_call(
        _combine_kernel,
        grid=(nt,),
        in_specs=[row(D_MODEL), row(HALF_D), pl.BlockSpec((tm, HALF_D), lambda i: (i + nt, 0)),
                  row(ROUTE_LANES), vec(D_MODEL), vec(D_MODEL)],
        out_specs=row(D_MODEL),
        out_shape=jax.ShapeDtypeStruct((n, D_MODEL), F32),
        compiler_params=_cparams(("parallel",)),
        name="combine",
    )(x1, yab, yab, route, lg, lb)


def _q_perm():
    order = [h for j in range(ATTN_HEADS // 2) for h in (j, j + ATTN_HEADS // 2)]
    return jnp.concatenate([jnp.arange(h * HEAD_DIM, (h + 1) * HEAD_DIM) for h in order])


def _block_diag(blocks):
    n = len(blocks)
    r, c = blocks[0].shape
    out = jnp.zeros((n * r, n * c), blocks[0].dtype)
    for i, blk in enumerate(blocks):
        out = out.at[i * r:(i + 1) * r, i * c:(i + 1) * c].set(blk)
    return out


def _rope_tables(seq):
    rows = seq // GRID_W
    row_id = jnp.repeat(jnp.arange(rows), GRID_W).astype(F32)
    col_id = jnp.tile(jnp.arange(GRID_W), rows).astype(F32)
    half = HEAD_DIM // 2
    inv_freq = ROPE_THETA ** (-jnp.arange(0, half, 2, dtype=F32) / half)
    ang_r = row_id[:, None] * inv_freq
    ang_c = col_id[:, None] * inv_freq
    ang = jnp.concatenate([ang_r, ang_r, ang_c, ang_c], -1)
    sign = jnp.where((jnp.arange(HEAD_DIM) // 16) % 2 == 0, -1.0, 1.0).astype(F32)
    cos8 = jnp.tile(jnp.cos(ang), (1, ATTN_HEADS))
    sin8 = jnp.tile(jnp.sin(ang) * sign, (1, ATTN_HEADS))
    return cos8, sin8


def _dispatch(route, n):
    m = n * 2
    experts = jnp.arange(N_EXPERTS, dtype=jnp.int32)
    e_flat = route[:, 0:2].astype(jnp.int32).reshape(m)
    pair = jnp.arange(m, dtype=jnp.int32)
    e_sorted, order = lax.sort((e_flat, pair), num_keys=1, is_stable=True)
    onehot_sorted = e_sorted[:, None] == experts[None, :]
    counts = jnp.sum(onehot_sorted, axis=0, dtype=jnp.int32)
    padded = (counts + MOE_BLOCK - 1) // MOE_BLOCK * MOE_BLOCK
    start = jnp.cumsum(counts) - counts
    ends_p = jnp.cumsum(padded)
    pstart = ends_p - padded
    shift = jnp.sum(jnp.where(onehot_sorted, (pstart - start)[None, :], 0), axis=1, dtype=jnp.int32)
    dest = pair + shift
    _, pos = lax.sort((order, dest), num_keys=1)
    n_blocks = -(-(m + N_EXPERTS * (MOE_BLOCK - 1)) // MOE_BLOCK)
    block_start = jnp.arange(n_blocks, dtype=jnp.int32) * MOE_BLOCK
    block_exp = jnp.minimum(jnp.sum(ends_p[None, :] <= block_start[:, None], axis=1, dtype=jnp.int32),
                            N_EXPERTS - 1)
    lane = jnp.arange(MOE_BLOCK, dtype=jnp.int32)[None, :]
    in_expert = (block_start - pstart[block_exp])[:, None] + lane
    valid = in_expert < counts[block_exp][:, None]
    src_pair = jnp.clip((block_start - (pstart - start)[block_exp])[:, None] + lane, 0, m - 1)
    filler = (block_start[:, None] + lane) % n
    row_tok = jnp.where(valid, order[src_pair.reshape(-1)].reshape(n_blocks, MOE_BLOCK) // 2,
                        filler).reshape(-1)
    n_used = (ends_p[-1] // MOE_BLOCK).astype(jnp.int32).reshape(1)
    return row_tok, pos.reshape(n, 2), block_exp, n_used


def kernel(x, w_in, mu_prev, mu_next, pool_w, pool_scale, rw_w0, rw_w_up, rw_a0, rw_a_up, rw_g_up, rw_k_k, rw_k_a, rw_r_k, rw_gn_g, rw_gn_b, q_norm, k_norm, w_o, ln1_g, ln1_b, router_group, router_group_b, router_expert, router_expert_b, exp_gate, exp_up, exp_down, ln2_g, ln2_b):
    b, s, d = x.shape
    n = b * s
    w = RWKV_WIDTH
    tm = min(512, s)
    qperm = _q_perm()
    cos8, sin8 = _rope_tables(s)
    ones_q = _block_diag([jnp.ones((HEAD_DIM, HEAD_DIM), BF16)] * ATTN_HEADS)
    ones_r = ones_q[0:w, 0:w]
    a_end = POOL_WIDTH
    b_end = POOL_WIDTH + RWKV_IN

    xc = x.reshape(n, d)
    pending = None
    for l in range(DEPTH):
        wq = w_in[l][:, b_end:b_end + ATTN_WIDTH][:, qperm]
        w_proj = jnp.concatenate(
            [w_in[l][:, :a_end], w_in[l][:, a_end:b_end],
             jnp.zeros((d, RWKV_PAD - RWKV_IN), F32), wq, w_in[l][:, b_end + ATTN_WIDTH:]],
            axis=1).astype(BF16)
        pool_bd = _block_diag([pool_w[l, g] for g in range(len(POOL_WINDOWS))]).astype(BF16)
        pad_vec = jnp.zeros((RWKV_PAD - RWKV_IN,), F32)
        mup = jnp.concatenate([mu_prev[l], pad_vec]).reshape(1, RWKV_PAD)
        mun = jnp.concatenate([mu_next[l], pad_vec]).reshape(1, RWKV_PAD)
        w_lora = jnp.zeros((w, 5 * w), F32)
        for di in range(2):
            w_lora = w_lora.at[di * DECAY_LORA:(di + 1) * DECAY_LORA, di * w:(di + 1) * w].set(rw_w_up[l, di])
            o = 2 * DECAY_LORA
            w_lora = w_lora.at[o + di * AAA_LORA:o + (di + 1) * AAA_LORA, (2 + di) * w:(3 + di) * w].set(rw_a_up[l, di])
        o = 2 * DECAY_LORA + 2 * AAA_LORA
        w_lora = w_lora.at[o:o + GATE_LORA, 4 * w:5 * w].set(rw_g_up[l]).astype(BF16)
        lora_bias = jnp.concatenate([rw_w0[l, 0], rw_w0[l, 1], rw_a0[l, 0], rw_a0[l, 1],
                                     jnp.zeros((w,), F32)]).reshape(1, 5 * w)
        qg = jnp.tile(q_norm[l], ATTN_HEADS).reshape(1, ATTN_WIDTH)
        kg = jnp.tile(k_norm[l], 2).reshape(1, ATTN_KV_WIDTH)
        wo_attn = w_o[l][POOL_WIDTH + w:][qperm]
        wo = jnp.concatenate([w_o[l][:POOL_WIDTH + w], wo_attn], axis=0).astype(BF16)
        wr = jnp.concatenate([router_group[l], router_expert[l],
                              jnp.zeros((d, ROUTE_LANES - N_GROUPS - N_EXPERTS), F32)], axis=1)
        wrh = wr.astype(BF16)
        wrl = (wr - wrh.astype(F32)).astype(BF16)
        br = jnp.concatenate([router_group_b[l], router_expert_b[l],
                              jnp.zeros((ROUTE_LANES - N_GROUPS - N_EXPERTS,), F32)]).reshape(1, ROUTE_LANES)

        if pending is None:
            pool_in, rw_in, qh, kh, vh = _proj(xc, w_proj, cos8, sin8, qg, kg, ones_q, s, tm)
        else:
            xc, pool_in, rw_in, qh, kh, vh = _proj(None, w_proj, cos8, sin8, qg, kg, ones_q, s, tm, prev=pending)
        y_pool = _pool(pool_in.reshape(b, s, POOL_WIDTH), pool_bd, pool_scale[l].reshape(1, POOL_WIDTH))
        shared, dirp, post = _rwkv_prep(rw_in.reshape(b, s, RWKV_PAD), mup, mun, w_lora, lora_bias,
                                        rw_k_k[l].reshape(1, w), rw_k_a[l].reshape(1, w),
                                        rw_r_k[l].reshape(1, w), ones_r, tm)
        wkv_f, wkv_r = _wkv(shared, dirp, min(256, s))
        y_attn = _attn(qh, kh, vh, s, min(512, s))
        x1, x1p, route = _outproj(xc, y_pool.reshape(n, POOL_WIDTH), wkv_f.reshape(n, w), wkv_r.reshape(n, w),
                                  post.reshape(n, 2 * w), y_attn, wo,
                                  rw_gn_g[l].reshape(1, w), rw_gn_b[l].reshape(1, w), ones_r,
                                  ln1_g[l].reshape(1, d), ln1_b[l].reshape(1, d), wrh, wrl, br, tm)

        row_tok, pos, block_exp, n_used = _dispatch(route, n)
        xs = _gather_rows(x1p, row_tok)
        ys = _experts(block_exp, n_used, xs, exp_gate, exp_up, exp_down, l)
        yab = _gather_rows(ys, pos.T.reshape(2 * n))
        pending = (x1, yab, route, ln2_g[l].reshape(1, d), ln2_b[l].reshape(1, d))
    return _combine(*pending, tm).reshape(b, s, d)
```

```python
import functools
import math

import jax
import jax.numpy as jnp
from jax import lax
from jax.experimental import pallas as pl
from jax.experimental.pallas import tpu as pltpu
from jax.experimental.pallas import tpu_sc as plsc

F32 = jnp.float32
BF16 = jnp.bfloat16

D_MODEL = 1024
DEPTH = 4
GRID_W = 64
HEAD_DIM = 64
POOL_WIDTH = 256
POOL_WINDOWS = (2, 4, 8, 16)
POOL_GROUP = 64
RWKV_WIDTH = 256
RWKV_HEADS = 4
DECAY_LORA = 32
AAA_LORA = 32
GATE_LORA = 64
GN_EPS = 64e-5
RWKV_IN = 960
RWKV_PAD = 1024
ATTN_WIDTH = 512
ATTN_HEADS = 8
ATTN_KV_WIDTH = 128
ATTN_IN = ATTN_WIDTH + 2 * ATTN_KV_WIDTH
ROPE_THETA = 10000.0
QK_EPS = 1e-6
N_GROUPS = 4
EXPERTS_PER_GROUP = 8
N_EXPERTS = 32
EXPERT_HIDDEN = 512
MOE_BLOCK = 512
DEEPNORM_ALPHA = float((2 * DEPTH) ** 0.25)
LN_EPS = 1e-5
PROJ_WIDTH = POOL_WIDTH + RWKV_PAD + ATTN_IN
ROUTE_LANES = 128
WKV_CHUNK = 64
VMEM_LIMIT = 48 * 1024 * 1024


def _cparams(sem):
    return pltpu.CompilerParams(dimension_semantics=sem, vmem_limit_bytes=VMEM_LIMIT)


def _dot(a, b):
    return jnp.dot(a, b, preferred_element_type=F32)


def _dot_nt(a, b):
    return lax.dot_general(a, b, (((1,), (1,)), ((), ())), preferred_element_type=F32)


def _dot_tn(a, b):
    return lax.dot_general(a, b, (((0,), (0,)), ((), ())), preferred_element_type=F32)


def _split2(x):
    hi = x.astype(BF16)
    lo = (x - hi.astype(F32)).astype(BF16)
    return hi, lo


def _split3(x):
    hi = x.astype(BF16)
    r1 = x - hi.astype(F32)
    mid = r1.astype(BF16)
    lo = (r1 - mid.astype(F32)).astype(BF16)
    return hi, mid, lo


HALF_D = D_MODEL // 2
U32 = jnp.uint32


def _pack_rows(x):
    hi = lax.bitcast_convert_type(x[:, :HALF_D].astype(BF16).astype(F32), U32)
    lo = lax.bitcast_convert_type(x[:, HALF_D:].astype(BF16).astype(F32), U32)
    return hi | (lo >> 16)


def _unpack_rows(u):
    a = lax.bitcast_convert_type(u & jnp.uint32(0xFFFF0000), F32).astype(BF16)
    b = lax.bitcast_convert_type(u << 16, F32).astype(BF16)
    return a, b


def _headsum(x, ones_bf16):
    hi, lo = _split2(x)
    return _dot(hi, ones_bf16) + _dot(lo, ones_bf16)


def _rope(x, cos, sin_signed):
    n = x.shape[-1]
    lane = lax.broadcasted_iota(jnp.int32, x.shape, 1)
    first = (lane // 16) % 2 == 0
    partner = jnp.where(first, pltpu.roll(x, n - 16, 1), pltpu.roll(x, 16, 1))
    return x * cos + partner * sin_signed


def _layer_norm(z, g, b):
    mu = jnp.mean(z, axis=-1, keepdims=True)
    zc = z - mu
    var = jnp.mean(zc * zc, axis=-1, keepdims=True)
    return zc * lax.rsqrt(var + LN_EPS) * g + b


def _combine_rows(x1, ya, yb, route, g, b):
    g1 = route[:, 2:3]
    g2 = route[:, 3:4]
    a_hi, a_lo = _unpack_rows(ya)
    b_hi, b_lo = _unpack_rows(yb)
    m = jnp.concatenate([g1 * a_hi.astype(F32) + g2 * b_hi.astype(F32),
                         g1 * a_lo.astype(F32) + g2 * b_lo.astype(F32)], axis=1)
    return _layer_norm(DEEPNORM_ALPHA * x1 + m, g, b)


def _proj_kernel(x_ref, w_ref, cos_ref, sin_ref, qg_ref, kg_ref, onesq_ref,
                 pool_ref, rw_ref, q_ref, k_ref, v_ref):
    _proj_rows(x_ref[...], w_ref, cos_ref, sin_ref, qg_ref, kg_ref, onesq_ref,
               pool_ref, rw_ref, q_ref, k_ref, v_ref)


def _combine_proj_kernel(x1_ref, ya_ref, yb_ref, route_ref, lg_ref, lb_ref,
                         w_ref, cos_ref, sin_ref, qg_ref, kg_ref, onesq_ref,
                         x_ref, pool_ref, rw_ref, q_ref, k_ref, v_ref):
    x = _combine_rows(x1_ref[...], ya_ref[...], yb_ref[...], route_ref[...], lg_ref[...], lb_ref[...])
    x_ref[...] = x
    _proj_rows(x, w_ref, cos_ref, sin_ref, qg_ref, kg_ref, onesq_ref, pool_ref, rw_ref, q_ref, k_ref, v_ref)


def _proj_rows(x, w_ref, cos_ref, sin_ref, qg_ref, kg_ref, onesq_ref, pool_ref, rw_ref, q_ref, k_ref, v_ref):
    xb = x.astype(BF16)
    pool_ref[...] = _dot(xb, w_ref[:, 0:POOL_WIDTH])
    rw_ref[...] = _dot(xb, w_ref[:, POOL_WIDTH:POOL_WIDTH + RWKV_PAD])
    at = _dot(xb, w_ref[:, POOL_WIDTH + RWKV_PAD:PROJ_WIDTH])

    q = at[:, 0:ATTN_WIDTH]
    k = at[:, ATTN_WIDTH:ATTN_WIDTH + ATTN_KV_WIDTH]
    v = at[:, ATTN_WIDTH + ATTN_KV_WIDTH:ATTN_IN]
    ones_q = onesq_ref[...]
    ones_k = onesq_ref[0:ATTN_KV_WIDTH, 0:ATTN_KV_WIDTH]
    inv = 1.0 / HEAD_DIM
    qn = q * lax.rsqrt(_headsum(q * q, ones_q) * inv + QK_EPS) * qg_ref[...]
    kn = k * lax.rsqrt(_headsum(k * k, ones_k) * inv + QK_EPS) * kg_ref[...]
    cos = cos_ref[...]
    sin = sin_ref[...]
    q_ref[...] = (_rope(qn, cos, sin) * (HEAD_DIM ** -0.5 * math.log2(math.e))).astype(BF16)
    k_ref[...] = _rope(kn, cos[:, 0:ATTN_KV_WIDTH], sin[:, 0:ATTN_KV_WIDTH]).astype(BF16)
    v_ref[...] = v.astype(BF16)


def _proj(x2d, w, cos8, sin8, qg, kg, ones_q, seq, tm, prev=None):
    n = (x2d if prev is None else prev[0]).shape[0]
    nts = seq // tm
    nt = n // tm
    row = lambda width: pl.BlockSpec((tm, width), lambda i: (i, 0))
    const = lambda r, c: pl.BlockSpec((r, c), lambda i: (0, 0))
    proj_in = [const(D_MODEL, PROJ_WIDTH),
               pl.BlockSpec((tm, ATTN_WIDTH), lambda i: (i % nts, 0)),
               pl.BlockSpec((tm, ATTN_WIDTH), lambda i: (i % nts, 0)),
               const(1, ATTN_WIDTH), const(1, ATTN_KV_WIDTH), const(ATTN_WIDTH, ATTN_WIDTH)]
    proj_out = [row(POOL_WIDTH), row(RWKV_PAD), row(ATTN_WIDTH), row(ATTN_KV_WIDTH), row(ATTN_KV_WIDTH)]
    proj_shape = [jax.ShapeDtypeStruct((n, POOL_WIDTH), F32),
                  jax.ShapeDtypeStruct((n, RWKV_PAD), F32),
                  jax.ShapeDtypeStruct((n, ATTN_WIDTH), BF16),
                  jax.ShapeDtypeStruct((n, ATTN_KV_WIDTH), BF16),
                  jax.ShapeDtypeStruct((n, ATTN_KV_WIDTH), BF16)]
    if prev is None:
        return pl.pallas_call(
            _proj_kernel,
            grid=(nt,),
            in_specs=[row(D_MODEL)] + proj_in,
            out_specs=proj_out,
            out_shape=proj_shape,
            compiler_params=_cparams(("parallel",)),
            name="proj",
        )(x2d, w, cos8, sin8, qg, kg, ones_q)
    x1, yab, route, lg, lb = prev
    return pl.pallas_call(
        _combine_proj_kernel,
        grid=(nt,),
        in_specs=[row(D_MODEL), row(HALF_D), pl.BlockSpec((tm, HALF_D), lambda i: (i + nt, 0)),
                  row(ROUTE_LANES), const(1, D_MODEL), const(1, D_MODEL)] + proj_in,
        out_specs=[row(D_MODEL)] + proj_out,
        out_shape=[jax.ShapeDtypeStruct((n, D_MODEL), F32)] + proj_shape,
        compiler_params=_cparams(("parallel",)),
        name="combine_proj",
    )(x1, yab, yab, route, lg, lb, w, cos8, sin8, qg, kg, ones_q)


def _pool_kernel(u_ref, w_ref, scale_ref, o_ref):
    s = u_ref.shape[1]
    u = u_ref[0]
    t = lax.broadcasted_iota(jnp.int32, (s, POOL_WIDTH), 0)

    def down(x, k):
        return jnp.where(t >= k, pltpu.roll(x, k, 0), 0.0)

    def up(x, k):
        return jnp.where(t < s - k, pltpu.roll(x, s - k, 0), 0.0)

    left = [down(u, 1)]
    right = [u]
    for k in (1, 2, 4):
        left.append(left[-1] + down(left[-1], k))
        right.append(right[-1] + up(right[-1], k))
    sums = [l + r for l, r in zip(left, right)]

    grp = lax.broadcasted_iota(jnp.int32, (s, POOL_WIDTH), 1) // POOL_GROUP
    half = jnp.where(grp == 0, 1, jnp.where(grp == 1, 2, jnp.where(grp == 2, 4, 8)))
    cnt = (jnp.minimum(t + half, s) - jnp.maximum(t - half, 0)).astype(F32)
    tot = jnp.where(grp == 0, sums[0], jnp.where(grp == 1, sums[1], jnp.where(grp == 2, sums[2], sums[3])))
    d = tot / cnt - u
    o_ref[0] = _dot(d.astype(BF16), w_ref[...]) * scale_ref[...]


def _pool(u3d, w_bd, scale):
    b, s, _ = u3d.shape
    return pl.pallas_call(
        _pool_kernel,
        grid=(b,),
        in_specs=[pl.BlockSpec((1, s, POOL_WIDTH), lambda i: (i, 0, 0)),
                  pl.BlockSpec((POOL_WIDTH, POOL_WIDTH), lambda i: (0, 0)),
                  pl.BlockSpec((1, POOL_WIDTH), lambda i: (0, 0))],
        out_specs=pl.BlockSpec((1, s, POOL_WIDTH), lambda i: (i, 0, 0)),
        out_shape=jax.ShapeDtypeStruct((b, s, POOL_WIDTH), F32),
        compiler_params=_cparams(("parallel",)),
        name="pool",
    )(u3d, w_bd, scale)


_SHIFT_HALO = 8


def _rwkv_prep_kernel(cur_ref, prev_ref, next_ref, mup_ref, mun_ref, wl_ref, bias_ref,
                      kk_ref, ka_ref, rk_ref, ones_ref,
                      sh_ref, dir_ref, post_ref):
    ts = cur_ref.shape[1]
    ti = pl.program_id(1)
    nt = pl.num_programs(1)
    cur = cur_ref[0]
    row = lax.broadcasted_iota(jnp.int32, cur.shape, 0)
    before = jnp.where(ti > 0, prev_ref[0, _SHIFT_HALO - 1:_SHIFT_HALO, :], 0.0)
    after = jnp.where(ti < nt - 1, next_ref[0, 0:1, :], 0.0)
    prev = jnp.where(row == 0, before, pltpu.roll(cur, 1, 0))
    nxt = jnp.where(row == ts - 1, after, pltpu.roll(cur, ts - 1, 0))
    f = cur + mup_ref[...] * (prev - cur) + mun_ref[...] * (nxt - cur)

    w = RWKV_WIDTH
    r = f[:, 0:w]
    k = f[:, w:2 * w]
    v = f[:, 2 * w:3 * w]
    lora = f[:, 3 * w:4 * w]
    lane = lax.broadcasted_iota(jnp.int32, lora.shape, 1)
    z = jnp.where(lane < 2 * DECAY_LORA, jnp.tanh(lora),
                  jnp.where(lane < 2 * DECAY_LORA + 2 * AAA_LORA, lora,
                            jnp.where(lane < 2 * DECAY_LORA + 2 * AAA_LORA + GATE_LORA,
                                      jax.nn.sigmoid(lora), 0.0)))
    up = _dot(z.astype(BF16), wl_ref[...]) + bias_ref[...]
    ones = ones_ref[...]

    kk0 = k * kk_ref[...]
    nrm = jnp.sqrt(_headsum(kk0 * kk0, ones))
    kk = kk0 / jnp.maximum(nrm, 1e-12)
    sh_ref[0, :, 0:w] = r
    sh_ref[0, :, w:2 * w] = v
    sh_ref[0, :, 2 * w:3 * w] = kk

    ksum = jnp.zeros_like(k)
    for di in range(2):
        logw = -math.exp(-0.5) * jax.nn.sigmoid(up[:, di * w:(di + 1) * w])
        a = jax.nn.sigmoid(up[:, (2 + di) * w:(3 + di) * w])
        kh = k * (1.0 + (a - 1.0) * ka_ref[...])
        ksum = ksum + kh
        dir_ref[di, 0, :, 0:w] = logw
        dir_ref[di, 0, :, w:2 * w] = kh
        dir_ref[di, 0, :, 2 * w:3 * w] = -(a * kk)
    bonus = _headsum(r * ksum * rk_ref[...], ones) * v
    post_ref[0, :, 0:w] = bonus
    post_ref[0, :, w:2 * w] = up[:, 4 * w:5 * w]


def _rwkv_prep(rw3d, mup, mun, wl, bias, k_k, k_a, r_k, ones, ts):
    b, s, _ = rw3d.shape
    nt = s // ts
    hb = ts // _SHIFT_HALO
    nhb = s // _SHIFT_HALO
    w = RWKV_WIDTH
    vec = lambda width: pl.BlockSpec((1, width), lambda i, j: (0, 0))
    return pl.pallas_call(
        _rwkv_prep_kernel,
        grid=(b, nt),
        in_specs=[pl.BlockSpec((1, ts, RWKV_PAD), lambda i, j: (i, j, 0)),
                  pl.BlockSpec((1, _SHIFT_HALO, RWKV_PAD),
                               lambda i, j: (i, jnp.maximum(j * hb - 1, 0), 0)),
                  pl.BlockSpec((1, _SHIFT_HALO, RWKV_PAD),
                               lambda i, j: (i, jnp.minimum((j + 1) * hb, nhb - 1), 0)),
                  vec(RWKV_PAD), vec(RWKV_PAD),
                  pl.BlockSpec((w, 5 * w), lambda i, j: (0, 0)),
                  vec(5 * w), vec(w), vec(w), vec(w),
                  pl.BlockSpec((w, w), lambda i, j: (0, 0))],
        out_specs=[pl.BlockSpec((1, ts, 3 * w), lambda i, j: (i, j, 0)),
                   pl.BlockSpec((2, 1, ts, 3 * w), lambda i, j: (0, i, j, 0)),
                   pl.BlockSpec((1, ts, 2 * w), lambda i, j: (i, j, 0))],
        out_shape=[jax.ShapeDtypeStruct((b, s, 3 * w), F32),
                   jax.ShapeDtypeStruct((2, b, s, 3 * w), F32),
                   jax.ShapeDtypeStruct((b, s, 2 * w), F32)],
        compiler_params=_cparams(("parallel", "parallel")),
        name="rwkv_prep",
    )(rw3d, rw3d, rw3d, mup, mun, wl, bias, k_k, k_a, r_k, ones)


def _expand_bd(x_bf16, mask_bd):
    return jnp.where(mask_bd, jnp.concatenate([x_bf16] * RWKV_HEADS, axis=0), jnp.zeros((), BF16))


def _wkv_kernel(shf_ref, shr_ref, df_ref, dr_ref, yf_ref, yr_ref, hf_ref, hr_ref):
    c = WKV_CHUNK
    w = RWKV_WIDTH
    tt = shf_ref.shape[1]
    ncs = tt // c

    @pl.when(pl.program_id(1) == 0)
    def _():
        hf_ref[...] = jnp.zeros_like(hf_ref)
        hr_ref[...] = jnp.zeros_like(hr_ref)

    row_c = lax.broadcasted_iota(jnp.int32, (c, c), 0)
    col_c = lax.broadcasted_iota(jnp.int32, (c, c), 1)
    t_i = lax.broadcasted_iota(jnp.int32, (c, w), 0)
    s_i = lax.broadcasted_iota(jnp.int32, (c, w), 1) % c
    eye_c = (s_i == t_i).astype(F32)
    row_w = lax.broadcasted_iota(jnp.int32, (w, w), 0)
    col_w = lax.broadcasted_iota(jnp.int32, (w, w), 1)
    mask_bd = (row_w // c) == (col_w // HEAD_DIM)
    eye_w = row_w == col_w
    tri_incl = [jnp.where(col_c <= row_c, 1.0, 0.0).astype(BF16),
                jnp.where(col_c >= row_c, 1.0, 0.0).astype(BF16)]
    strict = [s_i < t_i, s_i > t_i]
    incl = [s_i <= t_i, s_i >= t_i]

    def bf(x):
        return x.astype(BF16)

    def bd(x):
        return _expand_bd(bf(x), mask_bd)

    chunks = []
    for ci in range(ncs):
        chunks.append((0, ci * c, shf_ref, df_ref))
        chunks.append((1, (ncs - 1 - ci) * c, shr_ref, dr_ref))

    st = []
    for d, start, sh_ref, dir_ref in chunks:
        rows = pl.ds(start, c)
        st.append(dict(d=d, rows=rows,
                       r=sh_ref[0, rows, 0:w], v=sh_ref[0, rows, w:2 * w], kk=sh_ref[0, rows, 2 * w:3 * w],
                       lw=dir_ref[0, rows, 0:w], kh=dir_ref[0, rows, w:2 * w], nb=dir_ref[0, rows, 2 * w:3 * w]))

    for q in st:
        l_hi, l_mid, l_lo = _split3(q["lw"])
        tri = tri_incl[q["d"]]
        q["cum"] = _dot(tri, l_hi) + _dot(tri, l_mid) + _dot(tri, l_lo)
    for q in st:
        cum, lw = q["cum"], q["lw"]
        tot = jnp.sum(lw, axis=0, keepdims=True)
        e_inv = jnp.exp(-cum)
        e_end = jnp.exp(tot - cum)
        q["tot"] = tot
        q["a_bar"] = q["kk"] * jnp.exp(cum - lw)
        q["r_bar"] = q["r"] * jnp.exp(cum)
        q["b_hat"] = bf(q["nb"] * e_end)
        q["k_hat"] = bf(q["kh"] * e_end)
        q["v_bf"] = bf(q["v"])
        q["v_bd"] = _expand_bd(q["v_bf"], mask_bd)
        q["lhs"] = jnp.concatenate([bf(q["a_bar"]), bf(q["r_bar"])], axis=0)
        q["rhs"] = jnp.concatenate([bd(q["nb"] * e_inv), bd(q["kh"] * e_inv)], axis=0)
    for q in st:
        sc = _dot_nt(q["lhs"], q["rhs"])
        sm, im = strict[q["d"]], incl[q["d"]]
        q["x"] = jnp.where(sm, sc[0:c, 0:w], 0.0)
        q["a_ak"] = bf(jnp.where(sm, sc[0:c, w:2 * w], 0.0))
        q["m_rb"] = bf(jnp.where(im, sc[c:2 * c, 0:w], 0.0))
        q["m_rk"] = bf(jnp.where(im, sc[c:2 * c, w:2 * w], 0.0))
        q["tinv"] = eye_c + q["x"]

    for q in st:
        q["x"] = _dot(bf(q["x"]), bd(q["x"]))
    n_lv = int(math.log2(c)) - 1
    for lv in range(n_lv):
        last = lv == n_lv - 1
        for q in st:
            xb = bd(q["x"])
            if last:
                q["tinv"] = q["tinv"] + _dot(bf(q["tinv"]), xb)
            else:
                res = _dot(jnp.concatenate([bf(q["x"]), bf(q["tinv"])], axis=0), xb)
                q["tinv"] = q["tinv"] + res[c:2 * c]
                q["x"] = res[0:c]
    for q in st:
        res = _dot(jnp.concatenate([q["a_ak"], q["m_rk"]], axis=0), q["v_bd"])
        q["akv"] = res[0:c]
        q["mrkv"] = res[c:2 * c]
    for q in st:
        tinv_bf = bf(q["tinv"])
        q["a_pr"] = bf(_dot(tinv_bf, bd(q["a_bar"])))
        q["v_pr"] = bf(_dot(tinv_bf, bd(q["akv"])))
    for q in st:
        q["r_pr"] = bf(q["r_bar"] + _dot(q["m_rb"], _expand_bd(q["a_pr"], mask_bd)))
        q["y_pr"] = _dot(q["m_rb"], _expand_bd(q["v_pr"], mask_bd)) + q["mrkv"]
    for q in st:
        q["p_bd"] = bf(jnp.where(eye_w, jnp.exp(q["tot"]), 0.0)
                       + jnp.where(mask_bd, _dot_tn(q["b_hat"], q["a_pr"]), 0.0))
        q["q_bd"] = jnp.where(mask_bd,
                              _dot_tn(jnp.concatenate([q["b_hat"], q["k_hat"]], axis=0),
                                      jnp.concatenate([q["v_pr"], q["v_bf"]], axis=0)), 0.0)

    h = [hf_ref[...], hr_ref[...]]
    y_refs = [yf_ref, yr_ref]
    for q in st:
        d = q["d"]
        res = _dot(jnp.concatenate([q["r_pr"], q["p_bd"]], axis=0), bf(h[d]))
        y_refs[d][0, q["rows"], :] = res[0:c] + q["y_pr"]
        h[d] = res[c:c + w] + q["q_bd"]
    hf_ref[...] = h[0]
    hr_ref[...] = h[1]


def _wkv(shared, dirp, tt):
    b, s, _ = shared.shape
    nt = s // tt
    w = RWKV_WIDTH
    fwd = lambda i, j: (i, j, 0)
    bwd = lambda i, j: (i, nt - 1 - j, 0)
    return pl.pallas_call(
        _wkv_kernel,
        grid=(b, nt),
        in_specs=[pl.BlockSpec((1, tt, 3 * w), fwd), pl.BlockSpec((1, tt, 3 * w), bwd),
                  pl.BlockSpec((None, 1, tt, 3 * w), lambda i, j: (0, i, j, 0)),
                  pl.BlockSpec((None, 1, tt, 3 * w), lambda i, j: (1, i, nt - 1 - j, 0))],
        out_specs=[pl.BlockSpec((1, tt, w), fwd), pl.BlockSpec((1, tt, w), bwd)],
        out_shape=[jax.ShapeDtypeStruct((b, s, w), F32), jax.ShapeDtypeStruct((b, s, w), F32)],
        scratch_shapes=[pltpu.VMEM((w, w), F32), pltpu.VMEM((w, w), F32)],
        compiler_params=_cparams(("parallel", "arbitrary")),
        name="wkv",
    )(shared, shared, dirp, dirp)


def _attn_kernel(q_ref, k_ref, v_ref, o_ref):
    tq = q_ref.shape[0]
    k = k_ref[...]
    v = v_ref[...]
    lane = lax.broadcasted_iota(jnp.int32, (tq, ATTN_KV_WIDTH), 1)
    low = lane < HEAD_DIM
    zero = jnp.zeros((), BF16)
    for j in range(ATTN_HEADS // 2):
        qb = q_ref[:, j * ATTN_KV_WIDTH:(j + 1) * ATTN_KV_WIDTH]
        outs = []
        for half in range(2):
            qm = jnp.where(low if half == 0 else ~low, qb, zero)
            s = _dot_nt(qm, k)
            m = jnp.max(s, axis=-1, keepdims=True)
            p = jnp.exp2(s - m)
            l = jnp.sum(p, axis=-1, keepdims=True)
            outs.append(_dot(p.astype(BF16), v) / l)
        o_ref[:, j * ATTN_KV_WIDTH:(j + 1) * ATTN_KV_WIDTH] = jnp.where(low, outs[0], outs[1]).astype(BF16)


def _attn(q, k, v, seq, tq):
    n = q.shape[0]
    nq = seq // tq
    return pl.pallas_call(
        _attn_kernel,
        grid=(n // seq, nq),
        in_specs=[pl.BlockSpec((tq, ATTN_WIDTH), lambda b, i: (b * nq + i, 0)),
                  pl.BlockSpec((seq, ATTN_KV_WIDTH), lambda b, i: (b, 0)),
                  pl.BlockSpec((seq, ATTN_KV_WIDTH), lambda b, i: (b, 0))],
        out_specs=pl.BlockSpec((tq, ATTN_WIDTH), lambda b, i: (b * nq + i, 0)),
        out_shape=jax.ShapeDtypeStruct((n, ATTN_WIDTH), BF16),
        compiler_params=_cparams(("parallel", "parallel")),
        name="attn",
    )(q, k, v)


def _outproj_kernel(x_ref, pool_ref, wkvf_ref, wkvr_ref, post_ref, attn_ref, wo_ref, gng_ref, gnb_ref,
                    ones_ref, lg_ref, lb_ref, wrh_ref, wrl_ref, br_ref,
                    x1_ref, x1p_ref, route_ref):
    w = RWKV_WIDTH
    ones = ones_ref[...]
    wkv = wkvf_ref[...] + wkvr_ref[...]
    inv = 1.0 / HEAD_DIM
    mu = _headsum(wkv, ones) * inv
    cen = wkv - mu
    var = _headsum(cen * cen, ones) * inv
    yr = cen * lax.rsqrt(var + GN_EPS) * gng_ref[...] + gnb_ref[...] + post_ref[:, 0:w]
    yr = yr * post_ref[:, w:2 * w]
    y = (_dot(pool_ref[...].astype(BF16), wo_ref[0:POOL_WIDTH, :])
         + _dot(yr.astype(BF16), wo_ref[POOL_WIDTH:POOL_WIDTH + w, :])
         + _dot(attn_ref[...], wo_ref[POOL_WIDTH + w:D_MODEL, :]))
    x1 = _layer_norm(DEEPNORM_ALPHA * x_ref[...] + y, lg_ref[...], lb_ref[...])
    x1_ref[...] = x1
    x1p_ref[...] = _pack_rows(x1)

    xh, xl = _split2(x1)
    logits = (_dot(xh, wrh_ref[...]) + _dot(xl, wrh_ref[...]) + _dot(xh, wrl_ref[...])) + br_ref[...]
    lane = lax.broadcasted_iota(jnp.int32, logits.shape, 1)
    lane_f = lane.astype(F32)
    lane_grp_f = ((lane - N_GROUPS) // EXPERTS_PER_GROUP).astype(F32)
    neg = -jnp.inf
    big = jnp.float32(1 << 20)
    gl = jnp.where(lane < N_GROUPS, logits, neg)
    gmax = jnp.max(gl, axis=-1, keepdims=True)
    grp = jnp.min(jnp.where(gl == gmax, lane_f, big), axis=-1, keepdims=True)
    gw = 1.0 / jnp.sum(jnp.exp(gl - gmax), axis=-1, keepdims=True)
    in_grp = (lane >= N_GROUPS) & (lane < N_GROUPS + N_EXPERTS) & (lane_grp_f == grp)
    el = jnp.where(in_grp, logits, neg)
    v1 = jnp.max(el, axis=-1, keepdims=True)
    i1 = jnp.min(jnp.where(el == v1, lane_f, big), axis=-1, keepdims=True)
    el2 = jnp.where(lane_f == i1, neg, el)
    v2 = jnp.max(el2, axis=-1, keepdims=True)
    i2 = jnp.min(jnp.where(el2 == v2, lane_f, big), axis=-1, keepdims=True)
    e21 = jnp.exp(v2 - v1)
    g1 = gw / (1.0 + e21)
    g2 = gw * e21 / (1.0 + e21)
    route = jnp.where(lane == 0, i1 - N_GROUPS,
                      jnp.where(lane == 1, i2 - N_GROUPS,
                                jnp.where(lane == 2, g1, jnp.where(lane == 3, g2, 0.0))))
    route_ref[...] = route


def _outproj(x2d, pool2d, wkvf, wkvr, post2d, attn2d, wo, gng, gnb, ones, lg, lb, wrh, wrl, br, tm):
    n = x2d.shape[0]
    w = RWKV_WIDTH
    row = lambda width: pl.BlockSpec((tm, width), lambda i: (i, 0))
    vec = lambda width: pl.BlockSpec((1, width), lambda i: (0, 0))
    return pl.pallas_call(
        _outproj_kernel,
        grid=(n // tm,),
        in_specs=[row(D_MODEL), row(POOL_WIDTH), row(w), row(w),
                  row(2 * w), row(ATTN_WIDTH),
                  pl.BlockSpec((D_MODEL, D_MODEL), lambda i: (0, 0)),
                  vec(w), vec(w),
                  pl.BlockSpec((w, w), lambda i: (0, 0)),
                  vec(D_MODEL), vec(D_MODEL),
                  pl.BlockSpec((D_MODEL, ROUTE_LANES), lambda i: (0, 0)),
                  pl.BlockSpec((D_MODEL, ROUTE_LANES), lambda i: (0, 0)),
                  vec(ROUTE_LANES)],
        out_specs=[row(D_MODEL), row(HALF_D), row(ROUTE_LANES)],
        out_shape=[jax.ShapeDtypeStruct((n, D_MODEL), F32),
                   jax.ShapeDtypeStruct((n, HALF_D), U32),
                   jax.ShapeDtypeStruct((n, ROUTE_LANES), F32)],
        compiler_params=_cparams(("parallel",)),
        name="outproj",
    )(x2d, pool2d, wkvf, wkvr, post2d, attn2d, wo, gng, gnb, ones, lg, lb, wrh, wrl, br)


def _expert_kernel(bexp_ref, nused_ref, xs_ref, wg_ref, wu_ref, wd_ref, ys_ref, wgb_ref, wub_ref, wdb_ref):
    i = pl.program_id(0)
    used = i < nused_ref[0]

    @pl.when(used & ((i == 0) | (bexp_ref[i] != bexp_ref[jnp.maximum(i - 1, 0)])))
    def _():
        wgb_ref[...] = wg_ref[0].astype(BF16)
        wub_ref[...] = wu_ref[0].astype(BF16)
        wdb_ref[...] = wd_ref[0].astype(BF16)

    @pl.when(used)
    def _():
        xa, xb = _unpack_rows(xs_ref[...])
        h1 = _dot(xa, wgb_ref[0:HALF_D, :]) + _dot(xb, wgb_ref[HALF_D:D_MODEL, :])
        h2 = _dot(xa, wub_ref[0:HALF_D, :]) + _dot(xb, wub_ref[HALF_D:D_MODEL, :])
        h = (h1 * jax.nn.sigmoid(h1) * h2).astype(BF16)
        ys_ref[...] = _pack_rows(_dot(h, wdb_ref[...]))

    @pl.when(jnp.logical_not(used))
    def _():
        ys_ref[...] = jnp.zeros_like(ys_ref)


def _experts(block_exp, n_used, xs, wg, wu, wd, layer):
    p = xs.shape[0]
    nb = p // MOE_BLOCK
    wmap = lambda i, be, nu: (layer, be[i], 0, 0)
    grid_spec = pltpu.PrefetchScalarGridSpec(
        num_scalar_prefetch=2,
        grid=(nb,),
        in_specs=[pl.BlockSpec((MOE_BLOCK, HALF_D), lambda i, be, nu: (i, 0)),
                  pl.BlockSpec((None, 1, D_MODEL, EXPERT_HIDDEN), wmap),
                  pl.BlockSpec((None, 1, D_MODEL, EXPERT_HIDDEN), wmap),
                  pl.BlockSpec((None, 1, EXPERT_HIDDEN, D_MODEL), wmap)],
        out_specs=pl.BlockSpec((MOE_BLOCK, HALF_D), lambda i, be, nu: (i, 0)),
        scratch_shapes=[pltpu.VMEM((D_MODEL, EXPERT_HIDDEN), BF16),
                        pltpu.VMEM((D_MODEL, EXPERT_HIDDEN), BF16),
                        pltpu.VMEM((EXPERT_HIDDEN, D_MODEL), BF16)],
    )
    return pl.pallas_call(
        _expert_kernel,
        grid_spec=grid_spec,
        out_shape=jax.ShapeDtypeStruct((p, HALF_D), U32),
        compiler_params=_cparams(("arbitrary",)),
        name="experts",
    )(block_exp, n_used, xs, wg, wu, wd)


SC_CORES = 2
SC_SUBCORES = 16
SC_GATHER_ROWS = 64


def _sc_gather(table, idx):
    v, d = table.shape
    b = idx.shape[0]
    nw = SC_CORES * SC_SUBCORES
    ch = SC_GATHER_ROWS
    assert b % (nw * ch * 2) == 0
    b_per_w = b // nw
    nch = b_per_w // ch
    mesh = plsc.VectorSubcoreMesh(core_axis_name="c", subcore_axis_name="s")

    @functools.partial(
        pl.kernel, mesh=mesh,
        out_type=jax.ShapeDtypeStruct((b, d), table.dtype),
        scratch_types=[pltpu.VMEM((b_per_w,), jnp.int32),
                       pltpu.VMEM((2, ch, d), table.dtype),
                       pltpu.SemaphoreType.DMA, pltpu.SemaphoreType.DMA,
                       pltpu.SemaphoreType.DMA, pltpu.SemaphoreType.DMA],
    )
    def gather_kernel(table_hbm, idx_hbm, out_hbm, idx_v, rows_v, gsem0, gsem1, wsem0, wsem1):
        wid = lax.axis_index("s") * SC_CORES + lax.axis_index("c")
        base = wid * b_per_w
        pltpu.sync_copy(idx_hbm.at[pl.ds(base, b_per_w)], idx_v)
        gsem = (gsem0, gsem1)
        wsem = (wsem0, wsem1)

        def gather(jj, slot):
            off = pl.multiple_of(jj * ch, ch)
            return pltpu.make_async_copy(table_hbm.at[idx_v.at[pl.ds(off, ch)]], rows_v.at[slot], gsem[slot])

        def write(jj, slot):
            off = pl.multiple_of(jj * ch, ch)
            return pltpu.make_async_copy(rows_v.at[slot], out_hbm.at[pl.ds(base + off, ch)], wsem[slot])

        gather(0, 0).start()

        @pl.loop(0, nch, step=2)
        def _(j):
            for slot in range(2):
                jj = j + slot

                gather(jj, slot).wait()

                @pl.when(jj >= 1)
                def _():
                    write(jj - 1, 1 - slot).wait()

                @pl.when(jj + 1 < nch)
                def _():
                    gather(jj + 1, 1 - slot).start()

                write(jj, slot).start()

        write(nch - 1, (nch - 1) % 2).wait()

    return gather_kernel(table, idx)


def _gather_rows(table, idx):
    b = idx.shape[0]
    unit = SC_CORES * SC_SUBCORES * SC_GATHER_ROWS * 2
    bp = -(-b // unit) * unit
    if bp == b:
        return _sc_gather(table, idx)
    return _sc_gather(table, jnp.pad(idx, (0, bp - b)))[:b]


def _combine_kernel(x1_ref, ya_ref, yb_ref, route_ref, lg_ref, lb_ref, o_ref):
    o_ref[...] = _combine_rows(x1_ref[...], ya_ref[...], yb_ref[...], route_ref[...], lg_ref[...], lb_ref[...])


def _combine(x1, yab, route, lg, lb, tm):
    n = x1.shape[0]
    nt = n // tm
    row = lambda width: pl.BlockSpec((tm, width), lambda i: (i, 0))
    vec = lambda width: pl.BlockSpec((1, width), lambda i: (0, 0))
    return pl.pallas_call(
        _combine_kernel,
        grid=(nt,),
        in_specs=[row(D_MODEL), row(HALF_D), pl.BlockSpec((tm, HALF_D), lambda i: (i + nt, 0)),
                  row(ROUTE_LANES), vec(D_MODEL), vec(D_MODEL)],
        out_specs=row(D_MODEL),
        out_shape=jax.ShapeDtypeStruct((n, D_MODEL), F32),
        compiler_params=_cparams(("parallel",)),
        name="combine",
    )(x1, yab, yab, route, lg, lb)


def _q_perm():
    order = [h for j in range(ATTN_HEADS // 2) for h in (j, j + ATTN_HEADS // 2)]
    return jnp.concatenate([jnp.arange(h * HEAD_DIM, (h + 1) * HEAD_DIM) for h in order])


def _block_diag(blocks):
    n = len(blocks)
    r, c = blocks[0].shape
    out = jnp.zeros((n * r, n * c), blocks[0].dtype)
    for i, blk in enumerate(blocks):
        out = out.at[i * r:(i + 1) * r, i * c:(i + 1) * c].set(blk)
    return out


def _rope_tables(seq):
    rows = seq // GRID_W
    row_id = jnp.repeat(jnp.arange(rows), GRID_W).astype(F32)
    col_id = jnp.tile(jnp.arange(GRID_W), rows).astype(F32)
    half = HEAD_DIM // 2
    inv_freq = ROPE_THETA ** (-jnp.arange(0, half, 2, dtype=F32) / half)
    ang_r = row_id[:, None] * inv_freq
    ang_c = col_id[:, None] * inv_freq
    ang = jnp.concatenate([ang_r, ang_r, ang_c, ang_c], -1)
    sign = jnp.where((jnp.arange(HEAD_DIM) // 16) % 2 == 0, -1.0, 1.0).astype(F32)
    cos8 = jnp.tile(jnp.cos(ang), (1, ATTN_HEADS))
    sin8 = jnp.tile(jnp.sin(ang) * sign, (1, ATTN_HEADS))
    return cos8, sin8


def _dispatch(route, n):
    m = n * 2
    experts = jnp.arange(N_EXPERTS, dtype=jnp.int32)
    e_flat = jnp.concatenate([route[:, 0], route[:, 1]]).astype(jnp.int32)
    pair = jnp.arange(m, dtype=jnp.int32)
    e_sorted, order = lax.sort((e_flat, pair), num_keys=1, is_stable=True)
    start = jnp.sum(e_sorted[:, None] < experts[None, :], axis=0, dtype=jnp.int32)
    counts = jnp.concatenate([start[1:], jnp.full((1,), m, jnp.int32)]) - start
    padded = (counts + MOE_BLOCK - 1) // MOE_BLOCK * MOE_BLOCK
    ends_p = jnp.cumsum(padded)
    pstart = ends_p - padded
    delta = pstart - start
    n_blocks = -(-(m + N_EXPERTS * (MOE_BLOCK - 1)) // MOE_BLOCK)
    block_start = jnp.arange(n_blocks, dtype=jnp.int32) * MOE_BLOCK
    block_exp = jnp.minimum(jnp.sum(ends_p[None, :] <= block_start[:, None], axis=1, dtype=jnp.int32),
                            N_EXPERTS - 1)
    lane = jnp.arange(MOE_BLOCK, dtype=jnp.int32)[None, :]
    valid = (block_start - pstart[block_exp])[:, None] + lane < counts[block_exp][:, None]
    first = jnp.clip(block_start - delta[block_exp], 0, m)
    order_pad = jnp.concatenate([order, jnp.zeros((MOE_BLOCK,), jnp.int32)])
    run = jax.vmap(lambda f: lax.dynamic_slice(order_pad, (f,), (MOE_BLOCK,)))(first)
    row_tok = jnp.where(valid, run % n, (block_start[:, None] + lane) % n).reshape(-1)
    n_used = (ends_p[-1] // MOE_BLOCK).astype(jnp.int32).reshape(1)
    dest = pair + jnp.sum(jnp.where(e_sorted[:, None] == experts[None, :], delta[None, :], 0), axis=1,
                          dtype=jnp.int32)
    _, pos = lax.sort((order, dest), num_keys=1)
    return row_tok, pos, block_exp, n_used


def kernel(x, w_in, mu_prev, mu_next, pool_w, pool_scale, rw_w0, rw_w_up, rw_a0, rw_a_up, rw_g_up, rw_k_k, rw_k_a, rw_r_k, rw_gn_g, rw_gn_b, q_norm, k_norm, w_o, ln1_g, ln1_b, router_group, router_group_b, router_expert, router_expert_b, exp_gate, exp_up, exp_down, ln2_g, ln2_b):
    b, s, d = x.shape
    n = b * s
    w = RWKV_WIDTH
    tm = min(512, s)
    qperm = _q_perm()
    cos8, sin8 = _rope_tables(s)
    ones_q = _block_diag([jnp.ones((HEAD_DIM, HEAD_DIM), BF16)] * ATTN_HEADS)
    ones_r = ones_q[0:w, 0:w]
    a_end = POOL_WIDTH
    b_end = POOL_WIDTH + RWKV_IN

    xc = x.reshape(n, d)
    pending = None
    for l in range(DEPTH):
        wq = w_in[l][:, b_end:b_end + ATTN_WIDTH][:, qperm]
        w_proj = jnp.concatenate(
            [w_in[l][:, :a_end], w_in[l][:, a_end:b_end],
             jnp.zeros((d, RWKV_PAD - RWKV_IN), F32), wq, w_in[l][:, b_end + ATTN_WIDTH:]],
            axis=1).astype(BF16)
        pool_bd = _block_diag([pool_w[l, g] for g in range(len(POOL_WINDOWS))]).astype(BF16)
        pad_vec = jnp.zeros((RWKV_PAD - RWKV_IN,), F32)
        mup = jnp.concatenate([mu_prev[l], pad_vec]).reshape(1, RWKV_PAD)
        mun = jnp.concatenate([mu_next[l], pad_vec]).reshape(1, RWKV_PAD)
        w_lora = jnp.zeros((w, 5 * w), F32)
        for di in range(2):
            w_lora = w_lora.at[di * DECAY_LORA:(di + 1) * DECAY_LORA, di * w:(di + 1) * w].set(rw_w_up[l, di])
            o = 2 * DECAY_LORA
            w_lora = w_lora.at[o + di * AAA_LORA:o + (di + 1) * AAA_LORA, (2 + di) * w:(3 + di) * w].set(rw_a_up[l, di])
        o = 2 * DECAY_LORA + 2 * AAA_LORA
        w_lora = w_lora.at[o:o + GATE_LORA, 4 * w:5 * w].set(rw_g_up[l]).astype(BF16)
        lora_bias = jnp.concatenate([rw_w0[l, 0], rw_w0[l, 1], rw_a0[l, 0], rw_a0[l, 1],
                                     jnp.zeros((w,), F32)]).reshape(1, 5 * w)
        qg = jnp.tile(q_norm[l], ATTN_HEADS).reshape(1, ATTN_WIDTH)
        kg = jnp.tile(k_norm[l], 2).reshape(1, ATTN_KV_WIDTH)
        wo_attn = w_o[l][POOL_WIDTH + w:][qperm]
        wo = jnp.concatenate([w_o[l][:POOL_WIDTH + w], wo_attn], axis=0).astype(BF16)
        wr = jnp.concatenate([router_group[l], router_expert[l],
                              jnp.zeros((d, ROUTE_LANES - N_GROUPS - N_EXPERTS), F32)], axis=1)
        wrh = wr.astype(BF16)
        wrl = (wr - wrh.astype(F32)).astype(BF16)
        br = jnp.concatenate([router_group_b[l], router_expert_b[l],
                              jnp.zeros((ROUTE_LANES - N_GROUPS - N_EXPERTS,), F32)]).reshape(1, ROUTE_LANES)

        if pending is None:
            pool_in, rw_in, qh, kh, vh = _proj(xc, w_proj, cos8, sin8, qg, kg, ones_q, s, tm)
        else:
            xc, pool_in, rw_in, qh, kh, vh = _proj(None, w_proj, cos8, sin8, qg, kg, ones_q, s, tm, prev=pending)
        y_pool = _pool(pool_in.reshape(b, s, POOL_WIDTH), pool_bd, pool_scale[l].reshape(1, POOL_WIDTH))
        shared, dirp, post = _rwkv_prep(rw_in.reshape(b, s, RWKV_PAD), mup, mun, w_lora, lora_bias,
                                        rw_k_k[l].reshape(1, w), rw_k_a[l].reshape(1, w),
                                        rw_r_k[l].reshape(1, w), ones_r, tm)
        wkv_f, wkv_r = _wkv(shared, dirp, min(256, s))
        y_attn = _attn(qh, kh, vh, s, min(512, s))
        x1, x1p, route = _outproj(xc, y_pool.reshape(n, POOL_WIDTH), wkv_f.reshape(n, w), wkv_r.reshape(n, w),
                                  post.reshape(n, 2 * w), y_attn, wo,
                                  rw_gn_g[l].reshape(1, w), rw_gn_b[l].reshape(1, w), ones_r,
                                  ln1_g[l].reshape(1, d), ln1_b[l].reshape(1, d), wrh, wrl, br, tm)

        row_tok, pos, block_exp, n_used = _dispatch(route, n)
        xs = _gather_rows(x1p, row_tok)
        ys = _experts(block_exp, n_used, xs, exp_gate, exp_up, exp_down, l)
        yab = _gather_rows(ys, pos)
        pending = (x1, yab, route, ln2_g[l].reshape(1, d), ln2_b[l].reshape(1, d))
    return _combine(*pending, tm).reshape(b, s, d)
```

```python
import functools
import math

import jax
import jax.numpy as jnp
from jax import lax
from jax.experimental import pallas as pl
from jax.experimental.pallas import tpu as pltpu
from jax.experimental.pallas import tpu_sc as plsc

F32 = jnp.float32
BF16 = jnp.bfloat16

D_MODEL = 1024
DEPTH = 4
GRID_W = 64
HEAD_DIM = 64
POOL_WIDTH = 256
POOL_WINDOWS = (2, 4, 8, 16)
POOL_GROUP = 64
RWKV_WIDTH = 256
RWKV_HEADS = 4
DECAY_LORA = 32
AAA_LORA = 32
GATE_LORA = 64
GN_EPS = 64e-5
RWKV_IN = 960
RWKV_PAD = 1024
ATTN_WIDTH = 512
ATTN_HEADS = 8
ATTN_KV_WIDTH = 128
ATTN_IN = ATTN_WIDTH + 2 * ATTN_KV_WIDTH
ROPE_THETA = 10000.0
QK_EPS = 1e-6
N_GROUPS = 4
EXPERTS_PER_GROUP = 8
N_EXPERTS = 32
EXPERT_HIDDEN = 512
MOE_BLOCK = 512
DEEPNORM_ALPHA = float((2 * DEPTH) ** 0.25)
LN_EPS = 1e-5
PROJ_WIDTH = POOL_WIDTH + RWKV_PAD + ATTN_IN
ROUTE_LANES = 128
WKV_CHUNK = 64
VMEM_LIMIT = 48 * 1024 * 1024


def _cparams(sem):
    return pltpu.CompilerParams(dimension_semantics=sem, vmem_limit_bytes=VMEM_LIMIT)


def _dot(a, b):
    return jnp.dot(a, b, preferred_element_type=F32)


def _dot_nt(a, b):
    return lax.dot_general(a, b, (((1,), (1,)), ((), ())), preferred_element_type=F32)


def _dot_tn(a, b):
    return lax.dot_general(a, b, (((0,), (0,)), ((), ())), preferred_element_type=F32)


def _split2(x):
    hi = x.astype(BF16)
    lo = (x - hi.astype(F32)).astype(BF16)
    return hi, lo


def _split3(x):
    hi = x.astype(BF16)
    r1 = x - hi.astype(F32)
    mid = r1.astype(BF16)
    lo = (r1 - mid.astype(F32)).astype(BF16)
    return hi, mid, lo


HALF_D = D_MODEL // 2
U32 = jnp.uint32


def _pack_rows(x):
    hi = lax.bitcast_convert_type(x[:, :HALF_D].astype(BF16).astype(F32), U32)
    lo = lax.bitcast_convert_type(x[:, HALF_D:].astype(BF16).astype(F32), U32)
    return hi | (lo >> 16)


def _unpack_rows(u):
    a = lax.bitcast_convert_type(u & jnp.uint32(0xFFFF0000), F32).astype(BF16)
    b = lax.bitcast_convert_type(u << 16, F32).astype(BF16)
    return a, b


def _headsum(x, ones_bf16):
    hi, lo = _split2(x)
    return _dot(hi, ones_bf16) + _dot(lo, ones_bf16)


def _rope(x, cos, sin_signed):
    n = x.shape[-1]
    lane = lax.broadcasted_iota(jnp.int32, x.shape, 1)
    first = (lane // 16) % 2 == 0
    partner = jnp.where(first, pltpu.roll(x, n - 16, 1), pltpu.roll(x, 16, 1))
    return x * cos + partner * sin_signed


def _layer_norm(z, g, b):
    mu = jnp.mean(z, axis=-1, keepdims=True)
    zc = z - mu
    var = jnp.mean(zc * zc, axis=-1, keepdims=True)
    return zc * lax.rsqrt(var + LN_EPS) * g + b


def _combine_rows(x1, ya, yb, route, g, b):
    g1 = route[:, 2:3]
    g2 = route[:, 3:4]
    a_hi, a_lo = _unpack_rows(ya)
    b_hi, b_lo = _unpack_rows(yb)
    m = jnp.concatenate([g1 * a_hi.astype(F32) + g2 * b_hi.astype(F32),
                         g1 * a_lo.astype(F32) + g2 * b_lo.astype(F32)], axis=1)
    return _layer_norm(DEEPNORM_ALPHA * x1 + m, g, b)


def _proj_kernel(x_ref, w_ref, cos_ref, sin_ref, qg_ref, kg_ref, onesq_ref,
                 pool_ref, rw_ref, q_ref, k_ref, v_ref):
    _proj_rows(x_ref[...], w_ref, cos_ref, sin_ref, qg_ref, kg_ref, onesq_ref,
               pool_ref, rw_ref, q_ref, k_ref, v_ref)


def _combine_proj_kernel(x1_ref, ya_ref, yb_ref, route_ref, lg_ref, lb_ref,
                         w_ref, cos_ref, sin_ref, qg_ref, kg_ref, onesq_ref,
                         x_ref, pool_ref, rw_ref, q_ref, k_ref, v_ref):
    x = _combine_rows(x1_ref[...], ya_ref[...], yb_ref[...], route_ref[...], lg_ref[...], lb_ref[...])
    x_ref[...] = x
    _proj_rows(x, w_ref, cos_ref, sin_ref, qg_ref, kg_ref, onesq_ref, pool_ref, rw_ref, q_ref, k_ref, v_ref)


def _proj_rows(x, w_ref, cos_ref, sin_ref, qg_ref, kg_ref, onesq_ref, pool_ref, rw_ref, q_ref, k_ref, v_ref):
    xb = x.astype(BF16)
    pool_ref[...] = _dot(xb, w_ref[:, 0:POOL_WIDTH])
    rw_ref[...] = _dot(xb, w_ref[:, POOL_WIDTH:POOL_WIDTH + RWKV_PAD])
    at = _dot(xb, w_ref[:, POOL_WIDTH + RWKV_PAD:PROJ_WIDTH])

    q = at[:, 0:ATTN_WIDTH]
    k = at[:, ATTN_WIDTH:ATTN_WIDTH + ATTN_KV_WIDTH]
    v = at[:, ATTN_WIDTH + ATTN_KV_WIDTH:ATTN_IN]
    ones_q = onesq_ref[...]
    ones_k = onesq_ref[0:ATTN_KV_WIDTH, 0:ATTN_KV_WIDTH]
    inv = 1.0 / HEAD_DIM
    qn = q * lax.rsqrt(_headsum(q * q, ones_q) * inv + QK_EPS) * qg_ref[...]
    kn = k * lax.rsqrt(_headsum(k * k, ones_k) * inv + QK_EPS) * kg_ref[...]
    cos = cos_ref[...]
    sin = sin_ref[...]
    q_ref[...] = (_rope(qn, cos, sin) * (HEAD_DIM ** -0.5 * math.log2(math.e))).astype(BF16)
    k_ref[...] = _rope(kn, cos[:, 0:ATTN_KV_WIDTH], sin[:, 0:ATTN_KV_WIDTH]).astype(BF16)
    v_ref[...] = v.astype(BF16)


def _proj(x2d, w, cos8, sin8, qg, kg, ones_q, seq, tm, prev=None):
    n = (x2d if prev is None else prev[0]).shape[0]
    nts = seq // tm
    nt = n // tm
    row = lambda width: pl.BlockSpec((tm, width), lambda i: (i, 0))
    const = lambda r, c: pl.BlockSpec((r, c), lambda i: (0, 0))
    proj_in = [const(D_MODEL, PROJ_WIDTH),
               pl.BlockSpec((tm, ATTN_WIDTH), lambda i: (i % nts, 0)),
               pl.BlockSpec((tm, ATTN_WIDTH), lambda i: (i % nts, 0)),
               const(1, ATTN_WIDTH), const(1, ATTN_KV_WIDTH), const(ATTN_WIDTH, ATTN_WIDTH)]
    proj_out = [row(POOL_WIDTH), row(RWKV_PAD), row(ATTN_WIDTH), row(ATTN_KV_WIDTH), row(ATTN_KV_WIDTH)]
    proj_shape = [jax.ShapeDtypeStruct((n, POOL_WIDTH), F32),
                  jax.ShapeDtypeStruct((n, RWKV_PAD), F32),
                  jax.ShapeDtypeStruct((n, ATTN_WIDTH), BF16),
                  jax.ShapeDtypeStruct((n, ATTN_KV_WIDTH), BF16),
                  jax.ShapeDtypeStruct((n, ATTN_KV_WIDTH), BF16)]
    if prev is None:
        return pl.pallas_call(
            _proj_kernel,
            grid=(nt,),
            in_specs=[row(D_MODEL)] + proj_in,
            out_specs=proj_out,
            out_shape=proj_shape,
            compiler_params=_cparams(("parallel",)),
            name="proj",
        )(x2d, w, cos8, sin8, qg, kg, ones_q)
    x1, yab, route, lg, lb = prev
    return pl.pallas_call(
        _combine_proj_kernel,
        grid=(nt,),
        in_specs=[row(D_MODEL), row(HALF_D), pl.BlockSpec((tm, HALF_D), lambda i: (i + nt, 0)),
                  row(ROUTE_LANES), const(1, D_MODEL), const(1, D_MODEL)] + proj_in,
        out_specs=[row(D_MODEL)] + proj_out,
        out_shape=[jax.ShapeDtypeStruct((n, D_MODEL), F32)] + proj_shape,
        compiler_params=_cparams(("parallel",)),
        name="combine_proj",
    )(x1, yab, yab, route, lg, lb, w, cos8, sin8, qg, kg, ones_q)


def _pool_kernel(u_ref, w_ref, scale_ref, o_ref):
    s = u_ref.shape[1]
    u = u_ref[0]
    t = lax.broadcasted_iota(jnp.int32, (s, POOL_WIDTH), 0)

    def down(x, k):
        return jnp.where(t >= k, pltpu.roll(x, k, 0), 0.0)

    def up(x, k):
        return jnp.where(t < s - k, pltpu.roll(x, s - k, 0), 0.0)

    left = [down(u, 1)]
    right = [u]
    for k in (1, 2, 4):
        left.append(left[-1] + down(left[-1], k))
        right.append(right[-1] + up(right[-1], k))
    sums = [l + r for l, r in zip(left, right)]

    grp = lax.broadcasted_iota(jnp.int32, (s, POOL_WIDTH), 1) // POOL_GROUP
    half = jnp.where(grp == 0, 1, jnp.where(grp == 1, 2, jnp.where(grp == 2, 4, 8)))
    cnt = (jnp.minimum(t + half, s) - jnp.maximum(t - half, 0)).astype(F32)
    tot = jnp.where(grp == 0, sums[0], jnp.where(grp == 1, sums[1], jnp.where(grp == 2, sums[2], sums[3])))
    d = tot / cnt - u
    o_ref[0] = _dot(d.astype(BF16), w_ref[...]) * scale_ref[...]


def _pool(u3d, w_bd, scale):
    b, s, _ = u3d.shape
    return pl.pallas_call(
        _pool_kernel,
        grid=(b,),
        in_specs=[pl.BlockSpec((1, s, POOL_WIDTH), lambda i: (i, 0, 0)),
                  pl.BlockSpec((POOL_WIDTH, POOL_WIDTH), lambda i: (0, 0)),
                  pl.BlockSpec((1, POOL_WIDTH), lambda i: (0, 0))],
        out_specs=pl.BlockSpec((1, s, POOL_WIDTH), lambda i: (i, 0, 0)),
        out_shape=jax.ShapeDtypeStruct((b, s, POOL_WIDTH), F32),
        compiler_params=_cparams(("parallel",)),
        name="pool",
    )(u3d, w_bd, scale)


_SHIFT_HALO = 8


def _rwkv_prep_kernel(cur_ref, prev_ref, next_ref, mup_ref, mun_ref, wl_ref, bias_ref,
                      kk_ref, ka_ref, rk_ref, ones_ref,
                      sh_ref, dir_ref, post_ref):
    ts = cur_ref.shape[1]
    ti = pl.program_id(1)
    nt = pl.num_programs(1)
    cur = cur_ref[0]
    row = lax.broadcasted_iota(jnp.int32, cur.shape, 0)
    before = jnp.where(ti > 0, prev_ref[0, _SHIFT_HALO - 1:_SHIFT_HALO, :], 0.0)
    after = jnp.where(ti < nt - 1, next_ref[0, 0:1, :], 0.0)
    prev = jnp.where(row == 0, before, pltpu.roll(cur, 1, 0))
    nxt = jnp.where(row == ts - 1, after, pltpu.roll(cur, ts - 1, 0))
    f = cur + mup_ref[...] * (prev - cur) + mun_ref[...] * (nxt - cur)

    w = RWKV_WIDTH
    r = f[:, 0:w]
    k = f[:, w:2 * w]
    v = f[:, 2 * w:3 * w]
    lora = f[:, 3 * w:4 * w]
    lane = lax.broadcasted_iota(jnp.int32, lora.shape, 1)
    z = jnp.where(lane < 2 * DECAY_LORA, jnp.tanh(lora),
                  jnp.where(lane < 2 * DECAY_LORA + 2 * AAA_LORA, lora,
                            jnp.where(lane < 2 * DECAY_LORA + 2 * AAA_LORA + GATE_LORA,
                                      jax.nn.sigmoid(lora), 0.0)))
    up = _dot(z.astype(BF16), wl_ref[...]) + bias_ref[...]
    ones = ones_ref[...]

    kk0 = k * kk_ref[...]
    nrm = jnp.sqrt(_headsum(kk0 * kk0, ones))
    kk = kk0 / jnp.maximum(nrm, 1e-12)
    sh_ref[0, :, 0:w] = r
    sh_ref[0, :, w:2 * w] = v
    sh_ref[0, :, 2 * w:3 * w] = kk

    ksum = jnp.zeros_like(k)
    for di in range(2):
        logw = -math.exp(-0.5) * jax.nn.sigmoid(up[:, di * w:(di + 1) * w])
        a = jax.nn.sigmoid(up[:, (2 + di) * w:(3 + di) * w])
        kh = k * (1.0 + (a - 1.0) * ka_ref[...])
        ksum = ksum + kh
        dir_ref[di, 0, :, 0:w] = logw
        dir_ref[di, 0, :, w:2 * w] = kh
        dir_ref[di, 0, :, 2 * w:3 * w] = -(a * kk)
    bonus = _headsum(r * ksum * rk_ref[...], ones) * v
    post_ref[0, :, 0:w] = bonus
    post_ref[0, :, w:2 * w] = up[:, 4 * w:5 * w]


def _rwkv_prep(rw3d, mup, mun, wl, bias, k_k, k_a, r_k, ones, ts):
    b, s, _ = rw3d.shape
    nt = s // ts
    hb = ts // _SHIFT_HALO
    nhb = s // _SHIFT_HALO
    w = RWKV_WIDTH
    vec = lambda width: pl.BlockSpec((1, width), lambda i, j: (0, 0))
    return pl.pallas_call(
        _rwkv_prep_kernel,
        grid=(b, nt),
        in_specs=[pl.BlockSpec((1, ts, RWKV_PAD), lambda i, j: (i, j, 0)),
                  pl.BlockSpec((1, _SHIFT_HALO, RWKV_PAD),
                               lambda i, j: (i, jnp.maximum(j * hb - 1, 0), 0)),
                  pl.BlockSpec((1, _SHIFT_HALO, RWKV_PAD),
                               lambda i, j: (i, jnp.minimum((j + 1) * hb, nhb - 1), 0)),
                  vec(RWKV_PAD), vec(RWKV_PAD),
                  pl.BlockSpec((w, 5 * w), lambda i, j: (0, 0)),
                  vec(5 * w), vec(w), vec(w), vec(w),
                  pl.BlockSpec((w, w), lambda i, j: (0, 0))],
        out_specs=[pl.BlockSpec((1, ts, 3 * w), lambda i, j: (i, j, 0)),
                   pl.BlockSpec((2, 1, ts, 3 * w), lambda i, j: (0, i, j, 0)),
                   pl.BlockSpec((1, ts, 2 * w), lambda i, j: (i, j, 0))],
        out_shape=[jax.ShapeDtypeStruct((b, s, 3 * w), F32),
                   jax.ShapeDtypeStruct((2, b, s, 3 * w), F32),
                   jax.ShapeDtypeStruct((b, s, 2 * w), F32)],
        compiler_params=_cparams(("parallel", "parallel")),
        name="rwkv_prep",
    )(rw3d, rw3d, rw3d, mup, mun, wl, bias, k_k, k_a, r_k, ones)


def _expand_bd(x_bf16, mask_bd):
    return jnp.where(mask_bd, jnp.concatenate([x_bf16] * RWKV_HEADS, axis=0), jnp.zeros((), BF16))


def _wkv_kernel(shf_ref, shr_ref, df_ref, dr_ref, yf_ref, yr_ref, hf_ref, hr_ref):
    c = WKV_CHUNK
    w = RWKV_WIDTH
    tt = shf_ref.shape[1]
    ncs = tt // c

    @pl.when(pl.program_id(1) == 0)
    def _():
        hf_ref[...] = jnp.zeros_like(hf_ref)
        hr_ref[...] = jnp.zeros_like(hr_ref)

    row_c = lax.broadcasted_iota(jnp.int32, (c, c), 0)
    col_c = lax.broadcasted_iota(jnp.int32, (c, c), 1)
    t_i = lax.broadcasted_iota(jnp.int32, (c, w), 0)
    s_i = lax.broadcasted_iota(jnp.int32, (c, w), 1) % c
    eye_c = (s_i == t_i).astype(F32)
    row_w = lax.broadcasted_iota(jnp.int32, (w, w), 0)
    col_w = lax.broadcasted_iota(jnp.int32, (w, w), 1)
    mask_bd = (row_w // c) == (col_w // HEAD_DIM)
    eye_w = row_w == col_w
    tri_incl = [jnp.where(col_c <= row_c, 1.0, 0.0).astype(BF16),
                jnp.where(col_c >= row_c, 1.0, 0.0).astype(BF16)]
    strict = [s_i < t_i, s_i > t_i]
    incl = [s_i <= t_i, s_i >= t_i]

    def bf(x):
        return x.astype(BF16)

    def bd(x):
        return _expand_bd(bf(x), mask_bd)

    chunks = []
    for ci in range(ncs):
        chunks.append((0, ci * c, shf_ref, df_ref))
        chunks.append((1, (ncs - 1 - ci) * c, shr_ref, dr_ref))

    st = []
    for d, start, sh_ref, dir_ref in chunks:
        rows = pl.ds(start, c)
        st.append(dict(d=d, rows=rows,
                       r=sh_ref[0, rows, 0:w], v=sh_ref[0, rows, w:2 * w], kk=sh_ref[0, rows, 2 * w:3 * w],
                       lw=dir_ref[0, rows, 0:w], kh=dir_ref[0, rows, w:2 * w], nb=dir_ref[0, rows, 2 * w:3 * w]))

    for q in st:
        l_hi, l_mid, l_lo = _split3(q["lw"])
        tri = tri_incl[q["d"]]
        q["cum"] = _dot(tri, l_hi) + _dot(tri, l_mid) + _dot(tri, l_lo)
    for q in st:
        cum, lw = q["cum"], q["lw"]
        tot = jnp.sum(lw, axis=0, keepdims=True)
        e_inv = jnp.exp(-cum)
        e_end = jnp.exp(tot - cum)
        q["tot"] = tot
        q["a_bar"] = q["kk"] * jnp.exp(cum - lw)
        q["r_bar"] = q["r"] * jnp.exp(cum)
        q["b_hat"] = bf(q["nb"] * e_end)
        q["k_hat"] = bf(q["kh"] * e_end)
        q["v_bf"] = bf(q["v"])
        q["v_bd"] = _expand_bd(q["v_bf"], mask_bd)
        q["lhs"] = jnp.concatenate([bf(q["a_bar"]), bf(q["r_bar"])], axis=0)
        q["rhs"] = jnp.concatenate([bd(q["nb"] * e_inv), bd(q["kh"] * e_inv)], axis=0)
    for q in st:
        sc = _dot_nt(q["lhs"], q["rhs"])
        sm, im = strict[q["d"]], incl[q["d"]]
        q["x"] = jnp.where(sm, sc[0:c, 0:w], 0.0)
        q["a_ak"] = bf(jnp.where(sm, sc[0:c, w:2 * w], 0.0))
        q["m_rb"] = bf(jnp.where(im, sc[c:2 * c, 0:w], 0.0))
        q["m_rk"] = bf(jnp.where(im, sc[c:2 * c, w:2 * w], 0.0))
        q["tinv"] = eye_c + q["x"]

    for q in st:
        q["x"] = _dot(bf(q["x"]), bd(q["x"]))
    n_lv = int(math.log2(c)) - 1
    for lv in range(n_lv):
        last = lv == n_lv - 1
        for q in st:
            xb = bd(q["x"])
            if last:
                q["tinv"] = q["tinv"] + _dot(bf(q["tinv"]), xb)
            else:
                res = _dot(jnp.concatenate([bf(q["x"]), bf(q["tinv"])], axis=0), xb)
                q["tinv"] = q["tinv"] + res[c:2 * c]
                q["x"] = res[0:c]
    for q in st:
        res = _dot(jnp.concatenate([q["a_ak"], q["m_rk"]], axis=0), q["v_bd"])
        q["akv"] = res[0:c]
        q["mrkv"] = res[c:2 * c]
    for q in st:
        tinv_bf = bf(q["tinv"])
        q["a_pr"] = bf(_dot(tinv_bf, bd(q["a_bar"])))
        q["v_pr"] = bf(_dot(tinv_bf, bd(q["akv"])))
    for q in st:
        q["r_pr"] = bf(q["r_bar"] + _dot(q["m_rb"], _expand_bd(q["a_pr"], mask_bd)))
        q["y_pr"] = _dot(q["m_rb"], _expand_bd(q["v_pr"], mask_bd)) + q["mrkv"]
    for q in st:
        q["p_bd"] = bf(jnp.where(eye_w, jnp.exp(q["tot"]), 0.0)
                       + jnp.where(mask_bd, _dot_tn(q["b_hat"], q["a_pr"]), 0.0))
        q["q_bd"] = jnp.where(mask_bd,
                              _dot_tn(jnp.concatenate([q["b_hat"], q["k_hat"]], axis=0),
                                      jnp.concatenate([q["v_pr"], q["v_bf"]], axis=0)), 0.0)

    h = [hf_ref[...], hr_ref[...]]
    y_refs = [yf_ref, yr_ref]
    for q in st:
        d = q["d"]
        res = _dot(jnp.concatenate([q["r_pr"], q["p_bd"]], axis=0), bf(h[d]))
        y_refs[d][0, q["rows"], :] = res[0:c] + q["y_pr"]
        h[d] = res[c:c + w] + q["q_bd"]
    hf_ref[...] = h[0]
    hr_ref[...] = h[1]


def _wkv(shared, dirp, tt):
    b, s, _ = shared.shape
    nt = s // tt
    w = RWKV_WIDTH
    fwd = lambda i, j: (i, j, 0)
    bwd = lambda i, j: (i, nt - 1 - j, 0)
    return pl.pallas_call(
        _wkv_kernel,
        grid=(b, nt),
        in_specs=[pl.BlockSpec((1, tt, 3 * w), fwd), pl.BlockSpec((1, tt, 3 * w), bwd),
                  pl.BlockSpec((None, 1, tt, 3 * w), lambda i, j: (0, i, j, 0)),
                  pl.BlockSpec((None, 1, tt, 3 * w), lambda i, j: (1, i, nt - 1 - j, 0))],
        out_specs=[pl.BlockSpec((1, tt, w), fwd), pl.BlockSpec((1, tt, w), bwd)],
        out_shape=[jax.ShapeDtypeStruct((b, s, w), F32), jax.ShapeDtypeStruct((b, s, w), F32)],
        scratch_shapes=[pltpu.VMEM((w, w), F32), pltpu.VMEM((w, w), F32)],
        compiler_params=_cparams(("parallel", "arbitrary")),
        name="wkv",
    )(shared, shared, dirp, dirp)


def _attn_kernel(q_ref, k_ref, v_ref, o_ref):
    tq = q_ref.shape[0]
    k = k_ref[...]
    v = v_ref[...]
    lane = lax.broadcasted_iota(jnp.int32, (tq, ATTN_KV_WIDTH), 1)
    low = lane < HEAD_DIM
    zero = jnp.zeros((), BF16)
    for j in range(ATTN_HEADS // 2):
        qb = q_ref[:, j * ATTN_KV_WIDTH:(j + 1) * ATTN_KV_WIDTH]
        outs = []
        for half in range(2):
            qm = jnp.where(low if half == 0 else ~low, qb, zero)
            s = _dot_nt(qm, k)
            m = jnp.max(s, axis=-1, keepdims=True)
            p = jnp.exp2(s - m)
            l = jnp.sum(p, axis=-1, keepdims=True)
            outs.append(_dot(p.astype(BF16), v) / l)
        o_ref[:, j * ATTN_KV_WIDTH:(j + 1) * ATTN_KV_WIDTH] = jnp.where(low, outs[0], outs[1]).astype(BF16)


def _attn(q, k, v, seq, tq):
    n = q.shape[0]
    nq = seq // tq
    return pl.pallas_call(
        _attn_kernel,
        grid=(n // seq, nq),
        in_specs=[pl.BlockSpec((tq, ATTN_WIDTH), lambda b, i: (b * nq + i, 0)),
                  pl.BlockSpec((seq, ATTN_KV_WIDTH), lambda b, i: (b, 0)),
                  pl.BlockSpec((seq, ATTN_KV_WIDTH), lambda b, i: (b, 0))],
        out_specs=pl.BlockSpec((tq, ATTN_WIDTH), lambda b, i: (b * nq + i, 0)),
        out_shape=jax.ShapeDtypeStruct((n, ATTN_WIDTH), BF16),
        compiler_params=_cparams(("parallel", "parallel")),
        name="attn",
    )(q, k, v)


def _outproj_kernel(x_ref, pool_ref, wkvf_ref, wkvr_ref, post_ref, attn_ref, wo_ref, gng_ref, gnb_ref,
                    ones_ref, lg_ref, lb_ref, wrh_ref, wrl_ref, br_ref,
                    x1_ref, x1p_ref, route_ref):
    w = RWKV_WIDTH
    ones = ones_ref[...]
    wkv = wkvf_ref[...] + wkvr_ref[...]
    inv = 1.0 / HEAD_DIM
    mu = _headsum(wkv, ones) * inv
    cen = wkv - mu
    var = _headsum(cen * cen, ones) * inv
    yr = cen * lax.rsqrt(var + GN_EPS) * gng_ref[...] + gnb_ref[...] + post_ref[:, 0:w]
    yr = yr * post_ref[:, w:2 * w]
    y = (_dot(pool_ref[...].astype(BF16), wo_ref[0:POOL_WIDTH, :])
         + _dot(yr.astype(BF16), wo_ref[POOL_WIDTH:POOL_WIDTH + w, :])
         + _dot(attn_ref[...], wo_ref[POOL_WIDTH + w:D_MODEL, :]))
    x1 = _layer_norm(DEEPNORM_ALPHA * x_ref[...] + y, lg_ref[...], lb_ref[...])
    x1_ref[...] = x1
    x1p_ref[...] = _pack_rows(x1)

    xh, xl = _split2(x1)
    logits = (_dot(xh, wrh_ref[...]) + _dot(xl, wrh_ref[...]) + _dot(xh, wrl_ref[...])) + br_ref[...]
    lane = lax.broadcasted_iota(jnp.int32, logits.shape, 1)
    lane_f = lane.astype(F32)
    lane_grp_f = ((lane - N_GROUPS) // EXPERTS_PER_GROUP).astype(F32)
    neg = -jnp.inf
    big = jnp.float32(1 << 20)
    gl = jnp.where(lane < N_GROUPS, logits, neg)
    gmax = jnp.max(gl, axis=-1, keepdims=True)
    grp = jnp.min(jnp.where(gl == gmax, lane_f, big), axis=-1, keepdims=True)
    gw = 1.0 / jnp.sum(jnp.exp(gl - gmax), axis=-1, keepdims=True)
    in_grp = (lane >= N_GROUPS) & (lane < N_GROUPS + N_EXPERTS) & (lane_grp_f == grp)
    el = jnp.where(in_grp, logits, neg)
    v1 = jnp.max(el, axis=-1, keepdims=True)
    i1 = jnp.min(jnp.where(el == v1, lane_f, big), axis=-1, keepdims=True)
    el2 = jnp.where(lane_f == i1, neg, el)
    v2 = jnp.max(el2, axis=-1, keepdims=True)
    i2 = jnp.min(jnp.where(el2 == v2, lane_f, big), axis=-1, keepdims=True)
    e21 = jnp.exp(v2 - v1)
    g1 = gw / (1.0 + e21)
    g2 = gw * e21 / (1.0 + e21)
    route = jnp.where(lane == 0, i1 - N_GROUPS,
                      jnp.where(lane == 1, i2 - N_GROUPS,
                                jnp.where(lane == 2, g1, jnp.where(lane == 3, g2, 0.0))))
    route_ref[...] = route


def _outproj(x2d, pool2d, wkvf, wkvr, post2d, attn2d, wo, gng, gnb, ones, lg, lb, wrh, wrl, br, tm):
    n = x2d.shape[0]
    w = RWKV_WIDTH
    row = lambda width: pl.BlockSpec((tm, width), lambda i: (i, 0))
    vec = lambda width: pl.BlockSpec((1, width), lambda i: (0, 0))
    return pl.pallas_call(
        _outproj_kernel,
        grid=(n // tm,),
        in_specs=[row(D_MODEL), row(POOL_WIDTH), row(w), row(w),
                  row(2 * w), row(ATTN_WIDTH),
                  pl.BlockSpec((D_MODEL, D_MODEL), lambda i: (0, 0)),
                  vec(w), vec(w),
                  pl.BlockSpec((w, w), lambda i: (0, 0)),
                  vec(D_MODEL), vec(D_MODEL),
                  pl.BlockSpec((D_MODEL, ROUTE_LANES), lambda i: (0, 0)),
                  pl.BlockSpec((D_MODEL, ROUTE_LANES), lambda i: (0, 0)),
                  vec(ROUTE_LANES)],
        out_specs=[row(D_MODEL), row(HALF_D), row(ROUTE_LANES)],
        out_shape=[jax.ShapeDtypeStruct((n, D_MODEL), F32),
                   jax.ShapeDtypeStruct((n, HALF_D), U32),
                   jax.ShapeDtypeStruct((n, ROUTE_LANES), F32)],
        compiler_params=_cparams(("parallel",)),
        name="outproj",
    )(x2d, pool2d, wkvf, wkvr, post2d, attn2d, wo, gng, gnb, ones, lg, lb, wrh, wrl, br)


def _expert_kernel(bexp_ref, nused_ref, xs_ref, wg_ref, wu_ref, wd_ref, ys_ref, wgb_ref, wub_ref, wdb_ref):
    i = pl.program_id(0)
    used = i < nused_ref[0]

    @pl.when(used & ((i == 0) | (bexp_ref[i] != bexp_ref[jnp.maximum(i - 1, 0)])))
    def _():
        wgb_ref[...] = wg_ref[0].astype(BF16)
        wub_ref[...] = wu_ref[0].astype(BF16)
        wdb_ref[...] = wd_ref[0].astype(BF16)

    @pl.when(used)
    def _():
        xa, xb = _unpack_rows(xs_ref[...])
        h1 = _dot(xa, wgb_ref[0:HALF_D, :]) + _dot(xb, wgb_ref[HALF_D:D_MODEL, :])
        h2 = _dot(xa, wub_ref[0:HALF_D, :]) + _dot(xb, wub_ref[HALF_D:D_MODEL, :])
        h = (h1 * jax.nn.sigmoid(h1) * h2).astype(BF16)
        ys_ref[...] = _pack_rows(_dot(h, wdb_ref[...]))

    @pl.when(jnp.logical_not(used))
    def _():
        ys_ref[...] = jnp.zeros_like(ys_ref)


def _experts(block_exp, n_used, xs, wg, wu, wd, layer):
    p = xs.shape[0]
    nb = p // MOE_BLOCK
    wmap = lambda i, be, nu: (layer, be[i], 0, 0)
    grid_spec = pltpu.PrefetchScalarGridSpec(
        num_scalar_prefetch=2,
        grid=(nb,),
        in_specs=[pl.BlockSpec((MOE_BLOCK, HALF_D), lambda i, be, nu: (i, 0)),
                  pl.BlockSpec((None, 1, D_MODEL, EXPERT_HIDDEN), wmap),
                  pl.BlockSpec((None, 1, D_MODEL, EXPERT_HIDDEN), wmap),
                  pl.BlockSpec((None, 1, EXPERT_HIDDEN, D_MODEL), wmap)],
        out_specs=pl.BlockSpec((MOE_BLOCK, HALF_D), lambda i, be, nu: (i, 0)),
        scratch_shapes=[pltpu.VMEM((D_MODEL, EXPERT_HIDDEN), BF16),
                        pltpu.VMEM((D_MODEL, EXPERT_HIDDEN), BF16),
                        pltpu.VMEM((EXPERT_HIDDEN, D_MODEL), BF16)],
    )
    return pl.pallas_call(
        _expert_kernel,
        grid_spec=grid_spec,
        out_shape=jax.ShapeDtypeStruct((p, HALF_D), U32),
        compiler_params=_cparams(("arbitrary",)),
        name="experts",
    )(block_exp, n_used, xs, wg, wu, wd)


SC_CORES = 2
SC_SUBCORES = 16
SC_GATHER_ROWS = 64


def _sc_gather(table, idx):
    v, d = table.shape
    b = idx.shape[0]
    nw = SC_CORES * SC_SUBCORES
    ch = SC_GATHER_ROWS
    assert b % (nw * ch * 2) == 0
    b_per_w = b // nw
    nch = b_per_w // ch
    mesh = plsc.VectorSubcoreMesh(core_axis_name="c", subcore_axis_name="s")

    @functools.partial(
        pl.kernel, mesh=mesh,
        out_type=jax.ShapeDtypeStruct((b, d), table.dtype),
        scratch_types=[pltpu.VMEM((b_per_w,), jnp.int32),
                       pltpu.VMEM((2, ch, d), table.dtype),
                       pltpu.SemaphoreType.DMA, pltpu.SemaphoreType.DMA,
                       pltpu.SemaphoreType.DMA, pltpu.SemaphoreType.DMA],
    )
    def gather_kernel(table_hbm, idx_hbm, out_hbm, idx_v, rows_v, gsem0, gsem1, wsem0, wsem1):
        wid = lax.axis_index("s") * SC_CORES + lax.axis_index("c")
        base = wid * b_per_w
        pltpu.sync_copy(idx_hbm.at[pl.ds(base, b_per_w)], idx_v)
        gsem = (gsem0, gsem1)
        wsem = (wsem0, wsem1)

        def gather(jj, slot):
            off = pl.multiple_of(jj * ch, ch)
            return pltpu.make_async_copy(table_hbm.at[idx_v.at[pl.ds(off, ch)]], rows_v.at[slot], gsem[slot])

        def write(jj, slot):
            off = pl.multiple_of(jj * ch, ch)
            return pltpu.make_async_copy(rows_v.at[slot], out_hbm.at[pl.ds(base + off, ch)], wsem[slot])

        gather(0, 0).start()

        @pl.loop(0, nch, step=2)
        def _(j):
            for slot in range(2):
                jj = j + slot

                gather(jj, slot).wait()

                @pl.when(jj >= 1)
                def _():
                    write(jj - 1, 1 - slot).wait()

                @pl.when(jj + 1 < nch)
                def _():
                    gather(jj + 1, 1 - slot).start()

                write(jj, slot).start()

        write(nch - 1, (nch - 1) % 2).wait()

    return gather_kernel(table, idx)


def _gather_rows(table, idx):
    b = idx.shape[0]
    unit = SC_CORES * SC_SUBCORES * SC_GATHER_ROWS * 2
    bp = -(-b // unit) * unit
    if bp == b:
        return _sc_gather(table, idx)
    return _sc_gather(table, jnp.pad(idx, (0, bp - b)))[:b]


def _combine_kernel(x1_ref, ya_ref, yb_ref, route_ref, lg_ref, lb_ref, o_ref):
    o_ref[...] = _combine_rows(x1_ref[...], ya_ref[...], yb_ref[...], route_ref[...], lg_ref[...], lb_ref[...])


def _combine(x1, yab, route, lg, lb, tm):
    n = x1.shape[0]
    nt = n // tm
    row = lambda width: pl.BlockSpec((tm, width), lambda i: (i, 0))
    vec = lambda width: pl.BlockSpec((1, width), lambda i: (0, 0))
    return pl.pallas_call(
        _combine_kernel,
        grid=(nt,),
        in_specs=[row(D_MODEL), row(HALF_D), pl.BlockSpec((tm, HALF_D), lambda i: (i + nt, 0)),
                  row(ROUTE_LANES), vec(D_MODEL), vec(D_MODEL)],
        out_specs=row(D_MODEL),
        out_shape=jax.ShapeDtypeStruct((n, D_MODEL), F32),
        compiler_params=_cparams(("parallel",)),
        name="combine",
    )(x1, yab, yab, route, lg, lb)


def _q_perm():
    order = [h for j in range(ATTN_HEADS // 2) for h in (j, j + ATTN_HEADS // 2)]
    return jnp.concatenate([jnp.arange(h * HEAD_DIM, (h + 1) * HEAD_DIM) for h in order])


def _block_diag(blocks):
    n = len(blocks)
    r, c = blocks[0].shape
    out = jnp.zeros((n * r, n * c), blocks[0].dtype)
    for i, blk in enumerate(blocks):
        out = out.at[i * r:(i + 1) * r, i * c:(i + 1) * c].set(blk)
    return out


def _rope_tables(seq):
    rows = seq // GRID_W
    row_id = jnp.repeat(jnp.arange(rows), GRID_W).astype(F32)
    col_id = jnp.tile(jnp.arange(GRID_W), rows).astype(F32)
    half = HEAD_DIM // 2
    inv_freq = ROPE_THETA ** (-jnp.arange(0, half, 2, dtype=F32) / half)
    ang_r = row_id[:, None] * inv_freq
    ang_c = col_id[:, None] * inv_freq
    ang = jnp.concatenate([ang_r, ang_r, ang_c, ang_c], -1)
    sign = jnp.where((jnp.arange(HEAD_DIM) // 16) % 2 == 0, -1.0, 1.0).astype(F32)
    cos8 = jnp.tile(jnp.cos(ang), (1, ATTN_HEADS))
    sin8 = jnp.tile(jnp.sin(ang) * sign, (1, ATTN_HEADS))
    return cos8, sin8


def _dispatch(route, n):
    m = n * 2
    experts = jnp.arange(N_EXPERTS, dtype=jnp.int32)
    e_flat = jnp.concatenate([route[:, 0], route[:, 1]]).astype(jnp.int32)
    pair = jnp.arange(m, dtype=jnp.int32)
    e_sorted, order = lax.sort((e_flat, pair), num_keys=1, is_stable=True)
    start = jnp.sum(e_sorted[:, None] < experts[None, :], axis=0, dtype=jnp.int32)
    counts = jnp.concatenate([start[1:], jnp.full((1,), m, jnp.int32)]) - start
    padded = (counts + MOE_BLOCK - 1) // MOE_BLOCK * MOE_BLOCK
    ends_p = jnp.cumsum(padded)
    pstart = ends_p - padded
    delta = pstart - start
    n_blocks = -(-(m + N_EXPERTS * (MOE_BLOCK - 1)) // MOE_BLOCK)
    block_start = jnp.arange(n_blocks, dtype=jnp.int32) * MOE_BLOCK
    block_exp = jnp.minimum(jnp.sum(ends_p[None, :] <= block_start[:, None], axis=1, dtype=jnp.int32),
                            N_EXPERTS - 1)
    lane = jnp.arange(MOE_BLOCK, dtype=jnp.int32)[None, :]
    valid = (block_start - pstart[block_exp])[:, None] + lane < counts[block_exp][:, None]
    src = jnp.clip((block_start - delta[block_exp])[:, None] + lane, 0, m - 1)
    run = order[src.reshape(-1)].reshape(n_blocks, MOE_BLOCK)
    row_tok = jnp.where(valid, run % n, (block_start[:, None] + lane) % n).reshape(-1)
    n_used = (ends_p[-1] // MOE_BLOCK).astype(jnp.int32).reshape(1)
    dest = pair + jnp.sum(jnp.where(e_sorted[:, None] == experts[None, :], delta[None, :], 0), axis=1,
                          dtype=jnp.int32)
    _, pos = lax.sort((order, dest), num_keys=1)
    return row_tok, pos, block_exp, n_used


def kernel(x, w_in, mu_prev, mu_next, pool_w, pool_scale, rw_w0, rw_w_up, rw_a0, rw_a_up, rw_g_up, rw_k_k, rw_k_a, rw_r_k, rw_gn_g, rw_gn_b, q_norm, k_norm, w_o, ln1_g, ln1_b, router_group, router_group_b, router_expert, router_expert_b, exp_gate, exp_up, exp_down, ln2_g, ln2_b):
    b, s, d = x.shape
    n = b * s
    w = RWKV_WIDTH
    tm = min(512, s)
    qperm = _q_perm()
    cos8, sin8 = _rope_tables(s)
    ones_q = _block_diag([jnp.ones((HEAD_DIM, HEAD_DIM), BF16)] * ATTN_HEADS)
    ones_r = ones_q[0:w, 0:w]
    a_end = POOL_WIDTH
    b_end = POOL_WIDTH + RWKV_IN

    xc = x.reshape(n, d)
    pending = None
    for l in range(DEPTH):
        wq = w_in[l][:, b_end:b_end + ATTN_WIDTH][:, qperm]
        w_proj = jnp.concatenate(
            [w_in[l][:, :a_end], w_in[l][:, a_end:b_end],
             jnp.zeros((d, RWKV_PAD - RWKV_IN), F32), wq, w_in[l][:, b_end + ATTN_WIDTH:]],
            axis=1).astype(BF16)
        pool_bd = _block_diag([pool_w[l, g] for g in range(len(POOL_WINDOWS))]).astype(BF16)
        pad_vec = jnp.zeros((RWKV_PAD - RWKV_IN,), F32)
        mup = jnp.concatenate([mu_prev[l], pad_vec]).reshape(1, RWKV_PAD)
        mun = jnp.concatenate([mu_next[l], pad_vec]).reshape(1, RWKV_PAD)
        w_lora = jnp.zeros((w, 5 * w), F32)
        for di in range(2):
            w_lora = w_lora.at[di * DECAY_LORA:(di + 1) * DECAY_LORA, di * w:(di + 1) * w].set(rw_w_up[l, di])
            o = 2 * DECAY_LORA
            w_lora = w_lora.at[o + di * AAA_LORA:o + (di + 1) * AAA_LORA, (2 + di) * w:(3 + di) * w].set(rw_a_up[l, di])
        o = 2 * DECAY_LORA + 2 * AAA_LORA
        w_lora = w_lora.at[o:o + GATE_LORA, 4 * w:5 * w].set(rw_g_up[l]).astype(BF16)
        lora_bias = jnp.concatenate([rw_w0[l, 0], rw_w0[l, 1], rw_a0[l, 0], rw_a0[l, 1],
                                     jnp.zeros((w,), F32)]).reshape(1, 5 * w)
        qg = jnp.tile(q_norm[l], ATTN_HEADS).reshape(1, ATTN_WIDTH)
        kg = jnp.tile(k_norm[l], 2).reshape(1, ATTN_KV_WIDTH)
        wo_attn = w_o[l][POOL_WIDTH + w:][qperm]
        wo = jnp.concatenate([w_o[l][:POOL_WIDTH + w], wo_attn], axis=0).astype(BF16)
        wr = jnp.concatenate([router_group[l], router_expert[l],
                              jnp.zeros((d, ROUTE_LANES - N_GROUPS - N_EXPERTS), F32)], axis=1)
        wrh = wr.astype(BF16)
        wrl = (wr - wrh.astype(F32)).astype(BF16)
        br = jnp.concatenate([router_group_b[l], router_expert_b[l],
                              jnp.zeros((ROUTE_LANES - N_GROUPS - N_EXPERTS,), F32)]).reshape(1, ROUTE_LANES)

        if pending is None:
            pool_in, rw_in, qh, kh, vh = _proj(xc, w_proj, cos8, sin8, qg, kg, ones_q, s, tm)
        else:
            xc, pool_in, rw_in, qh, kh, vh = _proj(None, w_proj, cos8, sin8, qg, kg, ones_q, s, tm, prev=pending)
        y_pool = _pool(pool_in.reshape(b, s, POOL_WIDTH), pool_bd, pool_scale[l].reshape(1, POOL_WIDTH))
        shared, dirp, post = _rwkv_prep(rw_in.reshape(b, s, RWKV_PAD), mup, mun, w_lora, lora_bias,
                                        rw_k_k[l].reshape(1, w), rw_k_a[l].reshape(1, w),
                                        rw_r_k[l].reshape(1, w), ones_r, tm)
        wkv_f, wkv_r = _wkv(shared, dirp, min(256, s))
        y_attn = _attn(qh, kh, vh, s, min(512, s))
        x1, x1p, route = _outproj(xc, y_pool.reshape(n, POOL_WIDTH), wkv_f.reshape(n, w), wkv_r.reshape(n, w),
                                  post.reshape(n, 2 * w), y_attn, wo,
                                  rw_gn_g[l].reshape(1, w), rw_gn_b[l].reshape(1, w), ones_r,
                                  ln1_g[l].reshape(1, d), ln1_b[l].reshape(1, d), wrh, wrl, br, tm)

        row_tok, pos, block_exp, n_used = _dispatch(route, n)
        xs = _gather_rows(x1p, row_tok)
        ys = _experts(block_exp, n_used, xs, exp_gate, exp_up, exp_down, l)
        yab = _gather_rows(ys, pos)
        pending = (x1, yab, route, ln2_g[l].reshape(1, d), ln2_b[l].reshape(1, d))
    return _combine(*pending, tm).reshape(b, s, d)
```

```python
import functools
import math

import jax
import jax.numpy as jnp
from jax import lax
from jax.experimental import pallas as pl
from jax.experimental.pallas import tpu as pltpu
from jax.experimental.pallas import tpu_sc as plsc

F32 = jnp.float32
BF16 = jnp.bfloat16

D_MODEL = 1024
DEPTH = 4
GRID_W = 64
HEAD_DIM = 64
POOL_WIDTH = 256
POOL_WINDOWS = (2, 4, 8, 16)
POOL_GROUP = 64
RWKV_WIDTH = 256
RWKV_HEADS = 4
DECAY_LORA = 32
AAA_LORA = 32
GATE_LORA = 64
GN_EPS = 64e-5
RWKV_IN = 960
RWKV_PAD = 1024
ATTN_WIDTH = 512
ATTN_HEADS = 8
ATTN_KV_WIDTH = 128
ATTN_IN = ATTN_WIDTH + 2 * ATTN_KV_WIDTH
ROPE_THETA = 10000.0
QK_EPS = 1e-6
N_GROUPS = 4
EXPERTS_PER_GROUP = 8
N_EXPERTS = 32
EXPERT_HIDDEN = 512
MOE_BLOCK = 512
DEEPNORM_ALPHA = float((2 * DEPTH) ** 0.25)
LN_EPS = 1e-5
PROJ_WIDTH = POOL_WIDTH + RWKV_PAD + ATTN_IN
ROUTE_LANES = 128
WKV_CHUNK = 64
VMEM_LIMIT = 48 * 1024 * 1024


def _cparams(sem):
    return pltpu.CompilerParams(dimension_semantics=sem, vmem_limit_bytes=VMEM_LIMIT)


def _dot(a, b):
    return jnp.dot(a, b, preferred_element_type=F32)


def _dot_nt(a, b):
    return lax.dot_general(a, b, (((1,), (1,)), ((), ())), preferred_element_type=F32)


def _dot_tn(a, b):
    return lax.dot_general(a, b, (((0,), (0,)), ((), ())), preferred_element_type=F32)


def _split2(x):
    hi = x.astype(BF16)
    lo = (x - hi.astype(F32)).astype(BF16)
    return hi, lo


def _split3(x):
    hi = x.astype(BF16)
    r1 = x - hi.astype(F32)
    mid = r1.astype(BF16)
    lo = (r1 - mid.astype(F32)).astype(BF16)
    return hi, mid, lo


HALF_D = D_MODEL // 2
U32 = jnp.uint32


def _pack_rows(x):
    hi = lax.bitcast_convert_type(x[:, :HALF_D].astype(BF16).astype(F32), U32)
    lo = lax.bitcast_convert_type(x[:, HALF_D:].astype(BF16).astype(F32), U32)
    return hi | (lo >> 16)


def _unpack_rows(u):
    a = lax.bitcast_convert_type(u & jnp.uint32(0xFFFF0000), F32).astype(BF16)
    b = lax.bitcast_convert_type(u << 16, F32).astype(BF16)
    return a, b


def _headsum(x, ones_bf16):
    hi, lo = _split2(x)
    return _dot(hi, ones_bf16) + _dot(lo, ones_bf16)


def _rope(x, cos, sin_signed):
    n = x.shape[-1]
    lane = lax.broadcasted_iota(jnp.int32, x.shape, 1)
    first = (lane // 16) % 2 == 0
    partner = jnp.where(first, pltpu.roll(x, n - 16, 1), pltpu.roll(x, 16, 1))
    return x * cos + partner * sin_signed


def _layer_norm(z, g, b):
    mu = jnp.mean(z, axis=-1, keepdims=True)
    zc = z - mu
    var = jnp.mean(zc * zc, axis=-1, keepdims=True)
    return zc * lax.rsqrt(var + LN_EPS) * g + b


def _combine_rows(x1, ya, yb, route, g, b):
    g1 = route[:, 2:3]
    g2 = route[:, 3:4]
    a_hi, a_lo = _unpack_rows(ya)
    b_hi, b_lo = _unpack_rows(yb)
    m = jnp.concatenate([g1 * a_hi.astype(F32) + g2 * b_hi.astype(F32),
                         g1 * a_lo.astype(F32) + g2 * b_lo.astype(F32)], axis=1)
    return _layer_norm(DEEPNORM_ALPHA * x1 + m, g, b)


def _proj_kernel(x_ref, w_ref, cos_ref, sin_ref, qg_ref, kg_ref, onesq_ref,
                 pool_ref, rw_ref, q_ref, k_ref, v_ref):
    _proj_rows(x_ref[...], w_ref, cos_ref, sin_ref, qg_ref, kg_ref, onesq_ref,
               pool_ref, rw_ref, q_ref, k_ref, v_ref)


def _combine_proj_kernel(x1_ref, ya_ref, yb_ref, route_ref, lg_ref, lb_ref,
                         w_ref, cos_ref, sin_ref, qg_ref, kg_ref, onesq_ref,
                         x_ref, pool_ref, rw_ref, q_ref, k_ref, v_ref):
    x = _combine_rows(x1_ref[...], ya_ref[...], yb_ref[...], route_ref[...], lg_ref[...], lb_ref[...])
    x_ref[...] = x
    _proj_rows(x, w_ref, cos_ref, sin_ref, qg_ref, kg_ref, onesq_ref, pool_ref, rw_ref, q_ref, k_ref, v_ref)


def _proj_rows(x, w_ref, cos_ref, sin_ref, qg_ref, kg_ref, onesq_ref, pool_ref, rw_ref, q_ref, k_ref, v_ref):
    xb = x.astype(BF16)
    pool_ref[...] = _dot(xb, w_ref[:, 0:POOL_WIDTH])
    rw_ref[...] = _dot(xb, w_ref[:, POOL_WIDTH:POOL_WIDTH + RWKV_PAD])
    at = _dot(xb, w_ref[:, POOL_WIDTH + RWKV_PAD:PROJ_WIDTH])

    q = at[:, 0:ATTN_WIDTH]
    k = at[:, ATTN_WIDTH:ATTN_WIDTH + ATTN_KV_WIDTH]
    v = at[:, ATTN_WIDTH + ATTN_KV_WIDTH:ATTN_IN]
    ones_q = onesq_ref[...]
    ones_k = onesq_ref[0:ATTN_KV_WIDTH, 0:ATTN_KV_WIDTH]
    inv = 1.0 / HEAD_DIM
    qn = q * lax.rsqrt(_headsum(q * q, ones_q) * inv + QK_EPS) * qg_ref[...]
    kn = k * lax.rsqrt(_headsum(k * k, ones_k) * inv + QK_EPS) * kg_ref[...]
    cos = cos_ref[...]
    sin = sin_ref[...]
    q_ref[...] = (_rope(qn, cos, sin) * (HEAD_DIM ** -0.5 * math.log2(math.e))).astype(BF16)
    k_ref[...] = _rope(kn, cos[:, 0:ATTN_KV_WIDTH], sin[:, 0:ATTN_KV_WIDTH]).astype(BF16)
    v_ref[...] = v.astype(BF16)


def _proj(x2d, w, cos8, sin8, qg, kg, ones_q, seq, tm, prev=None):
    n = (x2d if prev is None else prev[0]).shape[0]
    nts = seq // tm
    nt = n // tm
    row = lambda width: pl.BlockSpec((tm, width), lambda i: (i, 0))
    const = lambda r, c: pl.BlockSpec((r, c), lambda i: (0, 0))
    proj_in = [const(D_MODEL, PROJ_WIDTH),
               pl.BlockSpec((tm, ATTN_WIDTH), lambda i: (i % nts, 0)),
               pl.BlockSpec((tm, ATTN_WIDTH), lambda i: (i % nts, 0)),
               const(1, ATTN_WIDTH), const(1, ATTN_KV_WIDTH), const(ATTN_WIDTH, ATTN_WIDTH)]
    proj_out = [row(POOL_WIDTH), row(RWKV_PAD), row(ATTN_WIDTH), row(ATTN_KV_WIDTH), row(ATTN_KV_WIDTH)]
    proj_shape = [jax.ShapeDtypeStruct((n, POOL_WIDTH), F32),
                  jax.ShapeDtypeStruct((n, RWKV_PAD), F32),
                  jax.ShapeDtypeStruct((n, ATTN_WIDTH), BF16),
                  jax.ShapeDtypeStruct((n, ATTN_KV_WIDTH), BF16),
                  jax.ShapeDtypeStruct((n, ATTN_KV_WIDTH), BF16)]
    if prev is None:
        return pl.pallas_call(
            _proj_kernel,
            grid=(nt,),
            in_specs=[row(D_MODEL)] + proj_in,
            out_specs=proj_out,
            out_shape=proj_shape,
            compiler_params=_cparams(("parallel",)),
            name="proj",
        )(x2d, w, cos8, sin8, qg, kg, ones_q)
    x1, yab, route, lg, lb = prev
    return pl.pallas_call(
        _combine_proj_kernel,
        grid=(nt,),
        in_specs=[row(D_MODEL), row(HALF_D), pl.BlockSpec((tm, HALF_D), lambda i: (i + nt, 0)),
                  row(ROUTE_LANES), const(1, D_MODEL), const(1, D_MODEL)] + proj_in,
        out_specs=[row(D_MODEL)] + proj_out,
        out_shape=[jax.ShapeDtypeStruct((n, D_MODEL), F32)] + proj_shape,
        compiler_params=_cparams(("parallel",)),
        name="combine_proj",
    )(x1, yab, yab, route, lg, lb, w, cos8, sin8, qg, kg, ones_q)


def _pool_kernel(u_ref, w_ref, scale_ref, o_ref):
    s = u_ref.shape[1]
    u = u_ref[0]
    t = lax.broadcasted_iota(jnp.int32, (s, POOL_WIDTH), 0)

    def down(x, k):
        return jnp.where(t >= k, pltpu.roll(x, k, 0), 0.0)

    def up(x, k):
        return jnp.where(t < s - k, pltpu.roll(x, s - k, 0), 0.0)

    left = [down(u, 1)]
    right = [u]
    for k in (1, 2, 4):
        left.append(left[-1] + down(left[-1], k))
        right.append(right[-1] + up(right[-1], k))
    sums = [l + r for l, r in zip(left, right)]

    grp = lax.broadcasted_iota(jnp.int32, (s, POOL_WIDTH), 1) // POOL_GROUP
    half = jnp.where(grp == 0, 1, jnp.where(grp == 1, 2, jnp.where(grp == 2, 4, 8)))
    cnt = (jnp.minimum(t + half, s) - jnp.maximum(t - half, 0)).astype(F32)
    tot = jnp.where(grp == 0, sums[0], jnp.where(grp == 1, sums[1], jnp.where(grp == 2, sums[2], sums[3])))
    d = tot / cnt - u
    o_ref[0] = _dot(d.astype(BF16), w_ref[...]) * scale_ref[...]


def _pool(u3d, w_bd, scale):
    b, s, _ = u3d.shape
    return pl.pallas_call(
        _pool_kernel,
        grid=(b,),
        in_specs=[pl.BlockSpec((1, s, POOL_WIDTH), lambda i: (i, 0, 0)),
                  pl.BlockSpec((POOL_WIDTH, POOL_WIDTH), lambda i: (0, 0)),
                  pl.BlockSpec((1, POOL_WIDTH), lambda i: (0, 0))],
        out_specs=pl.BlockSpec((1, s, POOL_WIDTH), lambda i: (i, 0, 0)),
        out_shape=jax.ShapeDtypeStruct((b, s, POOL_WIDTH), F32),
        compiler_params=_cparams(("parallel",)),
        name="pool",
    )(u3d, w_bd, scale)


_SHIFT_HALO = 8


def _rwkv_prep_kernel(cur_ref, prev_ref, next_ref, mup_ref, mun_ref, wl_ref, bias_ref,
                      kk_ref, ka_ref, rk_ref, ones_ref,
                      sh_ref, dir_ref, post_ref):
    ts = cur_ref.shape[1]
    ti = pl.program_id(1)
    nt = pl.num_programs(1)
    cur = cur_ref[0]
    row = lax.broadcasted_iota(jnp.int32, cur.shape, 0)
    before = jnp.where(ti > 0, prev_ref[0, _SHIFT_HALO - 1:_SHIFT_HALO, :], 0.0)
    after = jnp.where(ti < nt - 1, next_ref[0, 0:1, :], 0.0)
    prev = jnp.where(row == 0, before, pltpu.roll(cur, 1, 0))
    nxt = jnp.where(row == ts - 1, after, pltpu.roll(cur, ts - 1, 0))
    f = cur + mup_ref[...] * (prev - cur) + mun_ref[...] * (nxt - cur)

    w = RWKV_WIDTH
    r = f[:, 0:w]
    k = f[:, w:2 * w]
    v = f[:, 2 * w:3 * w]
    lora = f[:, 3 * w:4 * w]
    lane = lax.broadcasted_iota(jnp.int32, lora.shape, 1)
    z = jnp.where(lane < 2 * DECAY_LORA, jnp.tanh(lora),
                  jnp.where(lane < 2 * DECAY_LORA + 2 * AAA_LORA, lora,
                            jnp.where(lane < 2 * DECAY_LORA + 2 * AAA_LORA + GATE_LORA,
                                      jax.nn.sigmoid(lora), 0.0)))
    up = _dot(z.astype(BF16), wl_ref[...]) + bias_ref[...]
    ones = ones_ref[...]

    kk0 = k * kk_ref[...]
    nrm = jnp.sqrt(_headsum(kk0 * kk0, ones))
    kk = kk0 / jnp.maximum(nrm, 1e-12)
    sh_ref[0, :, 0:w] = r
    sh_ref[0, :, w:2 * w] = v
    sh_ref[0, :, 2 * w:3 * w] = kk

    ksum = jnp.zeros_like(k)
    for di in range(2):
        logw = -math.exp(-0.5) * jax.nn.sigmoid(up[:, di * w:(di + 1) * w])
        a = jax.nn.sigmoid(up[:, (2 + di) * w:(3 + di) * w])
        kh = k * (1.0 + (a - 1.0) * ka_ref[...])
        ksum = ksum + kh
        dir_ref[di, 0, :, 0:w] = logw
        dir_ref[di, 0, :, w:2 * w] = kh
        dir_ref[di, 0, :, 2 * w:3 * w] = -(a * kk)
    bonus = _headsum(r * ksum * rk_ref[...], ones) * v
    post_ref[0, :, 0:w] = bonus
    post_ref[0, :, w:2 * w] = up[:, 4 * w:5 * w]


def _rwkv_prep(rw3d, mup, mun, wl, bias, k_k, k_a, r_k, ones, ts):
    b, s, _ = rw3d.shape
    nt = s // ts
    hb = ts // _SHIFT_HALO
    nhb = s // _SHIFT_HALO
    w = RWKV_WIDTH
    vec = lambda width: pl.BlockSpec((1, width), lambda i, j: (0, 0))
    return pl.pallas_call(
        _rwkv_prep_kernel,
        grid=(b, nt),
        in_specs=[pl.BlockSpec((1, ts, RWKV_PAD), lambda i, j: (i, j, 0)),
                  pl.BlockSpec((1, _SHIFT_HALO, RWKV_PAD),
                               lambda i, j: (i, jnp.maximum(j * hb - 1, 0), 0)),
                  pl.BlockSpec((1, _SHIFT_HALO, RWKV_PAD),
                               lambda i, j: (i, jnp.minimum((j + 1) * hb, nhb - 1), 0)),
                  vec(RWKV_PAD), vec(RWKV_PAD),
                  pl.BlockSpec((w, 5 * w), lambda i, j: (0, 0)),
                  vec(5 * w), vec(w), vec(w), vec(w),
                  pl.BlockSpec((w, w), lambda i, j: (0, 0))],
        out_specs=[pl.BlockSpec((1, ts, 3 * w), lambda i, j: (i, j, 0)),
                   pl.BlockSpec((2, 1, ts, 3 * w), lambda i, j: (0, i, j, 0)),
                   pl.BlockSpec((1, ts, 2 * w), lambda i, j: (i, j, 0))],
        out_shape=[jax.ShapeDtypeStruct((b, s, 3 * w), F32),
                   jax.ShapeDtypeStruct((2, b, s, 3 * w), F32),
                   jax.ShapeDtypeStruct((b, s, 2 * w), F32)],
        compiler_params=_cparams(("parallel", "parallel")),
        name="rwkv_prep",
    )(rw3d, rw3d, rw3d, mup, mun, wl, bias, k_k, k_a, r_k, ones)


def _expand_bd(x_bf16, mask_bd):
    return jnp.where(mask_bd, jnp.concatenate([x_bf16] * RWKV_HEADS, axis=0), jnp.zeros((), BF16))


def _wkv_kernel(shf_ref, shr_ref, df_ref, dr_ref, yf_ref, yr_ref, hf_ref, hr_ref):
    c = WKV_CHUNK
    w = RWKV_WIDTH
    tt = shf_ref.shape[1]
    ncs = tt // c

    @pl.when(pl.program_id(1) == 0)
    def _():
        hf_ref[...] = jnp.zeros_like(hf_ref)
        hr_ref[...] = jnp.zeros_like(hr_ref)

    row_c = lax.broadcasted_iota(jnp.int32, (c, c), 0)
    col_c = lax.broadcasted_iota(jnp.int32, (c, c), 1)
    t_i = lax.broadcasted_iota(jnp.int32, (c, w), 0)
    s_i = lax.broadcasted_iota(jnp.int32, (c, w), 1) % c
    eye_c = (s_i == t_i).astype(F32)
    row_w = lax.broadcasted_iota(jnp.int32, (w, w), 0)
    col_w = lax.broadcasted_iota(jnp.int32, (w, w), 1)
    mask_bd = (row_w // c) == (col_w // HEAD_DIM)
    eye_w = row_w == col_w
    tri_incl = [jnp.where(col_c <= row_c, 1.0, 0.0).astype(BF16),
                jnp.where(col_c >= row_c, 1.0, 0.0).astype(BF16)]
    strict = [s_i < t_i, s_i > t_i]
    incl = [s_i <= t_i, s_i >= t_i]

    def bf(x):
        return x.astype(BF16)

    def bd(x):
        return _expand_bd(bf(x), mask_bd)

    chunks = []
    for ci in range(ncs):
        chunks.append((0, ci * c, shf_ref, df_ref))
        chunks.append((1, (ncs - 1 - ci) * c, shr_ref, dr_ref))

    st = []
    for d, start, sh_ref, dir_ref in chunks:
        rows = pl.ds(start, c)
        st.append(dict(d=d, rows=rows,
                       r=sh_ref[0, rows, 0:w], v=sh_ref[0, rows, w:2 * w], kk=sh_ref[0, rows, 2 * w:3 * w],
                       lw=dir_ref[0, rows, 0:w], kh=dir_ref[0, rows, w:2 * w], nb=dir_ref[0, rows, 2 * w:3 * w]))

    for q in st:
        l_hi, l_mid, l_lo = _split3(q["lw"])
        tri = tri_incl[q["d"]]
        q["cum"] = _dot(tri, l_hi) + _dot(tri, l_mid) + _dot(tri, l_lo)
    for q in st:
        cum, lw = q["cum"], q["lw"]
        tot = jnp.sum(lw, axis=0, keepdims=True)
        e_inv = jnp.exp(-cum)
        e_end = jnp.exp(tot - cum)
        q["tot"] = tot
        q["a_bar"] = q["kk"] * jnp.exp(cum - lw)
        q["r_bar"] = q["r"] * jnp.exp(cum)
        q["b_hat"] = bf(q["nb"] * e_end)
        q["k_hat"] = bf(q["kh"] * e_end)
        q["v_bf"] = bf(q["v"])
        q["v_bd"] = _expand_bd(q["v_bf"], mask_bd)
        q["lhs"] = jnp.concatenate([bf(q["a_bar"]), bf(q["r_bar"])], axis=0)
        q["rhs"] = jnp.concatenate([bd(q["nb"] * e_inv), bd(q["kh"] * e_inv)], axis=0)
    for q in st:
        sc = _dot_nt(q["lhs"], q["rhs"])
        sm, im = strict[q["d"]], incl[q["d"]]
        q["x"] = jnp.where(sm, sc[0:c, 0:w], 0.0)
        q["a_ak"] = bf(jnp.where(sm, sc[0:c, w:2 * w], 0.0))
        q["m_rb"] = bf(jnp.where(im, sc[c:2 * c, 0:w], 0.0))
        q["m_rk"] = bf(jnp.where(im, sc[c:2 * c, w:2 * w], 0.0))
        q["tinv"] = eye_c + q["x"]

    for q in st:
        q["x"] = _dot(bf(q["x"]), bd(q["x"]))
    n_lv = int(math.log2(c)) - 1
    for lv in range(n_lv):
        last = lv == n_lv - 1
        for q in st:
            xb = bd(q["x"])
            if last:
                q["tinv"] = q["tinv"] + _dot(bf(q["tinv"]), xb)
            else:
                res = _dot(jnp.concatenate([bf(q["x"]), bf(q["tinv"])], axis=0), xb)
                q["tinv"] = q["tinv"] + res[c:2 * c]
                q["x"] = res[0:c]
    for q in st:
        res = _dot(jnp.concatenate([q["a_ak"], q["m_rk"]], axis=0), q["v_bd"])
        q["akv"] = res[0:c]
        q["mrkv"] = res[c:2 * c]
    for q in st:
        tinv_bf = bf(q["tinv"])
        q["a_pr"] = bf(_dot(tinv_bf, bd(q["a_bar"])))
        q["v_pr"] = bf(_dot(tinv_bf, bd(q["akv"])))
    for q in st:
        q["r_pr"] = bf(q["r_bar"] + _dot(q["m_rb"], _expand_bd(q["a_pr"], mask_bd)))
        q["y_pr"] = _dot(q["m_rb"], _expand_bd(q["v_pr"], mask_bd)) + q["mrkv"]
    for q in st:
        q["p_bd"] = bf(jnp.where(eye_w, jnp.exp(q["tot"]), 0.0)
                       + jnp.where(mask_bd, _dot_tn(q["b_hat"], q["a_pr"]), 0.0))
        q["q_bd"] = jnp.where(mask_bd,
                              _dot_tn(jnp.concatenate([q["b_hat"], q["k_hat"]], axis=0),
                                      jnp.concatenate([q["v_pr"], q["v_bf"]], axis=0)), 0.0)

    h = [hf_ref[...], hr_ref[...]]
    y_refs = [yf_ref, yr_ref]
    for q in st:
        d = q["d"]
        res = _dot(jnp.concatenate([q["r_pr"], q["p_bd"]], axis=0), bf(h[d]))
        y_refs[d][0, q["rows"], :] = res[0:c] + q["y_pr"]
        h[d] = res[c:c + w] + q["q_bd"]
    hf_ref[...] = h[0]
    hr_ref[...] = h[1]


def _wkv(shared, dirp, tt):
    b, s, _ = shared.shape
    nt = s // tt
    w = RWKV_WIDTH
    fwd = lambda i, j: (i, j, 0)
    bwd = lambda i, j: (i, nt - 1 - j, 0)
    return pl.pallas_call(
        _wkv_kernel,
        grid=(b, nt),
        in_specs=[pl.BlockSpec((1, tt, 3 * w), fwd), pl.BlockSpec((1, tt, 3 * w), bwd),
                  pl.BlockSpec((None, 1, tt, 3 * w), lambda i, j: (0, i, j, 0)),
                  pl.BlockSpec((None, 1, tt, 3 * w), lambda i, j: (1, i, nt - 1 - j, 0))],
        out_specs=[pl.BlockSpec((1, tt, w), fwd), pl.BlockSpec((1, tt, w), bwd)],
        out_shape=[jax.ShapeDtypeStruct((b, s, w), F32), jax.ShapeDtypeStruct((b, s, w), F32)],
        scratch_shapes=[pltpu.VMEM((w, w), F32), pltpu.VMEM((w, w), F32)],
        compiler_params=_cparams(("parallel", "arbitrary")),
        name="wkv",
    )(shared, shared, dirp, dirp)


def _attn_kernel(q_ref, k_ref, v_ref, o_ref):
    tq = q_ref.shape[0]
    k = k_ref[...]
    v = v_ref[...]
    lane = lax.broadcasted_iota(jnp.int32, (tq, ATTN_KV_WIDTH), 1)
    low = lane < HEAD_DIM
    zero = jnp.zeros((), BF16)
    for j in range(ATTN_HEADS // 2):
        qb = q_ref[:, j * ATTN_KV_WIDTH:(j + 1) * ATTN_KV_WIDTH]
        outs = []
        for half in range(2):
            qm = jnp.where(low if half == 0 else ~low, qb, zero)
            s = _dot_nt(qm, k)
            m = jnp.max(s, axis=-1, keepdims=True)
            p = jnp.exp2(s - m)
            l = jnp.sum(p, axis=-1, keepdims=True)
            outs.append(_dot(p.astype(BF16), v) / l)
        o_ref[:, j * ATTN_KV_WIDTH:(j + 1) * ATTN_KV_WIDTH] = jnp.where(low, outs[0], outs[1]).astype(BF16)


def _attn(q, k, v, seq, tq):
    n = q.shape[0]
    nq = seq // tq
    return pl.pallas_call(
        _attn_kernel,
        grid=(n // seq, nq),
        in_specs=[pl.BlockSpec((tq, ATTN_WIDTH), lambda b, i: (b * nq + i, 0)),
                  pl.BlockSpec((seq, ATTN_KV_WIDTH), lambda b, i: (b, 0)),
                  pl.BlockSpec((seq, ATTN_KV_WIDTH), lambda b, i: (b, 0))],
        out_specs=pl.BlockSpec((tq, ATTN_WIDTH), lambda b, i: (b * nq + i, 0)),
        out_shape=jax.ShapeDtypeStruct((n, ATTN_WIDTH), BF16),
        compiler_params=_cparams(("parallel", "parallel")),
        name="attn",
    )(q, k, v)


def _outproj_kernel(x_ref, pool_ref, wkvf_ref, wkvr_ref, post_ref, attn_ref, wo_ref, gng_ref, gnb_ref,
                    ones_ref, lg_ref, lb_ref, wrh_ref, wrl_ref, br_ref,
                    x1_ref, x1p_ref, route_ref):
    w = RWKV_WIDTH
    ones = ones_ref[...]
    wkv = wkvf_ref[...] + wkvr_ref[...]
    inv = 1.0 / HEAD_DIM
    mu = _headsum(wkv, ones) * inv
    cen = wkv - mu
    var = _headsum(cen * cen, ones) * inv
    yr = cen * lax.rsqrt(var + GN_EPS) * gng_ref[...] + gnb_ref[...] + post_ref[:, 0:w]
    yr = yr * post_ref[:, w:2 * w]
    y = (_dot(pool_ref[...].astype(BF16), wo_ref[0:POOL_WIDTH, :])
         + _dot(yr.astype(BF16), wo_ref[POOL_WIDTH:POOL_WIDTH + w, :])
         + _dot(attn_ref[...], wo_ref[POOL_WIDTH + w:D_MODEL, :]))
    x1 = _layer_norm(DEEPNORM_ALPHA * x_ref[...] + y, lg_ref[...], lb_ref[...])
    x1_ref[...] = x1
    x1p_ref[...] = _pack_rows(x1)

    xh, xl = _split2(x1)
    logits = (_dot(xh, wrh_ref[...]) + _dot(xl, wrh_ref[...]) + _dot(xh, wrl_ref[...])) + br_ref[...]
    lane = lax.broadcasted_iota(jnp.int32, logits.shape, 1)
    lane_f = lane.astype(F32)
    lane_grp_f = ((lane - N_GROUPS) // EXPERTS_PER_GROUP).astype(F32)
    neg = -jnp.inf
    big = jnp.float32(1 << 20)
    gl = jnp.where(lane < N_GROUPS, logits, neg)
    gmax = jnp.max(gl, axis=-1, keepdims=True)
    grp = jnp.min(jnp.where(gl == gmax, lane_f, big), axis=-1, keepdims=True)
    gw = 1.0 / jnp.sum(jnp.exp(gl - gmax), axis=-1, keepdims=True)
    in_grp = (lane >= N_GROUPS) & (lane < N_GROUPS + N_EXPERTS) & (lane_grp_f == grp)
    el = jnp.where(in_grp, logits, neg)
    v1 = jnp.max(el, axis=-1, keepdims=True)
    i1 = jnp.min(jnp.where(el == v1, lane_f, big), axis=-1, keepdims=True)
    el2 = jnp.where(lane_f == i1, neg, el)
    v2 = jnp.max(el2, axis=-1, keepdims=True)
    i2 = jnp.min(jnp.where(el2 == v2, lane_f, big), axis=-1, keepdims=True)
    e21 = jnp.exp(v2 - v1)
    g1 = gw / (1.0 + e21)
    g2 = gw * e21 / (1.0 + e21)
    route = jnp.where(lane == 0, i1 - N_GROUPS,
                      jnp.where(lane == 1, i2 - N_GROUPS,
                                jnp.where(lane == 2, g1, jnp.where(lane == 3, g2, 0.0))))
    route_ref[...] = route


def _outproj(x2d, pool2d, wkvf, wkvr, post2d, attn2d, wo, gng, gnb, ones, lg, lb, wrh, wrl, br, tm):
    n = x2d.shape[0]
    w = RWKV_WIDTH
    row = lambda width: pl.BlockSpec((tm, width), lambda i: (i, 0))
    vec = lambda width: pl.BlockSpec((1, width), lambda i: (0, 0))
    return pl.pallas_call(
        _outproj_kernel,
        grid=(n // tm,),
        in_specs=[row(D_MODEL), row(POOL_WIDTH), row(w), row(w),
                  row(2 * w), row(ATTN_WIDTH),
                  pl.BlockSpec((D_MODEL, D_MODEL), lambda i: (0, 0)),
                  vec(w), vec(w),
                  pl.BlockSpec((w, w), lambda i: (0, 0)),
                  vec(D_MODEL), vec(D_MODEL),
                  pl.BlockSpec((D_MODEL, ROUTE_LANES), lambda i: (0, 0)),
                  pl.BlockSpec((D_MODEL, ROUTE_LANES), lambda i: (0, 0)),
                  vec(ROUTE_LANES)],
        out_specs=[row(D_MODEL), row(HALF_D), row(ROUTE_LANES)],
        out_shape=[jax.ShapeDtypeStruct((n, D_MODEL), F32),
                   jax.ShapeDtypeStruct((n, HALF_D), U32),
                   jax.ShapeDtypeStruct((n, ROUTE_LANES), F32)],
        compiler_params=_cparams(("parallel",)),
        name="outproj",
    )(x2d, pool2d, wkvf, wkvr, post2d, attn2d, wo, gng, gnb, ones, lg, lb, wrh, wrl, br)


def _expert_kernel(bexp_ref, nused_ref, xs_ref, wg_ref, wu_ref, wd_ref, ys_ref, wgb_ref, wub_ref, wdb_ref):
    i = pl.program_id(0)
    used = i < nused_ref[0]

    @pl.when(used & ((i == 0) | (bexp_ref[i] != bexp_ref[jnp.maximum(i - 1, 0)])))
    def _():
        wgb_ref[...] = wg_ref[0].astype(BF16)
        wub_ref[...] = wu_ref[0].astype(BF16)
        wdb_ref[...] = wd_ref[0].astype(BF16)

    @pl.when(used)
    def _():
        xa, xb = _unpack_rows(xs_ref[...])
        h1 = _dot(xa, wgb_ref[0:HALF_D, :]) + _dot(xb, wgb_ref[HALF_D:D_MODEL, :])
        h2 = _dot(xa, wub_ref[0:HALF_D, :]) + _dot(xb, wub_ref[HALF_D:D_MODEL, :])
        h = (h1 * jax.nn.sigmoid(h1) * h2).astype(BF16)
        ys_ref[...] = _pack_rows(_dot(h, wdb_ref[...]))

    @pl.when(jnp.logical_not(used))
    def _():
        ys_ref[...] = jnp.zeros_like(ys_ref)


def _experts(block_exp, n_used, xs, wg, wu, wd, layer):
    p = xs.shape[0]
    nb = p // MOE_BLOCK
    wmap = lambda i, be, nu: (layer, be[i], 0, 0)
    grid_spec = pltpu.PrefetchScalarGridSpec(
        num_scalar_prefetch=2,
        grid=(nb,),
        in_specs=[pl.BlockSpec((MOE_BLOCK, HALF_D), lambda i, be, nu: (i, 0)),
                  pl.BlockSpec((None, 1, D_MODEL, EXPERT_HIDDEN), wmap),
                  pl.BlockSpec((None, 1, D_MODEL, EXPERT_HIDDEN), wmap),
                  pl.BlockSpec((None, 1, EXPERT_HIDDEN, D_MODEL), wmap)],
        out_specs=pl.BlockSpec((MOE_BLOCK, HALF_D), lambda i, be, nu: (i, 0)),
        scratch_shapes=[pltpu.VMEM((D_MODEL, EXPERT_HIDDEN), BF16),
                        pltpu.VMEM((D_MODEL, EXPERT_HIDDEN), BF16),
                        pltpu.VMEM((EXPERT_HIDDEN, D_MODEL), BF16)],
    )
    return pl.pallas_call(
        _expert_kernel,
        grid_spec=grid_spec,
        out_shape=jax.ShapeDtypeStruct((p, HALF_D), U32),
        compiler_params=_cparams(("arbitrary",)),
        name="experts",
    )(block_exp, n_used, xs, wg, wu, wd)


SC_CORES = 2
SC_SUBCORES = 16
SC_GATHER_ROWS = 64


def _sc_gather(table, idx):
    v, d = table.shape
    b = idx.shape[0]
    nw = SC_CORES * SC_SUBCORES
    ch = SC_GATHER_ROWS
    assert b % (nw * ch * 2) == 0
    b_per_w = b // nw
    nch = b_per_w // ch
    mesh = plsc.VectorSubcoreMesh(core_axis_name="c", subcore_axis_name="s")

    @functools.partial(
        pl.kernel, mesh=mesh,
        out_type=jax.ShapeDtypeStruct((b, d), table.dtype),
        scratch_types=[pltpu.VMEM((b_per_w,), jnp.int32),
                       pltpu.VMEM((2, ch, d), table.dtype),
                       pltpu.SemaphoreType.DMA, pltpu.SemaphoreType.DMA,
                       pltpu.SemaphoreType.DMA, pltpu.SemaphoreType.DMA],
    )
    def gather_kernel(table_hbm, idx_hbm, out_hbm, idx_v, rows_v, gsem0, gsem1, wsem0, wsem1):
        wid = lax.axis_index("s") * SC_CORES + lax.axis_index("c")
        base = wid * b_per_w
        pltpu.sync_copy(idx_hbm.at[pl.ds(base, b_per_w)], idx_v)
        gsem = (gsem0, gsem1)
        wsem = (wsem0, wsem1)

        def gather(jj, slot):
            off = pl.multiple_of(jj * ch, ch)
            return pltpu.make_async_copy(table_hbm.at[idx_v.at[pl.ds(off, ch)]], rows_v.at[slot], gsem[slot])

        def write(jj, slot):
            off = pl.multiple_of(jj * ch, ch)
            return pltpu.make_async_copy(rows_v.at[slot], out_hbm.at[pl.ds(base + off, ch)], wsem[slot])

        gather(0, 0).start()

        @pl.loop(0, nch, step=2)
        def _(j):
            for slot in range(2):
                jj = j + slot

                gather(jj, slot).wait()

                @pl.when(jj >= 1)
                def _():
                    write(jj - 1, 1 - slot).wait()

                @pl.when(jj + 1 < nch)
                def _():
                    gather(jj + 1, 1 - slot).start()

                write(jj, slot).start()

        write(nch - 1, (nch - 1) % 2).wait()

    return gather_kernel(table, idx)


def _gather_rows(table, idx):
    b = idx.shape[0]
    unit = SC_CORES * SC_SUBCORES * SC_GATHER_ROWS * 2
    bp = -(-b // unit) * unit
    if bp == b:
        return _sc_gather(table, idx)
    return _sc_gather(table, jnp.pad(idx, (0, bp - b)))[:b]


def _combine_kernel(x1_ref, ya_ref, yb_ref, route_ref, lg_ref, lb_ref, o_ref):
    o_ref[...] = _combine_rows(x1_ref[...], ya_ref[...], yb_ref[...], route_ref[...], lg_ref[...], lb_ref[...])


def _combine(x1, yab, route, lg, lb, tm):
    n = x1.shape[0]
    nt = n // tm
    row = lambda width: pl.BlockSpec((tm, width), lambda i: (i, 0))
    vec = lambda width: pl.BlockSpec((1, width), lambda i: (0, 0))
    return pl.pallas_call(
        _combine_kernel,
        grid=(nt,),
        in_specs=[row(D_MODEL), row(HALF_D), pl.BlockSpec((tm, HALF_D), lambda i: (i + nt, 0)),
                  row(ROUTE_LANES), vec(D_MODEL), vec(D_MODEL)],
        out_specs=row(D_MODEL),
        out_shape=jax.ShapeDtypeStruct((n, D_MODEL), F32),
        compiler_params=_cparams(("parallel",)),
        name="combine",
    )(x1, yab, yab, route, lg, lb)


def _q_perm():
    order = [h for j in range(ATTN_HEADS // 2) for h in (j, j + ATTN_HEADS // 2)]
    return jnp.concatenate([jnp.arange(h * HEAD_DIM, (h + 1) * HEAD_DIM) for h in order])


def _block_diag(blocks):
    n = len(blocks)
    r, c = blocks[0].shape
    out = jnp.zeros((n * r, n * c), blocks[0].dtype)
    for i, blk in enumerate(blocks):
        out = out.at[i * r:(i + 1) * r, i * c:(i + 1) * c].set(blk)
    return out


def _rope_tables(seq):
    rows = seq // GRID_W
    row_id = jnp.repeat(jnp.arange(rows), GRID_W).astype(F32)
    col_id = jnp.tile(jnp.arange(GRID_W), rows).astype(F32)
    half = HEAD_DIM // 2
    inv_freq = ROPE_THETA ** (-jnp.arange(0, half, 2, dtype=F32) / half)
    ang_r = row_id[:, None] * inv_freq
    ang_c = col_id[:, None] * inv_freq
    ang = jnp.concatenate([ang_r, ang_r, ang_c, ang_c], -1)
    sign = jnp.where((jnp.arange(HEAD_DIM) // 16) % 2 == 0, -1.0, 1.0).astype(F32)
    cos8 = jnp.tile(jnp.cos(ang), (1, ATTN_HEADS))
    sin8 = jnp.tile(jnp.sin(ang) * sign, (1, ATTN_HEADS))
    return cos8, sin8


def _dispatch(route, n):
    m = n * 2
    experts = jnp.arange(N_EXPERTS, dtype=jnp.int32)
    e_flat = jnp.concatenate([route[:, 0], route[:, 1]]).astype(jnp.int32)
    pair = jnp.arange(m, dtype=jnp.int32)
    e_sorted, order = lax.sort((e_flat, pair), num_keys=1, is_stable=True)
    start = jnp.sum(e_sorted[None, :] < experts[:, None], axis=1, dtype=jnp.int32)
    counts = jnp.concatenate([start[1:], jnp.full((1,), m, jnp.int32)]) - start
    padded = (counts + MOE_BLOCK - 1) // MOE_BLOCK * MOE_BLOCK
    ends_p = jnp.cumsum(padded)
    pstart = ends_p - padded
    delta = pstart - start
    n_blocks = -(-(m + N_EXPERTS * (MOE_BLOCK - 1)) // MOE_BLOCK)
    block_start = jnp.arange(n_blocks, dtype=jnp.int32) * MOE_BLOCK
    block_exp = jnp.minimum(jnp.sum(ends_p[None, :] <= block_start[:, None], axis=1, dtype=jnp.int32),
                            N_EXPERTS - 1)
    lane = jnp.arange(MOE_BLOCK, dtype=jnp.int32)[None, :]
    valid = (block_start - pstart[block_exp])[:, None] + lane < counts[block_exp][:, None]
    src = jnp.clip((block_start - delta[block_exp])[:, None] + lane, 0, m - 1)
    run = order[src.reshape(-1)].reshape(n_blocks, MOE_BLOCK)
    row_tok = jnp.where(valid, run % n, (block_start[:, None] + lane) % n).reshape(-1)
    n_used = (ends_p[-1] // MOE_BLOCK).astype(jnp.int32).reshape(1)
    dest = pair + jnp.sum(jnp.where(e_sorted[None, :] == experts[:, None], delta[:, None], 0), axis=0,
                          dtype=jnp.int32)
    _, pos = lax.sort((order, dest), num_keys=1)
    return row_tok, pos, block_exp, n_used


def kernel(x, w_in, mu_prev, mu_next, pool_w, pool_scale, rw_w0, rw_w_up, rw_a0, rw_a_up, rw_g_up, rw_k_k, rw_k_a, rw_r_k, rw_gn_g, rw_gn_b, q_norm, k_norm, w_o, ln1_g, ln1_b, router_group, router_group_b, router_expert, router_expert_b, exp_gate, exp_up, exp_down, ln2_g, ln2_b):
    b, s, d = x.shape
    n = b * s
    w = RWKV_WIDTH
    tm = min(512, s)
    qperm = _q_perm()
    cos8, sin8 = _rope_tables(s)
    ones_q = _block_diag([jnp.ones((HEAD_DIM, HEAD_DIM), BF16)] * ATTN_HEADS)
    ones_r = ones_q[0:w, 0:w]
    a_end = POOL_WIDTH
    b_end = POOL_WIDTH + RWKV_IN

    xc = x.reshape(n, d)
    pending = None
    for l in range(DEPTH):
        wq = w_in[l][:, b_end:b_end + ATTN_WIDTH][:, qperm]
        w_proj = jnp.concatenate(
            [w_in[l][:, :a_end], w_in[l][:, a_end:b_end],
             jnp.zeros((d, RWKV_PAD - RWKV_IN), F32), wq, w_in[l][:, b_end + ATTN_WIDTH:]],
            axis=1).astype(BF16)
        pool_bd = _block_diag([pool_w[l, g] for g in range(len(POOL_WINDOWS))]).astype(BF16)
        pad_vec = jnp.zeros((RWKV_PAD - RWKV_IN,), F32)
        mup = jnp.concatenate([mu_prev[l], pad_vec]).reshape(1, RWKV_PAD)
        mun = jnp.concatenate([mu_next[l], pad_vec]).reshape(1, RWKV_PAD)
        w_lora = jnp.zeros((w, 5 * w), F32)
        for di in range(2):
            w_lora = w_lora.at[di * DECAY_LORA:(di + 1) * DECAY_LORA, di * w:(di + 1) * w].set(rw_w_up[l, di])
            o = 2 * DECAY_LORA
            w_lora = w_lora.at[o + di * AAA_LORA:o + (di + 1) * AAA_LORA, (2 + di) * w:(3 + di) * w].set(rw_a_up[l, di])
        o = 2 * DECAY_LORA + 2 * AAA_LORA
        w_lora = w_lora.at[o:o + GATE_LORA, 4 * w:5 * w].set(rw_g_up[l]).astype(BF16)
        lora_bias = jnp.concatenate([rw_w0[l, 0], rw_w0[l, 1], rw_a0[l, 0], rw_a0[l, 1],
                                     jnp.zeros((w,), F32)]).reshape(1, 5 * w)
        qg = jnp.tile(q_norm[l], ATTN_HEADS).reshape(1, ATTN_WIDTH)
        kg = jnp.tile(k_norm[l], 2).reshape(1, ATTN_KV_WIDTH)
        wo_attn = w_o[l][POOL_WIDTH + w:][qperm]
        wo = jnp.concatenate([w_o[l][:POOL_WIDTH + w], wo_attn], axis=0).astype(BF16)
        wr = jnp.concatenate([router_group[l], router_expert[l],
                              jnp.zeros((d, ROUTE_LANES - N_GROUPS - N_EXPERTS), F32)], axis=1)
        wrh = wr.astype(BF16)
        wrl = (wr - wrh.astype(F32)).astype(BF16)
        br = jnp.concatenate([router_group_b[l], router_expert_b[l],
                              jnp.zeros((ROUTE_LANES - N_GROUPS - N_EXPERTS,), F32)]).reshape(1, ROUTE_LANES)

        if pending is None:
            pool_in, rw_in, qh, kh, vh = _proj(xc, w_proj, cos8, sin8, qg, kg, ones_q, s, tm)
        else:
            xc, pool_in, rw_in, qh, kh, vh = _proj(None, w_proj, cos8, sin8, qg, kg, ones_q, s, tm, prev=pending)
        y_pool = _pool(pool_in.reshape(b, s, POOL_WIDTH), pool_bd, pool_scale[l].reshape(1, POOL_WIDTH))
        shared, dirp, post = _rwkv_prep(rw_in.reshape(b, s, RWKV_PAD), mup, mun, w_lora, lora_bias,
                                        rw_k_k[l].reshape(1, w), rw_k_a[l].reshape(1, w),
                                        rw_r_k[l].reshape(1, w), ones_r, tm)
        wkv_f, wkv_r = _wkv(shared, dirp, min(512, s))
        y_attn = _attn(qh, kh, vh, s, min(512, s))
        x1, x1p, route = _outproj(xc, y_pool.reshape(n, POOL_WIDTH), wkv_f.reshape(n, w), wkv_r.reshape(n, w),
                                  post.reshape(n, 2 * w), y_attn, wo,
                                  rw_gn_g[l].reshape(1, w), rw_gn_b[l].reshape(1, w), ones_r,
                                  ln1_g[l].reshape(1, d), ln1_b[l].reshape(1, d), wrh, wrl, br, tm)

        row_tok, pos, block_exp, n_used = _dispatch(route, n)
        xs = _gather_rows(x1p, row_tok)
        ys = _experts(block_exp, n_used, xs, exp_gate, exp_up, exp_down, l)
        yab = _gather_rows(ys, pos)
        pending = (x1, yab, route, ln2_g[l].reshape(1, d), ln2_b[l].reshape(1, d))
    return _combine(*pending, tm).reshape(b, s, d)
```

```python
import functools
import math

import jax
import jax.numpy as jnp
from jax import lax
from jax.experimental import pallas as pl
from jax.experimental.pallas import tpu as pltpu
from jax.experimental.pallas import tpu_sc as plsc

F32 = jnp.float32
BF16 = jnp.bfloat16

D_MODEL = 1024
DEPTH = 4
GRID_W = 64
HEAD_DIM = 64
POOL_WIDTH = 256
POOL_WINDOWS = (2, 4, 8, 16)
POOL_GROUP = 64
RWKV_WIDTH = 256
RWKV_HEADS = 4
DECAY_LORA = 32
AAA_LORA = 32
GATE_LORA = 64
GN_EPS = 64e-5
RWKV_IN = 960
RWKV_PAD = 1024
ATTN_WIDTH = 512
ATTN_HEADS = 8
ATTN_KV_WIDTH = 128
ATTN_IN = ATTN_WIDTH + 2 * ATTN_KV_WIDTH
ROPE_THETA = 10000.0
QK_EPS = 1e-6
N_GROUPS = 4
EXPERTS_PER_GROUP = 8
N_EXPERTS = 32
EXPERT_HIDDEN = 512
MOE_BLOCK = 512
DEEPNORM_ALPHA = float((2 * DEPTH) ** 0.25)
LN_EPS = 1e-5
PROJ_WIDTH = POOL_WIDTH + RWKV_PAD + ATTN_IN
ROUTE_LANES = 128
WKV_CHUNK = 64
VMEM_LIMIT = 48 * 1024 * 1024


def _cparams(sem):
    return pltpu.CompilerParams(dimension_semantics=sem, vmem_limit_bytes=VMEM_LIMIT)


def _dot(a, b):
    return jnp.dot(a, b, preferred_element_type=F32)


def _dot_nt(a, b):
    return lax.dot_general(a, b, (((1,), (1,)), ((), ())), preferred_element_type=F32)


def _dot_tn(a, b):
    return lax.dot_general(a, b, (((0,), (0,)), ((), ())), preferred_element_type=F32)


def _split2(x):
    hi = x.astype(BF16)
    lo = (x - hi.astype(F32)).astype(BF16)
    return hi, lo


def _split3(x):
    hi = x.astype(BF16)
    r1 = x - hi.astype(F32)
    mid = r1.astype(BF16)
    lo = (r1 - mid.astype(F32)).astype(BF16)
    return hi, mid, lo


HALF_D = D_MODEL // 2
U32 = jnp.uint32


def _pack_rows(x):
    hi = lax.bitcast_convert_type(x[:, :HALF_D].astype(BF16).astype(F32), U32)
    lo = lax.bitcast_convert_type(x[:, HALF_D:].astype(BF16).astype(F32), U32)
    return hi | (lo >> 16)


def _unpack_rows(u):
    a = lax.bitcast_convert_type(u & jnp.uint32(0xFFFF0000), F32).astype(BF16)
    b = lax.bitcast_convert_type(u << 16, F32).astype(BF16)
    return a, b


def _headsum(x, ones_bf16):
    hi, lo = _split2(x)
    return _dot(hi, ones_bf16) + _dot(lo, ones_bf16)


def _rope(x, cos, sin_signed):
    n = x.shape[-1]
    lane = lax.broadcasted_iota(jnp.int32, x.shape, 1)
    first = (lane // 16) % 2 == 0
    partner = jnp.where(first, pltpu.roll(x, n - 16, 1), pltpu.roll(x, 16, 1))
    return x * cos + partner * sin_signed


def _layer_norm(z, g, b):
    mu = jnp.mean(z, axis=-1, keepdims=True)
    zc = z - mu
    var = jnp.mean(zc * zc, axis=-1, keepdims=True)
    return zc * lax.rsqrt(var + LN_EPS) * g + b


def _combine_rows(x1, ya, yb, route, g, b):
    g1 = route[:, 2:3]
    g2 = route[:, 3:4]
    a_hi, a_lo = _unpack_rows(ya)
    b_hi, b_lo = _unpack_rows(yb)
    m = jnp.concatenate([g1 * a_hi.astype(F32) + g2 * b_hi.astype(F32),
                         g1 * a_lo.astype(F32) + g2 * b_lo.astype(F32)], axis=1)
    return _layer_norm(DEEPNORM_ALPHA * x1 + m, g, b)


def _proj_kernel(x_ref, w_ref, cos_ref, sin_ref, qg_ref, kg_ref, onesq_ref,
                 pool_ref, rw_ref, q_ref, k_ref, v_ref):
    _proj_rows(x_ref[...], w_ref, cos_ref, sin_ref, qg_ref, kg_ref, onesq_ref,
               pool_ref, rw_ref, q_ref, k_ref, v_ref)


def _combine_proj_kernel(x1_ref, ya_ref, yb_ref, route_ref, lg_ref, lb_ref,
                         w_ref, cos_ref, sin_ref, qg_ref, kg_ref, onesq_ref,
                         x_ref, pool_ref, rw_ref, q_ref, k_ref, v_ref):
    x = _combine_rows(x1_ref[...], ya_ref[...], yb_ref[...], route_ref[...], lg_ref[...], lb_ref[...])
    x_ref[...] = x
    _proj_rows(x, w_ref, cos_ref, sin_ref, qg_ref, kg_ref, onesq_ref, pool_ref, rw_ref, q_ref, k_ref, v_ref)


def _proj_rows(x, w_ref, cos_ref, sin_ref, qg_ref, kg_ref, onesq_ref, pool_ref, rw_ref, q_ref, k_ref, v_ref):
    xb = x.astype(BF16)
    pool_ref[...] = _dot(xb, w_ref[:, 0:POOL_WIDTH])
    rw_ref[...] = _dot(xb, w_ref[:, POOL_WIDTH:POOL_WIDTH + RWKV_PAD])
    at = _dot(xb, w_ref[:, POOL_WIDTH + RWKV_PAD:PROJ_WIDTH])

    q = at[:, 0:ATTN_WIDTH]
    k = at[:, ATTN_WIDTH:ATTN_WIDTH + ATTN_KV_WIDTH]
    v = at[:, ATTN_WIDTH + ATTN_KV_WIDTH:ATTN_IN]
    ones_q = onesq_ref[...]
    ones_k = onesq_ref[0:ATTN_KV_WIDTH, 0:ATTN_KV_WIDTH]
    inv = 1.0 / HEAD_DIM
    qn = q * lax.rsqrt(_headsum(q * q, ones_q) * inv + QK_EPS) * qg_ref[...]
    kn = k * lax.rsqrt(_headsum(k * k, ones_k) * inv + QK_EPS) * kg_ref[...]
    cos = cos_ref[...]
    sin = sin_ref[...]
    q_ref[...] = (_rope(qn, cos, sin) * (HEAD_DIM ** -0.5 * math.log2(math.e))).astype(BF16)
    k_ref[...] = _rope(kn, cos[:, 0:ATTN_KV_WIDTH], sin[:, 0:ATTN_KV_WIDTH]).astype(BF16)
    v_ref[...] = v.astype(BF16)


def _proj(x2d, w, cos8, sin8, qg, kg, ones_q, seq, tm, prev=None):
    n = (x2d if prev is None else prev[0]).shape[0]
    nts = seq // tm
    nt = n // tm
    row = lambda width: pl.BlockSpec((tm, width), lambda i: (i, 0))
    const = lambda r, c: pl.BlockSpec((r, c), lambda i: (0, 0))
    proj_in = [const(D_MODEL, PROJ_WIDTH),
               pl.BlockSpec((tm, ATTN_WIDTH), lambda i: (i % nts, 0)),
               pl.BlockSpec((tm, ATTN_WIDTH), lambda i: (i % nts, 0)),
               const(1, ATTN_WIDTH), const(1, ATTN_KV_WIDTH), const(ATTN_WIDTH, ATTN_WIDTH)]
    proj_out = [row(POOL_WIDTH), row(RWKV_PAD), row(ATTN_WIDTH), row(ATTN_KV_WIDTH), row(ATTN_KV_WIDTH)]
    proj_shape = [jax.ShapeDtypeStruct((n, POOL_WIDTH), F32),
                  jax.ShapeDtypeStruct((n, RWKV_PAD), F32),
                  jax.ShapeDtypeStruct((n, ATTN_WIDTH), BF16),
                  jax.ShapeDtypeStruct((n, ATTN_KV_WIDTH), BF16),
                  jax.ShapeDtypeStruct((n, ATTN_KV_WIDTH), BF16)]
    if prev is None:
        return pl.pallas_call(
            _proj_kernel,
            grid=(nt,),
            in_specs=[row(D_MODEL)] + proj_in,
            out_specs=proj_out,
            out_shape=proj_shape,
            compiler_params=_cparams(("parallel",)),
            name="proj",
        )(x2d, w, cos8, sin8, qg, kg, ones_q)
    x1, yab, route, lg, lb = prev
    return pl.pallas_call(
        _combine_proj_kernel,
        grid=(nt,),
        in_specs=[row(D_MODEL), row(HALF_D), pl.BlockSpec((tm, HALF_D), lambda i: (i + nt, 0)),
                  row(ROUTE_LANES), const(1, D_MODEL), const(1, D_MODEL)] + proj_in,
        out_specs=[row(D_MODEL)] + proj_out,
        out_shape=[jax.ShapeDtypeStruct((n, D_MODEL), F32)] + proj_shape,
        compiler_params=_cparams(("parallel",)),
        name="combine_proj",
    )(x1, yab, yab, route, lg, lb, w, cos8, sin8, qg, kg, ones_q)


def _pool_kernel(u_ref, w_ref, scale_ref, o_ref):
    s = u_ref.shape[1]
    u = u_ref[0]
    t = lax.broadcasted_iota(jnp.int32, (s, POOL_WIDTH), 0)

    def down(x, k):
        return jnp.where(t >= k, pltpu.roll(x, k, 0), 0.0)

    def up(x, k):
        return jnp.where(t < s - k, pltpu.roll(x, s - k, 0), 0.0)

    left = [down(u, 1)]
    right = [u]
    for k in (1, 2, 4):
        left.append(left[-1] + down(left[-1], k))
        right.append(right[-1] + up(right[-1], k))
    sums = [l + r for l, r in zip(left, right)]

    grp = lax.broadcasted_iota(jnp.int32, (s, POOL_WIDTH), 1) // POOL_GROUP
    half = jnp.where(grp == 0, 1, jnp.where(grp == 1, 2, jnp.where(grp == 2, 4, 8)))
    cnt = (jnp.minimum(t + half, s) - jnp.maximum(t - half, 0)).astype(F32)
    tot = jnp.where(grp == 0, sums[0], jnp.where(grp == 1, sums[1], jnp.where(grp == 2, sums[2], sums[3])))
    d = tot / cnt - u
    o_ref[0] = _dot(d.astype(BF16), w_ref[...]) * scale_ref[...]


def _pool(u3d, w_bd, scale):
    b, s, _ = u3d.shape
    return pl.pallas_call(
        _pool_kernel,
        grid=(b,),
        in_specs=[pl.BlockSpec((1, s, POOL_WIDTH), lambda i: (i, 0, 0)),
                  pl.BlockSpec((POOL_WIDTH, POOL_WIDTH), lambda i: (0, 0)),
                  pl.BlockSpec((1, POOL_WIDTH), lambda i: (0, 0))],
        out_specs=pl.BlockSpec((1, s, POOL_WIDTH), lambda i: (i, 0, 0)),
        out_shape=jax.ShapeDtypeStruct((b, s, POOL_WIDTH), F32),
        compiler_params=_cparams(("parallel",)),
        name="pool",
    )(u3d, w_bd, scale)


_SHIFT_HALO = 8


def _rwkv_prep_kernel(cur_ref, prev_ref, next_ref, mup_ref, mun_ref, wl_ref, bias_ref,
                      kk_ref, ka_ref, rk_ref, ones_ref,
                      sh_ref, dir_ref, post_ref):
    ts = cur_ref.shape[1]
    ti = pl.program_id(1)
    nt = pl.num_programs(1)
    cur = cur_ref[0]
    row = lax.broadcasted_iota(jnp.int32, cur.shape, 0)
    before = jnp.where(ti > 0, prev_ref[0, _SHIFT_HALO - 1:_SHIFT_HALO, :], 0.0)
    after = jnp.where(ti < nt - 1, next_ref[0, 0:1, :], 0.0)
    prev = jnp.where(row == 0, before, pltpu.roll(cur, 1, 0))
    nxt = jnp.where(row == ts - 1, after, pltpu.roll(cur, ts - 1, 0))
    f = cur + mup_ref[...] * (prev - cur) + mun_ref[...] * (nxt - cur)

    w = RWKV_WIDTH
    r = f[:, 0:w]
    k = f[:, w:2 * w]
    v = f[:, 2 * w:3 * w]
    lora = f[:, 3 * w:4 * w]
    lane = lax.broadcasted_iota(jnp.int32, lora.shape, 1)
    z = jnp.where(lane < 2 * DECAY_LORA, jnp.tanh(lora),
                  jnp.where(lane < 2 * DECAY_LORA + 2 * AAA_LORA, lora,
                            jnp.where(lane < 2 * DECAY_LORA + 2 * AAA_LORA + GATE_LORA,
                                      jax.nn.sigmoid(lora), 0.0)))
    up = _dot(z.astype(BF16), wl_ref[...]) + bias_ref[...]
    ones = ones_ref[...]

    kk0 = k * kk_ref[...]
    nrm = jnp.sqrt(_headsum(kk0 * kk0, ones))
    kk = kk0 / jnp.maximum(nrm, 1e-12)
    sh_ref[0, :, 0:w] = r
    sh_ref[0, :, w:2 * w] = v
    sh_ref[0, :, 2 * w:3 * w] = kk

    ksum = jnp.zeros_like(k)
    for di in range(2):
        logw = -math.exp(-0.5) * jax.nn.sigmoid(up[:, di * w:(di + 1) * w])
        a = jax.nn.sigmoid(up[:, (2 + di) * w:(3 + di) * w])
        kh = k * (1.0 + (a - 1.0) * ka_ref[...])
        ksum = ksum + kh
        dir_ref[di, 0, :, 0:w] = logw
        dir_ref[di, 0, :, w:2 * w] = kh
        dir_ref[di, 0, :, 2 * w:3 * w] = -(a * kk)
    bonus = _headsum(r * ksum * rk_ref[...], ones) * v
    post_ref[0, :, 0:w] = bonus
    post_ref[0, :, w:2 * w] = up[:, 4 * w:5 * w]


def _rwkv_prep(rw3d, mup, mun, wl, bias, k_k, k_a, r_k, ones, ts):
    b, s, _ = rw3d.shape
    nt = s // ts
    hb = ts // _SHIFT_HALO
    nhb = s // _SHIFT_HALO
    w = RWKV_WIDTH
    vec = lambda width: pl.BlockSpec((1, width), lambda i, j: (0, 0))
    return pl.pallas_call(
        _rwkv_prep_kernel,
        grid=(b, nt),
        in_specs=[pl.BlockSpec((1, ts, RWKV_PAD), lambda i, j: (i, j, 0)),
                  pl.BlockSpec((1, _SHIFT_HALO, RWKV_PAD),
                               lambda i, j: (i, jnp.maximum(j * hb - 1, 0), 0)),
                  pl.BlockSpec((1, _SHIFT_HALO, RWKV_PAD),
                               lambda i, j: (i, jnp.minimum((j + 1) * hb, nhb - 1), 0)),
                  vec(RWKV_PAD), vec(RWKV_PAD),
                  pl.BlockSpec((w, 5 * w), lambda i, j: (0, 0)),
                  vec(5 * w), vec(w), vec(w), vec(w),
                  pl.BlockSpec((w, w), lambda i, j: (0, 0))],
        out_specs=[pl.BlockSpec((1, ts, 3 * w), lambda i, j: (i, j, 0)),
                   pl.BlockSpec((2, 1, ts, 3 * w), lambda i, j: (0, i, j, 0)),
                   pl.BlockSpec((1, ts, 2 * w), lambda i, j: (i, j, 0))],
        out_shape=[jax.ShapeDtypeStruct((b, s, 3 * w), F32),
                   jax.ShapeDtypeStruct((2, b, s, 3 * w), F32),
                   jax.ShapeDtypeStruct((b, s, 2 * w), F32)],
        compiler_params=_cparams(("parallel", "parallel")),
        name="rwkv_prep",
    )(rw3d, rw3d, rw3d, mup, mun, wl, bias, k_k, k_a, r_k, ones)


def _expand_bd(x_bf16, mask_bd):
    return jnp.where(mask_bd, jnp.concatenate([x_bf16] * RWKV_HEADS, axis=0), jnp.zeros((), BF16))


def _wkv_kernel(shf_ref, shr_ref, df_ref, dr_ref, yf_ref, yr_ref, hf_ref, hr_ref):
    c = WKV_CHUNK
    w = RWKV_WIDTH
    tt = shf_ref.shape[1]
    ncs = tt // c

    @pl.when(pl.program_id(1) == 0)
    def _():
        hf_ref[...] = jnp.zeros_like(hf_ref)
        hr_ref[...] = jnp.zeros_like(hr_ref)

    row_c = lax.broadcasted_iota(jnp.int32, (c, c), 0)
    col_c = lax.broadcasted_iota(jnp.int32, (c, c), 1)
    t_i = lax.broadcasted_iota(jnp.int32, (c, w), 0)
    s_i = lax.broadcasted_iota(jnp.int32, (c, w), 1) % c
    eye_c = (s_i == t_i).astype(F32)
    row_w = lax.broadcasted_iota(jnp.int32, (w, w), 0)
    col_w = lax.broadcasted_iota(jnp.int32, (w, w), 1)
    mask_bd = (row_w // c) == (col_w // HEAD_DIM)
    eye_w = row_w == col_w
    tri_incl = [jnp.where(col_c <= row_c, 1.0, 0.0).astype(BF16),
                jnp.where(col_c >= row_c, 1.0, 0.0).astype(BF16)]
    strict = [s_i < t_i, s_i > t_i]
    incl = [s_i <= t_i, s_i >= t_i]

    def bf(x):
        return x.astype(BF16)

    def bd(x):
        return _expand_bd(bf(x), mask_bd)

    chunks = []
    for ci in range(ncs):
        chunks.append((0, ci * c, shf_ref, df_ref))
        chunks.append((1, (ncs - 1 - ci) * c, shr_ref, dr_ref))

    st = []
    for d, start, sh_ref, dir_ref in chunks:
        rows = pl.ds(start, c)
        st.append(dict(d=d, rows=rows,
                       r=sh_ref[0, rows, 0:w], v=sh_ref[0, rows, w:2 * w], kk=sh_ref[0, rows, 2 * w:3 * w],
                       lw=dir_ref[0, rows, 0:w], kh=dir_ref[0, rows, w:2 * w], nb=dir_ref[0, rows, 2 * w:3 * w]))

    for q in st:
        l_hi, l_mid, l_lo = _split3(q["lw"])
        tri = tri_incl[q["d"]]
        q["cum"] = _dot(tri, l_hi) + _dot(tri, l_mid) + _dot(tri, l_lo)
    for q in st:
        cum, lw = q["cum"], q["lw"]
        tot = jnp.sum(lw, axis=0, keepdims=True)
        e_inv = jnp.exp(-cum)
        e_end = jnp.exp(tot - cum)
        q["tot"] = tot
        q["a_bar"] = q["kk"] * jnp.exp(cum - lw)
        q["r_bar"] = q["r"] * jnp.exp(cum)
        q["b_hat"] = bf(q["nb"] * e_end)
        q["k_hat"] = bf(q["kh"] * e_end)
        q["v_bf"] = bf(q["v"])
        q["v_bd"] = _expand_bd(q["v_bf"], mask_bd)
        q["lhs"] = jnp.concatenate([bf(q["a_bar"]), bf(q["r_bar"])], axis=0)
        q["rhs"] = jnp.concatenate([bd(q["nb"] * e_inv), bd(q["kh"] * e_inv)], axis=0)
    for q in st:
        sc = _dot_nt(q["lhs"], q["rhs"])
        sm, im = strict[q["d"]], incl[q["d"]]
        q["x"] = jnp.where(sm, sc[0:c, 0:w], 0.0)
        q["a_ak"] = bf(jnp.where(sm, sc[0:c, w:2 * w], 0.0))
        q["m_rb"] = bf(jnp.where(im, sc[c:2 * c, 0:w], 0.0))
        q["m_rk"] = bf(jnp.where(im, sc[c:2 * c, w:2 * w], 0.0))
        q["tinv"] = eye_c + q["x"]

    for q in st:
        q["x"] = _dot(bf(q["x"]), bd(q["x"]))
    n_lv = int(math.log2(c)) - 1
    for lv in range(n_lv):
        last = lv == n_lv - 1
        for q in st:
            xb = bd(q["x"])
            if last:
                q["tinv"] = q["tinv"] + _dot(bf(q["tinv"]), xb)
            else:
                res = _dot(jnp.concatenate([bf(q["x"]), bf(q["tinv"])], axis=0), xb)
                q["tinv"] = q["tinv"] + res[c:2 * c]
                q["x"] = res[0:c]
    for q in st:
        res = _dot(jnp.concatenate([q["a_ak"], q["m_rk"]], axis=0), q["v_bd"])
        q["akv"] = res[0:c]
        q["mrkv"] = res[c:2 * c]
    for q in st:
        tinv_bf = bf(q["tinv"])
        q["a_pr"] = bf(_dot(tinv_bf, bd(q["a_bar"])))
        q["v_pr"] = bf(_dot(tinv_bf, bd(q["akv"])))
    for q in st:
        q["r_pr"] = bf(q["r_bar"] + _dot(q["m_rb"], _expand_bd(q["a_pr"], mask_bd)))
        q["y_pr"] = _dot(q["m_rb"], _expand_bd(q["v_pr"], mask_bd)) + q["mrkv"]
    for q in st:
        q["p_bd"] = bf(jnp.where(eye_w, jnp.exp(q["tot"]), 0.0)
                       + jnp.where(mask_bd, _dot_tn(q["b_hat"], q["a_pr"]), 0.0))
        q["q_bd"] = jnp.where(mask_bd,
                              _dot_tn(jnp.concatenate([q["b_hat"], q["k_hat"]], axis=0),
                                      jnp.concatenate([q["v_pr"], q["v_bf"]], axis=0)), 0.0)

    h = [hf_ref[...], hr_ref[...]]
    y_refs = [yf_ref, yr_ref]
    for q in st:
        d = q["d"]
        res = _dot(jnp.concatenate([q["r_pr"], q["p_bd"]], axis=0), bf(h[d]))
        y_refs[d][0, q["rows"], :] = res[0:c] + q["y_pr"]
        h[d] = res[c:c + w] + q["q_bd"]
    hf_ref[...] = h[0]
    hr_ref[...] = h[1]


def _wkv(shared, dirp, tt):
    b, s, _ = shared.shape
    nt = s // tt
    w = RWKV_WIDTH
    fwd = lambda i, j: (i, j, 0)
    bwd = lambda i, j: (i, nt - 1 - j, 0)
    return pl.pallas_call(
        _wkv_kernel,
        grid=(b, nt),
        in_specs=[pl.BlockSpec((1, tt, 3 * w), fwd), pl.BlockSpec((1, tt, 3 * w), bwd),
                  pl.BlockSpec((None, 1, tt, 3 * w), lambda i, j: (0, i, j, 0)),
                  pl.BlockSpec((None, 1, tt, 3 * w), lambda i, j: (1, i, nt - 1 - j, 0))],
        out_specs=[pl.BlockSpec((1, tt, w), fwd), pl.BlockSpec((1, tt, w), bwd)],
        out_shape=[jax.ShapeDtypeStruct((b, s, w), F32), jax.ShapeDtypeStruct((b, s, w), F32)],
        scratch_shapes=[pltpu.VMEM((w, w), F32), pltpu.VMEM((w, w), F32)],
        compiler_params=_cparams(("parallel", "arbitrary")),
        name="wkv",
    )(shared, shared, dirp, dirp)


def _attn_kernel(q_ref, k_ref, v_ref, o_ref):
    tq = q_ref.shape[0]
    k = k_ref[...]
    v = v_ref[...]
    lane = lax.broadcasted_iota(jnp.int32, (tq, ATTN_KV_WIDTH), 1)
    low = lane < HEAD_DIM
    zero = jnp.zeros((), BF16)
    for j in range(ATTN_HEADS // 2):
        qb = q_ref[:, j * ATTN_KV_WIDTH:(j + 1) * ATTN_KV_WIDTH]
        outs = []
        for half in range(2):
            qm = jnp.where(low if half == 0 else ~low, qb, zero)
            s = _dot_nt(qm, k)
            m = jnp.max(s, axis=-1, keepdims=True)
            p = jnp.exp2(s - m)
            l = jnp.sum(p, axis=-1, keepdims=True)
            outs.append(_dot(p.astype(BF16), v) / l)
        o_ref[:, j * ATTN_KV_WIDTH:(j + 1) * ATTN_KV_WIDTH] = jnp.where(low, outs[0], outs[1]).astype(BF16)


def _attn(q, k, v, seq, tq):
    n = q.shape[0]
    nq = seq // tq
    return pl.pallas_call(
        _attn_kernel,
        grid=(n // seq, nq),
        in_specs=[pl.BlockSpec((tq, ATTN_WIDTH), lambda b, i: (b * nq + i, 0)),
                  pl.BlockSpec((seq, ATTN_KV_WIDTH), lambda b, i: (b, 0)),
                  pl.BlockSpec((seq, ATTN_KV_WIDTH), lambda b, i: (b, 0))],
        out_specs=pl.BlockSpec((tq, ATTN_WIDTH), lambda b, i: (b * nq + i, 0)),
        out_shape=jax.ShapeDtypeStruct((n, ATTN_WIDTH), BF16),
        compiler_params=_cparams(("parallel", "parallel")),
        name="attn",
    )(q, k, v)


def _outproj_kernel(x_ref, pool_ref, wkvf_ref, wkvr_ref, post_ref, attn_ref, wo_ref, gng_ref, gnb_ref,
                    ones_ref, lg_ref, lb_ref, wrh_ref, wrl_ref, br_ref,
                    x1_ref, x1p_ref, route_ref):
    w = RWKV_WIDTH
    ones = ones_ref[...]
    wkv = wkvf_ref[...] + wkvr_ref[...]
    inv = 1.0 / HEAD_DIM
    mu = _headsum(wkv, ones) * inv
    cen = wkv - mu
    var = _headsum(cen * cen, ones) * inv
    yr = cen * lax.rsqrt(var + GN_EPS) * gng_ref[...] + gnb_ref[...] + post_ref[:, 0:w]
    yr = yr * post_ref[:, w:2 * w]
    y = (_dot(pool_ref[...].astype(BF16), wo_ref[0:POOL_WIDTH, :])
         + _dot(yr.astype(BF16), wo_ref[POOL_WIDTH:POOL_WIDTH + w, :])
         + _dot(attn_ref[...], wo_ref[POOL_WIDTH + w:D_MODEL, :]))
    x1 = _layer_norm(DEEPNORM_ALPHA * x_ref[...] + y, lg_ref[...], lb_ref[...])
    x1_ref[...] = x1
    x1p_ref[...] = _pack_rows(x1)

    xh, xl = _split2(x1)
    logits = (_dot(xh, wrh_ref[...]) + _dot(xl, wrh_ref[...]) + _dot(xh, wrl_ref[...])) + br_ref[...]
    lane = lax.broadcasted_iota(jnp.int32, logits.shape, 1)
    lane_f = lane.astype(F32)
    lane_grp_f = ((lane - N_GROUPS) // EXPERTS_PER_GROUP).astype(F32)
    neg = -jnp.inf
    big = jnp.float32(1 << 20)
    gl = jnp.where(lane < N_GROUPS, logits, neg)
    gmax = jnp.max(gl, axis=-1, keepdims=True)
    grp = jnp.min(jnp.where(gl == gmax, lane_f, big), axis=-1, keepdims=True)
    gw = 1.0 / jnp.sum(jnp.exp(gl - gmax), axis=-1, keepdims=True)
    in_grp = (lane >= N_GROUPS) & (lane < N_GROUPS + N_EXPERTS) & (lane_grp_f == grp)
    el = jnp.where(in_grp, logits, neg)
    v1 = jnp.max(el, axis=-1, keepdims=True)
    i1 = jnp.min(jnp.where(el == v1, lane_f, big), axis=-1, keepdims=True)
    el2 = jnp.where(lane_f == i1, neg, el)
    v2 = jnp.max(el2, axis=-1, keepdims=True)
    i2 = jnp.min(jnp.where(el2 == v2, lane_f, big), axis=-1, keepdims=True)
    e21 = jnp.exp(v2 - v1)
    g1 = gw / (1.0 + e21)
    g2 = gw * e21 / (1.0 + e21)
    route = jnp.where(lane == 0, i1 - N_GROUPS,
                      jnp.where(lane == 1, i2 - N_GROUPS,
                                jnp.where(lane == 2, g1, jnp.where(lane == 3, g2, 0.0))))
    route_ref[...] = route


def _outproj(x2d, pool2d, wkvf, wkvr, post2d, attn2d, wo, gng, gnb, ones, lg, lb, wrh, wrl, br, tm):
    n = x2d.shape[0]
    w = RWKV_WIDTH
    row = lambda width: pl.BlockSpec((tm, width), lambda i: (i, 0))
    vec = lambda width: pl.BlockSpec((1, width), lambda i: (0, 0))
    return pl.pallas_call(
        _outproj_kernel,
        grid=(n // tm,),
        in_specs=[row(D_MODEL), row(POOL_WIDTH), row(w), row(w),
                  row(2 * w), row(ATTN_WIDTH),
                  pl.BlockSpec((D_MODEL, D_MODEL), lambda i: (0, 0)),
                  vec(w), vec(w),
                  pl.BlockSpec((w, w), lambda i: (0, 0)),
                  vec(D_MODEL), vec(D_MODEL),
                  pl.BlockSpec((D_MODEL, ROUTE_LANES), lambda i: (0, 0)),
                  pl.BlockSpec((D_MODEL, ROUTE_LANES), lambda i: (0, 0)),
                  vec(ROUTE_LANES)],
        out_specs=[row(D_MODEL), row(HALF_D), row(ROUTE_LANES)],
        out_shape=[jax.ShapeDtypeStruct((n, D_MODEL), F32),
                   jax.ShapeDtypeStruct((n, HALF_D), U32),
                   jax.ShapeDtypeStruct((n, ROUTE_LANES), F32)],
        compiler_params=_cparams(("parallel",)),
        name="outproj",
    )(x2d, pool2d, wkvf, wkvr, post2d, attn2d, wo, gng, gnb, ones, lg, lb, wrh, wrl, br)


def _expert_kernel(bexp_ref, nused_ref, xs_ref, wg_ref, wu_ref, wd_ref, ys_ref, wgb_ref, wub_ref, wdb_ref):
    i = pl.program_id(0)
    used = i < nused_ref[0]

    @pl.when(used & ((i == 0) | (bexp_ref[i] != bexp_ref[jnp.maximum(i - 1, 0)])))
    def _():
        wgb_ref[...] = wg_ref[0].astype(BF16)
        wub_ref[...] = wu_ref[0].astype(BF16)
        wdb_ref[...] = wd_ref[0].astype(BF16)

    @pl.when(used)
    def _():
        xa, xb = _unpack_rows(xs_ref[...])
        h1 = _dot(xa, wgb_ref[0:HALF_D, :]) + _dot(xb, wgb_ref[HALF_D:D_MODEL, :])
        h2 = _dot(xa, wub_ref[0:HALF_D, :]) + _dot(xb, wub_ref[HALF_D:D_MODEL, :])
        h = (h1 * jax.nn.sigmoid(h1) * h2).astype(BF16)
        ys_ref[...] = _pack_rows(_dot(h, wdb_ref[...]))

    @pl.when(jnp.logical_not(used))
    def _():
        ys_ref[...] = jnp.zeros_like(ys_ref)


def _experts(block_exp, n_used, xs, wg, wu, wd, layer):
    p = xs.shape[0]
    nb = p // MOE_BLOCK
    wmap = lambda i, be, nu: (layer, be[i], 0, 0)
    grid_spec = pltpu.PrefetchScalarGridSpec(
        num_scalar_prefetch=2,
        grid=(nb,),
        in_specs=[pl.BlockSpec((MOE_BLOCK, HALF_D), lambda i, be, nu: (i, 0)),
                  pl.BlockSpec((None, 1, D_MODEL, EXPERT_HIDDEN), wmap),
                  pl.BlockSpec((None, 1, D_MODEL, EXPERT_HIDDEN), wmap),
                  pl.BlockSpec((None, 1, EXPERT_HIDDEN, D_MODEL), wmap)],
        out_specs=pl.BlockSpec((MOE_BLOCK, HALF_D), lambda i, be, nu: (i, 0)),
        scratch_shapes=[pltpu.VMEM((D_MODEL, EXPERT_HIDDEN), BF16),
                        pltpu.VMEM((D_MODEL, EXPERT_HIDDEN), BF16),
                        pltpu.VMEM((EXPERT_HIDDEN, D_MODEL), BF16)],
    )
    return pl.pallas_call(
        _expert_kernel,
        grid_spec=grid_spec,
        out_shape=jax.ShapeDtypeStruct((p, HALF_D), U32),
        compiler_params=_cparams(("arbitrary",)),
        name="experts",
    )(block_exp, n_used, xs, wg, wu, wd)


SC_CORES = 2
SC_SUBCORES = 16
SC_GATHER_ROWS = 64


def _sc_gather(table, idx):
    v, d = table.shape
    b = idx.shape[0]
    nw = SC_CORES * SC_SUBCORES
    ch = SC_GATHER_ROWS
    assert b % (nw * ch * 2) == 0
    b_per_w = b // nw
    nch = b_per_w // ch
    mesh = plsc.VectorSubcoreMesh(core_axis_name="c", subcore_axis_name="s")

    @functools.partial(
        pl.kernel, mesh=mesh,
        out_type=jax.ShapeDtypeStruct((b, d), table.dtype),
        scratch_types=[pltpu.VMEM((b_per_w,), jnp.int32),
                       pltpu.VMEM((2, ch, d), table.dtype),
                       pltpu.SemaphoreType.DMA, pltpu.SemaphoreType.DMA,
                       pltpu.SemaphoreType.DMA, pltpu.SemaphoreType.DMA],
    )
    def gather_kernel(table_hbm, idx_hbm, out_hbm, idx_v, rows_v, gsem0, gsem1, wsem0, wsem1):
        wid = lax.axis_index("s") * SC_CORES + lax.axis_index("c")
        base = wid * b_per_w
        pltpu.sync_copy(idx_hbm.at[pl.ds(base, b_per_w)], idx_v)
        gsem = (gsem0, gsem1)
        wsem = (wsem0, wsem1)

        def gather(jj, slot):
            off = pl.multiple_of(jj * ch, ch)
            return pltpu.make_async_copy(table_hbm.at[idx_v.at[pl.ds(off, ch)]], rows_v.at[slot], gsem[slot])

        def write(jj, slot):
            off = pl.multiple_of(jj * ch, ch)
            return pltpu.make_async_copy(rows_v.at[slot], out_hbm.at[pl.ds(base + off, ch)], wsem[slot])

        gather(0, 0).start()

        @pl.loop(0, nch, step=2)
        def _(j):
            for slot in range(2):
                jj = j + slot

                gather(jj, slot).wait()

                @pl.when(jj >= 1)
                def _():
                    write(jj - 1, 1 - slot).wait()

                @pl.when(jj + 1 < nch)
                def _():
                    gather(jj + 1, 1 - slot).start()

                write(jj, slot).start()

        write(nch - 1, (nch - 1) % 2).wait()

    return gather_kernel(table, idx)


def _gather_rows(table, idx):
    b = idx.shape[0]
    unit = SC_CORES * SC_SUBCORES * SC_GATHER_ROWS * 2
    bp = -(-b // unit) * unit
    if bp == b:
        return _sc_gather(table, idx)
    return _sc_gather(table, jnp.pad(idx, (0, bp - b)))[:b]


def _combine_kernel(x1_ref, ya_ref, yb_ref, route_ref, lg_ref, lb_ref, o_ref):
    o_ref[...] = _combine_rows(x1_ref[...], ya_ref[...], yb_ref[...], route_ref[...], lg_ref[...], lb_ref[...])


def _combine(x1, yab, route, lg, lb, tm):
    n = x1.shape[0]
    nt = n // tm
    row = lambda width: pl.BlockSpec((tm, width), lambda i: (i, 0))
    vec = lambda width: pl.BlockSpec((1, width), lambda i: (0, 0))
    return pl.pallas_call(
        _combine_kernel,
        grid=(nt,),
        in_specs=[row(D_MODEL), row(HALF_D), pl.BlockSpec((tm, HALF_D), lambda i: (i + nt, 0)),
                  row(ROUTE_LANES), vec(D_MODEL), vec(D_MODEL)],
        out_specs=row(D_MODEL),
        out_shape=jax.ShapeDtypeStruct((n, D_MODEL), F32),
        compiler_params=_cparams(("parallel",)),
        name="combine",
    )(x1, yab, yab, route, lg, lb)


def _q_perm():
    order = [h for j in range(ATTN_HEADS // 2) for h in (j, j + ATTN_HEADS // 2)]
    return jnp.concatenate([jnp.arange(h * HEAD_DIM, (h + 1) * HEAD_DIM) for h in order])


def _block_diag(blocks):
    n = len(blocks)
    r, c = blocks[0].shape
    out = jnp.zeros((n * r, n * c), blocks[0].dtype)
    for i, blk in enumerate(blocks):
        out = out.at[i * r:(i + 1) * r, i * c:(i + 1) * c].set(blk)
    return out


def _rope_tables(seq):
    rows = seq // GRID_W
    row_id = jnp.repeat(jnp.arange(rows), GRID_W).astype(F32)
    col_id = jnp.tile(jnp.arange(GRID_W), rows).astype(F32)
    half = HEAD_DIM // 2
    inv_freq = ROPE_THETA ** (-jnp.arange(0, half, 2, dtype=F32) / half)
    ang_r = row_id[:, None] * inv_freq
    ang_c = col_id[:, None] * inv_freq
    ang = jnp.concatenate([ang_r, ang_r, ang_c, ang_c], -1)
    sign = jnp.where((jnp.arange(HEAD_DIM) // 16) % 2 == 0, -1.0, 1.0).astype(F32)
    cos8 = jnp.tile(jnp.cos(ang), (1, ATTN_HEADS))
    sin8 = jnp.tile(jnp.sin(ang) * sign, (1, ATTN_HEADS))
    return cos8, sin8


def _dispatch(route, n):
    m = n * 2
    experts = jnp.arange(N_EXPERTS, dtype=jnp.int32)
    e_flat = jnp.concatenate([route[:, 0], route[:, 1]]).astype(jnp.int32)
    pair = jnp.arange(m, dtype=jnp.int32)
    e_sorted, order = lax.sort((e_flat, pair), num_keys=1, is_stable=True)
    start = jnp.sum(e_sorted[None, :] < experts[:, None], axis=1, dtype=jnp.int32)
    counts = jnp.concatenate([start[1:], jnp.full((1,), m, jnp.int32)]) - start
    padded = (counts + MOE_BLOCK - 1) // MOE_BLOCK * MOE_BLOCK
    ends_p = jnp.sum(jnp.where(experts[None, :] <= experts[:, None], padded[None, :], 0), axis=1, dtype=jnp.int32)
    pstart = ends_p - padded
    delta = pstart - start
    n_blocks = -(-(m + N_EXPERTS * (MOE_BLOCK - 1)) // MOE_BLOCK)
    block_start = jnp.arange(n_blocks, dtype=jnp.int32) * MOE_BLOCK
    block_exp = jnp.minimum(jnp.sum(ends_p[None, :] <= block_start[:, None], axis=1, dtype=jnp.int32),
                            N_EXPERTS - 1)
    lane = jnp.arange(MOE_BLOCK, dtype=jnp.int32)[None, :]
    blk = block_exp[:, None] == experts[None, :]

    def per_block(table):
        return jnp.sum(jnp.where(blk, table[None, :], 0), axis=1, dtype=jnp.int32)

    valid = (block_start - per_block(pstart))[:, None] + lane < per_block(counts)[:, None]
    src = jnp.clip((block_start - per_block(delta))[:, None] + lane, 0, m - 1)
    run = order[src.reshape(-1)].reshape(n_blocks, MOE_BLOCK)
    row_tok = jnp.where(valid, run % n, (block_start[:, None] + lane) % n).reshape(-1)
    n_used = (ends_p[-1] // MOE_BLOCK).astype(jnp.int32).reshape(1)
    dest = pair + jnp.sum(jnp.where(e_sorted[None, :] == experts[:, None], delta[:, None], 0), axis=0,
                          dtype=jnp.int32)
    _, pos = lax.sort((order, dest), num_keys=1)
    return row_tok, pos, block_exp, n_used


def kernel(x, w_in, mu_prev, mu_next, pool_w, pool_scale, rw_w0, rw_w_up, rw_a0, rw_a_up, rw_g_up, rw_k_k, rw_k_a, rw_r_k, rw_gn_g, rw_gn_b, q_norm, k_norm, w_o, ln1_g, ln1_b, router_group, router_group_b, router_expert, router_expert_b, exp_gate, exp_up, exp_down, ln2_g, ln2_b):
    b, s, d = x.shape
    n = b * s
    w = RWKV_WIDTH
    tm = min(512, s)
    qperm = _q_perm()
    cos8, sin8 = _rope_tables(s)
    ones_q = _block_diag([jnp.ones((HEAD_DIM, HEAD_DIM), BF16)] * ATTN_HEADS)
    ones_r = ones_q[0:w, 0:w]
    a_end = POOL_WIDTH
    b_end = POOL_WIDTH + RWKV_IN

    xc = x.reshape(n, d)
    pending = None
    for l in range(DEPTH):
        wq = w_in[l][:, b_end:b_end + ATTN_WIDTH][:, qperm]
        w_proj = jnp.concatenate(
            [w_in[l][:, :a_end], w_in[l][:, a_end:b_end],
             jnp.zeros((d, RWKV_PAD - RWKV_IN), F32), wq, w_in[l][:, b_end + ATTN_WIDTH:]],
            axis=1).astype(BF16)
        pool_bd = _block_diag([pool_w[l, g] for g in range(len(POOL_WINDOWS))]).astype(BF16)
        pad_vec = jnp.zeros((RWKV_PAD - RWKV_IN,), F32)
        mup = jnp.concatenate([mu_prev[l], pad_vec]).reshape(1, RWKV_PAD)
        mun = jnp.concatenate([mu_next[l], pad_vec]).reshape(1, RWKV_PAD)
        w_lora = jnp.zeros((w, 5 * w), F32)
        for di in range(2):
            w_lora = w_lora.at[di * DECAY_LORA:(di + 1) * DECAY_LORA, di * w:(di + 1) * w].set(rw_w_up[l, di])
            o = 2 * DECAY_LORA
            w_lora = w_lora.at[o + di * AAA_LORA:o + (di + 1) * AAA_LORA, (2 + di) * w:(3 + di) * w].set(rw_a_up[l, di])
        o = 2 * DECAY_LORA + 2 * AAA_LORA
        w_lora = w_lora.at[o:o + GATE_LORA, 4 * w:5 * w].set(rw_g_up[l]).astype(BF16)
        lora_bias = jnp.concatenate([rw_w0[l, 0], rw_w0[l, 1], rw_a0[l, 0], rw_a0[l, 1],
                                     jnp.zeros((w,), F32)]).reshape(1, 5 * w)
        qg = jnp.tile(q_norm[l], ATTN_HEADS).reshape(1, ATTN_WIDTH)
        kg = jnp.tile(k_norm[l], 2).reshape(1, ATTN_KV_WIDTH)
        wo_attn = w_o[l][POOL_WIDTH + w:][qperm]
        wo = jnp.concatenate([w_o[l][:POOL_WIDTH + w], wo_attn], axis=0).astype(BF16)
        wr = jnp.concatenate([router_group[l], router_expert[l],
                              jnp.zeros((d, ROUTE_LANES - N_GROUPS - N_EXPERTS), F32)], axis=1)
        wrh = wr.astype(BF16)
        wrl = (wr - wrh.astype(F32)).astype(BF16)
        br = jnp.concatenate([router_group_b[l], router_expert_b[l],
                              jnp.zeros((ROUTE_LANES - N_GROUPS - N_EXPERTS,), F32)]).reshape(1, ROUTE_LANES)

        if pending is None:
            pool_in, rw_in, qh, kh, vh = _proj(xc, w_proj, cos8, sin8, qg, kg, ones_q, s, tm)
        else:
            xc, pool_in, rw_in, qh, kh, vh = _proj(None, w_proj, cos8, sin8, qg, kg, ones_q, s, tm, prev=pending)
        y_pool = _pool(pool_in.reshape(b, s, POOL_WIDTH), pool_bd, pool_scale[l].reshape(1, POOL_WIDTH))
        shared, dirp, post = _rwkv_prep(rw_in.reshape(b, s, RWKV_PAD), mup, mun, w_lora, lora_bias,
                                        rw_k_k[l].reshape(1, w), rw_k_a[l].reshape(1, w),
                                        rw_r_k[l].reshape(1, w), ones_r, tm)
        wkv_f, wkv_r = _wkv(shared, dirp, min(256, s))
        y_attn = _attn(qh, kh, vh, s, min(512, s))
        x1, x1p, route = _outproj(xc, y_pool.reshape(n, POOL_WIDTH), wkv_f.reshape(n, w), wkv_r.reshape(n, w),
                                  post.reshape(n, 2 * w), y_attn, wo,
                                  rw_gn_g[l].reshape(1, w), rw_gn_b[l].reshape(1, w), ones_r,
                                  ln1_g[l].reshape(1, d), ln1_b[l].reshape(1, d), wrh, wrl, br, tm)

        row_tok, pos, block_exp, n_used = _dispatch(route, n)
        xs = _gather_rows(x1p, row_tok)
        ys = _experts(block_exp, n_used, xs, exp_gate, exp_up, exp_down, l)
        yab = _gather_rows(ys, pos)
        pending = (x1, yab, route, ln2_g[l].reshape(1, d), ln2_b[l].reshape(1, d))
    return _combine(*pending, tm).reshape(b, s, d)
```

```python
import functools
import math

import jax
import jax.numpy as jnp
from jax import lax
from jax.experimental import pallas as pl
from jax.experimental.pallas import tpu as pltpu
from jax.experimental.pallas import tpu_sc as plsc

F32 = jnp.float32
BF16 = jnp.bfloat16

D_MODEL = 1024
DEPTH = 4
GRID_W = 64
HEAD_DIM = 64
POOL_WIDTH = 256
POOL_WINDOWS = (2, 4, 8, 16)
POOL_GROUP = 64
RWKV_WIDTH = 256
RWKV_HEADS = 4
DECAY_LORA = 32
AAA_LORA = 32
GATE_LORA = 64
GN_EPS = 64e-5
RWKV_IN = 960
RWKV_PAD = 1024
ATTN_WIDTH = 512
ATTN_HEADS = 8
ATTN_KV_WIDTH = 128
ATTN_IN = ATTN_WIDTH + 2 * ATTN_KV_WIDTH
ROPE_THETA = 10000.0
QK_EPS = 1e-6
N_GROUPS = 4
EXPERTS_PER_GROUP = 8
N_EXPERTS = 32
EXPERT_HIDDEN = 512
MOE_BLOCK = 512
MOE_STREAMS = 2
DEEPNORM_ALPHA = float((2 * DEPTH) ** 0.25)
LN_EPS = 1e-5
PROJ_WIDTH = POOL_WIDTH + RWKV_PAD + ATTN_IN
ROUTE_LANES = 128
WKV_CHUNK = 64
VMEM_LIMIT = 48 * 1024 * 1024


def _cparams(sem):
    return pltpu.CompilerParams(dimension_semantics=sem, vmem_limit_bytes=VMEM_LIMIT)


def _dot(a, b):
    return jnp.dot(a, b, preferred_element_type=F32)


def _dot_nt(a, b):
    return lax.dot_general(a, b, (((1,), (1,)), ((), ())), preferred_element_type=F32)


def _dot_tn(a, b):
    return lax.dot_general(a, b, (((0,), (0,)), ((), ())), preferred_element_type=F32)


def _split2(x):
    hi = x.astype(BF16)
    lo = (x - hi.astype(F32)).astype(BF16)
    return hi, lo


def _split3(x):
    hi = x.astype(BF16)
    r1 = x - hi.astype(F32)
    mid = r1.astype(BF16)
    lo = (r1 - mid.astype(F32)).astype(BF16)
    return hi, mid, lo


HALF_D = D_MODEL // 2
U32 = jnp.uint32


def _pack_rows(x):
    hi = lax.bitcast_convert_type(x[:, :HALF_D].astype(BF16).astype(F32), U32)
    lo = lax.bitcast_convert_type(x[:, HALF_D:].astype(BF16).astype(F32), U32)
    return hi | (lo >> 16)


def _unpack_rows(u):
    a = lax.bitcast_convert_type(u & jnp.uint32(0xFFFF0000), F32).astype(BF16)
    b = lax.bitcast_convert_type(u << 16, F32).astype(BF16)
    return a, b


def _headsum(x, ones_bf16):
    hi, lo = _split2(x)
    return _dot(hi, ones_bf16) + _dot(lo, ones_bf16)


def _rope(x, cos, sin_signed):
    n = x.shape[-1]
    lane = lax.broadcasted_iota(jnp.int32, x.shape, 1)
    first = (lane // 16) % 2 == 0
    partner = jnp.where(first, pltpu.roll(x, n - 16, 1), pltpu.roll(x, 16, 1))
    return x * cos + partner * sin_signed


def _layer_norm(z, g, b):
    mu = jnp.mean(z, axis=-1, keepdims=True)
    zc = z - mu
    var = jnp.mean(zc * zc, axis=-1, keepdims=True)
    return zc * lax.rsqrt(var + LN_EPS) * g + b


def _combine_rows(x1, ya, yb, route, g, b):
    g1 = route[:, 2:3]
    g2 = route[:, 3:4]
    a_hi, a_lo = _unpack_rows(ya)
    b_hi, b_lo = _unpack_rows(yb)
    m = jnp.concatenate([g1 * a_hi.astype(F32) + g2 * b_hi.astype(F32),
                         g1 * a_lo.astype(F32) + g2 * b_lo.astype(F32)], axis=1)
    return _layer_norm(DEEPNORM_ALPHA * x1 + m, g, b)


def _proj_kernel(x_ref, w_ref, cos_ref, sin_ref, qg_ref, kg_ref, onesq_ref,
                 pool_ref, rw_ref, q_ref, k_ref, v_ref):
    _proj_rows(x_ref[...], w_ref, cos_ref, sin_ref, qg_ref, kg_ref, onesq_ref,
               pool_ref, rw_ref, q_ref, k_ref, v_ref)


def _combine_proj_kernel(x1_ref, ya_ref, yb_ref, route_ref, lg_ref, lb_ref,
                         w_ref, cos_ref, sin_ref, qg_ref, kg_ref, onesq_ref,
                         x_ref, pool_ref, rw_ref, q_ref, k_ref, v_ref):
    x = _combine_rows(x1_ref[...], ya_ref[...], yb_ref[...], route_ref[...], lg_ref[...], lb_ref[...])
    x_ref[...] = x
    _proj_rows(x, w_ref, cos_ref, sin_ref, qg_ref, kg_ref, onesq_ref, pool_ref, rw_ref, q_ref, k_ref, v_ref)


def _proj_rows(x, w_ref, cos_ref, sin_ref, qg_ref, kg_ref, onesq_ref, pool_ref, rw_ref, q_ref, k_ref, v_ref):
    xb = x.astype(BF16)
    pool_ref[...] = _dot(xb, w_ref[:, 0:POOL_WIDTH])
    rw_ref[...] = _dot(xb, w_ref[:, POOL_WIDTH:POOL_WIDTH + RWKV_PAD])
    at = _dot(xb, w_ref[:, POOL_WIDTH + RWKV_PAD:PROJ_WIDTH])

    q = at[:, 0:ATTN_WIDTH]
    k = at[:, ATTN_WIDTH:ATTN_WIDTH + ATTN_KV_WIDTH]
    v = at[:, ATTN_WIDTH + ATTN_KV_WIDTH:ATTN_IN]
    ones_q = onesq_ref[...]
    ones_k = onesq_ref[0:ATTN_KV_WIDTH, 0:ATTN_KV_WIDTH]
    inv = 1.0 / HEAD_DIM
    qn = q * lax.rsqrt(_headsum(q * q, ones_q) * inv + QK_EPS) * qg_ref[...]
    kn = k * lax.rsqrt(_headsum(k * k, ones_k) * inv + QK_EPS) * kg_ref[...]
    cos = cos_ref[...]
    sin = sin_ref[...]
    q_ref[...] = (_rope(qn, cos, sin) * (HEAD_DIM ** -0.5 * math.log2(math.e))).astype(BF16)
    k_ref[...] = _rope(kn, cos[:, 0:ATTN_KV_WIDTH], sin[:, 0:ATTN_KV_WIDTH]).astype(BF16)
    v_ref[...] = v.astype(BF16)


def _proj(x2d, w, cos8, sin8, qg, kg, ones_q, seq, tm, prev=None, rows=None):
    if prev is not None:
        n, blk0 = prev[0].shape[0], 0
    elif rows is None:
        n, blk0 = x2d.shape[0], 0
    else:
        n, blk0 = rows[1], rows[0]
    nts = seq // tm
    nt = n // tm
    row = lambda width: pl.BlockSpec((tm, width), lambda i: (i, 0))
    const = lambda r, c: pl.BlockSpec((r, c), lambda i: (0, 0))
    proj_in = [const(D_MODEL, PROJ_WIDTH),
               pl.BlockSpec((tm, ATTN_WIDTH), lambda i: (i % nts, 0)),
               pl.BlockSpec((tm, ATTN_WIDTH), lambda i: (i % nts, 0)),
               const(1, ATTN_WIDTH), const(1, ATTN_KV_WIDTH), const(ATTN_WIDTH, ATTN_WIDTH)]
    proj_out = [row(POOL_WIDTH), row(RWKV_PAD), row(ATTN_WIDTH), row(ATTN_KV_WIDTH), row(ATTN_KV_WIDTH)]
    proj_shape = [jax.ShapeDtypeStruct((n, POOL_WIDTH), F32),
                  jax.ShapeDtypeStruct((n, RWKV_PAD), F32),
                  jax.ShapeDtypeStruct((n, ATTN_WIDTH), BF16),
                  jax.ShapeDtypeStruct((n, ATTN_KV_WIDTH), BF16),
                  jax.ShapeDtypeStruct((n, ATTN_KV_WIDTH), BF16)]
    if prev is None:
        return pl.pallas_call(
            _proj_kernel,
            grid=(nt,),
            in_specs=[pl.BlockSpec((tm, D_MODEL), lambda i: (i + blk0, 0))] + proj_in,
            out_specs=proj_out,
            out_shape=proj_shape,
            compiler_params=_cparams(("parallel",)),
            name="proj",
        )(x2d, w, cos8, sin8, qg, kg, ones_q)
    x1, yab, route, lg, lb = prev
    return pl.pallas_call(
        _combine_proj_kernel,
        grid=(nt,),
        in_specs=[row(D_MODEL), row(HALF_D), pl.BlockSpec((tm, HALF_D), lambda i: (i + nt, 0)),
                  row(ROUTE_LANES), const(1, D_MODEL), const(1, D_MODEL)] + proj_in,
        out_specs=[row(D_MODEL)] + proj_out,
        out_shape=[jax.ShapeDtypeStruct((n, D_MODEL), F32)] + proj_shape,
        compiler_params=_cparams(("parallel",)),
        name="combine_proj",
    )(x1, yab, yab, route, lg, lb, w, cos8, sin8, qg, kg, ones_q)


def _pool_kernel(u_ref, w_ref, scale_ref, o_ref):
    s = u_ref.shape[1]
    u = u_ref[0]
    t = lax.broadcasted_iota(jnp.int32, (s, POOL_WIDTH), 0)

    def down(x, k):
        return jnp.where(t >= k, pltpu.roll(x, k, 0), 0.0)

    def up(x, k):
        return jnp.where(t < s - k, pltpu.roll(x, s - k, 0), 0.0)

    left = [down(u, 1)]
    right = [u]
    for k in (1, 2, 4):
        left.append(left[-1] + down(left[-1], k))
        right.append(right[-1] + up(right[-1], k))
    sums = [l + r for l, r in zip(left, right)]

    grp = lax.broadcasted_iota(jnp.int32, (s, POOL_WIDTH), 1) // POOL_GROUP
    half = jnp.where(grp == 0, 1, jnp.where(grp == 1, 2, jnp.where(grp == 2, 4, 8)))
    cnt = (jnp.minimum(t + half, s) - jnp.maximum(t - half, 0)).astype(F32)
    tot = jnp.where(grp == 0, sums[0], jnp.where(grp == 1, sums[1], jnp.where(grp == 2, sums[2], sums[3])))
    d = tot / cnt - u
    o_ref[0] = _dot(d.astype(BF16), w_ref[...]) * scale_ref[...]


def _pool(u3d, w_bd, scale):
    b, s, _ = u3d.shape
    return pl.pallas_call(
        _pool_kernel,
        grid=(b,),
        in_specs=[pl.BlockSpec((1, s, POOL_WIDTH), lambda i: (i, 0, 0)),
                  pl.BlockSpec((POOL_WIDTH, POOL_WIDTH), lambda i: (0, 0)),
                  pl.BlockSpec((1, POOL_WIDTH), lambda i: (0, 0))],
        out_specs=pl.BlockSpec((1, s, POOL_WIDTH), lambda i: (i, 0, 0)),
        out_shape=jax.ShapeDtypeStruct((b, s, POOL_WIDTH), F32),
        compiler_params=_cparams(("parallel",)),
        name="pool",
    )(u3d, w_bd, scale)


_SHIFT_HALO = 8


def _rwkv_prep_kernel(cur_ref, prev_ref, next_ref, mup_ref, mun_ref, wl_ref, bias_ref,
                      kk_ref, ka_ref, rk_ref, ones_ref,
                      sh_ref, dir_ref, post_ref):
    ts = cur_ref.shape[1]
    ti = pl.program_id(1)
    nt = pl.num_programs(1)
    cur = cur_ref[0]
    row = lax.broadcasted_iota(jnp.int32, cur.shape, 0)
    before = jnp.where(ti > 0, prev_ref[0, _SHIFT_HALO - 1:_SHIFT_HALO, :], 0.0)
    after = jnp.where(ti < nt - 1, next_ref[0, 0:1, :], 0.0)
    prev = jnp.where(row == 0, before, pltpu.roll(cur, 1, 0))
    nxt = jnp.where(row == ts - 1, after, pltpu.roll(cur, ts - 1, 0))
    f = cur + mup_ref[...] * (prev - cur) + mun_ref[...] * (nxt - cur)

    w = RWKV_WIDTH
    r = f[:, 0:w]
    k = f[:, w:2 * w]
    v = f[:, 2 * w:3 * w]
    lora = f[:, 3 * w:4 * w]
    lane = lax.broadcasted_iota(jnp.int32, lora.shape, 1)
    z = jnp.where(lane < 2 * DECAY_LORA, jnp.tanh(lora),
                  jnp.where(lane < 2 * DECAY_LORA + 2 * AAA_LORA, lora,
                            jnp.where(lane < 2 * DECAY_LORA + 2 * AAA_LORA + GATE_LORA,
                                      jax.nn.sigmoid(lora), 0.0)))
    up = _dot(z.astype(BF16), wl_ref[...]) + bias_ref[...]
    ones = ones_ref[...]

    kk0 = k * kk_ref[...]
    nrm = jnp.sqrt(_headsum(kk0 * kk0, ones))
    kk = kk0 / jnp.maximum(nrm, 1e-12)
    sh_ref[0, :, 0:w] = r
    sh_ref[0, :, w:2 * w] = v
    sh_ref[0, :, 2 * w:3 * w] = kk

    ksum = jnp.zeros_like(k)
    for di in range(2):
        logw = -math.exp(-0.5) * jax.nn.sigmoid(up[:, di * w:(di + 1) * w])
        a = jax.nn.sigmoid(up[:, (2 + di) * w:(3 + di) * w])
        kh = k * (1.0 + (a - 1.0) * ka_ref[...])
        ksum = ksum + kh
        dir_ref[di, 0, :, 0:w] = logw
        dir_ref[di, 0, :, w:2 * w] = kh
        dir_ref[di, 0, :, 2 * w:3 * w] = -(a * kk)
    bonus = _headsum(r * ksum * rk_ref[...], ones) * v
    post_ref[0, :, 0:w] = bonus
    post_ref[0, :, w:2 * w] = up[:, 4 * w:5 * w]


def _rwkv_prep(rw3d, mup, mun, wl, bias, k_k, k_a, r_k, ones, ts):
    b, s, _ = rw3d.shape
    nt = s // ts
    hb = ts // _SHIFT_HALO
    nhb = s // _SHIFT_HALO
    w = RWKV_WIDTH
    vec = lambda width: pl.BlockSpec((1, width), lambda i, j: (0, 0))
    return pl.pallas_call(
        _rwkv_prep_kernel,
        grid=(b, nt),
        in_specs=[pl.BlockSpec((1, ts, RWKV_PAD), lambda i, j: (i, j, 0)),
                  pl.BlockSpec((1, _SHIFT_HALO, RWKV_PAD),
                               lambda i, j: (i, jnp.maximum(j * hb - 1, 0), 0)),
                  pl.BlockSpec((1, _SHIFT_HALO, RWKV_PAD),
                               lambda i, j: (i, jnp.minimum((j + 1) * hb, nhb - 1), 0)),
                  vec(RWKV_PAD), vec(RWKV_PAD),
                  pl.BlockSpec((w, 5 * w), lambda i, j: (0, 0)),
                  vec(5 * w), vec(w), vec(w), vec(w),
                  pl.BlockSpec((w, w), lambda i, j: (0, 0))],
        out_specs=[pl.BlockSpec((1, ts, 3 * w), lambda i, j: (i, j, 0)),
                   pl.BlockSpec((2, 1, ts, 3 * w), lambda i, j: (0, i, j, 0)),
                   pl.BlockSpec((1, ts, 2 * w), lambda i, j: (i, j, 0))],
        out_shape=[jax.ShapeDtypeStruct((b, s, 3 * w), F32),
                   jax.ShapeDtypeStruct((2, b, s, 3 * w), F32),
                   jax.ShapeDtypeStruct((b, s, 2 * w), F32)],
        compiler_params=_cparams(("parallel", "parallel")),
        name="rwkv_prep",
    )(rw3d, rw3d, rw3d, mup, mun, wl, bias, k_k, k_a, r_k, ones)


def _expand_bd(x_bf16, mask_bd):
    return jnp.where(mask_bd, jnp.concatenate([x_bf16] * RWKV_HEADS, axis=0), jnp.zeros((), BF16))


def _wkv_kernel(shf_ref, shr_ref, df_ref, dr_ref, yf_ref, yr_ref, hf_ref, hr_ref):
    c = WKV_CHUNK
    w = RWKV_WIDTH
    tt = shf_ref.shape[1]
    ncs = tt // c

    @pl.when(pl.program_id(1) == 0)
    def _():
        hf_ref[...] = jnp.zeros_like(hf_ref)
        hr_ref[...] = jnp.zeros_like(hr_ref)

    row_c = lax.broadcasted_iota(jnp.int32, (c, c), 0)
    col_c = lax.broadcasted_iota(jnp.int32, (c, c), 1)
    t_i = lax.broadcasted_iota(jnp.int32, (c, w), 0)
    s_i = lax.broadcasted_iota(jnp.int32, (c, w), 1) % c
    eye_c = (s_i == t_i).astype(F32)
    row_w = lax.broadcasted_iota(jnp.int32, (w, w), 0)
    col_w = lax.broadcasted_iota(jnp.int32, (w, w), 1)
    mask_bd = (row_w // c) == (col_w // HEAD_DIM)
    eye_w = row_w == col_w
    tri_incl = [jnp.where(col_c <= row_c, 1.0, 0.0).astype(BF16),
                jnp.where(col_c >= row_c, 1.0, 0.0).astype(BF16)]
    strict = [s_i < t_i, s_i > t_i]
    incl = [s_i <= t_i, s_i >= t_i]

    def bf(x):
        return x.astype(BF16)

    def bd(x):
        return _expand_bd(bf(x), mask_bd)

    chunks = []
    for ci in range(ncs):
        chunks.append((0, ci * c, shf_ref, df_ref))
        chunks.append((1, (ncs - 1 - ci) * c, shr_ref, dr_ref))

    st = []
    for d, start, sh_ref, dir_ref in chunks:
        rows = pl.ds(start, c)
        st.append(dict(d=d, rows=rows,
                       r=sh_ref[0, rows, 0:w], v=sh_ref[0, rows, w:2 * w], kk=sh_ref[0, rows, 2 * w:3 * w],
                       lw=dir_ref[0, rows, 0:w], kh=dir_ref[0, rows, w:2 * w], nb=dir_ref[0, rows, 2 * w:3 * w]))

    for q in st:
        l_hi, l_mid, l_lo = _split3(q["lw"])
        tri = tri_incl[q["d"]]
        q["cum"] = _dot(tri, l_hi) + _dot(tri, l_mid) + _dot(tri, l_lo)
    for q in st:
        cum, lw = q["cum"], q["lw"]
        tot = jnp.sum(lw, axis=0, keepdims=True)
        e_inv = jnp.exp(-cum)
        e_end = jnp.exp(tot - cum)
        q["tot"] = tot
        q["a_bar"] = q["kk"] * jnp.exp(cum - lw)
        q["r_bar"] = q["r"] * jnp.exp(cum)
        q["b_hat"] = bf(q["nb"] * e_end)
        q["k_hat"] = bf(q["kh"] * e_end)
        q["v_bf"] = bf(q["v"])
        q["v_bd"] = _expand_bd(q["v_bf"], mask_bd)
        q["lhs"] = jnp.concatenate([bf(q["a_bar"]), bf(q["r_bar"])], axis=0)
        q["rhs"] = jnp.concatenate([bd(q["nb"] * e_inv), bd(q["kh"] * e_inv)], axis=0)
    for q in st:
        sc = _dot_nt(q["lhs"], q["rhs"])
        sm, im = strict[q["d"]], incl[q["d"]]
        q["x"] = jnp.where(sm, sc[0:c, 0:w], 0.0)
        q["a_ak"] = bf(jnp.where(sm, sc[0:c, w:2 * w], 0.0))
        q["m_rb"] = bf(jnp.where(im, sc[c:2 * c, 0:w], 0.0))
        q["m_rk"] = bf(jnp.where(im, sc[c:2 * c, w:2 * w], 0.0))
        q["tinv"] = eye_c + q["x"]

    for q in st:
        q["x"] = _dot(bf(q["x"]), bd(q["x"]))
    n_lv = int(math.log2(c)) - 1
    for lv in range(n_lv):
        last = lv == n_lv - 1
        for q in st:
            xb = bd(q["x"])
            if last:
                q["tinv"] = q["tinv"] + _dot(bf(q["tinv"]), xb)
            else:
                res = _dot(jnp.concatenate([bf(q["x"]), bf(q["tinv"])], axis=0), xb)
                q["tinv"] = q["tinv"] + res[c:2 * c]
                q["x"] = res[0:c]
    for q in st:
        res = _dot(jnp.concatenate([q["a_ak"], q["m_rk"]], axis=0), q["v_bd"])
        q["akv"] = res[0:c]
        q["mrkv"] = res[c:2 * c]
    for q in st:
        tinv_bf = bf(q["tinv"])
        q["a_pr"] = bf(_dot(tinv_bf, bd(q["a_bar"])))
        q["v_pr"] = bf(_dot(tinv_bf, bd(q["akv"])))
    for q in st:
        q["r_pr"] = bf(q["r_bar"] + _dot(q["m_rb"], _expand_bd(q["a_pr"], mask_bd)))
        q["y_pr"] = _dot(q["m_rb"], _expand_bd(q["v_pr"], mask_bd)) + q["mrkv"]
    for q in st:
        q["p_bd"] = bf(jnp.where(eye_w, jnp.exp(q["tot"]), 0.0)
                       + jnp.where(mask_bd, _dot_tn(q["b_hat"], q["a_pr"]), 0.0))
        q["q_bd"] = jnp.where(mask_bd,
                              _dot_tn(jnp.concatenate([q["b_hat"], q["k_hat"]], axis=0),
                                      jnp.concatenate([q["v_pr"], q["v_bf"]], axis=0)), 0.0)

    h = [hf_ref[...], hr_ref[...]]
    y_refs = [yf_ref, yr_ref]
    for q in st:
        d = q["d"]
        res = _dot(jnp.concatenate([q["r_pr"], q["p_bd"]], axis=0), bf(h[d]))
        y_refs[d][0, q["rows"], :] = res[0:c] + q["y_pr"]
        h[d] = res[c:c + w] + q["q_bd"]
    hf_ref[...] = h[0]
    hr_ref[...] = h[1]


def _wkv(shared, dirp, tt):
    b, s, _ = shared.shape
    nt = s // tt
    w = RWKV_WIDTH
    fwd = lambda i, j: (i, j, 0)
    bwd = lambda i, j: (i, nt - 1 - j, 0)
    return pl.pallas_call(
        _wkv_kernel,
        grid=(b, nt),
        in_specs=[pl.BlockSpec((1, tt, 3 * w), fwd), pl.BlockSpec((1, tt, 3 * w), bwd),
                  pl.BlockSpec((None, 1, tt, 3 * w), lambda i, j: (0, i, j, 0)),
                  pl.BlockSpec((None, 1, tt, 3 * w), lambda i, j: (1, i, nt - 1 - j, 0))],
        out_specs=[pl.BlockSpec((1, tt, w), fwd), pl.BlockSpec((1, tt, w), bwd)],
        out_shape=[jax.ShapeDtypeStruct((b, s, w), F32), jax.ShapeDtypeStruct((b, s, w), F32)],
        scratch_shapes=[pltpu.VMEM((w, w), F32), pltpu.VMEM((w, w), F32)],
        compiler_params=_cparams(("parallel", "arbitrary")),
        name="wkv",
    )(shared, shared, dirp, dirp)


def _attn_kernel(q_ref, k_ref, v_ref, o_ref):
    tq = q_ref.shape[0]
    k = k_ref[...]
    v = v_ref[...]
    lane = lax.broadcasted_iota(jnp.int32, (tq, ATTN_KV_WIDTH), 1)
    low = lane < HEAD_DIM
    zero = jnp.zeros((), BF16)
    for j in range(ATTN_HEADS // 2):
        qb = q_ref[:, j * ATTN_KV_WIDTH:(j + 1) * ATTN_KV_WIDTH]
        outs = []
        for half in range(2):
            qm = jnp.where(low if half == 0 else ~low, qb, zero)
            s = _dot_nt(qm, k)
            m = jnp.max(s, axis=-1, keepdims=True)
            p = jnp.exp2(s - m)
            l = jnp.sum(p, axis=-1, keepdims=True)
            outs.append(_dot(p.astype(BF16), v) / l)
        o_ref[:, j * ATTN_KV_WIDTH:(j + 1) * ATTN_KV_WIDTH] = jnp.where(low, outs[0], outs[1]).astype(BF16)


def _attn(q, k, v, seq, tq):
    n = q.shape[0]
    nq = seq // tq
    return pl.pallas_call(
        _attn_kernel,
        grid=(n // seq, nq),
        in_specs=[pl.BlockSpec((tq, ATTN_WIDTH), lambda b, i: (b * nq + i, 0)),
                  pl.BlockSpec((seq, ATTN_KV_WIDTH), lambda b, i: (b, 0)),
                  pl.BlockSpec((seq, ATTN_KV_WIDTH), lambda b, i: (b, 0))],
        out_specs=pl.BlockSpec((tq, ATTN_WIDTH), lambda b, i: (b * nq + i, 0)),
        out_shape=jax.ShapeDtypeStruct((n, ATTN_WIDTH), BF16),
        compiler_params=_cparams(("parallel", "parallel")),
        name="attn",
    )(q, k, v)


def _outproj_kernel(x_ref, pool_ref, wkvf_ref, wkvr_ref, post_ref, attn_ref, wo_ref, gng_ref, gnb_ref,
                    ones_ref, lg_ref, lb_ref, wrh_ref, wrl_ref, br_ref,
                    x1_ref, x1p_ref, route_ref):
    w = RWKV_WIDTH
    ones = ones_ref[...]
    wkv = wkvf_ref[...] + wkvr_ref[...]
    inv = 1.0 / HEAD_DIM
    mu = _headsum(wkv, ones) * inv
    cen = wkv - mu
    var = _headsum(cen * cen, ones) * inv
    yr = cen * lax.rsqrt(var + GN_EPS) * gng_ref[...] + gnb_ref[...] + post_ref[:, 0:w]
    yr = yr * post_ref[:, w:2 * w]
    y = (_dot(pool_ref[...].astype(BF16), wo_ref[0:POOL_WIDTH, :])
         + _dot(yr.astype(BF16), wo_ref[POOL_WIDTH:POOL_WIDTH + w, :])
         + _dot(attn_ref[...], wo_ref[POOL_WIDTH + w:D_MODEL, :]))
    x1 = _layer_norm(DEEPNORM_ALPHA * x_ref[...] + y, lg_ref[...], lb_ref[...])
    x1_ref[...] = x1
    x1p_ref[...] = _pack_rows(x1)

    xh, xl = _split2(x1)
    logits = (_dot(xh, wrh_ref[...]) + _dot(xl, wrh_ref[...]) + _dot(xh, wrl_ref[...])) + br_ref[...]
    lane = lax.broadcasted_iota(jnp.int32, logits.shape, 1)
    lane_f = lane.astype(F32)
    lane_grp_f = ((lane - N_GROUPS) // EXPERTS_PER_GROUP).astype(F32)
    neg = -jnp.inf
    big = jnp.float32(1 << 20)
    gl = jnp.where(lane < N_GROUPS, logits, neg)
    gmax = jnp.max(gl, axis=-1, keepdims=True)
    grp = jnp.min(jnp.where(gl == gmax, lane_f, big), axis=-1, keepdims=True)
    gw = 1.0 / jnp.sum(jnp.exp(gl - gmax), axis=-1, keepdims=True)
    in_grp = (lane >= N_GROUPS) & (lane < N_GROUPS + N_EXPERTS) & (lane_grp_f == grp)
    el = jnp.where(in_grp, logits, neg)
    v1 = jnp.max(el, axis=-1, keepdims=True)
    i1 = jnp.min(jnp.where(el == v1, lane_f, big), axis=-1, keepdims=True)
    el2 = jnp.where(lane_f == i1, neg, el)
    v2 = jnp.max(el2, axis=-1, keepdims=True)
    i2 = jnp.min(jnp.where(el2 == v2, lane_f, big), axis=-1, keepdims=True)
    e21 = jnp.exp(v2 - v1)
    g1 = gw / (1.0 + e21)
    g2 = gw * e21 / (1.0 + e21)
    route = jnp.where(lane == 0, i1 - N_GROUPS,
                      jnp.where(lane == 1, i2 - N_GROUPS,
                                jnp.where(lane == 2, g1, jnp.where(lane == 3, g2, 0.0))))
    route_ref[...] = route


def _outproj(x_res, pool2d, wkvf, wkvr, post2d, attn2d, wo, gng, gnb, ones, lg, lb, wrh, wrl, br, tm):
    x2d, blk0 = x_res
    n = pool2d.shape[0]
    w = RWKV_WIDTH
    row = lambda width: pl.BlockSpec((tm, width), lambda i: (i, 0))
    vec = lambda width: pl.BlockSpec((1, width), lambda i: (0, 0))
    return pl.pallas_call(
        _outproj_kernel,
        grid=(n // tm,),
        in_specs=[pl.BlockSpec((tm, D_MODEL), lambda i: (i + blk0, 0)), row(POOL_WIDTH), row(w), row(w),
                  row(2 * w), row(ATTN_WIDTH),
                  pl.BlockSpec((D_MODEL, D_MODEL), lambda i: (0, 0)),
                  vec(w), vec(w),
                  pl.BlockSpec((w, w), lambda i: (0, 0)),
                  vec(D_MODEL), vec(D_MODEL),
                  pl.BlockSpec((D_MODEL, ROUTE_LANES), lambda i: (0, 0)),
                  pl.BlockSpec((D_MODEL, ROUTE_LANES), lambda i: (0, 0)),
                  vec(ROUTE_LANES)],
        out_specs=[row(D_MODEL), row(HALF_D), row(ROUTE_LANES)],
        out_shape=[jax.ShapeDtypeStruct((n, D_MODEL), F32),
                   jax.ShapeDtypeStruct((n, HALF_D), U32),
                   jax.ShapeDtypeStruct((n, ROUTE_LANES), F32)],
        compiler_params=_cparams(("parallel",)),
        name="outproj",
    )(x2d, pool2d, wkvf, wkvr, post2d, attn2d, wo, gng, gnb, ones, lg, lb, wrh, wrl, br)


def _expert_kernel(bexp_ref, nused_ref, xs_ref, wg_ref, wu_ref, wd_ref, ys_ref, wgb_ref, wub_ref, wdb_ref):
    i = pl.program_id(0)
    used = i < nused_ref[0]

    @pl.when(used & ((i == 0) | (bexp_ref[i] != bexp_ref[jnp.maximum(i - 1, 0)])))
    def _():
        wgb_ref[...] = wg_ref[0].astype(BF16)
        wub_ref[...] = wu_ref[0].astype(BF16)
        wdb_ref[...] = wd_ref[0].astype(BF16)

    @pl.when(used)
    def _():
        xa, xb = _unpack_rows(xs_ref[...])
        h1 = _dot(xa, wgb_ref[0:HALF_D, :]) + _dot(xb, wgb_ref[HALF_D:D_MODEL, :])
        h2 = _dot(xa, wub_ref[0:HALF_D, :]) + _dot(xb, wub_ref[HALF_D:D_MODEL, :])
        h = (h1 * jax.nn.sigmoid(h1) * h2).astype(BF16)
        ys_ref[...] = _pack_rows(_dot(h, wdb_ref[...]))

    @pl.when(jnp.logical_not(used))
    def _():
        ys_ref[...] = jnp.zeros_like(ys_ref)


def _experts(block_exp, n_used, xs, wg, wu, wd, layer):
    p = xs.shape[0]
    nb = p // MOE_BLOCK
    wmap = lambda i, be, nu: (layer, be[i], 0, 0)
    grid_spec = pltpu.PrefetchScalarGridSpec(
        num_scalar_prefetch=2,
        grid=(nb,),
        in_specs=[pl.BlockSpec((MOE_BLOCK, HALF_D), lambda i, be, nu: (i, 0)),
                  pl.BlockSpec((None, 1, D_MODEL, EXPERT_HIDDEN), wmap),
                  pl.BlockSpec((None, 1, D_MODEL, EXPERT_HIDDEN), wmap),
                  pl.BlockSpec((None, 1, EXPERT_HIDDEN, D_MODEL), wmap)],
        out_specs=pl.BlockSpec((MOE_BLOCK, HALF_D), lambda i, be, nu: (i, 0)),
        scratch_shapes=[pltpu.VMEM((D_MODEL, EXPERT_HIDDEN), BF16),
                        pltpu.VMEM((D_MODEL, EXPERT_HIDDEN), BF16),
                        pltpu.VMEM((EXPERT_HIDDEN, D_MODEL), BF16)],
    )
    return pl.pallas_call(
        _expert_kernel,
        grid_spec=grid_spec,
        out_shape=jax.ShapeDtypeStruct((p, HALF_D), U32),
        compiler_params=_cparams(("arbitrary",)),
        name="experts",
    )(block_exp, n_used, xs, wg, wu, wd)


SC_CORES = 2
SC_SUBCORES = 16
SC_GATHER_ROWS = 64


def _sc_gather(table, idx):
    v, d = table.shape
    b = idx.shape[0]
    nw = SC_CORES * SC_SUBCORES
    ch = SC_GATHER_ROWS
    assert b % (nw * ch * 2) == 0
    b_per_w = b // nw
    nch = b_per_w // ch
    mesh = plsc.VectorSubcoreMesh(core_axis_name="c", subcore_axis_name="s")

    @functools.partial(
        pl.kernel, mesh=mesh,
        out_type=jax.ShapeDtypeStruct((b, d), table.dtype),
        scratch_types=[pltpu.VMEM((b_per_w,), jnp.int32),
                       pltpu.VMEM((2, ch, d), table.dtype),
                       pltpu.SemaphoreType.DMA, pltpu.SemaphoreType.DMA,
                       pltpu.SemaphoreType.DMA, pltpu.SemaphoreType.DMA],
    )
    def gather_kernel(table_hbm, idx_hbm, out_hbm, idx_v, rows_v, gsem0, gsem1, wsem0, wsem1):
        wid = lax.axis_index("s") * SC_CORES + lax.axis_index("c")
        base = wid * b_per_w
        pltpu.sync_copy(idx_hbm.at[pl.ds(base, b_per_w)], idx_v)
        gsem = (gsem0, gsem1)
        wsem = (wsem0, wsem1)

        def gather(jj, slot):
            off = pl.multiple_of(jj * ch, ch)
            return pltpu.make_async_copy(table_hbm.at[idx_v.at[pl.ds(off, ch)]], rows_v.at[slot], gsem[slot])

        def write(jj, slot):
            off = pl.multiple_of(jj * ch, ch)
            return pltpu.make_async_copy(rows_v.at[slot], out_hbm.at[pl.ds(base + off, ch)], wsem[slot])

        gather(0, 0).start()

        @pl.loop(0, nch, step=2)
        def _(j):
            for slot in range(2):
                jj = j + slot

                gather(jj, slot).wait()

                @pl.when(jj >= 1)
                def _():
                    write(jj - 1, 1 - slot).wait()

                @pl.when(jj + 1 < nch)
                def _():
                    gather(jj + 1, 1 - slot).start()

                write(jj, slot).start()

        write(nch - 1, (nch - 1) % 2).wait()

    return gather_kernel(table, idx)


def _gather_rows(table, idx):
    b = idx.shape[0]
    unit = SC_CORES * SC_SUBCORES * SC_GATHER_ROWS * 2
    bp = -(-b // unit) * unit
    if bp == b:
        return _sc_gather(table, idx)
    return _sc_gather(table, jnp.pad(idx, (0, bp - b)))[:b]


def _combine_kernel(x1_ref, ya_ref, yb_ref, route_ref, lg_ref, lb_ref, o_ref):
    o_ref[...] = _combine_rows(x1_ref[...], ya_ref[...], yb_ref[...], route_ref[...], lg_ref[...], lb_ref[...])


def _combine(x1, yab, route, lg, lb, tm):
    n = x1.shape[0]
    nt = n // tm
    row = lambda width: pl.BlockSpec((tm, width), lambda i: (i, 0))
    vec = lambda width: pl.BlockSpec((1, width), lambda i: (0, 0))
    return pl.pallas_call(
        _combine_kernel,
        grid=(nt,),
        in_specs=[row(D_MODEL), row(HALF_D), pl.BlockSpec((tm, HALF_D), lambda i: (i + nt, 0)),
                  row(ROUTE_LANES), vec(D_MODEL), vec(D_MODEL)],
        out_specs=row(D_MODEL),
        out_shape=jax.ShapeDtypeStruct((n, D_MODEL), F32),
        compiler_params=_cparams(("parallel",)),
        name="combine",
    )(x1, yab, yab, route, lg, lb)


def _q_perm():
    order = [h for j in range(ATTN_HEADS // 2) for h in (j, j + ATTN_HEADS // 2)]
    return jnp.concatenate([jnp.arange(h * HEAD_DIM, (h + 1) * HEAD_DIM) for h in order])


def _block_diag(blocks):
    n = len(blocks)
    r, c = blocks[0].shape
    out = jnp.zeros((n * r, n * c), blocks[0].dtype)
    for i, blk in enumerate(blocks):
        out = out.at[i * r:(i + 1) * r, i * c:(i + 1) * c].set(blk)
    return out


def _rope_tables(seq):
    rows = seq // GRID_W
    row_id = jnp.repeat(jnp.arange(rows), GRID_W).astype(F32)
    col_id = jnp.tile(jnp.arange(GRID_W), rows).astype(F32)
    half = HEAD_DIM // 2
    inv_freq = ROPE_THETA ** (-jnp.arange(0, half, 2, dtype=F32) / half)
    ang_r = row_id[:, None] * inv_freq
    ang_c = col_id[:, None] * inv_freq
    ang = jnp.concatenate([ang_r, ang_r, ang_c, ang_c], -1)
    sign = jnp.where((jnp.arange(HEAD_DIM) // 16) % 2 == 0, -1.0, 1.0).astype(F32)
    cos8 = jnp.tile(jnp.cos(ang), (1, ATTN_HEADS))
    sin8 = jnp.tile(jnp.sin(ang) * sign, (1, ATTN_HEADS))
    return cos8, sin8


def _dispatch(route, n):
    m = n * 2
    experts = jnp.arange(N_EXPERTS, dtype=jnp.int32)
    e_flat = jnp.concatenate([route[:, 0], route[:, 1]]).astype(jnp.int32)
    pair = jnp.arange(m, dtype=jnp.int32)
    e_sorted, order = lax.sort((e_flat, pair), num_keys=1, is_stable=True)
    start = jnp.sum(e_sorted[None, :] < experts[:, None], axis=1, dtype=jnp.int32)
    counts = jnp.concatenate([start[1:], jnp.full((1,), m, jnp.int32)]) - start
    padded = (counts + MOE_BLOCK - 1) // MOE_BLOCK * MOE_BLOCK
    ends_p = jnp.sum(jnp.where(experts[None, :] <= experts[:, None], padded[None, :], 0), axis=1, dtype=jnp.int32)
    pstart = ends_p - padded
    delta = pstart - start
    n_blocks = -(-(m + N_EXPERTS * (MOE_BLOCK - 1)) // MOE_BLOCK)
    block_start = jnp.arange(n_blocks, dtype=jnp.int32) * MOE_BLOCK
    block_exp = jnp.minimum(jnp.sum(ends_p[None, :] <= block_start[:, None], axis=1, dtype=jnp.int32),
                            N_EXPERTS - 1)
    lane = jnp.arange(MOE_BLOCK, dtype=jnp.int32)[None, :]
    blk = block_exp[:, None] == experts[None, :]

    def per_block(table):
        return jnp.sum(jnp.where(blk, table[None, :], 0), axis=1, dtype=jnp.int32)

    valid = (block_start - per_block(pstart))[:, None] + lane < per_block(counts)[:, None]
    src = jnp.clip((block_start - per_block(delta))[:, None] + lane, 0, m - 1)
    run = order[src.reshape(-1)].reshape(n_blocks, MOE_BLOCK)
    row_tok = jnp.where(valid, run % n, (block_start[:, None] + lane) % n).reshape(-1)
    n_used = (ends_p[-1] // MOE_BLOCK).astype(jnp.int32).reshape(1)
    dest = pair + jnp.sum(jnp.where(e_sorted[None, :] == experts[:, None], delta[:, None], 0), axis=0,
                          dtype=jnp.int32)
    _, pos = lax.sort((order, dest), num_keys=1)
    return row_tok, pos, block_exp, n_used


def kernel(x, w_in, mu_prev, mu_next, pool_w, pool_scale, rw_w0, rw_w_up, rw_a0, rw_a_up, rw_g_up, rw_k_k, rw_k_a, rw_r_k, rw_gn_g, rw_gn_b, q_norm, k_norm, w_o, ln1_g, ln1_b, router_group, router_group_b, router_expert, router_expert_b, exp_gate, exp_up, exp_down, ln2_g, ln2_b):
    b, s, d = x.shape
    n = b * s
    w = RWKV_WIDTH
    tm = min(512, s)
    qperm = _q_perm()
    cos8, sin8 = _rope_tables(s)
    ones_q = _block_diag([jnp.ones((HEAD_DIM, HEAD_DIM), BF16)] * ATTN_HEADS)
    ones_r = ones_q[0:w, 0:w]
    a_end = POOL_WIDTH
    b_end = POOL_WIDTH + RWKV_IN

    streams = MOE_STREAMS if b % MOE_STREAMS == 0 else 1
    bs = b // streams
    ns = bs * s
    x2d = x.reshape(n, d)
    pendings = [None] * streams
    for l in range(DEPTH):
        wq = w_in[l][:, b_end:b_end + ATTN_WIDTH][:, qperm]
        w_proj = jnp.concatenate(
            [w_in[l][:, :a_end], w_in[l][:, a_end:b_end],
             jnp.zeros((d, RWKV_PAD - RWKV_IN), F32), wq, w_in[l][:, b_end + ATTN_WIDTH:]],
            axis=1).astype(BF16)
        pool_bd = _block_diag([pool_w[l, g] for g in range(len(POOL_WINDOWS))]).astype(BF16)
        pad_vec = jnp.zeros((RWKV_PAD - RWKV_IN,), F32)
        mup = jnp.concatenate([mu_prev[l], pad_vec]).reshape(1, RWKV_PAD)
        mun = jnp.concatenate([mu_next[l], pad_vec]).reshape(1, RWKV_PAD)
        w_lora = jnp.zeros((w, 5 * w), F32)
        for di in range(2):
            w_lora = w_lora.at[di * DECAY_LORA:(di + 1) * DECAY_LORA, di * w:(di + 1) * w].set(rw_w_up[l, di])
            o = 2 * DECAY_LORA
            w_lora = w_lora.at[o + di * AAA_LORA:o + (di + 1) * AAA_LORA, (2 + di) * w:(3 + di) * w].set(rw_a_up[l, di])
        o = 2 * DECAY_LORA + 2 * AAA_LORA
        w_lora = w_lora.at[o:o + GATE_LORA, 4 * w:5 * w].set(rw_g_up[l]).astype(BF16)
        lora_bias = jnp.concatenate([rw_w0[l, 0], rw_w0[l, 1], rw_a0[l, 0], rw_a0[l, 1],
                                     jnp.zeros((w,), F32)]).reshape(1, 5 * w)
        qg = jnp.tile(q_norm[l], ATTN_HEADS).reshape(1, ATTN_WIDTH)
        kg = jnp.tile(k_norm[l], 2).reshape(1, ATTN_KV_WIDTH)
        wo_attn = w_o[l][POOL_WIDTH + w:][qperm]
        wo = jnp.concatenate([w_o[l][:POOL_WIDTH + w], wo_attn], axis=0).astype(BF16)
        wr = jnp.concatenate([router_group[l], router_expert[l],
                              jnp.zeros((d, ROUTE_LANES - N_GROUPS - N_EXPERTS), F32)], axis=1)
        wrh = wr.astype(BF16)
        wrl = (wr - wrh.astype(F32)).astype(BF16)
        br = jnp.concatenate([router_group_b[l], router_expert_b[l],
                              jnp.zeros((ROUTE_LANES - N_GROUPS - N_EXPERTS,), F32)]).reshape(1, ROUTE_LANES)

        for si in range(streams):
            pending = pendings[si]
            if pending is None:
                pool_in, rw_in, qh, kh, vh = _proj(x2d, w_proj, cos8, sin8, qg, kg, ones_q, s, tm,
                                                   rows=(si * (ns // tm), ns))
                x_res = (x2d, si * (ns // tm))
            else:
                xc, pool_in, rw_in, qh, kh, vh = _proj(None, w_proj, cos8, sin8, qg, kg, ones_q, s, tm, prev=pending)
                x_res = (xc, 0)
            y_pool = _pool(pool_in.reshape(bs, s, POOL_WIDTH), pool_bd, pool_scale[l].reshape(1, POOL_WIDTH))
            shared, dirp, post = _rwkv_prep(rw_in.reshape(bs, s, RWKV_PAD), mup, mun, w_lora, lora_bias,
                                            rw_k_k[l].reshape(1, w), rw_k_a[l].reshape(1, w),
                                            rw_r_k[l].reshape(1, w), ones_r, tm)
            wkv_f, wkv_r = _wkv(shared, dirp, min(256, s))
            y_attn = _attn(qh, kh, vh, s, min(512, s))
            x1, x1p, route = _outproj(x_res, y_pool.reshape(ns, POOL_WIDTH), wkv_f.reshape(ns, w),
                                      wkv_r.reshape(ns, w), post.reshape(ns, 2 * w), y_attn, wo,
                                      rw_gn_g[l].reshape(1, w), rw_gn_b[l].reshape(1, w), ones_r,
                                      ln1_g[l].reshape(1, d), ln1_b[l].reshape(1, d), wrh, wrl, br, tm)

            row_tok, pos, block_exp, n_used = _dispatch(route, ns)
            xs = _gather_rows(x1p, row_tok)
            ys = _experts(block_exp, n_used, xs, exp_gate, exp_up, exp_down, l)
            yab = _gather_rows(ys, pos)
            pendings[si] = (x1, yab, route, ln2_g[l].reshape(1, d), ln2_b[l].reshape(1, d))
    outs = [_combine(*p, tm).reshape(bs, s, d) for p in pendings]
    return outs[0] if streams == 1 else jnp.concatenate(outs, axis=0)
```

```python
import functools
import math

import jax
import jax.numpy as jnp
from jax import lax
from jax.experimental import pallas as pl
from jax.experimental.pallas import tpu as pltpu
from jax.experimental.pallas import tpu_sc as plsc

F32 = jnp.float32
BF16 = jnp.bfloat16

D_MODEL = 1024
DEPTH = 4
GRID_W = 64
HEAD_DIM = 64
POOL_WIDTH = 256
POOL_WINDOWS = (2, 4, 8, 16)
POOL_GROUP = 64
RWKV_WIDTH = 256
RWKV_HEADS = 4
DECAY_LORA = 32
AAA_LORA = 32
GATE_LORA = 64
GN_EPS = 64e-5
RWKV_IN = 960
RWKV_PAD = 1024
ATTN_WIDTH = 512
ATTN_HEADS = 8
ATTN_KV_WIDTH = 128
ATTN_IN = ATTN_WIDTH + 2 * ATTN_KV_WIDTH
ROPE_THETA = 10000.0
QK_EPS = 1e-6
N_GROUPS = 4
EXPERTS_PER_GROUP = 8
N_EXPERTS = 32
EXPERT_HIDDEN = 512
MOE_BLOCK = 512
DEEPNORM_ALPHA = float((2 * DEPTH) ** 0.25)
LN_EPS = 1e-5
PROJ_WIDTH = POOL_WIDTH + RWKV_PAD + ATTN_IN
ROUTE_LANES = 128
WKV_CHUNK = 64
VMEM_LIMIT = 48 * 1024 * 1024


def _cparams(sem):
    return pltpu.CompilerParams(dimension_semantics=sem, vmem_limit_bytes=VMEM_LIMIT)


def _dot(a, b):
    return jnp.dot(a, b, preferred_element_type=F32)


def _dot_nt(a, b):
    return lax.dot_general(a, b, (((1,), (1,)), ((), ())), preferred_element_type=F32)


def _dot_tn(a, b):
    return lax.dot_general(a, b, (((0,), (0,)), ((), ())), preferred_element_type=F32)


def _split2(x):
    hi = x.astype(BF16)
    lo = (x - hi.astype(F32)).astype(BF16)
    return hi, lo


def _split3(x):
    hi = x.astype(BF16)
    r1 = x - hi.astype(F32)
    mid = r1.astype(BF16)
    lo = (r1 - mid.astype(F32)).astype(BF16)
    return hi, mid, lo


HALF_D = D_MODEL // 2
U32 = jnp.uint32


def _pack_rows(x):
    hi = lax.bitcast_convert_type(x[:, :HALF_D].astype(BF16).astype(F32), U32)
    lo = lax.bitcast_convert_type(x[:, HALF_D:].astype(BF16).astype(F32), U32)
    return hi | (lo >> 16)


def _unpack_rows(u):
    a = lax.bitcast_convert_type(u & jnp.uint32(0xFFFF0000), F32).astype(BF16)
    b = lax.bitcast_convert_type(u << 16, F32).astype(BF16)
    return a, b


def _headsum(x, ones_bf16):
    hi, lo = _split2(x)
    return _dot(hi, ones_bf16) + _dot(lo, ones_bf16)


def _rope(x, cos, sin_signed):
    n = x.shape[-1]
    lane = lax.broadcasted_iota(jnp.int32, x.shape, 1)
    first = (lane // 16) % 2 == 0
    partner = jnp.where(first, pltpu.roll(x, n - 16, 1), pltpu.roll(x, 16, 1))
    return x * cos + partner * sin_signed


def _layer_norm(z, g, b):
    mu = jnp.mean(z, axis=-1, keepdims=True)
    zc = z - mu
    var = jnp.mean(zc * zc, axis=-1, keepdims=True)
    return zc * lax.rsqrt(var + LN_EPS) * g + b


def _combine_rows(x1, ya, yb, route, g, b):
    g1 = route[:, 2:3]
    g2 = route[:, 3:4]
    a_hi, a_lo = _unpack_rows(ya)
    b_hi, b_lo = _unpack_rows(yb)
    m = jnp.concatenate([g1 * a_hi.astype(F32) + g2 * b_hi.astype(F32),
                         g1 * a_lo.astype(F32) + g2 * b_lo.astype(F32)], axis=1)
    return _layer_norm(DEEPNORM_ALPHA * x1 + m, g, b)


def _proj_kernel(x_ref, w_ref, cos_ref, sin_ref, qg_ref, kg_ref, onesq_ref,
                 pool_ref, rw_ref, q_ref, k_ref, v_ref):
    _proj_rows(x_ref[...], w_ref, cos_ref, sin_ref, qg_ref, kg_ref, onesq_ref,
               pool_ref, rw_ref, q_ref, k_ref, v_ref)


def _combine_proj_kernel(x1_ref, ya_ref, yb_ref, route_ref, lg_ref, lb_ref,
                         w_ref, cos_ref, sin_ref, qg_ref, kg_ref, onesq_ref,
                         x_ref, pool_ref, rw_ref, q_ref, k_ref, v_ref):
    x = _combine_rows(x1_ref[...], ya_ref[...], yb_ref[...], route_ref[...], lg_ref[...], lb_ref[...])
    x_ref[...] = x
    _proj_rows(x, w_ref, cos_ref, sin_ref, qg_ref, kg_ref, onesq_ref, pool_ref, rw_ref, q_ref, k_ref, v_ref)


def _proj_rows(x, w_ref, cos_ref, sin_ref, qg_ref, kg_ref, onesq_ref, pool_ref, rw_ref, q_ref, k_ref, v_ref):
    xb = x.astype(BF16)
    pool_ref[...] = _dot(xb, w_ref[:, 0:POOL_WIDTH])
    rw_ref[...] = _dot(xb, w_ref[:, POOL_WIDTH:POOL_WIDTH + RWKV_PAD])
    at = _dot(xb, w_ref[:, POOL_WIDTH + RWKV_PAD:PROJ_WIDTH])

    q = at[:, 0:ATTN_WIDTH]
    k = at[:, ATTN_WIDTH:ATTN_WIDTH + ATTN_KV_WIDTH]
    v = at[:, ATTN_WIDTH + ATTN_KV_WIDTH:ATTN_IN]
    ones_q = onesq_ref[...]
    ones_k = onesq_ref[0:ATTN_KV_WIDTH, 0:ATTN_KV_WIDTH]
    inv = 1.0 / HEAD_DIM
    qn = q * lax.rsqrt(_headsum(q * q, ones_q) * inv + QK_EPS) * qg_ref[...]
    kn = k * lax.rsqrt(_headsum(k * k, ones_k) * inv + QK_EPS) * kg_ref[...]
    cos = cos_ref[...]
    sin = sin_ref[...]
    q_ref[...] = (_rope(qn, cos, sin) * (HEAD_DIM ** -0.5 * math.log2(math.e))).astype(BF16)
    k_ref[...] = _rope(kn, cos[:, 0:ATTN_KV_WIDTH], sin[:, 0:ATTN_KV_WIDTH]).astype(BF16)
    v_ref[...] = v.astype(BF16)


def _proj(x2d, w, cos8, sin8, qg, kg, ones_q, seq, tm, prev=None):
    n = (x2d if prev is None else prev[0]).shape[0]
    nts = seq // tm
    nt = n // tm
    row = lambda width: pl.BlockSpec((tm, width), lambda i: (i, 0))
    const = lambda r, c: pl.BlockSpec((r, c), lambda i: (0, 0))
    proj_in = [const(D_MODEL, PROJ_WIDTH),
               pl.BlockSpec((tm, ATTN_WIDTH), lambda i: (i % nts, 0)),
               pl.BlockSpec((tm, ATTN_WIDTH), lambda i: (i % nts, 0)),
               const(1, ATTN_WIDTH), const(1, ATTN_KV_WIDTH), const(ATTN_WIDTH, ATTN_WIDTH)]
    proj_out = [row(POOL_WIDTH), row(RWKV_PAD), row(ATTN_WIDTH), row(ATTN_KV_WIDTH), row(ATTN_KV_WIDTH)]
    proj_shape = [jax.ShapeDtypeStruct((n, POOL_WIDTH), F32),
                  jax.ShapeDtypeStruct((n, RWKV_PAD), F32),
                  jax.ShapeDtypeStruct((n, ATTN_WIDTH), BF16),
                  jax.ShapeDtypeStruct((n, ATTN_KV_WIDTH), BF16),
                  jax.ShapeDtypeStruct((n, ATTN_KV_WIDTH), BF16)]
    if prev is None:
        return pl.pallas_call(
            _proj_kernel,
            grid=(nt,),
            in_specs=[row(D_MODEL)] + proj_in,
            out_specs=proj_out,
            out_shape=proj_shape,
            compiler_params=_cparams(("parallel",)),
            name="proj",
        )(x2d, w, cos8, sin8, qg, kg, ones_q)
    x1, yab, route, lg, lb = prev
    return pl.pallas_call(
        _combine_proj_kernel,
        grid=(nt,),
        in_specs=[row(D_MODEL), row(HALF_D), pl.BlockSpec((tm, HALF_D), lambda i: (i + nt, 0)),
                  row(ROUTE_LANES), const(1, D_MODEL), const(1, D_MODEL)] + proj_in,
        out_specs=[row(D_MODEL)] + proj_out,
        out_shape=[jax.ShapeDtypeStruct((n, D_MODEL), F32)] + proj_shape,
        compiler_params=_cparams(("parallel",)),
        name="combine_proj",
    )(x1, yab, yab, route, lg, lb, w, cos8, sin8, qg, kg, ones_q)


def _pool_kernel(u_ref, w_ref, scale_ref, o_ref):
    s = u_ref.shape[1]
    u = u_ref[0]
    t = lax.broadcasted_iota(jnp.int32, (s, POOL_WIDTH), 0)

    def down(x, k):
        return jnp.where(t >= k, pltpu.roll(x, k, 0), 0.0)

    def up(x, k):
        return jnp.where(t < s - k, pltpu.roll(x, s - k, 0), 0.0)

    left = [down(u, 1)]
    right = [u]
    for k in (1, 2, 4):
        left.append(left[-1] + down(left[-1], k))
        right.append(right[-1] + up(right[-1], k))
    sums = [l + r for l, r in zip(left, right)]

    grp = lax.broadcasted_iota(jnp.int32, (s, POOL_WIDTH), 1) // POOL_GROUP
    half = jnp.where(grp == 0, 1, jnp.where(grp == 1, 2, jnp.where(grp == 2, 4, 8)))
    cnt = (jnp.minimum(t + half, s) - jnp.maximum(t - half, 0)).astype(F32)
    tot = jnp.where(grp == 0, sums[0], jnp.where(grp == 1, sums[1], jnp.where(grp == 2, sums[2], sums[3])))
    d = tot / cnt - u
    o_ref[0] = _dot(d.astype(BF16), w_ref[...]) * scale_ref[...]


def _pool(u3d, w_bd, scale):
    b, s, _ = u3d.shape
    return pl.pallas_call(
        _pool_kernel,
        grid=(b,),
        in_specs=[pl.BlockSpec((1, s, POOL_WIDTH), lambda i: (i, 0, 0)),
                  pl.BlockSpec((POOL_WIDTH, POOL_WIDTH), lambda i: (0, 0)),
                  pl.BlockSpec((1, POOL_WIDTH), lambda i: (0, 0))],
        out_specs=pl.BlockSpec((1, s, POOL_WIDTH), lambda i: (i, 0, 0)),
        out_shape=jax.ShapeDtypeStruct((b, s, POOL_WIDTH), F32),
        compiler_params=_cparams(("parallel",)),
        name="pool",
    )(u3d, w_bd, scale)


_SHIFT_HALO = 8


def _rwkv_prep_kernel(cur_ref, prev_ref, next_ref, mup_ref, mun_ref, wl_ref, bias_ref,
                      kk_ref, ka_ref, rk_ref, ones_ref,
                      sh_ref, dir_ref, post_ref):
    ts = cur_ref.shape[1]
    ti = pl.program_id(1)
    nt = pl.num_programs(1)
    cur = cur_ref[0]
    row = lax.broadcasted_iota(jnp.int32, cur.shape, 0)
    before = jnp.where(ti > 0, prev_ref[0, _SHIFT_HALO - 1:_SHIFT_HALO, :], 0.0)
    after = jnp.where(ti < nt - 1, next_ref[0, 0:1, :], 0.0)
    prev = jnp.where(row == 0, before, pltpu.roll(cur, 1, 0))
    nxt = jnp.where(row == ts - 1, after, pltpu.roll(cur, ts - 1, 0))
    f = cur + mup_ref[...] * (prev - cur) + mun_ref[...] * (nxt - cur)

    w = RWKV_WIDTH
    r = f[:, 0:w]
    k = f[:, w:2 * w]
    v = f[:, 2 * w:3 * w]
    lora = f[:, 3 * w:4 * w]
    lane = lax.broadcasted_iota(jnp.int32, lora.shape, 1)
    z = jnp.where(lane < 2 * DECAY_LORA, jnp.tanh(lora),
                  jnp.where(lane < 2 * DECAY_LORA + 2 * AAA_LORA, lora,
                            jnp.where(lane < 2 * DECAY_LORA + 2 * AAA_LORA + GATE_LORA,
                                      jax.nn.sigmoid(lora), 0.0)))
    up = _dot(z.astype(BF16), wl_ref[...]) + bias_ref[...]
    ones = ones_ref[...]

    kk0 = k * kk_ref[...]
    nrm = jnp.sqrt(_headsum(kk0 * kk0, ones))
    kk = kk0 / jnp.maximum(nrm, 1e-12)
    sh_ref[0, :, 0:w] = r
    sh_ref[0, :, w:2 * w] = v
    sh_ref[0, :, 2 * w:3 * w] = kk

    ksum = jnp.zeros_like(k)
    for di in range(2):
        logw = -math.exp(-0.5) * jax.nn.sigmoid(up[:, di * w:(di + 1) * w])
        a = jax.nn.sigmoid(up[:, (2 + di) * w:(3 + di) * w])
        kh = k * (1.0 + (a - 1.0) * ka_ref[...])
        ksum = ksum + kh
        dir_ref[di, 0, :, 0:w] = logw
        dir_ref[di, 0, :, w:2 * w] = kh
        dir_ref[di, 0, :, 2 * w:3 * w] = -(a * kk)
    bonus = _headsum(r * ksum * rk_ref[...], ones) * v
    post_ref[0, :, 0:w] = bonus
    post_ref[0, :, w:2 * w] = up[:, 4 * w:5 * w]


def _rwkv_prep(rw3d, mup, mun, wl, bias, k_k, k_a, r_k, ones, ts):
    b, s, _ = rw3d.shape
    nt = s // ts
    hb = ts // _SHIFT_HALO
    nhb = s // _SHIFT_HALO
    w = RWKV_WIDTH
    vec = lambda width: pl.BlockSpec((1, width), lambda i, j: (0, 0))
    return pl.pallas_call(
        _rwkv_prep_kernel,
        grid=(b, nt),
        in_specs=[pl.BlockSpec((1, ts, RWKV_PAD), lambda i, j: (i, j, 0)),
                  pl.BlockSpec((1, _SHIFT_HALO, RWKV_PAD),
                               lambda i, j: (i, jnp.maximum(j * hb - 1, 0), 0)),
                  pl.BlockSpec((1, _SHIFT_HALO, RWKV_PAD),
                               lambda i, j: (i, jnp.minimum((j + 1) * hb, nhb - 1), 0)),
                  vec(RWKV_PAD), vec(RWKV_PAD),
                  pl.BlockSpec((w, 5 * w), lambda i, j: (0, 0)),
                  vec(5 * w), vec(w), vec(w), vec(w),
                  pl.BlockSpec((w, w), lambda i, j: (0, 0))],
        out_specs=[pl.BlockSpec((1, ts, 3 * w), lambda i, j: (i, j, 0)),
                   pl.BlockSpec((2, 1, ts, 3 * w), lambda i, j: (0, i, j, 0)),
                   pl.BlockSpec((1, ts, 2 * w), lambda i, j: (i, j, 0))],
        out_shape=[jax.ShapeDtypeStruct((b, s, 3 * w), F32),
                   jax.ShapeDtypeStruct((2, b, s, 3 * w), F32),
                   jax.ShapeDtypeStruct((b, s, 2 * w), F32)],
        compiler_params=_cparams(("parallel", "parallel")),
        name="rwkv_prep",
    )(rw3d, rw3d, rw3d, mup, mun, wl, bias, k_k, k_a, r_k, ones)


def _expand_bd(x_bf16, mask_bd):
    return jnp.where(mask_bd, jnp.concatenate([x_bf16] * RWKV_HEADS, axis=0), jnp.zeros((), BF16))


def _wkv_kernel(shf_ref, shr_ref, df_ref, dr_ref, yf_ref, yr_ref, hf_ref, hr_ref):
    c = WKV_CHUNK
    w = RWKV_WIDTH
    tt = shf_ref.shape[1]
    ncs = tt // c

    @pl.when(pl.program_id(1) == 0)
    def _():
        hf_ref[...] = jnp.zeros_like(hf_ref)
        hr_ref[...] = jnp.zeros_like(hr_ref)

    row_c = lax.broadcasted_iota(jnp.int32, (c, c), 0)
    col_c = lax.broadcasted_iota(jnp.int32, (c, c), 1)
    t_i = lax.broadcasted_iota(jnp.int32, (c, w), 0)
    s_i = lax.broadcasted_iota(jnp.int32, (c, w), 1) % c
    eye_c = (s_i == t_i).astype(F32)
    row_w = lax.broadcasted_iota(jnp.int32, (w, w), 0)
    col_w = lax.broadcasted_iota(jnp.int32, (w, w), 1)
    mask_bd = (row_w // c) == (col_w // HEAD_DIM)
    eye_w = row_w == col_w
    tri_incl = [jnp.where(col_c <= row_c, 1.0, 0.0).astype(BF16),
                jnp.where(col_c >= row_c, 1.0, 0.0).astype(BF16)]
    strict = [s_i < t_i, s_i > t_i]
    incl = [s_i <= t_i, s_i >= t_i]

    def bf(x):
        return x.astype(BF16)

    def bd(x):
        return _expand_bd(bf(x), mask_bd)

    chunks = []
    for ci in range(ncs):
        chunks.append((0, ci * c, shf_ref, df_ref))
        chunks.append((1, (ncs - 1 - ci) * c, shr_ref, dr_ref))

    st = []
    for d, start, sh_ref, dir_ref in chunks:
        rows = pl.ds(start, c)
        st.append(dict(d=d, rows=rows,
                       r=sh_ref[0, rows, 0:w], v=sh_ref[0, rows, w:2 * w], kk=sh_ref[0, rows, 2 * w:3 * w],
                       lw=dir_ref[0, rows, 0:w], kh=dir_ref[0, rows, w:2 * w], nb=dir_ref[0, rows, 2 * w:3 * w]))

    for q in st:
        l_hi, l_mid, l_lo = _split3(q["lw"])
        tri = tri_incl[q["d"]]
        q["cum"] = _dot(tri, l_hi) + _dot(tri, l_mid) + _dot(tri, l_lo)
    for q in st:
        cum, lw = q["cum"], q["lw"]
        tot = jnp.sum(lw, axis=0, keepdims=True)
        e_inv = jnp.exp(-cum)
        e_end = jnp.exp(tot - cum)
        q["tot"] = tot
        q["a_bar"] = q["kk"] * jnp.exp(cum - lw)
        q["r_bar"] = q["r"] * jnp.exp(cum)
        q["b_hat"] = bf(q["nb"] * e_end)
        q["k_hat"] = bf(q["kh"] * e_end)
        q["v_bf"] = bf(q["v"])
        q["v_bd"] = _expand_bd(q["v_bf"], mask_bd)
        q["lhs"] = jnp.concatenate([bf(q["a_bar"]), bf(q["r_bar"])], axis=0)
        q["rhs"] = jnp.concatenate([bd(q["nb"] * e_inv), bd(q["kh"] * e_inv)], axis=0)
    for q in st:
        sc = _dot_nt(q["lhs"], q["rhs"])
        sm, im = strict[q["d"]], incl[q["d"]]
        q["x"] = jnp.where(sm, sc[0:c, 0:w], 0.0)
        q["a_ak"] = bf(jnp.where(sm, sc[0:c, w:2 * w], 0.0))
        q["m_rb"] = bf(jnp.where(im, sc[c:2 * c, 0:w], 0.0))
        q["m_rk"] = bf(jnp.where(im, sc[c:2 * c, w:2 * w], 0.0))
        q["tinv"] = eye_c + q["x"]

    for q in st:
        q["x"] = _dot(bf(q["x"]), bd(q["x"]))
    n_lv = int(math.log2(c)) - 1
    for lv in range(n_lv):
        last = lv == n_lv - 1
        for q in st:
            xb = bd(q["x"])
            if last:
                q["tinv"] = q["tinv"] + _dot(bf(q["tinv"]), xb)
            else:
                res = _dot(jnp.concatenate([bf(q["x"]), bf(q["tinv"])], axis=0), xb)
                q["tinv"] = q["tinv"] + res[c:2 * c]
                q["x"] = res[0:c]
    for q in st:
        res = _dot(jnp.concatenate([q["a_ak"], q["m_rk"]], axis=0), q["v_bd"])
        q["akv"] = res[0:c]
        q["mrkv"] = res[c:2 * c]
    for q in st:
        tinv_bf = bf(q["tinv"])
        q["a_pr"] = bf(_dot(tinv_bf, bd(q["a_bar"])))
        q["v_pr"] = bf(_dot(tinv_bf, bd(q["akv"])))
    for q in st:
        q["r_pr"] = bf(q["r_bar"] + _dot(q["m_rb"], _expand_bd(q["a_pr"], mask_bd)))
        q["y_pr"] = _dot(q["m_rb"], _expand_bd(q["v_pr"], mask_bd)) + q["mrkv"]
    for q in st:
        q["p_bd"] = bf(jnp.where(eye_w, jnp.exp(q["tot"]), 0.0)
                       + jnp.where(mask_bd, _dot_tn(q["b_hat"], q["a_pr"]), 0.0))
        q["q_bd"] = jnp.where(mask_bd,
                              _dot_tn(jnp.concatenate([q["b_hat"], q["k_hat"]], axis=0),
                                      jnp.concatenate([q["v_pr"], q["v_bf"]], axis=0)), 0.0)

    h = [hf_ref[...], hr_ref[...]]
    y_refs = [yf_ref, yr_ref]
    for q in st:
        d = q["d"]
        res = _dot(jnp.concatenate([q["r_pr"], q["p_bd"]], axis=0), bf(h[d]))
        y_refs[d][0, q["rows"], :] = res[0:c] + q["y_pr"]
        h[d] = res[c:c + w] + q["q_bd"]
    hf_ref[...] = h[0]
    hr_ref[...] = h[1]


def _wkv(shared, dirp, tt):
    b, s, _ = shared.shape
    nt = s // tt
    w = RWKV_WIDTH
    fwd = lambda i, j: (i, j, 0)
    bwd = lambda i, j: (i, nt - 1 - j, 0)
    return pl.pallas_call(
        _wkv_kernel,
        grid=(b, nt),
        in_specs=[pl.BlockSpec((1, tt, 3 * w), fwd), pl.BlockSpec((1, tt, 3 * w), bwd),
                  pl.BlockSpec((None, 1, tt, 3 * w), lambda i, j: (0, i, j, 0)),
                  pl.BlockSpec((None, 1, tt, 3 * w), lambda i, j: (1, i, nt - 1 - j, 0))],
        out_specs=[pl.BlockSpec((1, tt, w), fwd), pl.BlockSpec((1, tt, w), bwd)],
        out_shape=[jax.ShapeDtypeStruct((b, s, w), F32), jax.ShapeDtypeStruct((b, s, w), F32)],
        scratch_shapes=[pltpu.VMEM((w, w), F32), pltpu.VMEM((w, w), F32)],
        compiler_params=_cparams(("parallel", "arbitrary")),
        name="wkv",
    )(shared, shared, dirp, dirp)


def _attn_kernel(q_ref, k_ref, v_ref, o_ref):
    tq = q_ref.shape[0]
    k = k_ref[...]
    v = v_ref[...]
    lane = lax.broadcasted_iota(jnp.int32, (tq, ATTN_KV_WIDTH), 1)
    low = lane < HEAD_DIM
    zero = jnp.zeros((), BF16)
    for j in range(ATTN_HEADS // 2):
        qb = q_ref[:, j * ATTN_KV_WIDTH:(j + 1) * ATTN_KV_WIDTH]
        outs = []
        for half in range(2):
            qm = jnp.where(low if half == 0 else ~low, qb, zero)
            s = _dot_nt(qm, k)
            m = jnp.max(s, axis=-1, keepdims=True)
            p = jnp.exp2(s - m)
            l = jnp.sum(p, axis=-1, keepdims=True)
            outs.append(_dot(p.astype(BF16), v) / l)
        o_ref[:, j * ATTN_KV_WIDTH:(j + 1) * ATTN_KV_WIDTH] = jnp.where(low, outs[0], outs[1]).astype(BF16)


def _attn(q, k, v, seq, tq):
    n = q.shape[0]
    nq = seq // tq
    return pl.pallas_call(
        _attn_kernel,
        grid=(n // seq, nq),
        in_specs=[pl.BlockSpec((tq, ATTN_WIDTH), lambda b, i: (b * nq + i, 0)),
                  pl.BlockSpec((seq, ATTN_KV_WIDTH), lambda b, i: (b, 0)),
                  pl.BlockSpec((seq, ATTN_KV_WIDTH), lambda b, i: (b, 0))],
        out_specs=pl.BlockSpec((tq, ATTN_WIDTH), lambda b, i: (b * nq + i, 0)),
        out_shape=jax.ShapeDtypeStruct((n, ATTN_WIDTH), BF16),
        compiler_params=_cparams(("parallel", "parallel")),
        name="attn",
    )(q, k, v)


def _outproj_kernel(x_ref, pool_ref, wkvf_ref, wkvr_ref, post_ref, attn_ref, wo_ref, gng_ref, gnb_ref,
                    ones_ref, lg_ref, lb_ref, wrh_ref, wrl_ref, br_ref,
                    x1_ref, x1p_ref, route_ref):
    w = RWKV_WIDTH
    ones = ones_ref[...]
    wkv = wkvf_ref[...] + wkvr_ref[...]
    inv = 1.0 / HEAD_DIM
    mu = _headsum(wkv, ones) * inv
    cen = wkv - mu
    var = _headsum(cen * cen, ones) * inv
    yr = cen * lax.rsqrt(var + GN_EPS) * gng_ref[...] + gnb_ref[...] + post_ref[:, 0:w]
    yr = yr * post_ref[:, w:2 * w]
    y = (_dot(pool_ref[...].astype(BF16), wo_ref[0:POOL_WIDTH, :])
         + _dot(yr.astype(BF16), wo_ref[POOL_WIDTH:POOL_WIDTH + w, :])
         + _dot(attn_ref[...], wo_ref[POOL_WIDTH + w:D_MODEL, :]))
    x1 = _layer_norm(DEEPNORM_ALPHA * x_ref[...] + y, lg_ref[...], lb_ref[...])
    x1_ref[...] = x1
    x1p_ref[...] = _pack_rows(x1)

    xh, xl = _split2(x1)
    logits = (_dot(xh, wrh_ref[...]) + _dot(xl, wrh_ref[...]) + _dot(xh, wrl_ref[...])) + br_ref[...]
    lane = lax.broadcasted_iota(jnp.int32, logits.shape, 1)
    lane_f = lane.astype(F32)
    lane_grp_f = ((lane - N_GROUPS) // EXPERTS_PER_GROUP).astype(F32)
    neg = -jnp.inf
    big = jnp.float32(1 << 20)
    gl = jnp.where(lane < N_GROUPS, logits, neg)
    gmax = jnp.max(gl, axis=-1, keepdims=True)
    grp = jnp.min(jnp.where(gl == gmax, lane_f, big), axis=-1, keepdims=True)
    gw = 1.0 / jnp.sum(jnp.exp(gl - gmax), axis=-1, keepdims=True)
    in_grp = (lane >= N_GROUPS) & (lane < N_GROUPS + N_EXPERTS) & (lane_grp_f == grp)
    el = jnp.where(in_grp, logits, neg)
    v1 = jnp.max(el, axis=-1, keepdims=True)
    i1 = jnp.min(jnp.where(el == v1, lane_f, big), axis=-1, keepdims=True)
    el2 = jnp.where(lane_f == i1, neg, el)
    v2 = jnp.max(el2, axis=-1, keepdims=True)
    i2 = jnp.min(jnp.where(el2 == v2, lane_f, big), axis=-1, keepdims=True)
    e21 = jnp.exp(v2 - v1)
    g1 = gw / (1.0 + e21)
    g2 = gw * e21 / (1.0 + e21)
    route = jnp.where(lane == 0, i1 - N_GROUPS,
                      jnp.where(lane == 1, i2 - N_GROUPS,
                                jnp.where(lane == 2, g1, jnp.where(lane == 3, g2, 0.0))))
    route_ref[...] = route


def _outproj(x2d, pool2d, wkvf, wkvr, post2d, attn2d, wo, gng, gnb, ones, lg, lb, wrh, wrl, br, tm):
    n = x2d.shape[0]
    w = RWKV_WIDTH
    row = lambda width: pl.BlockSpec((tm, width), lambda i: (i, 0))
    vec = lambda width: pl.BlockSpec((1, width), lambda i: (0, 0))
    return pl.pallas_call(
        _outproj_kernel,
        grid=(n // tm,),
        in_specs=[row(D_MODEL), row(POOL_WIDTH), row(w), row(w),
                  row(2 * w), row(ATTN_WIDTH),
                  pl.BlockSpec((D_MODEL, D_MODEL), lambda i: (0, 0)),
                  vec(w), vec(w),
                  pl.BlockSpec((w, w), lambda i: (0, 0)),
                  vec(D_MODEL), vec(D_MODEL),
                  pl.BlockSpec((D_MODEL, ROUTE_LANES), lambda i: (0, 0)),
                  pl.BlockSpec((D_MODEL, ROUTE_LANES), lambda i: (0, 0)),
                  vec(ROUTE_LANES)],
        out_specs=[row(D_MODEL), row(HALF_D), row(ROUTE_LANES)],
        out_shape=[jax.ShapeDtypeStruct((n, D_MODEL), F32),
                   jax.ShapeDtypeStruct((n, HALF_D), U32),
                   jax.ShapeDtypeStruct((n, ROUTE_LANES), F32)],
        compiler_params=_cparams(("parallel",)),
        name="outproj",
    )(x2d, pool2d, wkvf, wkvr, post2d, attn2d, wo, gng, gnb, ones, lg, lb, wrh, wrl, br)


def _expert_kernel(bexp_ref, nused_ref, xs_ref, wg_ref, wu_ref, wd_ref, *rest):
    ys_ref, wgb_ref, wub_ref, wdb_ref = rest[-4:]
    i = pl.program_id(0)
    used = i < nused_ref[0]

    @pl.when(used & ((i == 0) | (bexp_ref[i] != bexp_ref[jnp.maximum(i - 1, 0)])))
    def _():
        wgb_ref[...] = wg_ref[0].astype(BF16)
        wub_ref[...] = wu_ref[0].astype(BF16)
        wdb_ref[...] = wd_ref[0].astype(BF16)

    @pl.when(used)
    def _():
        xa, xb = _unpack_rows(xs_ref[...])
        h1 = _dot(xa, wgb_ref[0:HALF_D, :]) + _dot(xb, wgb_ref[HALF_D:D_MODEL, :])
        h2 = _dot(xa, wub_ref[0:HALF_D, :]) + _dot(xb, wub_ref[HALF_D:D_MODEL, :])
        h = (h1 * jax.nn.sigmoid(h1) * h2).astype(BF16)
        ys_ref[...] = _pack_rows(_dot(h, wdb_ref[...]))

    @pl.when(jnp.logical_not(used))
    def _():
        ys_ref[...] = jnp.zeros_like(ys_ref)


def _experts(block_exp, n_used, xs_parts, wg, wu, wd, layer):
    p = sum(xs.shape[0] for xs in xs_parts)
    wmap = lambda i, be, nu: (layer, be[i], 0, 0)
    ys = None
    first = 0
    for xs in xs_parts:
        nb = xs.shape[0] // MOE_BLOCK
        in_specs = [pl.BlockSpec((MOE_BLOCK, HALF_D), lambda i, be, nu: (i, 0)),
                    pl.BlockSpec((None, 1, D_MODEL, EXPERT_HIDDEN), wmap),
                    pl.BlockSpec((None, 1, D_MODEL, EXPERT_HIDDEN), wmap),
                    pl.BlockSpec((None, 1, EXPERT_HIDDEN, D_MODEL), wmap)]
        args = [block_exp[first:first + nb], n_used - first, xs, wg, wu, wd]
        aliases = {}
        if ys is not None:
            in_specs.append(pl.BlockSpec(memory_space=pl.ANY))
            aliases = {len(args): 0}
            args.append(ys)
        grid_spec = pltpu.PrefetchScalarGridSpec(
            num_scalar_prefetch=2,
            grid=(nb,),
            in_specs=in_specs,
            out_specs=pl.BlockSpec((MOE_BLOCK, HALF_D), functools.partial(_block_at, first)),
            scratch_shapes=[pltpu.VMEM((D_MODEL, EXPERT_HIDDEN), BF16),
                            pltpu.VMEM((D_MODEL, EXPERT_HIDDEN), BF16),
                            pltpu.VMEM((EXPERT_HIDDEN, D_MODEL), BF16)],
        )
        ys = pl.pallas_call(
            _expert_kernel,
            grid_spec=grid_spec,
            out_shape=jax.ShapeDtypeStruct((p, HALF_D), U32),
            input_output_aliases=aliases,
            compiler_params=_cparams(("arbitrary",)),
            name="experts",
        )(*args)
        first += nb
    return ys


def _block_at(first, i, be, nu):
    return (i + first, 0)


SC_CORES = 2
SC_SUBCORES = 16
SC_GATHER_ROWS = 64


def _sc_gather(table, idx):
    v, d = table.shape
    b = idx.shape[0]
    nw = SC_CORES * SC_SUBCORES
    ch = SC_GATHER_ROWS
    assert b % (nw * ch * 2) == 0
    b_per_w = b // nw
    nch = b_per_w // ch
    mesh = plsc.VectorSubcoreMesh(core_axis_name="c", subcore_axis_name="s")

    @functools.partial(
        pl.kernel, mesh=mesh,
        out_type=jax.ShapeDtypeStruct((b, d), table.dtype),
        scratch_types=[pltpu.VMEM((b_per_w,), jnp.int32),
                       pltpu.VMEM((2, ch, d), table.dtype),
                       pltpu.SemaphoreType.DMA, pltpu.SemaphoreType.DMA,
                       pltpu.SemaphoreType.DMA, pltpu.SemaphoreType.DMA],
    )
    def gather_kernel(table_hbm, idx_hbm, out_hbm, idx_v, rows_v, gsem0, gsem1, wsem0, wsem1):
        wid = lax.axis_index("s") * SC_CORES + lax.axis_index("c")
        base = wid * b_per_w
        pltpu.sync_copy(idx_hbm.at[pl.ds(base, b_per_w)], idx_v)
        gsem = (gsem0, gsem1)
        wsem = (wsem0, wsem1)

        def gather(jj, slot):
            off = pl.multiple_of(jj * ch, ch)
            return pltpu.make_async_copy(table_hbm.at[idx_v.at[pl.ds(off, ch)]], rows_v.at[slot], gsem[slot])

        def write(jj, slot):
            off = pl.multiple_of(jj * ch, ch)
            return pltpu.make_async_copy(rows_v.at[slot], out_hbm.at[pl.ds(base + off, ch)], wsem[slot])

        gather(0, 0).start()

        @pl.loop(0, nch, step=2)
        def _(j):
            for slot in range(2):
                jj = j + slot

                gather(jj, slot).wait()

                @pl.when(jj >= 1)
                def _():
                    write(jj - 1, 1 - slot).wait()

                @pl.when(jj + 1 < nch)
                def _():
                    gather(jj + 1, 1 - slot).start()

                write(jj, slot).start()

        write(nch - 1, (nch - 1) % 2).wait()

    return gather_kernel(table, idx)


def _gather_rows(table, idx):
    b = idx.shape[0]
    unit = SC_CORES * SC_SUBCORES * SC_GATHER_ROWS * 2
    bp = -(-b // unit) * unit
    if bp == b:
        return _sc_gather(table, idx)
    return _sc_gather(table, jnp.pad(idx, (0, bp - b)))[:b]


def _combine_kernel(x1_ref, ya_ref, yb_ref, route_ref, lg_ref, lb_ref, o_ref):
    o_ref[...] = _combine_rows(x1_ref[...], ya_ref[...], yb_ref[...], route_ref[...], lg_ref[...], lb_ref[...])


def _combine(x1, yab, route, lg, lb, tm):
    n = x1.shape[0]
    nt = n // tm
    row = lambda width: pl.BlockSpec((tm, width), lambda i: (i, 0))
    vec = lambda width: pl.BlockSpec((1, width), lambda i: (0, 0))
    return pl.pallas_call(
        _combine_kernel,
        grid=(nt,),
        in_specs=[row(D_MODEL), row(HALF_D), pl.BlockSpec((tm, HALF_D), lambda i: (i + nt, 0)),
                  row(ROUTE_LANES), vec(D_MODEL), vec(D_MODEL)],
        out_specs=row(D_MODEL),
        out_shape=jax.ShapeDtypeStruct((n, D_MODEL), F32),
        compiler_params=_cparams(("parallel",)),
        name="combine",
    )(x1, yab, yab, route, lg, lb)


def _q_perm():
    order = [h for j in range(ATTN_HEADS // 2) for h in (j, j + ATTN_HEADS // 2)]
    return jnp.concatenate([jnp.arange(h * HEAD_DIM, (h + 1) * HEAD_DIM) for h in order])


def _block_diag(blocks):
    n = len(blocks)
    r, c = blocks[0].shape
    out = jnp.zeros((n * r, n * c), blocks[0].dtype)
    for i, blk in enumerate(blocks):
        out = out.at[i * r:(i + 1) * r, i * c:(i + 1) * c].set(blk)
    return out


def _rope_tables(seq):
    rows = seq // GRID_W
    row_id = jnp.repeat(jnp.arange(rows), GRID_W).astype(F32)
    col_id = jnp.tile(jnp.arange(GRID_W), rows).astype(F32)
    half = HEAD_DIM // 2
    inv_freq = ROPE_THETA ** (-jnp.arange(0, half, 2, dtype=F32) / half)
    ang_r = row_id[:, None] * inv_freq
    ang_c = col_id[:, None] * inv_freq
    ang = jnp.concatenate([ang_r, ang_r, ang_c, ang_c], -1)
    sign = jnp.where((jnp.arange(HEAD_DIM) // 16) % 2 == 0, -1.0, 1.0).astype(F32)
    cos8 = jnp.tile(jnp.cos(ang), (1, ATTN_HEADS))
    sin8 = jnp.tile(jnp.sin(ang) * sign, (1, ATTN_HEADS))
    return cos8, sin8


def _dispatch(route, n):
    m = n * 2
    experts = jnp.arange(N_EXPERTS, dtype=jnp.int32)
    e_flat = jnp.concatenate([route[:, 0], route[:, 1]]).astype(jnp.int32)
    pair = jnp.arange(m, dtype=jnp.int32)
    e_sorted, order = lax.sort((e_flat, pair), num_keys=1, is_stable=True)
    start = jnp.sum(e_sorted[None, :] < experts[:, None], axis=1, dtype=jnp.int32)
    counts = jnp.concatenate([start[1:], jnp.full((1,), m, jnp.int32)]) - start
    padded = (counts + MOE_BLOCK - 1) // MOE_BLOCK * MOE_BLOCK
    ends_p = jnp.sum(jnp.where(experts[None, :] <= experts[:, None], padded[None, :], 0), axis=1, dtype=jnp.int32)
    pstart = ends_p - padded
    delta = pstart - start
    n_blocks = -(-(m + N_EXPERTS * (MOE_BLOCK - 1)) // MOE_BLOCK)
    block_start = jnp.arange(n_blocks, dtype=jnp.int32) * MOE_BLOCK
    block_exp = jnp.minimum(jnp.sum(ends_p[None, :] <= block_start[:, None], axis=1, dtype=jnp.int32),
                            N_EXPERTS - 1)
    lane = jnp.arange(MOE_BLOCK, dtype=jnp.int32)[None, :]
    blk = block_exp[:, None] == experts[None, :]

    def per_block(table):
        return jnp.sum(jnp.where(blk, table[None, :], 0), axis=1, dtype=jnp.int32)

    valid = (block_start - per_block(pstart))[:, None] + lane < per_block(counts)[:, None]
    src = jnp.clip((block_start - per_block(delta))[:, None] + lane, 0, m - 1)
    run = order[src.reshape(-1)].reshape(n_blocks, MOE_BLOCK)
    row_tok = jnp.where(valid, run % n, (block_start[:, None] + lane) % n).reshape(-1)
    n_used = (ends_p[-1] // MOE_BLOCK).astype(jnp.int32).reshape(1)
    dest = pair + jnp.sum(jnp.where(e_sorted[None, :] == experts[:, None], delta[:, None], 0), axis=0,
                          dtype=jnp.int32)
    _, pos = lax.sort((order, dest), num_keys=1)
    return row_tok, pos, block_exp, n_used


def kernel(x, w_in, mu_prev, mu_next, pool_w, pool_scale, rw_w0, rw_w_up, rw_a0, rw_a_up, rw_g_up, rw_k_k, rw_k_a, rw_r_k, rw_gn_g, rw_gn_b, q_norm, k_norm, w_o, ln1_g, ln1_b, router_group, router_group_b, router_expert, router_expert_b, exp_gate, exp_up, exp_down, ln2_g, ln2_b):
    b, s, d = x.shape
    n = b * s
    w = RWKV_WIDTH
    tm = min(512, s)
    qperm = _q_perm()
    cos8, sin8 = _rope_tables(s)
    ones_q = _block_diag([jnp.ones((HEAD_DIM, HEAD_DIM), BF16)] * ATTN_HEADS)
    ones_r = ones_q[0:w, 0:w]
    a_end = POOL_WIDTH
    b_end = POOL_WIDTH + RWKV_IN

    xc = x.reshape(n, d)
    pending = None
    for l in range(DEPTH):
        wq = w_in[l][:, b_end:b_end + ATTN_WIDTH][:, qperm]
        w_proj = jnp.concatenate(
            [w_in[l][:, :a_end], w_in[l][:, a_end:b_end],
             jnp.zeros((d, RWKV_PAD - RWKV_IN), F32), wq, w_in[l][:, b_end + ATTN_WIDTH:]],
            axis=1).astype(BF16)
        pool_bd = _block_diag([pool_w[l, g] for g in range(len(POOL_WINDOWS))]).astype(BF16)
        pad_vec = jnp.zeros((RWKV_PAD - RWKV_IN,), F32)
        mup = jnp.concatenate([mu_prev[l], pad_vec]).reshape(1, RWKV_PAD)
        mun = jnp.concatenate([mu_next[l], pad_vec]).reshape(1, RWKV_PAD)
        w_lora = jnp.zeros((w, 5 * w), F32)
        for di in range(2):
            w_lora = w_lora.at[di * DECAY_LORA:(di + 1) * DECAY_LORA, di * w:(di + 1) * w].set(rw_w_up[l, di])
            o = 2 * DECAY_LORA
            w_lora = w_lora.at[o + di * AAA_LORA:o + (di + 1) * AAA_LORA, (2 + di) * w:(3 + di) * w].set(rw_a_up[l, di])
        o = 2 * DECAY_LORA + 2 * AAA_LORA
        w_lora = w_lora.at[o:o + GATE_LORA, 4 * w:5 * w].set(rw_g_up[l]).astype(BF16)
        lora_bias = jnp.concatenate([rw_w0[l, 0], rw_w0[l, 1], rw_a0[l, 0], rw_a0[l, 1],
                                     jnp.zeros((w,), F32)]).reshape(1, 5 * w)
        qg = jnp.tile(q_norm[l], ATTN_HEADS).reshape(1, ATTN_WIDTH)
        kg = jnp.tile(k_norm[l], 2).reshape(1, ATTN_KV_WIDTH)
        wo_attn = w_o[l][POOL_WIDTH + w:][qperm]
        wo = jnp.concatenate([w_o[l][:POOL_WIDTH + w], wo_attn], axis=0).astype(BF16)
        wr = jnp.concatenate([router_group[l], router_expert[l],
                              jnp.zeros((d, ROUTE_LANES - N_GROUPS - N_EXPERTS), F32)], axis=1)
        wrh = wr.astype(BF16)
        wrl = (wr - wrh.astype(F32)).astype(BF16)
        br = jnp.concatenate([router_group_b[l], router_expert_b[l],
                              jnp.zeros((ROUTE_LANES - N_GROUPS - N_EXPERTS,), F32)]).reshape(1, ROUTE_LANES)

        if pending is None:
            pool_in, rw_in, qh, kh, vh = _proj(xc, w_proj, cos8, sin8, qg, kg, ones_q, s, tm)
        else:
            xc, pool_in, rw_in, qh, kh, vh = _proj(None, w_proj, cos8, sin8, qg, kg, ones_q, s, tm, prev=pending)
        y_pool = _pool(pool_in.reshape(b, s, POOL_WIDTH), pool_bd, pool_scale[l].reshape(1, POOL_WIDTH))
        shared, dirp, post = _rwkv_prep(rw_in.reshape(b, s, RWKV_PAD), mup, mun, w_lora, lora_bias,
                                        rw_k_k[l].reshape(1, w), rw_k_a[l].reshape(1, w),
                                        rw_r_k[l].reshape(1, w), ones_r, tm)
        wkv_f, wkv_r = _wkv(shared, dirp, min(256, s))
        y_attn = _attn(qh, kh, vh, s, min(512, s))
        x1, x1p, route = _outproj(xc, y_pool.reshape(n, POOL_WIDTH), wkv_f.reshape(n, w), wkv_r.reshape(n, w),
                                  post.reshape(n, 2 * w), y_attn, wo,
                                  rw_gn_g[l].reshape(1, w), rw_gn_b[l].reshape(1, w), ones_r,
                                  ln1_g[l].reshape(1, d), ln1_b[l].reshape(1, d), wrh, wrl, br, tm)

        row_tok, pos, block_exp, n_used = _dispatch(route, n)
        split = block_exp.shape[0] // 2 * MOE_BLOCK
        xs_parts = [_gather_rows(x1p, row_tok[:split]), _gather_rows(x1p, row_tok[split:])]
        ys = _experts(block_exp, n_used, xs_parts, exp_gate, exp_up, exp_down, l)
        yab = _gather_rows(ys, pos)
        pending = (x1, yab, route, ln2_g[l].reshape(1, d), ln2_b[l].reshape(1, d))
    return _combine(*pending, tm).reshape(b, s, d)
```

```python
import functools
import math

import jax
import jax.numpy as jnp
from jax import lax
from jax.experimental import pallas as pl
from jax.experimental.pallas import tpu as pltpu
from jax.experimental.pallas import tpu_sc as plsc

F32 = jnp.float32
BF16 = jnp.bfloat16

D_MODEL = 1024
DEPTH = 4
GRID_W = 64
HEAD_DIM = 64
POOL_WIDTH = 256
POOL_WINDOWS = (2, 4, 8, 16)
POOL_GROUP = 64
RWKV_WIDTH = 256
RWKV_HEADS = 4
DECAY_LORA = 32
AAA_LORA = 32
GATE_LORA = 64
GN_EPS = 64e-5
RWKV_IN = 960
RWKV_PAD = 1024
ATTN_WIDTH = 512
ATTN_HEADS = 8
ATTN_KV_WIDTH = 128
ATTN_IN = ATTN_WIDTH + 2 * ATTN_KV_WIDTH
ROPE_THETA = 10000.0
QK_EPS = 1e-6
N_GROUPS = 4
EXPERTS_PER_GROUP = 8
N_EXPERTS = 32
EXPERT_HIDDEN = 512
MOE_BLOCK = 512
DEEPNORM_ALPHA = float((2 * DEPTH) ** 0.25)
LN_EPS = 1e-5
PROJ_WIDTH = POOL_WIDTH + RWKV_PAD + ATTN_IN
ROUTE_LANES = 128
WKV_CHUNK = 64
ROW_TILE = 512
WKV_TILE = 256
ATTN_Q_TILE = 512
VMEM_LIMIT = 48 * 1024 * 1024


def _cparams(sem):
    return pltpu.CompilerParams(dimension_semantics=sem, vmem_limit_bytes=VMEM_LIMIT)


def _dot(a, b):
    return jnp.dot(a, b, preferred_element_type=F32)


def _dot_nt(a, b):
    return lax.dot_general(a, b, (((1,), (1,)), ((), ())), preferred_element_type=F32)


def _dot_tn(a, b):
    return lax.dot_general(a, b, (((0,), (0,)), ((), ())), preferred_element_type=F32)


def _split2(x):
    hi = x.astype(BF16)
    lo = (x - hi.astype(F32)).astype(BF16)
    return hi, lo


def _split3(x):
    hi = x.astype(BF16)
    r1 = x - hi.astype(F32)
    mid = r1.astype(BF16)
    lo = (r1 - mid.astype(F32)).astype(BF16)
    return hi, mid, lo


HALF_D = D_MODEL // 2
U32 = jnp.uint32


def _pack_rows(x):
    hi = lax.bitcast_convert_type(x[:, :HALF_D].astype(BF16).astype(F32), U32)
    lo = lax.bitcast_convert_type(x[:, HALF_D:].astype(BF16).astype(F32), U32)
    return hi | (lo >> 16)


def _unpack_rows(u):
    a = lax.bitcast_convert_type(u & jnp.uint32(0xFFFF0000), F32).astype(BF16)
    b = lax.bitcast_convert_type(u << 16, F32).astype(BF16)
    return a, b


def _headsum(x, ones_bf16):
    hi, lo = _split2(x)
    return _dot(hi, ones_bf16) + _dot(lo, ones_bf16)


def _rope(x, cos, sin_signed):
    n = x.shape[-1]
    lane = lax.broadcasted_iota(jnp.int32, x.shape, 1)
    first = (lane // 16) % 2 == 0
    partner = jnp.where(first, pltpu.roll(x, n - 16, 1), pltpu.roll(x, 16, 1))
    return x * cos + partner * sin_signed


def _layer_norm(z, g, b):
    mu = jnp.mean(z, axis=-1, keepdims=True)
    zc = z - mu
    var = jnp.mean(zc * zc, axis=-1, keepdims=True)
    return zc * lax.rsqrt(var + LN_EPS) * g + b


def _combine_rows(x1, ya, yb, route, g, b):
    g1 = route[:, 2:3]
    g2 = route[:, 3:4]
    a_hi, a_lo = _unpack_rows(ya)
    b_hi, b_lo = _unpack_rows(yb)
    m = jnp.concatenate([g1 * a_hi.astype(F32) + g2 * b_hi.astype(F32),
                         g1 * a_lo.astype(F32) + g2 * b_lo.astype(F32)], axis=1)
    return _layer_norm(DEEPNORM_ALPHA * x1 + m, g, b)


def _proj_kernel(x_ref, w_ref, cos_ref, sin_ref, qg_ref, kg_ref, onesq_ref,
                 pool_ref, rw_ref, q_ref, k_ref, v_ref):
    _proj_rows(x_ref[...], w_ref, cos_ref, sin_ref, qg_ref, kg_ref, onesq_ref,
               pool_ref, rw_ref, q_ref, k_ref, v_ref)


def _combine_proj_kernel(x1_ref, ya_ref, yb_ref, route_ref, lg_ref, lb_ref,
                         w_ref, cos_ref, sin_ref, qg_ref, kg_ref, onesq_ref,
                         x_ref, pool_ref, rw_ref, q_ref, k_ref, v_ref):
    x = _combine_rows(x1_ref[...], ya_ref[...], yb_ref[...], route_ref[...], lg_ref[...], lb_ref[...])
    x_ref[...] = x
    _proj_rows(x, w_ref, cos_ref, sin_ref, qg_ref, kg_ref, onesq_ref, pool_ref, rw_ref, q_ref, k_ref, v_ref)


def _proj_rows(x, w_ref, cos_ref, sin_ref, qg_ref, kg_ref, onesq_ref, pool_ref, rw_ref, q_ref, k_ref, v_ref):
    xb = x.astype(BF16)
    pool_ref[...] = _dot(xb, w_ref[:, 0:POOL_WIDTH])
    rw_ref[...] = _dot(xb, w_ref[:, POOL_WIDTH:POOL_WIDTH + RWKV_PAD])
    at = _dot(xb, w_ref[:, POOL_WIDTH + RWKV_PAD:PROJ_WIDTH])

    q = at[:, 0:ATTN_WIDTH]
    k = at[:, ATTN_WIDTH:ATTN_WIDTH + ATTN_KV_WIDTH]
    v = at[:, ATTN_WIDTH + ATTN_KV_WIDTH:ATTN_IN]
    ones_q = onesq_ref[...]
    ones_k = onesq_ref[0:ATTN_KV_WIDTH, 0:ATTN_KV_WIDTH]
    inv = 1.0 / HEAD_DIM
    qn = q * lax.rsqrt(_headsum(q * q, ones_q) * inv + QK_EPS) * qg_ref[...]
    kn = k * lax.rsqrt(_headsum(k * k, ones_k) * inv + QK_EPS) * kg_ref[...]
    cos = cos_ref[...]
    sin = sin_ref[...]
    q_ref[...] = (_rope(qn, cos, sin) * (HEAD_DIM ** -0.5 * math.log2(math.e))).astype(BF16)
    k_ref[...] = _rope(kn, cos[:, 0:ATTN_KV_WIDTH], sin[:, 0:ATTN_KV_WIDTH]).astype(BF16)
    v_ref[...] = v.astype(BF16)


def _proj(x2d, w, cos8, sin8, qg, kg, ones_q, seq, tm, prev=None):
    n = (x2d if prev is None else prev[0]).shape[0]
    nts = seq // tm
    nt = n // tm
    row = lambda width: pl.BlockSpec((tm, width), lambda i: (i, 0))
    const = lambda r, c: pl.BlockSpec((r, c), lambda i: (0, 0))
    proj_in = [const(D_MODEL, PROJ_WIDTH),
               pl.BlockSpec((tm, ATTN_WIDTH), lambda i: (i % nts, 0)),
               pl.BlockSpec((tm, ATTN_WIDTH), lambda i: (i % nts, 0)),
               const(1, ATTN_WIDTH), const(1, ATTN_KV_WIDTH), const(ATTN_WIDTH, ATTN_WIDTH)]
    proj_out = [row(POOL_WIDTH), row(RWKV_PAD), row(ATTN_WIDTH), row(ATTN_KV_WIDTH), row(ATTN_KV_WIDTH)]
    proj_shape = [jax.ShapeDtypeStruct((n, POOL_WIDTH), F32),
                  jax.ShapeDtypeStruct((n, RWKV_PAD), F32),
                  jax.ShapeDtypeStruct((n, ATTN_WIDTH), BF16),
                  jax.ShapeDtypeStruct((n, ATTN_KV_WIDTH), BF16),
                  jax.ShapeDtypeStruct((n, ATTN_KV_WIDTH), BF16)]
    if prev is None:
        return pl.pallas_call(
            _proj_kernel,
            grid=(nt,),
            in_specs=[row(D_MODEL)] + proj_in,
            out_specs=proj_out,
            out_shape=proj_shape,
            compiler_params=_cparams(("parallel",)),
            name="proj",
        )(x2d, w, cos8, sin8, qg, kg, ones_q)
    x1, yab, route, lg, lb = prev
    return pl.pallas_call(
        _combine_proj_kernel,
        grid=(nt,),
        in_specs=[row(D_MODEL), row(HALF_D), pl.BlockSpec((tm, HALF_D), lambda i: (i + nt, 0)),
                  row(ROUTE_LANES), const(1, D_MODEL), const(1, D_MODEL)] + proj_in,
        out_specs=[row(D_MODEL)] + proj_out,
        out_shape=[jax.ShapeDtypeStruct((n, D_MODEL), F32)] + proj_shape,
        compiler_params=_cparams(("parallel",)),
        name="combine_proj",
    )(x1, yab, yab, route, lg, lb, w, cos8, sin8, qg, kg, ones_q)


def _pool_kernel(u_ref, w_ref, scale_ref, o_ref):
    s = u_ref.shape[1]
    u = u_ref[0]
    t = lax.broadcasted_iota(jnp.int32, (s, POOL_WIDTH), 0)

    def down(x, k):
        return jnp.where(t >= k, pltpu.roll(x, k, 0), 0.0)

    def up(x, k):
        return jnp.where(t < s - k, pltpu.roll(x, s - k, 0), 0.0)

    left = [down(u, 1)]
    right = [u]
    for k in (1, 2, 4):
        left.append(left[-1] + down(left[-1], k))
        right.append(right[-1] + up(right[-1], k))
    sums = [l + r for l, r in zip(left, right)]

    grp = lax.broadcasted_iota(jnp.int32, (s, POOL_WIDTH), 1) // POOL_GROUP
    half = jnp.where(grp == 0, 1, jnp.where(grp == 1, 2, jnp.where(grp == 2, 4, 8)))
    cnt = (jnp.minimum(t + half, s) - jnp.maximum(t - half, 0)).astype(F32)
    tot = jnp.where(grp == 0, sums[0], jnp.where(grp == 1, sums[1], jnp.where(grp == 2, sums[2], sums[3])))
    d = tot / cnt - u
    o_ref[0] = _dot(d.astype(BF16), w_ref[...]) * scale_ref[...]


def _pool(u3d, w_bd, scale):
    b, s, _ = u3d.shape
    return pl.pallas_call(
        _pool_kernel,
        grid=(b,),
        in_specs=[pl.BlockSpec((1, s, POOL_WIDTH), lambda i: (i, 0, 0)),
                  pl.BlockSpec((POOL_WIDTH, POOL_WIDTH), lambda i: (0, 0)),
                  pl.BlockSpec((1, POOL_WIDTH), lambda i: (0, 0))],
        out_specs=pl.BlockSpec((1, s, POOL_WIDTH), lambda i: (i, 0, 0)),
        out_shape=jax.ShapeDtypeStruct((b, s, POOL_WIDTH), F32),
        compiler_params=_cparams(("parallel",)),
        name="pool",
    )(u3d, w_bd, scale)


_SHIFT_HALO = 8


def _rwkv_prep_kernel(cur_ref, prev_ref, next_ref, mup_ref, mun_ref, wl_ref, bias_ref,
                      kk_ref, ka_ref, rk_ref, ones_ref,
                      sh_ref, lw_ref, kn_ref, post_ref):
    ts = cur_ref.shape[1]
    ti = pl.program_id(1)
    nt = pl.num_programs(1)
    cur = cur_ref[0]
    row = lax.broadcasted_iota(jnp.int32, cur.shape, 0)
    before = jnp.where(ti > 0, prev_ref[0, _SHIFT_HALO - 1:_SHIFT_HALO, :], 0.0)
    after = jnp.where(ti < nt - 1, next_ref[0, 0:1, :], 0.0)
    prev = jnp.where(row == 0, before, pltpu.roll(cur, 1, 0))
    nxt = jnp.where(row == ts - 1, after, pltpu.roll(cur, ts - 1, 0))
    f = cur + mup_ref[...] * (prev - cur) + mun_ref[...] * (nxt - cur)

    w = RWKV_WIDTH
    r = f[:, 0:w]
    k = f[:, w:2 * w]
    v = f[:, 2 * w:3 * w]
    lora = f[:, 3 * w:4 * w]
    lane = lax.broadcasted_iota(jnp.int32, lora.shape, 1)
    z = jnp.where(lane < 2 * DECAY_LORA, jnp.tanh(lora),
                  jnp.where(lane < 2 * DECAY_LORA + 2 * AAA_LORA, lora,
                            jnp.where(lane < 2 * DECAY_LORA + 2 * AAA_LORA + GATE_LORA,
                                      jax.nn.sigmoid(lora), 0.0)))
    up = _dot(z.astype(BF16), wl_ref[...]) + bias_ref[...]
    ones = ones_ref[...]

    kk0 = k * kk_ref[...]
    nrm = jnp.sqrt(_headsum(kk0 * kk0, ones))
    kk = kk0 / jnp.maximum(nrm, 1e-12)
    sh_ref[0, :, 0:w] = r.astype(BF16)
    sh_ref[0, :, w:2 * w] = v.astype(BF16)
    sh_ref[0, :, 2 * w:3 * w] = kk.astype(BF16)

    ksum = jnp.zeros_like(k)
    for di in range(2):
        logw = -math.exp(-0.5) * jax.nn.sigmoid(up[:, di * w:(di + 1) * w])
        a = jax.nn.sigmoid(up[:, (2 + di) * w:(3 + di) * w])
        kh = k * (1.0 + (a - 1.0) * ka_ref[...])
        ksum = ksum + kh
        lw_ref[di, 0] = logw
        kn_ref[di, 0, :, 0:w] = kh.astype(BF16)
        kn_ref[di, 0, :, w:2 * w] = (-(a * kk)).astype(BF16)
    bonus = _headsum(r * ksum * rk_ref[...], ones) * v
    post_ref[0, :, 0:w] = bonus
    post_ref[0, :, w:2 * w] = up[:, 4 * w:5 * w]


def _rwkv_prep(rw3d, mup, mun, wl, bias, k_k, k_a, r_k, ones, ts):
    b, s, _ = rw3d.shape
    nt = s // ts
    hb = ts // _SHIFT_HALO
    nhb = s // _SHIFT_HALO
    w = RWKV_WIDTH
    vec = lambda width: pl.BlockSpec((1, width), lambda i, j: (0, 0))
    return pl.pallas_call(
        _rwkv_prep_kernel,
        grid=(b, nt),
        in_specs=[pl.BlockSpec((1, ts, RWKV_PAD), lambda i, j: (i, j, 0)),
                  pl.BlockSpec((1, _SHIFT_HALO, RWKV_PAD),
                               lambda i, j: (i, jnp.maximum(j * hb - 1, 0), 0)),
                  pl.BlockSpec((1, _SHIFT_HALO, RWKV_PAD),
                               lambda i, j: (i, jnp.minimum((j + 1) * hb, nhb - 1), 0)),
                  vec(RWKV_PAD), vec(RWKV_PAD),
                  pl.BlockSpec((w, 5 * w), lambda i, j: (0, 0)),
                  vec(5 * w), vec(w), vec(w), vec(w),
                  pl.BlockSpec((w, w), lambda i, j: (0, 0))],
        out_specs=[pl.BlockSpec((1, ts, 3 * w), lambda i, j: (i, j, 0)),
                   pl.BlockSpec((2, 1, ts, w), lambda i, j: (0, i, j, 0)),
                   pl.BlockSpec((2, 1, ts, 2 * w), lambda i, j: (0, i, j, 0)),
                   pl.BlockSpec((1, ts, 2 * w), lambda i, j: (i, j, 0))],
        out_shape=[jax.ShapeDtypeStruct((b, s, 3 * w), BF16),
                   jax.ShapeDtypeStruct((2, b, s, w), F32),
                   jax.ShapeDtypeStruct((2, b, s, 2 * w), BF16),
                   jax.ShapeDtypeStruct((b, s, 2 * w), F32)],
        compiler_params=_cparams(("parallel", "parallel")),
        name="rwkv_prep",
    )(rw3d, rw3d, rw3d, mup, mun, wl, bias, k_k, k_a, r_k, ones)


def _expand_bd(x_bf16, mask_bd):
    return jnp.where(mask_bd, jnp.concatenate([x_bf16] * RWKV_HEADS, axis=0), jnp.zeros((), BF16))


def _wkv_kernel(shf_ref, shr_ref, lwf_ref, lwr_ref, knf_ref, knr_ref, yf_ref, yr_ref, hf_ref, hr_ref):
    c = WKV_CHUNK
    w = RWKV_WIDTH
    tt = shf_ref.shape[1]
    ncs = tt // c

    @pl.when(pl.program_id(1) == 0)
    def _():
        hf_ref[...] = jnp.zeros_like(hf_ref)
        hr_ref[...] = jnp.zeros_like(hr_ref)

    row_c = lax.broadcasted_iota(jnp.int32, (c, c), 0)
    col_c = lax.broadcasted_iota(jnp.int32, (c, c), 1)
    t_i = lax.broadcasted_iota(jnp.int32, (c, w), 0)
    s_i = lax.broadcasted_iota(jnp.int32, (c, w), 1) % c
    eye_c = (s_i == t_i).astype(F32)
    row_w = lax.broadcasted_iota(jnp.int32, (w, w), 0)
    col_w = lax.broadcasted_iota(jnp.int32, (w, w), 1)
    mask_bd = (row_w // c) == (col_w // HEAD_DIM)
    eye_w = row_w == col_w
    tri_incl = [jnp.where(col_c <= row_c, 1.0, 0.0).astype(BF16),
                jnp.where(col_c >= row_c, 1.0, 0.0).astype(BF16)]
    strict = [s_i < t_i, s_i > t_i]
    incl = [s_i <= t_i, s_i >= t_i]

    def bf(x):
        return x.astype(BF16)

    def bd(x):
        return _expand_bd(bf(x), mask_bd)

    chunks = []
    for ci in range(ncs):
        chunks.append((0, ci * c, shf_ref, lwf_ref, knf_ref))
        chunks.append((1, (ncs - 1 - ci) * c, shr_ref, lwr_ref, knr_ref))

    st = []
    for d, start, sh_ref, lw_ref, kn_ref in chunks:
        rows = pl.ds(start, c)
        st.append(dict(d=d, rows=rows,
                       r=sh_ref[0, rows, 0:w].astype(F32), v_bf=sh_ref[0, rows, w:2 * w],
                       kk=sh_ref[0, rows, 2 * w:3 * w].astype(F32), lw=lw_ref[0, rows, :],
                       kh=kn_ref[0, rows, 0:w].astype(F32), nb=kn_ref[0, rows, w:2 * w].astype(F32)))

    for q in st:
        l_hi, l_mid, l_lo = _split3(q["lw"])
        tri = tri_incl[q["d"]]
        q["cum"] = _dot(tri, l_hi) + _dot(tri, l_mid) + _dot(tri, l_lo)
    for q in st:
        cum, lw = q["cum"], q["lw"]
        tot = jnp.sum(lw, axis=0, keepdims=True)
        e_inv = jnp.exp(-cum)
        e_end = jnp.exp(tot - cum)
        q["tot"] = tot
        q["a_bar"] = q["kk"] * jnp.exp(cum - lw)
        q["r_bar"] = q["r"] * jnp.exp(cum)
        q["b_hat"] = bf(q["nb"] * e_end)
        q["k_hat"] = bf(q["kh"] * e_end)
        q["v_bd"] = _expand_bd(q["v_bf"], mask_bd)
        q["lhs"] = jnp.concatenate([bf(q["a_bar"]), bf(q["r_bar"])], axis=0)
        q["rhs"] = jnp.concatenate([bd(q["nb"] * e_inv), bd(q["kh"] * e_inv)], axis=0)
    for q in st:
        sc = _dot_nt(q["lhs"], q["rhs"])
        sm, im = strict[q["d"]], incl[q["d"]]
        q["x"] = jnp.where(sm, sc[0:c, 0:w], 0.0)
        q["a_ak"] = bf(jnp.where(sm, sc[0:c, w:2 * w], 0.0))
        q["m_rb"] = bf(jnp.where(im, sc[c:2 * c, 0:w], 0.0))
        q["m_rk"] = bf(jnp.where(im, sc[c:2 * c, w:2 * w], 0.0))
        q["tinv"] = eye_c + q["x"]

    for q in st:
        q["x"] = _dot(bf(q["x"]), bd(q["x"]))
    n_lv = int(math.log2(c)) - 1
    for lv in range(n_lv):
        last = lv == n_lv - 1
        for q in st:
            xb = bd(q["x"])
            if last:
                q["tinv"] = q["tinv"] + _dot(bf(q["tinv"]), xb)
            else:
                res = _dot(jnp.concatenate([bf(q["x"]), bf(q["tinv"])], axis=0), xb)
                q["tinv"] = q["tinv"] + res[c:2 * c]
                q["x"] = res[0:c]
    for q in st:
        res = _dot(jnp.concatenate([q["a_ak"], q["m_rk"]], axis=0), q["v_bd"])
        q["akv"] = res[0:c]
        q["mrkv"] = res[c:2 * c]
    for q in st:
        tinv_bf = bf(q["tinv"])
        q["a_pr"] = bf(_dot(tinv_bf, bd(q["a_bar"])))
        q["v_pr"] = bf(_dot(tinv_bf, bd(q["akv"])))
    for q in st:
        q["r_pr"] = bf(q["r_bar"] + _dot(q["m_rb"], _expand_bd(q["a_pr"], mask_bd)))
        q["y_pr"] = _dot(q["m_rb"], _expand_bd(q["v_pr"], mask_bd)) + q["mrkv"]
    for q in st:
        q["p_bd"] = bf(jnp.where(eye_w, jnp.exp(q["tot"]), 0.0)
                       + jnp.where(mask_bd, _dot_tn(q["b_hat"], q["a_pr"]), 0.0))
        q["q_bd"] = jnp.where(mask_bd,
                              _dot_tn(jnp.concatenate([q["b_hat"], q["k_hat"]], axis=0),
                                      jnp.concatenate([q["v_pr"], q["v_bf"]], axis=0)), 0.0)

    h = [hf_ref[...], hr_ref[...]]
    y_refs = [yf_ref, yr_ref]
    for q in st:
        d = q["d"]
        res = _dot(jnp.concatenate([q["r_pr"], q["p_bd"]], axis=0), bf(h[d]))
        y_refs[d][0, q["rows"], :] = res[0:c] + q["y_pr"]
        h[d] = res[c:c + w] + q["q_bd"]
    hf_ref[...] = h[0]
    hr_ref[...] = h[1]


def _wkv(shared, logw, khnb, tt):
    b, s, _ = shared.shape
    nt = s // tt
    w = RWKV_WIDTH
    fwd = lambda i, j: (i, j, 0)
    bwd = lambda i, j: (i, nt - 1 - j, 0)
    return pl.pallas_call(
        _wkv_kernel,
        grid=(b, nt),
        in_specs=[pl.BlockSpec((1, tt, 3 * w), fwd), pl.BlockSpec((1, tt, 3 * w), bwd),
                  pl.BlockSpec((None, 1, tt, w), lambda i, j: (0, i, j, 0)),
                  pl.BlockSpec((None, 1, tt, w), lambda i, j: (1, i, nt - 1 - j, 0)),
                  pl.BlockSpec((None, 1, tt, 2 * w), lambda i, j: (0, i, j, 0)),
                  pl.BlockSpec((None, 1, tt, 2 * w), lambda i, j: (1, i, nt - 1 - j, 0))],
        out_specs=[pl.BlockSpec((1, tt, w), fwd), pl.BlockSpec((1, tt, w), bwd)],
        out_shape=[jax.ShapeDtypeStruct((b, s, w), F32), jax.ShapeDtypeStruct((b, s, w), F32)],
        scratch_shapes=[pltpu.VMEM((w, w), F32), pltpu.VMEM((w, w), F32)],
        compiler_params=_cparams(("parallel", "arbitrary")),
        name="wkv",
    )(shared, shared, logw, logw, khnb, khnb)


def _attn_kernel(q_ref, k_ref, v_ref, o_ref):
    tq = q_ref.shape[0]
    k = k_ref[...]
    v = v_ref[...]
    lane = lax.broadcasted_iota(jnp.int32, (tq, ATTN_KV_WIDTH), 1)
    low = lane < HEAD_DIM
    zero = jnp.zeros((), BF16)
    for j in range(ATTN_HEADS // 2):
        qb = q_ref[:, j * ATTN_KV_WIDTH:(j + 1) * ATTN_KV_WIDTH]
        outs = []
        for half in range(2):
            qm = jnp.where(low if half == 0 else ~low, qb, zero)
            s = _dot_nt(qm, k)
            m = jnp.max(s, axis=-1, keepdims=True)
            p = jnp.exp2(s - m)
            l = jnp.sum(p, axis=-1, keepdims=True)
            outs.append(_dot(p.astype(BF16), v) / l)
        o_ref[:, j * ATTN_KV_WIDTH:(j + 1) * ATTN_KV_WIDTH] = jnp.where(low, outs[0], outs[1]).astype(BF16)


def _attn(q, k, v, seq, tq):
    n = q.shape[0]
    nq = seq // tq
    return pl.pallas_call(
        _attn_kernel,
        grid=(n // seq, nq),
        in_specs=[pl.BlockSpec((tq, ATTN_WIDTH), lambda b, i: (b * nq + i, 0)),
                  pl.BlockSpec((seq, ATTN_KV_WIDTH), lambda b, i: (b, 0)),
                  pl.BlockSpec((seq, ATTN_KV_WIDTH), lambda b, i: (b, 0))],
        out_specs=pl.BlockSpec((tq, ATTN_WIDTH), lambda b, i: (b * nq + i, 0)),
        out_shape=jax.ShapeDtypeStruct((n, ATTN_WIDTH), BF16),
        compiler_params=_cparams(("parallel", "parallel")),
        name="attn",
    )(q, k, v)


def _outproj_kernel(x_ref, pool_ref, wkvf_ref, wkvr_ref, post_ref, attn_ref, wo_ref, gng_ref, gnb_ref,
                    ones_ref, lg_ref, lb_ref, wrh_ref, wrl_ref, br_ref,
                    x1_ref, x1p_ref, route_ref):
    w = RWKV_WIDTH
    ones = ones_ref[...]
    wkv = wkvf_ref[...] + wkvr_ref[...]
    inv = 1.0 / HEAD_DIM
    mu = _headsum(wkv, ones) * inv
    cen = wkv - mu
    var = _headsum(cen * cen, ones) * inv
    yr = cen * lax.rsqrt(var + GN_EPS) * gng_ref[...] + gnb_ref[...] + post_ref[:, 0:w]
    yr = yr * post_ref[:, w:2 * w]
    y = (_dot(pool_ref[...].astype(BF16), wo_ref[0:POOL_WIDTH, :])
         + _dot(yr.astype(BF16), wo_ref[POOL_WIDTH:POOL_WIDTH + w, :])
         + _dot(attn_ref[...], wo_ref[POOL_WIDTH + w:D_MODEL, :]))
    x1 = _layer_norm(DEEPNORM_ALPHA * x_ref[...] + y, lg_ref[...], lb_ref[...])
    x1_ref[...] = x1
    x1p_ref[...] = _pack_rows(x1)

    xh, xl = _split2(x1)
    logits = (_dot(xh, wrh_ref[...]) + _dot(xl, wrh_ref[...]) + _dot(xh, wrl_ref[...])) + br_ref[...]
    lane = lax.broadcasted_iota(jnp.int32, logits.shape, 1)
    lane_f = lane.astype(F32)
    lane_grp_f = ((lane - N_GROUPS) // EXPERTS_PER_GROUP).astype(F32)
    neg = -jnp.inf
    big = jnp.float32(1 << 20)
    gl = jnp.where(lane < N_GROUPS, logits, neg)
    gmax = jnp.max(gl, axis=-1, keepdims=True)
    grp = jnp.min(jnp.where(gl == gmax, lane_f, big), axis=-1, keepdims=True)
    gw = 1.0 / jnp.sum(jnp.exp(gl - gmax), axis=-1, keepdims=True)
    in_grp = (lane >= N_GROUPS) & (lane < N_GROUPS + N_EXPERTS) & (lane_grp_f == grp)
    el = jnp.where(in_grp, logits, neg)
    v1 = jnp.max(el, axis=-1, keepdims=True)
    i1 = jnp.min(jnp.where(el == v1, lane_f, big), axis=-1, keepdims=True)
    el2 = jnp.where(lane_f == i1, neg, el)
    v2 = jnp.max(el2, axis=-1, keepdims=True)
    i2 = jnp.min(jnp.where(el2 == v2, lane_f, big), axis=-1, keepdims=True)
    e21 = jnp.exp(v2 - v1)
    g1 = gw / (1.0 + e21)
    g2 = gw * e21 / (1.0 + e21)
    route = jnp.where(lane == 0, i1 - N_GROUPS,
                      jnp.where(lane == 1, i2 - N_GROUPS,
                                jnp.where(lane == 2, g1, jnp.where(lane == 3, g2, 0.0))))
    route_ref[...] = route


def _outproj(x2d, pool2d, wkvf, wkvr, post2d, attn2d, wo, gng, gnb, ones, lg, lb, wrh, wrl, br, tm):
    n = x2d.shape[0]
    w = RWKV_WIDTH
    row = lambda width: pl.BlockSpec((tm, width), lambda i: (i, 0))
    vec = lambda width: pl.BlockSpec((1, width), lambda i: (0, 0))
    return pl.pallas_call(
        _outproj_kernel,
        grid=(n // tm,),
        in_specs=[row(D_MODEL), row(POOL_WIDTH), row(w), row(w),
                  row(2 * w), row(ATTN_WIDTH),
                  pl.BlockSpec((D_MODEL, D_MODEL), lambda i: (0, 0)),
                  vec(w), vec(w),
                  pl.BlockSpec((w, w), lambda i: (0, 0)),
                  vec(D_MODEL), vec(D_MODEL),
                  pl.BlockSpec((D_MODEL, ROUTE_LANES), lambda i: (0, 0)),
                  pl.BlockSpec((D_MODEL, ROUTE_LANES), lambda i: (0, 0)),
                  vec(ROUTE_LANES)],
        out_specs=[row(D_MODEL), row(HALF_D), row(ROUTE_LANES)],
        out_shape=[jax.ShapeDtypeStruct((n, D_MODEL), F32),
                   jax.ShapeDtypeStruct((n, HALF_D), U32),
                   jax.ShapeDtypeStruct((n, ROUTE_LANES), F32)],
        compiler_params=_cparams(("parallel",)),
        name="outproj",
    )(x2d, pool2d, wkvf, wkvr, post2d, attn2d, wo, gng, gnb, ones, lg, lb, wrh, wrl, br)


def _expert_kernel(bexp_ref, nused_ref, xs_ref, wg_ref, wu_ref, wd_ref, *rest):
    ys_ref, wgb_ref, wub_ref, wdb_ref = rest[-4:]
    i = pl.program_id(0)
    used = i < nused_ref[0]

    @pl.when(used & ((i == 0) | (bexp_ref[i] != bexp_ref[jnp.maximum(i - 1, 0)])))
    def _():
        wgb_ref[...] = wg_ref[0].astype(BF16)
        wub_ref[...] = wu_ref[0].astype(BF16)
        wdb_ref[...] = wd_ref[0].astype(BF16)

    @pl.when(used)
    def _():
        xa, xb = _unpack_rows(xs_ref[...])
        h1 = _dot(xa, wgb_ref[0:HALF_D, :]) + _dot(xb, wgb_ref[HALF_D:D_MODEL, :])
        h2 = _dot(xa, wub_ref[0:HALF_D, :]) + _dot(xb, wub_ref[HALF_D:D_MODEL, :])
        h = (h1 * jax.nn.sigmoid(h1) * h2).astype(BF16)
        ys_ref[...] = _pack_rows(_dot(h, wdb_ref[...]))

    @pl.when(jnp.logical_not(used))
    def _():
        ys_ref[...] = jnp.zeros_like(ys_ref)


def _experts(block_exp, n_used, xs_parts, wg, wu, wd, layer):
    p = sum(xs.shape[0] for xs in xs_parts)
    wmap = lambda i, be, nu: (layer, be[i], 0, 0)
    ys = None
    first = 0
    for xs in xs_parts:
        nb = xs.shape[0] // MOE_BLOCK
        in_specs = [pl.BlockSpec((MOE_BLOCK, HALF_D), lambda i, be, nu: (i, 0)),
                    pl.BlockSpec((None, 1, D_MODEL, EXPERT_HIDDEN), wmap),
                    pl.BlockSpec((None, 1, D_MODEL, EXPERT_HIDDEN), wmap),
                    pl.BlockSpec((None, 1, EXPERT_HIDDEN, D_MODEL), wmap)]
        args = [block_exp[first:first + nb], n_used - first, xs, wg, wu, wd]
        aliases = {}
        if ys is not None:
            in_specs.append(pl.BlockSpec(memory_space=pl.ANY))
            aliases = {len(args): 0}
            args.append(ys)
        grid_spec = pltpu.PrefetchScalarGridSpec(
            num_scalar_prefetch=2,
            grid=(nb,),
            in_specs=in_specs,
            out_specs=pl.BlockSpec((MOE_BLOCK, HALF_D), functools.partial(_block_at, first)),
            scratch_shapes=[pltpu.VMEM((D_MODEL, EXPERT_HIDDEN), BF16),
                            pltpu.VMEM((D_MODEL, EXPERT_HIDDEN), BF16),
                            pltpu.VMEM((EXPERT_HIDDEN, D_MODEL), BF16)],
        )
        ys = pl.pallas_call(
            _expert_kernel,
            grid_spec=grid_spec,
            out_shape=jax.ShapeDtypeStruct((p, HALF_D), U32),
            input_output_aliases=aliases,
            compiler_params=_cparams(("arbitrary",)),
            name="experts",
        )(*args)
        first += nb
    return ys


def _block_at(first, i, be, nu):
    return (i + first, 0)


SC_CORES = 2
SC_SUBCORES = 16
SC_GATHER_ROWS = 64


def _sc_gather(table, idx):
    v, d = table.shape
    b = idx.shape[0]
    nw = SC_CORES * SC_SUBCORES
    ch = SC_GATHER_ROWS
    assert b % (nw * ch * 2) == 0
    b_per_w = b // nw
    nch = b_per_w // ch
    mesh = plsc.VectorSubcoreMesh(core_axis_name="c", subcore_axis_name="s")

    @functools.partial(
        pl.kernel, mesh=mesh,
        out_type=jax.ShapeDtypeStruct((b, d), table.dtype),
        scratch_types=[pltpu.VMEM((b_per_w,), jnp.int32),
                       pltpu.VMEM((2, ch, d), table.dtype),
                       pltpu.SemaphoreType.DMA, pltpu.SemaphoreType.DMA,
                       pltpu.SemaphoreType.DMA, pltpu.SemaphoreType.DMA],
    )
    def gather_kernel(table_hbm, idx_hbm, out_hbm, idx_v, rows_v, gsem0, gsem1, wsem0, wsem1):
        wid = lax.axis_index("s") * SC_CORES + lax.axis_index("c")
        base = wid * b_per_w
        pltpu.sync_copy(idx_hbm.at[pl.ds(base, b_per_w)], idx_v)
        gsem = (gsem0, gsem1)
        wsem = (wsem0, wsem1)

        def gather(jj, slot):
            off = pl.multiple_of(jj * ch, ch)
            return pltpu.make_async_copy(table_hbm.at[idx_v.at[pl.ds(off, ch)]], rows_v.at[slot], gsem[slot])

        def write(jj, slot):
            off = pl.multiple_of(jj * ch, ch)
            return pltpu.make_async_copy(rows_v.at[slot], out_hbm.at[pl.ds(base + off, ch)], wsem[slot])

        gather(0, 0).start()

        @pl.loop(0, nch, step=2)
        def _(j):
            for slot in range(2):
                jj = j + slot

                gather(jj, slot).wait()

                @pl.when(jj >= 1)
                def _():
                    write(jj - 1, 1 - slot).wait()

                @pl.when(jj + 1 < nch)
                def _():
                    gather(jj + 1, 1 - slot).start()

                write(jj, slot).start()

        write(nch - 1, (nch - 1) % 2).wait()

    return gather_kernel(table, idx)


def _gather_rows(table, idx):
    b = idx.shape[0]
    unit = SC_CORES * SC_SUBCORES * SC_GATHER_ROWS * 2
    bp = -(-b // unit) * unit
    if bp == b:
        return _sc_gather(table, idx)
    return _sc_gather(table, jnp.pad(idx, (0, bp - b)))[:b]


def _combine_kernel(x1_ref, ya_ref, yb_ref, route_ref, lg_ref, lb_ref, o_ref):
    o_ref[...] = _combine_rows(x1_ref[...], ya_ref[...], yb_ref[...], route_ref[...], lg_ref[...], lb_ref[...])


def _combine(x1, yab, route, lg, lb, tm):
    n = x1.shape[0]
    nt = n // tm
    row = lambda width: pl.BlockSpec((tm, width), lambda i: (i, 0))
    vec = lambda width: pl.BlockSpec((1, width), lambda i: (0, 0))
    return pl.pallas_call(
        _combine_kernel,
        grid=(nt,),
        in_specs=[row(D_MODEL), row(HALF_D), pl.BlockSpec((tm, HALF_D), lambda i: (i + nt, 0)),
                  row(ROUTE_LANES), vec(D_MODEL), vec(D_MODEL)],
        out_specs=row(D_MODEL),
        out_shape=jax.ShapeDtypeStruct((n, D_MODEL), F32),
        compiler_params=_cparams(("parallel",)),
        name="combine",
    )(x1, yab, yab, route, lg, lb)


def _q_perm():
    order = [h for j in range(ATTN_HEADS // 2) for h in (j, j + ATTN_HEADS // 2)]
    return jnp.concatenate([jnp.arange(h * HEAD_DIM, (h + 1) * HEAD_DIM) for h in order])


def _block_diag(blocks):
    n = len(blocks)
    r, c = blocks[0].shape
    out = jnp.zeros((n * r, n * c), blocks[0].dtype)
    for i, blk in enumerate(blocks):
        out = out.at[i * r:(i + 1) * r, i * c:(i + 1) * c].set(blk)
    return out


def _rope_tables(seq):
    rows = seq // GRID_W
    row_id = jnp.repeat(jnp.arange(rows), GRID_W).astype(F32)
    col_id = jnp.tile(jnp.arange(GRID_W), rows).astype(F32)
    half = HEAD_DIM // 2
    inv_freq = ROPE_THETA ** (-jnp.arange(0, half, 2, dtype=F32) / half)
    ang_r = row_id[:, None] * inv_freq
    ang_c = col_id[:, None] * inv_freq
    ang = jnp.concatenate([ang_r, ang_r, ang_c, ang_c], -1)
    sign = jnp.where((jnp.arange(HEAD_DIM) // 16) % 2 == 0, -1.0, 1.0).astype(F32)
    cos8 = jnp.tile(jnp.cos(ang), (1, ATTN_HEADS))
    sin8 = jnp.tile(jnp.sin(ang) * sign, (1, ATTN_HEADS))
    return cos8, sin8


def _dispatch(route, n):
    m = n * 2
    experts = jnp.arange(N_EXPERTS, dtype=jnp.int32)
    e_flat = jnp.concatenate([route[:, 0], route[:, 1]]).astype(jnp.int32)
    pair = jnp.arange(m, dtype=jnp.int32)
    e_sorted, order = lax.sort((e_flat, pair), num_keys=1, is_stable=True)
    start = jnp.sum(e_sorted[None, :] < experts[:, None], axis=1, dtype=jnp.int32)
    counts = jnp.concatenate([start[1:], jnp.full((1,), m, jnp.int32)]) - start
    padded = (counts + MOE_BLOCK - 1) // MOE_BLOCK * MOE_BLOCK
    ends_p = jnp.sum(jnp.where(experts[None, :] <= experts[:, None], padded[None, :], 0), axis=1, dtype=jnp.int32)
    pstart = ends_p - padded
    delta = pstart - start
    n_blocks = -(-(m + N_EXPERTS * (MOE_BLOCK - 1)) // MOE_BLOCK)
    block_start = jnp.arange(n_blocks, dtype=jnp.int32) * MOE_BLOCK
    block_exp = jnp.minimum(jnp.sum(ends_p[None, :] <= block_start[:, None], axis=1, dtype=jnp.int32),
                            N_EXPERTS - 1)
    lane = jnp.arange(MOE_BLOCK, dtype=jnp.int32)[None, :]
    blk = block_exp[:, None] == experts[None, :]

    def per_block(table):
        return jnp.sum(jnp.where(blk, table[None, :], 0), axis=1, dtype=jnp.int32)

    valid = (block_start - per_block(pstart))[:, None] + lane < per_block(counts)[:, None]
    src = jnp.clip((block_start - per_block(delta))[:, None] + lane, 0, m - 1)
    run = order[src.reshape(-1)].reshape(n_blocks, MOE_BLOCK)
    row_tok = jnp.where(valid, run % n, (block_start[:, None] + lane) % n).reshape(-1)
    n_used = (ends_p[-1] // MOE_BLOCK).astype(jnp.int32).reshape(1)
    dest = pair + jnp.sum(jnp.where(e_sorted[None, :] == experts[:, None], delta[:, None], 0), axis=0,
                          dtype=jnp.int32)
    _, pos = lax.sort((order, dest), num_keys=1)
    return row_tok, pos, block_exp, n_used


def kernel(x, w_in, mu_prev, mu_next, pool_w, pool_scale, rw_w0, rw_w_up, rw_a0, rw_a_up, rw_g_up, rw_k_k, rw_k_a, rw_r_k, rw_gn_g, rw_gn_b, q_norm, k_norm, w_o, ln1_g, ln1_b, router_group, router_group_b, router_expert, router_expert_b, exp_gate, exp_up, exp_down, ln2_g, ln2_b):
    b, s, d = x.shape
    n = b * s
    w = RWKV_WIDTH
    tm = min(ROW_TILE, s)
    qperm = _q_perm()
    cos8, sin8 = _rope_tables(s)
    ones_q = _block_diag([jnp.ones((HEAD_DIM, HEAD_DIM), BF16)] * ATTN_HEADS)
    ones_r = ones_q[0:w, 0:w]
    a_end = POOL_WIDTH
    b_end = POOL_WIDTH + RWKV_IN

    xc = x.reshape(n, d)
    pending = None
    for l in range(DEPTH):
        wq = w_in[l][:, b_end:b_end + ATTN_WIDTH][:, qperm]
        w_proj = jnp.concatenate(
            [w_in[l][:, :a_end], w_in[l][:, a_end:b_end],
             jnp.zeros((d, RWKV_PAD - RWKV_IN), F32), wq, w_in[l][:, b_end + ATTN_WIDTH:]],
            axis=1).astype(BF16)
        pool_bd = _block_diag([pool_w[l, g] for g in range(len(POOL_WINDOWS))]).astype(BF16)
        pad_vec = jnp.zeros((RWKV_PAD - RWKV_IN,), F32)
        mup = jnp.concatenate([mu_prev[l], pad_vec]).reshape(1, RWKV_PAD)
        mun = jnp.concatenate([mu_next[l], pad_vec]).reshape(1, RWKV_PAD)
        w_lora = jnp.zeros((w, 5 * w), F32)
        for di in range(2):
            w_lora = w_lora.at[di * DECAY_LORA:(di + 1) * DECAY_LORA, di * w:(di + 1) * w].set(rw_w_up[l, di])
            o = 2 * DECAY_LORA
            w_lora = w_lora.at[o + di * AAA_LORA:o + (di + 1) * AAA_LORA, (2 + di) * w:(3 + di) * w].set(rw_a_up[l, di])
        o = 2 * DECAY_LORA + 2 * AAA_LORA
        w_lora = w_lora.at[o:o + GATE_LORA, 4 * w:5 * w].set(rw_g_up[l]).astype(BF16)
        lora_bias = jnp.concatenate([rw_w0[l, 0], rw_w0[l, 1], rw_a0[l, 0], rw_a0[l, 1],
                                     jnp.zeros((w,), F32)]).reshape(1, 5 * w)
        qg = jnp.tile(q_norm[l], ATTN_HEADS).reshape(1, ATTN_WIDTH)
        kg = jnp.tile(k_norm[l], 2).reshape(1, ATTN_KV_WIDTH)
        wo_attn = w_o[l][POOL_WIDTH + w:][qperm]
        wo = jnp.concatenate([w_o[l][:POOL_WIDTH + w], wo_attn], axis=0).astype(BF16)
        wr = jnp.concatenate([router_group[l], router_expert[l],
                              jnp.zeros((d, ROUTE_LANES - N_GROUPS - N_EXPERTS), F32)], axis=1)
        wrh = wr.astype(BF16)
        wrl = (wr - wrh.astype(F32)).astype(BF16)
        br = jnp.concatenate([router_group_b[l], router_expert_b[l],
                              jnp.zeros((ROUTE_LANES - N_GROUPS - N_EXPERTS,), F32)]).reshape(1, ROUTE_LANES)

        if pending is None:
            pool_in, rw_in, qh, kh, vh = _proj(xc, w_proj, cos8, sin8, qg, kg, ones_q, s, tm)
        else:
            xc, pool_in, rw_in, qh, kh, vh = _proj(None, w_proj, cos8, sin8, qg, kg, ones_q, s, tm, prev=pending)
        y_pool = _pool(pool_in.reshape(b, s, POOL_WIDTH), pool_bd, pool_scale[l].reshape(1, POOL_WIDTH))
        shared, logw, khnb, post = _rwkv_prep(rw_in.reshape(b, s, RWKV_PAD), mup, mun, w_lora, lora_bias,
                                        rw_k_k[l].reshape(1, w), rw_k_a[l].reshape(1, w),
                                        rw_r_k[l].reshape(1, w), ones_r, tm)
        wkv_f, wkv_r = _wkv(shared, logw, khnb, min(WKV_TILE, s))
        y_attn = _attn(qh, kh, vh, s, min(ATTN_Q_TILE, s))
        x1, x1p, route = _outproj(xc, y_pool.reshape(n, POOL_WIDTH), wkv_f.reshape(n, w), wkv_r.reshape(n, w),
                                  post.reshape(n, 2 * w), y_attn, wo,
                                  rw_gn_g[l].reshape(1, w), rw_gn_b[l].reshape(1, w), ones_r,
                                  ln1_g[l].reshape(1, d), ln1_b[l].reshape(1, d), wrh, wrl, br, tm)

        row_tok, pos, block_exp, n_used = _dispatch(route, n)
        split = block_exp.shape[0] // 2 * MOE_BLOCK
        xs_parts = [_gather_rows(x1p, row_tok[:split]), _gather_rows(x1p, row_tok[split:])]
        ys = _experts(block_exp, n_used, xs_parts, exp_gate, exp_up, exp_down, l)
        yab = _gather_rows(ys, pos)
        pending = (x1, yab, route, ln2_g[l].reshape(1, d), ln2_b[l].reshape(1, d))
    return _combine(*pending, tm).reshape(b, s, d)
```

```python
import functools
import math

import jax
import jax.numpy as jnp
from jax import lax
from jax.experimental import pallas as pl
from jax.experimental.pallas import tpu as pltpu
from jax.experimental.pallas import tpu_sc as plsc

F32 = jnp.float32
BF16 = jnp.bfloat16

D_MODEL = 1024
DEPTH = 4
GRID_W = 64
HEAD_DIM = 64
POOL_WIDTH = 256
POOL_WINDOWS = (2, 4, 8, 16)
POOL_GROUP = 64
RWKV_WIDTH = 256
RWKV_HEADS = 4
DECAY_LORA = 32
AAA_LORA = 32
GATE_LORA = 64
GN_EPS = 64e-5
RWKV_IN = 960
RWKV_PAD = 1024
ATTN_WIDTH = 512
ATTN_HEADS = 8
ATTN_KV_WIDTH = 128
ATTN_IN = ATTN_WIDTH + 2 * ATTN_KV_WIDTH
ROPE_THETA = 10000.0
QK_EPS = 1e-6
N_GROUPS = 4
EXPERTS_PER_GROUP = 8
N_EXPERTS = 32
EXPERT_HIDDEN = 512
MOE_BLOCK = 512
DEEPNORM_ALPHA = float((2 * DEPTH) ** 0.25)
LN_EPS = 1e-5
PROJ_WIDTH = POOL_WIDTH + RWKV_PAD + ATTN_IN
ROUTE_LANES = 128
WKV_CHUNK = 64
ROW_TILE = 512
WKV_TILE = 256
ATTN_Q_TILE = 512
VMEM_LIMIT = 48 * 1024 * 1024


def _cparams(sem):
    return pltpu.CompilerParams(dimension_semantics=sem, vmem_limit_bytes=VMEM_LIMIT)


def _dot(a, b):
    return jnp.dot(a, b, preferred_element_type=F32)


def _dot_nt(a, b):
    return lax.dot_general(a, b, (((1,), (1,)), ((), ())), preferred_element_type=F32)


def _dot_tn(a, b):
    return lax.dot_general(a, b, (((0,), (0,)), ((), ())), preferred_element_type=F32)


def _split2(x):
    hi = x.astype(BF16)
    lo = (x - hi.astype(F32)).astype(BF16)
    return hi, lo


def _split3(x):
    hi = x.astype(BF16)
    r1 = x - hi.astype(F32)
    mid = r1.astype(BF16)
    lo = (r1 - mid.astype(F32)).astype(BF16)
    return hi, mid, lo


HALF_D = D_MODEL // 2
U32 = jnp.uint32


def _pack_rows(x):
    hi = lax.bitcast_convert_type(x[:, :HALF_D].astype(BF16).astype(F32), U32)
    lo = lax.bitcast_convert_type(x[:, HALF_D:].astype(BF16).astype(F32), U32)
    return hi | (lo >> 16)


def _unpack_rows(u):
    a = lax.bitcast_convert_type(u & jnp.uint32(0xFFFF0000), F32).astype(BF16)
    b = lax.bitcast_convert_type(u << 16, F32).astype(BF16)
    return a, b


def _headsum(x, ones_bf16):
    hi, lo = _split2(x)
    return _dot(hi, ones_bf16) + _dot(lo, ones_bf16)


ROPE_QUARTER = HEAD_DIM // 4


def _rope(x, cos, sin_signed):
    n = x.shape[-1]
    lane = lax.broadcasted_iota(jnp.int32, x.shape, 1)
    first = (lane // ROPE_QUARTER) % 2 == 0
    partner = jnp.where(first, pltpu.roll(x, n - ROPE_QUARTER, 1), pltpu.roll(x, ROPE_QUARTER, 1))
    return x * cos + partner * sin_signed


def _layer_norm(z, g, b):
    mu = jnp.mean(z, axis=-1, keepdims=True)
    zc = z - mu
    var = jnp.mean(zc * zc, axis=-1, keepdims=True)
    return zc * lax.rsqrt(var + LN_EPS) * g + b


def _combine_rows(x1, ya, yb, route, g, b):
    g1 = route[:, 2:3]
    g2 = route[:, 3:4]
    a_hi, a_lo = _unpack_rows(ya)
    b_hi, b_lo = _unpack_rows(yb)
    m = jnp.concatenate([g1 * a_hi.astype(F32) + g2 * b_hi.astype(F32),
                         g1 * a_lo.astype(F32) + g2 * b_lo.astype(F32)], axis=1)
    return _layer_norm(DEEPNORM_ALPHA * x1 + m, g, b)


def _proj_kernel(x_ref, w_ref, cos_ref, sin_ref, qg_ref, kg_ref, onesq_ref,
                 pool_ref, rw_ref, q_ref, k_ref, v_ref):
    _proj_rows(x_ref[...], w_ref, cos_ref, sin_ref, qg_ref, kg_ref, onesq_ref,
               pool_ref, rw_ref, q_ref, k_ref, v_ref)


def _combine_proj_kernel(x1_ref, ya_ref, yb_ref, route_ref, lg_ref, lb_ref,
                         w_ref, cos_ref, sin_ref, qg_ref, kg_ref, onesq_ref,
                         x_ref, pool_ref, rw_ref, q_ref, k_ref, v_ref):
    x = _combine_rows(x1_ref[...], ya_ref[...], yb_ref[...], route_ref[...], lg_ref[...], lb_ref[...])
    x_ref[...] = x
    _proj_rows(x, w_ref, cos_ref, sin_ref, qg_ref, kg_ref, onesq_ref, pool_ref, rw_ref, q_ref, k_ref, v_ref)


def _proj_rows(x, w_ref, cos_ref, sin_ref, qg_ref, kg_ref, onesq_ref, pool_ref, rw_ref, q_ref, k_ref, v_ref):
    xb = x.astype(BF16)
    pool_ref[...] = _dot(xb, w_ref[:, 0:POOL_WIDTH])
    rw_ref[...] = _dot(xb, w_ref[:, POOL_WIDTH:POOL_WIDTH + RWKV_PAD])
    at = _dot(xb, w_ref[:, POOL_WIDTH + RWKV_PAD:PROJ_WIDTH])

    q = at[:, 0:ATTN_WIDTH]
    k = at[:, ATTN_WIDTH:ATTN_WIDTH + ATTN_KV_WIDTH]
    v = at[:, ATTN_WIDTH + ATTN_KV_WIDTH:ATTN_IN]
    ones_q = onesq_ref[...]
    ones_k = onesq_ref[0:ATTN_KV_WIDTH, 0:ATTN_KV_WIDTH]
    inv = 1.0 / HEAD_DIM
    qn = q * lax.rsqrt(_headsum(q * q, ones_q) * inv + QK_EPS) * qg_ref[...]
    kn = k * lax.rsqrt(_headsum(k * k, ones_k) * inv + QK_EPS) * kg_ref[...]
    cos = cos_ref[...]
    sin = sin_ref[...]
    q_ref[...] = (_rope(qn, cos, sin) * (HEAD_DIM ** -0.5 * math.log2(math.e))).astype(BF16)
    k_ref[...] = _rope(kn, cos[:, 0:ATTN_KV_WIDTH], sin[:, 0:ATTN_KV_WIDTH]).astype(BF16)
    v_ref[...] = v.astype(BF16)


def _proj(x2d, w, cos8, sin8, qg, kg, ones_q, seq, tm, prev=None):
    n = (x2d if prev is None else prev[0]).shape[0]
    nts = seq // tm
    nt = n // tm
    row = lambda width: pl.BlockSpec((tm, width), lambda i: (i, 0))
    const = lambda r, c: pl.BlockSpec((r, c), lambda i: (0, 0))
    proj_in = [const(D_MODEL, PROJ_WIDTH),
               pl.BlockSpec((tm, ATTN_WIDTH), lambda i: (i % nts, 0)),
               pl.BlockSpec((tm, ATTN_WIDTH), lambda i: (i % nts, 0)),
               const(1, ATTN_WIDTH), const(1, ATTN_KV_WIDTH), const(ATTN_WIDTH, ATTN_WIDTH)]
    proj_out = [row(POOL_WIDTH), row(RWKV_PAD), row(ATTN_WIDTH), row(ATTN_KV_WIDTH), row(ATTN_KV_WIDTH)]
    proj_shape = [jax.ShapeDtypeStruct((n, POOL_WIDTH), F32),
                  jax.ShapeDtypeStruct((n, RWKV_PAD), F32),
                  jax.ShapeDtypeStruct((n, ATTN_WIDTH), BF16),
                  jax.ShapeDtypeStruct((n, ATTN_KV_WIDTH), BF16),
                  jax.ShapeDtypeStruct((n, ATTN_KV_WIDTH), BF16)]
    if prev is None:
        return pl.pallas_call(
            _proj_kernel,
            grid=(nt,),
            in_specs=[row(D_MODEL)] + proj_in,
            out_specs=proj_out,
            out_shape=proj_shape,
            compiler_params=_cparams(("parallel",)),
            name="proj",
        )(x2d, w, cos8, sin8, qg, kg, ones_q)
    x1, yab, route, lg, lb = prev
    return pl.pallas_call(
        _combine_proj_kernel,
        grid=(nt,),
        in_specs=[row(D_MODEL), row(HALF_D), pl.BlockSpec((tm, HALF_D), lambda i: (i + nt, 0)),
                  row(ROUTE_LANES), const(1, D_MODEL), const(1, D_MODEL)] + proj_in,
        out_specs=[row(D_MODEL)] + proj_out,
        out_shape=[jax.ShapeDtypeStruct((n, D_MODEL), F32)] + proj_shape,
        compiler_params=_cparams(("parallel",)),
        name="combine_proj",
    )(x1, yab, yab, route, lg, lb, w, cos8, sin8, qg, kg, ones_q)


def _pool_kernel(u_ref, w_ref, scale_ref, o_ref):
    s = u_ref.shape[1]
    u = u_ref[0]
    t = lax.broadcasted_iota(jnp.int32, (s, POOL_WIDTH), 0)

    def down(x, k):
        return jnp.where(t >= k, pltpu.roll(x, k, 0), 0.0)

    def up(x, k):
        return jnp.where(t < s - k, pltpu.roll(x, s - k, 0), 0.0)

    left = [down(u, 1)]
    right = [u]
    for k in (1, 2, 4):
        left.append(left[-1] + down(left[-1], k))
        right.append(right[-1] + up(right[-1], k))
    sums = [l + r for l, r in zip(left, right)]

    grp = lax.broadcasted_iota(jnp.int32, (s, POOL_WIDTH), 1) // POOL_GROUP
    half = jnp.where(grp == 0, 1, jnp.where(grp == 1, 2, jnp.where(grp == 2, 4, 8)))
    cnt = (jnp.minimum(t + half, s) - jnp.maximum(t - half, 0)).astype(F32)
    tot = jnp.where(grp == 0, sums[0], jnp.where(grp == 1, sums[1], jnp.where(grp == 2, sums[2], sums[3])))
    d = tot / cnt - u
    o_ref[0] = _dot(d.astype(BF16), w_ref[...]) * scale_ref[...]


def _pool(u3d, w_bd, scale):
    b, s, _ = u3d.shape
    return pl.pallas_call(
        _pool_kernel,
        grid=(b,),
        in_specs=[pl.BlockSpec((1, s, POOL_WIDTH), lambda i: (i, 0, 0)),
                  pl.BlockSpec((POOL_WIDTH, POOL_WIDTH), lambda i: (0, 0)),
                  pl.BlockSpec((1, POOL_WIDTH), lambda i: (0, 0))],
        out_specs=pl.BlockSpec((1, s, POOL_WIDTH), lambda i: (i, 0, 0)),
        out_shape=jax.ShapeDtypeStruct((b, s, POOL_WIDTH), F32),
        compiler_params=_cparams(("parallel",)),
        name="pool",
    )(u3d, w_bd, scale)


_SHIFT_HALO = 8


def _rwkv_prep_kernel(cur_ref, prev_ref, next_ref, mup_ref, mun_ref, wl_ref, bias_ref,
                      kk_ref, ka_ref, rk_ref, ones_ref,
                      sh_ref, lw_ref, kn_ref, post_ref):
    ts = cur_ref.shape[1]
    ti = pl.program_id(1)
    nt = pl.num_programs(1)
    cur = cur_ref[0]
    row = lax.broadcasted_iota(jnp.int32, cur.shape, 0)
    before = jnp.where(ti > 0, prev_ref[0, _SHIFT_HALO - 1:_SHIFT_HALO, :], 0.0)
    after = jnp.where(ti < nt - 1, next_ref[0, 0:1, :], 0.0)
    prev = jnp.where(row == 0, before, pltpu.roll(cur, 1, 0))
    nxt = jnp.where(row == ts - 1, after, pltpu.roll(cur, ts - 1, 0))
    f = cur + mup_ref[...] * (prev - cur) + mun_ref[...] * (nxt - cur)

    w = RWKV_WIDTH
    r = f[:, 0:w]
    k = f[:, w:2 * w]
    v = f[:, 2 * w:3 * w]
    lora = f[:, 3 * w:4 * w]
    lane = lax.broadcasted_iota(jnp.int32, lora.shape, 1)
    z = jnp.where(lane < 2 * DECAY_LORA, jnp.tanh(lora),
                  jnp.where(lane < 2 * DECAY_LORA + 2 * AAA_LORA, lora,
                            jnp.where(lane < 2 * DECAY_LORA + 2 * AAA_LORA + GATE_LORA,
                                      jax.nn.sigmoid(lora), 0.0)))
    up = _dot(z.astype(BF16), wl_ref[...]) + bias_ref[...]
    ones = ones_ref[...]

    kk0 = k * kk_ref[...]
    nrm = jnp.sqrt(_headsum(kk0 * kk0, ones))
    kk = kk0 / jnp.maximum(nrm, 1e-12)
    sh_ref[0, :, 0:w] = r.astype(BF16)
    sh_ref[0, :, w:2 * w] = v.astype(BF16)
    sh_ref[0, :, 2 * w:3 * w] = kk.astype(BF16)

    ksum = jnp.zeros_like(k)
    for di in range(2):
        logw = -math.exp(-0.5) * jax.nn.sigmoid(up[:, di * w:(di + 1) * w])
        a = jax.nn.sigmoid(up[:, (2 + di) * w:(3 + di) * w])
        kh = k * (1.0 + (a - 1.0) * ka_ref[...])
        ksum = ksum + kh
        lw_ref[di, 0] = logw
        kn_ref[di, 0, :, 0:w] = kh.astype(BF16)
        kn_ref[di, 0, :, w:2 * w] = (-(a * kk)).astype(BF16)
    bonus = _headsum(r * ksum * rk_ref[...], ones) * v
    post_ref[0, :, 0:w] = bonus
    post_ref[0, :, w:2 * w] = up[:, 4 * w:5 * w]


def _rwkv_prep(rw3d, mup, mun, wl, bias, k_k, k_a, r_k, ones, ts):
    b, s, _ = rw3d.shape
    nt = s // ts
    hb = ts // _SHIFT_HALO
    nhb = s // _SHIFT_HALO
    w = RWKV_WIDTH
    vec = lambda width: pl.BlockSpec((1, width), lambda i, j: (0, 0))
    return pl.pallas_call(
        _rwkv_prep_kernel,
        grid=(b, nt),
        in_specs=[pl.BlockSpec((1, ts, RWKV_PAD), lambda i, j: (i, j, 0)),
                  pl.BlockSpec((1, _SHIFT_HALO, RWKV_PAD),
                               lambda i, j: (i, jnp.maximum(j * hb - 1, 0), 0)),
                  pl.BlockSpec((1, _SHIFT_HALO, RWKV_PAD),
                               lambda i, j: (i, jnp.minimum((j + 1) * hb, nhb - 1), 0)),
                  vec(RWKV_PAD), vec(RWKV_PAD),
                  pl.BlockSpec((w, 5 * w), lambda i, j: (0, 0)),
                  vec(5 * w), vec(w), vec(w), vec(w),
                  pl.BlockSpec((w, w), lambda i, j: (0, 0))],
        out_specs=[pl.BlockSpec((1, ts, 3 * w), lambda i, j: (i, j, 0)),
                   pl.BlockSpec((2, 1, ts, w), lambda i, j: (0, i, j, 0)),
                   pl.BlockSpec((2, 1, ts, 2 * w), lambda i, j: (0, i, j, 0)),
                   pl.BlockSpec((1, ts, 2 * w), lambda i, j: (i, j, 0))],
        out_shape=[jax.ShapeDtypeStruct((b, s, 3 * w), BF16),
                   jax.ShapeDtypeStruct((2, b, s, w), F32),
                   jax.ShapeDtypeStruct((2, b, s, 2 * w), BF16),
                   jax.ShapeDtypeStruct((b, s, 2 * w), F32)],
        compiler_params=_cparams(("parallel", "parallel")),
        name="rwkv_prep",
    )(rw3d, rw3d, rw3d, mup, mun, wl, bias, k_k, k_a, r_k, ones)


def _expand_bd(x_bf16, mask_bd):
    return jnp.where(mask_bd, jnp.concatenate([x_bf16] * RWKV_HEADS, axis=0), jnp.zeros((), BF16))


def _wkv_kernel(shf_ref, shr_ref, lwf_ref, lwr_ref, knf_ref, knr_ref, yf_ref, yr_ref, hf_ref, hr_ref):
    c = WKV_CHUNK
    w = RWKV_WIDTH
    tt = shf_ref.shape[1]
    ncs = tt // c

    @pl.when(pl.program_id(1) == 0)
    def _():
        hf_ref[...] = jnp.zeros_like(hf_ref)
        hr_ref[...] = jnp.zeros_like(hr_ref)

    row_c = lax.broadcasted_iota(jnp.int32, (c, c), 0)
    col_c = lax.broadcasted_iota(jnp.int32, (c, c), 1)
    t_i = lax.broadcasted_iota(jnp.int32, (c, w), 0)
    s_i = lax.broadcasted_iota(jnp.int32, (c, w), 1) % c
    eye_c = (s_i == t_i).astype(F32)
    row_w = lax.broadcasted_iota(jnp.int32, (w, w), 0)
    col_w = lax.broadcasted_iota(jnp.int32, (w, w), 1)
    mask_bd = (row_w // c) == (col_w // HEAD_DIM)
    eye_w = row_w == col_w
    tri_incl = [jnp.where(col_c <= row_c, 1.0, 0.0).astype(BF16),
                jnp.where(col_c >= row_c, 1.0, 0.0).astype(BF16)]
    strict = [s_i < t_i, s_i > t_i]
    incl = [s_i <= t_i, s_i >= t_i]

    def bf(x):
        return x.astype(BF16)

    def bd(x):
        return _expand_bd(bf(x), mask_bd)

    chunks = []
    for ci in range(ncs):
        chunks.append((0, ci * c, shf_ref, lwf_ref, knf_ref))
        chunks.append((1, (ncs - 1 - ci) * c, shr_ref, lwr_ref, knr_ref))

    st = []
    for d, start, sh_ref, lw_ref, kn_ref in chunks:
        rows = pl.ds(start, c)
        st.append(dict(d=d, rows=rows,
                       r=sh_ref[0, rows, 0:w].astype(F32), v_bf=sh_ref[0, rows, w:2 * w],
                       kk=sh_ref[0, rows, 2 * w:3 * w].astype(F32), lw=lw_ref[0, rows, :],
                       kh=kn_ref[0, rows, 0:w].astype(F32), nb=kn_ref[0, rows, w:2 * w].astype(F32)))

    for q in st:
        l_hi, l_mid, l_lo = _split3(q["lw"])
        tri = tri_incl[q["d"]]
        q["cum"] = _dot(tri, l_hi) + _dot(tri, l_mid) + _dot(tri, l_lo)
    for q in st:
        cum, lw = q["cum"], q["lw"]
        tot = jnp.sum(lw, axis=0, keepdims=True)
        e_inv = jnp.exp(-cum)
        e_end = jnp.exp(tot - cum)
        q["tot"] = tot
        q["a_bar"] = q["kk"] * jnp.exp(cum - lw)
        q["r_bar"] = q["r"] * jnp.exp(cum)
        q["b_hat"] = bf(q["nb"] * e_end)
        q["k_hat"] = bf(q["kh"] * e_end)
        q["v_bd"] = _expand_bd(q["v_bf"], mask_bd)
        q["lhs"] = jnp.concatenate([bf(q["a_bar"]), bf(q["r_bar"])], axis=0)
        q["rhs"] = jnp.concatenate([bd(q["nb"] * e_inv), bd(q["kh"] * e_inv)], axis=0)
    for q in st:
        sc = _dot_nt(q["lhs"], q["rhs"])
        sm, im = strict[q["d"]], incl[q["d"]]
        q["x"] = jnp.where(sm, sc[0:c, 0:w], 0.0)
        q["a_ak"] = bf(jnp.where(sm, sc[0:c, w:2 * w], 0.0))
        q["m_rb"] = bf(jnp.where(im, sc[c:2 * c, 0:w], 0.0))
        q["m_rk"] = bf(jnp.where(im, sc[c:2 * c, w:2 * w], 0.0))
        q["tinv"] = eye_c + q["x"]

    for q in st:
        q["x"] = _dot(bf(q["x"]), bd(q["x"]))
    n_lv = int(math.log2(c)) - 1
    for lv in range(n_lv):
        last = lv == n_lv - 1
        for q in st:
            xb = bd(q["x"])
            if last:
                q["tinv"] = q["tinv"] + _dot(bf(q["tinv"]), xb)
            else:
                res = _dot(jnp.concatenate([bf(q["x"]), bf(q["tinv"])], axis=0), xb)
                q["tinv"] = q["tinv"] + res[c:2 * c]
                q["x"] = res[0:c]
    for q in st:
        res = _dot(jnp.concatenate([q["a_ak"], q["m_rk"]], axis=0), q["v_bd"])
        q["akv"] = res[0:c]
        q["mrkv"] = res[c:2 * c]
    for q in st:
        tinv_bf = bf(q["tinv"])
        q["a_pr"] = bf(_dot(tinv_bf, bd(q["a_bar"])))
        q["v_pr"] = bf(_dot(tinv_bf, bd(q["akv"])))
    for q in st:
        q["r_pr"] = bf(q["r_bar"] + _dot(q["m_rb"], _expand_bd(q["a_pr"], mask_bd)))
        q["y_pr"] = _dot(q["m_rb"], _expand_bd(q["v_pr"], mask_bd)) + q["mrkv"]
    for q in st:
        q["p_bd"] = bf(jnp.where(eye_w, jnp.exp(q["tot"]), 0.0)
                       + jnp.where(mask_bd, _dot_tn(q["b_hat"], q["a_pr"]), 0.0))
        q["q_bd"] = jnp.where(mask_bd,
                              _dot_tn(jnp.concatenate([q["b_hat"], q["k_hat"]], axis=0),
                                      jnp.concatenate([q["v_pr"], q["v_bf"]], axis=0)), 0.0)

    h = [hf_ref[...], hr_ref[...]]
    y_refs = [yf_ref, yr_ref]
    for q in st:
        d = q["d"]
        res = _dot(jnp.concatenate([q["r_pr"], q["p_bd"]], axis=0), bf(h[d]))
        y_refs[d][0, q["rows"], :] = res[0:c] + q["y_pr"]
        h[d] = res[c:c + w] + q["q_bd"]
    hf_ref[...] = h[0]
    hr_ref[...] = h[1]


def _wkv(shared, logw, khnb, tt):
    b, s, _ = shared.shape
    nt = s // tt
    w = RWKV_WIDTH
    fwd = lambda i, j: (i, j, 0)
    bwd = lambda i, j: (i, nt - 1 - j, 0)
    return pl.pallas_call(
        _wkv_kernel,
        grid=(b, nt),
        in_specs=[pl.BlockSpec((1, tt, 3 * w), fwd), pl.BlockSpec((1, tt, 3 * w), bwd),
                  pl.BlockSpec((None, 1, tt, w), lambda i, j: (0, i, j, 0)),
                  pl.BlockSpec((None, 1, tt, w), lambda i, j: (1, i, nt - 1 - j, 0)),
                  pl.BlockSpec((None, 1, tt, 2 * w), lambda i, j: (0, i, j, 0)),
                  pl.BlockSpec((None, 1, tt, 2 * w), lambda i, j: (1, i, nt - 1 - j, 0))],
        out_specs=[pl.BlockSpec((1, tt, w), fwd), pl.BlockSpec((1, tt, w), bwd)],
        out_shape=[jax.ShapeDtypeStruct((b, s, w), F32), jax.ShapeDtypeStruct((b, s, w), F32)],
        scratch_shapes=[pltpu.VMEM((w, w), F32), pltpu.VMEM((w, w), F32)],
        compiler_params=_cparams(("parallel", "arbitrary")),
        name="wkv",
    )(shared, shared, logw, logw, khnb, khnb)


def _attn_kernel(q_ref, k_ref, v_ref, o_ref):
    tq = q_ref.shape[0]
    k = k_ref[...]
    v = v_ref[...]
    lane = lax.broadcasted_iota(jnp.int32, (tq, ATTN_KV_WIDTH), 1)
    low = lane < HEAD_DIM
    zero = jnp.zeros((), BF16)
    for j in range(ATTN_HEADS // 2):
        qb = q_ref[:, j * ATTN_KV_WIDTH:(j + 1) * ATTN_KV_WIDTH]
        outs = []
        for half in range(2):
            qm = jnp.where(low if half == 0 else ~low, qb, zero)
            s = _dot_nt(qm, k)
            m = jnp.max(s, axis=-1, keepdims=True)
            p = jnp.exp2(s - m)
            l = jnp.sum(p, axis=-1, keepdims=True)
            outs.append(_dot(p.astype(BF16), v) / l)
        o_ref[:, j * ATTN_KV_WIDTH:(j + 1) * ATTN_KV_WIDTH] = jnp.where(low, outs[0], outs[1]).astype(BF16)


def _attn(q, k, v, seq, tq):
    n = q.shape[0]
    nq = seq // tq
    return pl.pallas_call(
        _attn_kernel,
        grid=(n // seq, nq),
        in_specs=[pl.BlockSpec((tq, ATTN_WIDTH), lambda b, i: (b * nq + i, 0)),
                  pl.BlockSpec((seq, ATTN_KV_WIDTH), lambda b, i: (b, 0)),
                  pl.BlockSpec((seq, ATTN_KV_WIDTH), lambda b, i: (b, 0))],
        out_specs=pl.BlockSpec((tq, ATTN_WIDTH), lambda b, i: (b * nq + i, 0)),
        out_shape=jax.ShapeDtypeStruct((n, ATTN_WIDTH), BF16),
        compiler_params=_cparams(("parallel", "parallel")),
        name="attn",
    )(q, k, v)


def _outproj_kernel(x_ref, pool_ref, wkvf_ref, wkvr_ref, post_ref, attn_ref, wo_ref, gng_ref, gnb_ref,
                    ones_ref, lg_ref, lb_ref, wrh_ref, wrl_ref, br_ref,
                    x1_ref, x1p_ref, route_ref):
    w = RWKV_WIDTH
    ones = ones_ref[...]
    wkv = wkvf_ref[...] + wkvr_ref[...]
    inv = 1.0 / HEAD_DIM
    mu = _headsum(wkv, ones) * inv
    cen = wkv - mu
    var = _headsum(cen * cen, ones) * inv
    yr = cen * lax.rsqrt(var + GN_EPS) * gng_ref[...] + gnb_ref[...] + post_ref[:, 0:w]
    yr = yr * post_ref[:, w:2 * w]
    y = (_dot(pool_ref[...].astype(BF16), wo_ref[0:POOL_WIDTH, :])
         + _dot(yr.astype(BF16), wo_ref[POOL_WIDTH:POOL_WIDTH + w, :])
         + _dot(attn_ref[...], wo_ref[POOL_WIDTH + w:D_MODEL, :]))
    x1 = _layer_norm(DEEPNORM_ALPHA * x_ref[...] + y, lg_ref[...], lb_ref[...])
    x1_ref[...] = x1
    x1p_ref[...] = _pack_rows(x1)

    xh, xl = _split2(x1)
    logits = (_dot(xh, wrh_ref[...]) + _dot(xl, wrh_ref[...]) + _dot(xh, wrl_ref[...])) + br_ref[...]
    lane = lax.broadcasted_iota(jnp.int32, logits.shape, 1)
    lane_f = lane.astype(F32)
    lane_grp_f = ((lane - N_GROUPS) // EXPERTS_PER_GROUP).astype(F32)
    neg = -jnp.inf
    big = jnp.float32(1 << 20)
    gl = jnp.where(lane < N_GROUPS, logits, neg)
    gmax = jnp.max(gl, axis=-1, keepdims=True)
    grp = jnp.min(jnp.where(gl == gmax, lane_f, big), axis=-1, keepdims=True)
    gw = 1.0 / jnp.sum(jnp.exp(gl - gmax), axis=-1, keepdims=True)
    in_grp = (lane >= N_GROUPS) & (lane < N_GROUPS + N_EXPERTS) & (lane_grp_f == grp)
    el = jnp.where(in_grp, logits, neg)
    v1 = jnp.max(el, axis=-1, keepdims=True)
    i1 = jnp.min(jnp.where(el == v1, lane_f, big), axis=-1, keepdims=True)
    el2 = jnp.where(lane_f == i1, neg, el)
    v2 = jnp.max(el2, axis=-1, keepdims=True)
    i2 = jnp.min(jnp.where(el2 == v2, lane_f, big), axis=-1, keepdims=True)
    e21 = jnp.exp(v2 - v1)
    g1 = gw / (1.0 + e21)
    g2 = gw * e21 / (1.0 + e21)
    route = jnp.where(lane == 0, i1 - N_GROUPS,
                      jnp.where(lane == 1, i2 - N_GROUPS,
                                jnp.where(lane == 2, g1, jnp.where(lane == 3, g2, 0.0))))
    route_ref[...] = route


def _outproj(x2d, pool2d, wkvf, wkvr, post2d, attn2d, wo, gng, gnb, ones, lg, lb, wrh, wrl, br, tm):
    n = x2d.shape[0]
    w = RWKV_WIDTH
    row = lambda width: pl.BlockSpec((tm, width), lambda i: (i, 0))
    vec = lambda width: pl.BlockSpec((1, width), lambda i: (0, 0))
    return pl.pallas_call(
        _outproj_kernel,
        grid=(n // tm,),
        in_specs=[row(D_MODEL), row(POOL_WIDTH), row(w), row(w),
                  row(2 * w), row(ATTN_WIDTH),
                  pl.BlockSpec((D_MODEL, D_MODEL), lambda i: (0, 0)),
                  vec(w), vec(w),
                  pl.BlockSpec((w, w), lambda i: (0, 0)),
                  vec(D_MODEL), vec(D_MODEL),
                  pl.BlockSpec((D_MODEL, ROUTE_LANES), lambda i: (0, 0)),
                  pl.BlockSpec((D_MODEL, ROUTE_LANES), lambda i: (0, 0)),
                  vec(ROUTE_LANES)],
        out_specs=[row(D_MODEL), row(HALF_D), row(ROUTE_LANES)],
        out_shape=[jax.ShapeDtypeStruct((n, D_MODEL), F32),
                   jax.ShapeDtypeStruct((n, HALF_D), U32),
                   jax.ShapeDtypeStruct((n, ROUTE_LANES), F32)],
        compiler_params=_cparams(("parallel",)),
        name="outproj",
    )(x2d, pool2d, wkvf, wkvr, post2d, attn2d, wo, gng, gnb, ones, lg, lb, wrh, wrl, br)


def _expert_kernel(bexp_ref, nused_ref, xs_ref, wg_ref, wu_ref, wd_ref, *rest):
    ys_ref, wgb_ref, wub_ref, wdb_ref = rest[-4:]
    i = pl.program_id(0)
    used = i < nused_ref[0]

    @pl.when(used & ((i == 0) | (bexp_ref[i] != bexp_ref[jnp.maximum(i - 1, 0)])))
    def _():
        wgb_ref[...] = wg_ref[0].astype(BF16)
        wub_ref[...] = wu_ref[0].astype(BF16)
        wdb_ref[...] = wd_ref[0].astype(BF16)

    @pl.when(used)
    def _():
        xa, xb = _unpack_rows(xs_ref[...])
        h1 = _dot(xa, wgb_ref[0:HALF_D, :]) + _dot(xb, wgb_ref[HALF_D:D_MODEL, :])
        h2 = _dot(xa, wub_ref[0:HALF_D, :]) + _dot(xb, wub_ref[HALF_D:D_MODEL, :])
        h = (h1 * jax.nn.sigmoid(h1) * h2).astype(BF16)
        ys_ref[...] = _pack_rows(_dot(h, wdb_ref[...]))

    @pl.when(jnp.logical_not(used))
    def _():
        ys_ref[...] = jnp.zeros_like(ys_ref)


def _experts(block_exp, n_used, xs_parts, wg, wu, wd, layer):
    p = sum(xs.shape[0] for xs in xs_parts)
    wmap = lambda i, be, nu: (layer, be[i], 0, 0)
    ys = None
    first = 0
    for xs in xs_parts:
        nb = xs.shape[0] // MOE_BLOCK
        in_specs = [pl.BlockSpec((MOE_BLOCK, HALF_D), lambda i, be, nu: (i, 0)),
                    pl.BlockSpec((None, 1, D_MODEL, EXPERT_HIDDEN), wmap),
                    pl.BlockSpec((None, 1, D_MODEL, EXPERT_HIDDEN), wmap),
                    pl.BlockSpec((None, 1, EXPERT_HIDDEN, D_MODEL), wmap)]
        args = [block_exp[first:first + nb], n_used - first, xs, wg, wu, wd]
        aliases = {}
        if ys is not None:
            in_specs.append(pl.BlockSpec(memory_space=pl.ANY))
            aliases = {len(args): 0}
            args.append(ys)
        grid_spec = pltpu.PrefetchScalarGridSpec(
            num_scalar_prefetch=2,
            grid=(nb,),
            in_specs=in_specs,
            out_specs=pl.BlockSpec((MOE_BLOCK, HALF_D), functools.partial(_block_at, first)),
            scratch_shapes=[pltpu.VMEM((D_MODEL, EXPERT_HIDDEN), BF16),
                            pltpu.VMEM((D_MODEL, EXPERT_HIDDEN), BF16),
                            pltpu.VMEM((EXPERT_HIDDEN, D_MODEL), BF16)],
        )
        ys = pl.pallas_call(
            _expert_kernel,
            grid_spec=grid_spec,
            out_shape=jax.ShapeDtypeStruct((p, HALF_D), U32),
            input_output_aliases=aliases,
            compiler_params=_cparams(("arbitrary",)),
            name="experts",
        )(*args)
        first += nb
    return ys


def _block_at(first, i, be, nu):
    return (i + first, 0)


SC_CORES = 2
SC_SUBCORES = 16
SC_GATHER_ROWS = 64


def _sc_gather(table, idx):
    v, d = table.shape
    b = idx.shape[0]
    nw = SC_CORES * SC_SUBCORES
    ch = SC_GATHER_ROWS
    assert b % (nw * ch * 2) == 0
    b_per_w = b // nw
    nch = b_per_w // ch
    mesh = plsc.VectorSubcoreMesh(core_axis_name="c", subcore_axis_name="s")

    @functools.partial(
        pl.kernel, mesh=mesh,
        out_type=jax.ShapeDtypeStruct((b, d), table.dtype),
        scratch_types=[pltpu.VMEM((b_per_w,), jnp.int32),
                       pltpu.VMEM((2, ch, d), table.dtype),
                       pltpu.SemaphoreType.DMA, pltpu.SemaphoreType.DMA,
                       pltpu.SemaphoreType.DMA, pltpu.SemaphoreType.DMA],
    )
    def gather_kernel(table_hbm, idx_hbm, out_hbm, idx_v, rows_v, gsem0, gsem1, wsem0, wsem1):
        wid = lax.axis_index("s") * SC_CORES + lax.axis_index("c")
        base = wid * b_per_w
        pltpu.sync_copy(idx_hbm.at[pl.ds(base, b_per_w)], idx_v)
        gsem = (gsem0, gsem1)
        wsem = (wsem0, wsem1)

        def gather(jj, slot):
            off = pl.multiple_of(jj * ch, ch)
            return pltpu.make_async_copy(table_hbm.at[idx_v.at[pl.ds(off, ch)]], rows_v.at[slot], gsem[slot])

        def write(jj, slot):
            off = pl.multiple_of(jj * ch, ch)
            return pltpu.make_async_copy(rows_v.at[slot], out_hbm.at[pl.ds(base + off, ch)], wsem[slot])

        gather(0, 0).start()

        @pl.loop(0, nch, step=2)
        def _(j):
            for slot in range(2):
                jj = j + slot

                gather(jj, slot).wait()

                @pl.when(jj >= 1)
                def _():
                    write(jj - 1, 1 - slot).wait()

                @pl.when(jj + 1 < nch)
                def _():
                    gather(jj + 1, 1 - slot).start()

                write(jj, slot).start()

        write(nch - 1, (nch - 1) % 2).wait()

    return gather_kernel(table, idx)


def _gather_rows(table, idx):
    b = idx.shape[0]
    unit = SC_CORES * SC_SUBCORES * SC_GATHER_ROWS * 2
    bp = -(-b // unit) * unit
    if bp == b:
        return _sc_gather(table, idx)
    return _sc_gather(table, jnp.pad(idx, (0, bp - b)))[:b]


def _combine_kernel(x1_ref, ya_ref, yb_ref, route_ref, lg_ref, lb_ref, o_ref):
    o_ref[...] = _combine_rows(x1_ref[...], ya_ref[...], yb_ref[...], route_ref[...], lg_ref[...], lb_ref[...])


def _combine(x1, yab, route, lg, lb, tm):
    n = x1.shape[0]
    nt = n // tm
    row = lambda width: pl.BlockSpec((tm, width), lambda i: (i, 0))
    vec = lambda width: pl.BlockSpec((1, width), lambda i: (0, 0))
    return pl.pallas_call(
        _combine_kernel,
        grid=(nt,),
        in_specs=[row(D_MODEL), row(HALF_D), pl.BlockSpec((tm, HALF_D), lambda i: (i + nt, 0)),
                  row(ROUTE_LANES), vec(D_MODEL), vec(D_MODEL)],
        out_specs=row(D_MODEL),
        out_shape=jax.ShapeDtypeStruct((n, D_MODEL), F32),
        compiler_params=_cparams(("parallel",)),
        name="combine",
    )(x1, yab, yab, route, lg, lb)


def _q_perm():
    order = [h for j in range(ATTN_HEADS // 2) for h in (j, j + ATTN_HEADS // 2)]
    return jnp.concatenate([jnp.arange(h * HEAD_DIM, (h + 1) * HEAD_DIM) for h in order])


def _block_diag(blocks):
    n = len(blocks)
    r, c = blocks[0].shape
    out = jnp.zeros((n * r, n * c), blocks[0].dtype)
    for i, blk in enumerate(blocks):
        out = out.at[i * r:(i + 1) * r, i * c:(i + 1) * c].set(blk)
    return out


def _rope_tables(seq):
    rows = seq // GRID_W
    row_id = jnp.repeat(jnp.arange(rows), GRID_W).astype(F32)
    col_id = jnp.tile(jnp.arange(GRID_W), rows).astype(F32)
    half = HEAD_DIM // 2
    inv_freq = ROPE_THETA ** (-jnp.arange(0, half, 2, dtype=F32) / half)
    ang_r = row_id[:, None] * inv_freq
    ang_c = col_id[:, None] * inv_freq
    ang = jnp.concatenate([ang_r, ang_r, ang_c, ang_c], -1)
    sign = jnp.where((jnp.arange(HEAD_DIM) // 16) % 2 == 0, -1.0, 1.0).astype(F32)
    cos8 = jnp.tile(jnp.cos(ang), (1, ATTN_HEADS))
    sin8 = jnp.tile(jnp.sin(ang) * sign, (1, ATTN_HEADS))
    return cos8, sin8


def _dispatch(route, n):
    m = n * 2
    experts = jnp.arange(N_EXPERTS, dtype=jnp.int32)
    e_flat = jnp.concatenate([route[:, 0], route[:, 1]]).astype(jnp.int32)
    pair = jnp.arange(m, dtype=jnp.int32)
    e_sorted, order = lax.sort((e_flat, pair), num_keys=1, is_stable=True)
    start = jnp.sum(e_sorted[None, :] < experts[:, None], axis=1, dtype=jnp.int32)
    counts = jnp.concatenate([start[1:], jnp.full((1,), m, jnp.int32)]) - start
    padded = (counts + MOE_BLOCK - 1) // MOE_BLOCK * MOE_BLOCK
    ends_p = jnp.sum(jnp.where(experts[None, :] <= experts[:, None], padded[None, :], 0), axis=1, dtype=jnp.int32)
    pstart = ends_p - padded
    delta = pstart - start
    n_blocks = -(-(m + N_EXPERTS * (MOE_BLOCK - 1)) // MOE_BLOCK)
    block_start = jnp.arange(n_blocks, dtype=jnp.int32) * MOE_BLOCK
    block_exp = jnp.minimum(jnp.sum(ends_p[None, :] <= block_start[:, None], axis=1, dtype=jnp.int32),
                            N_EXPERTS - 1)
    lane = jnp.arange(MOE_BLOCK, dtype=jnp.int32)[None, :]
    blk = block_exp[:, None] == experts[None, :]

    def per_block(table):
        return jnp.sum(jnp.where(blk, table[None, :], 0), axis=1, dtype=jnp.int32)

    valid = (block_start - per_block(pstart))[:, None] + lane < per_block(counts)[:, None]
    src = jnp.clip((block_start - per_block(delta))[:, None] + lane, 0, m - 1)
    run = order[src.reshape(-1)].reshape(n_blocks, MOE_BLOCK)
    row_tok = jnp.where(valid, run % n, (block_start[:, None] + lane) % n).reshape(-1)
    n_used = (ends_p[-1] // MOE_BLOCK).astype(jnp.int32).reshape(1)
    dest = pair + jnp.sum(jnp.where(e_sorted[None, :] == experts[:, None], delta[:, None], 0), axis=0,
                          dtype=jnp.int32)
    _, pos = lax.sort((order, dest), num_keys=1)
    return row_tok, pos, block_exp, n_used


def kernel(x, w_in, mu_prev, mu_next, pool_w, pool_scale, rw_w0, rw_w_up, rw_a0, rw_a_up, rw_g_up, rw_k_k, rw_k_a, rw_r_k, rw_gn_g, rw_gn_b, q_norm, k_norm, w_o, ln1_g, ln1_b, router_group, router_group_b, router_expert, router_expert_b, exp_gate, exp_up, exp_down, ln2_g, ln2_b):
    b, s, d = x.shape
    n = b * s
    w = RWKV_WIDTH
    tm = min(ROW_TILE, s)
    qperm = _q_perm()
    cos8, sin8 = _rope_tables(s)
    ones_q = _block_diag([jnp.ones((HEAD_DIM, HEAD_DIM), BF16)] * ATTN_HEADS)
    ones_r = ones_q[0:w, 0:w]
    a_end = POOL_WIDTH
    b_end = POOL_WIDTH + RWKV_IN

    xc = x.reshape(n, d)
    pending = None
    for l in range(DEPTH):
        wq = w_in[l][:, b_end:b_end + ATTN_WIDTH][:, qperm]
        w_proj = jnp.concatenate(
            [w_in[l][:, :a_end], w_in[l][:, a_end:b_end],
             jnp.zeros((d, RWKV_PAD - RWKV_IN), F32), wq, w_in[l][:, b_end + ATTN_WIDTH:]],
            axis=1).astype(BF16)
        pool_bd = _block_diag([pool_w[l, g] for g in range(len(POOL_WINDOWS))]).astype(BF16)
        pad_vec = jnp.zeros((RWKV_PAD - RWKV_IN,), F32)
        mup = jnp.concatenate([mu_prev[l], pad_vec]).reshape(1, RWKV_PAD)
        mun = jnp.concatenate([mu_next[l], pad_vec]).reshape(1, RWKV_PAD)
        w_lora = jnp.zeros((w, 5 * w), F32)
        for di in range(2):
            w_lora = w_lora.at[di * DECAY_LORA:(di + 1) * DECAY_LORA, di * w:(di + 1) * w].set(rw_w_up[l, di])
            o = 2 * DECAY_LORA
            w_lora = w_lora.at[o + di * AAA_LORA:o + (di + 1) * AAA_LORA, (2 + di) * w:(3 + di) * w].set(rw_a_up[l, di])
        o = 2 * DECAY_LORA + 2 * AAA_LORA
        w_lora = w_lora.at[o:o + GATE_LORA, 4 * w:5 * w].set(rw_g_up[l]).astype(BF16)
        lora_bias = jnp.concatenate([rw_w0[l, 0], rw_w0[l, 1], rw_a0[l, 0], rw_a0[l, 1],
                                     jnp.zeros((w,), F32)]).reshape(1, 5 * w)
        qg = jnp.tile(q_norm[l], ATTN_HEADS).reshape(1, ATTN_WIDTH)
        kg = jnp.tile(k_norm[l], 2).reshape(1, ATTN_KV_WIDTH)
        wo_attn = w_o[l][POOL_WIDTH + w:][qperm]
        wo = jnp.concatenate([w_o[l][:POOL_WIDTH + w], wo_attn], axis=0).astype(BF16)
        wr = jnp.concatenate([router_group[l], router_expert[l],
                              jnp.zeros((d, ROUTE_LANES - N_GROUPS - N_EXPERTS), F32)], axis=1)
        wrh = wr.astype(BF16)
        wrl = (wr - wrh.astype(F32)).astype(BF16)
        br = jnp.concatenate([router_group_b[l], router_expert_b[l],
                              jnp.zeros((ROUTE_LANES - N_GROUPS - N_EXPERTS,), F32)]).reshape(1, ROUTE_LANES)

        if pending is None:
            pool_in, rw_in, qh, kh, vh = _proj(xc, w_proj, cos8, sin8, qg, kg, ones_q, s, tm)
        else:
            xc, pool_in, rw_in, qh, kh, vh = _proj(None, w_proj, cos8, sin8, qg, kg, ones_q, s, tm, prev=pending)
        y_pool = _pool(pool_in.reshape(b, s, POOL_WIDTH), pool_bd, pool_scale[l].reshape(1, POOL_WIDTH))
        shared, logw, khnb, post = _rwkv_prep(rw_in.reshape(b, s, RWKV_PAD), mup, mun, w_lora, lora_bias,
                                        rw_k_k[l].reshape(1, w), rw_k_a[l].reshape(1, w),
                                        rw_r_k[l].reshape(1, w), ones_r, tm)
        wkv_f, wkv_r = _wkv(shared, logw, khnb, min(WKV_TILE, s))
        y_attn = _attn(qh, kh, vh, s, min(ATTN_Q_TILE, s))
        x1, x1p, route = _outproj(xc, y_pool.reshape(n, POOL_WIDTH), wkv_f.reshape(n, w), wkv_r.reshape(n, w),
                                  post.reshape(n, 2 * w), y_attn, wo,
                                  rw_gn_g[l].reshape(1, w), rw_gn_b[l].reshape(1, w), ones_r,
                                  ln1_g[l].reshape(1, d), ln1_b[l].reshape(1, d), wrh, wrl, br, tm)

        row_tok, pos, block_exp, n_used = _dispatch(route, n)
        split = block_exp.shape[0] // 2 * MOE_BLOCK
        xs_parts = [_gather_rows(x1p, row_tok[:split]), _gather_rows(x1p, row_tok[split:])]
        ys = _experts(block_exp, n_used, xs_parts, exp_gate, exp_up, exp_down, l)
        yab = _gather_rows(ys, pos)
        pending = (x1, yab, route, ln2_g[l].reshape(1, d), ln2_b[l].reshape(1, d))
    return _combine(*pending, tm).reshape(b, s, d)
```

```python
import functools
import math

import jax
import jax.numpy as jnp
from jax import lax
from jax.experimental import pallas as pl
from jax.experimental.pallas import tpu as pltpu
from jax.experimental.pallas import tpu_sc as plsc

F32 = jnp.float32
BF16 = jnp.bfloat16

D_MODEL = 1024
DEPTH = 4
GRID_W = 64
HEAD_DIM = 64
POOL_WIDTH = 256
POOL_WINDOWS = (2, 4, 8, 16)
POOL_GROUP = 64
RWKV_WIDTH = 256
RWKV_HEADS = 4
DECAY_LORA = 32
AAA_LORA = 32
GATE_LORA = 64
GN_EPS = 64e-5
RWKV_IN = 960
RWKV_PAD = 1024
ATTN_WIDTH = 512
ATTN_HEADS = 8
ATTN_KV_WIDTH = 128
ATTN_IN = ATTN_WIDTH + 2 * ATTN_KV_WIDTH
ROPE_THETA = 10000.0
QK_EPS = 1e-6
N_GROUPS = 4
EXPERTS_PER_GROUP = 8
N_EXPERTS = 32
EXPERT_HIDDEN = 512
MOE_BLOCK = 512
DEEPNORM_ALPHA = float((2 * DEPTH) ** 0.25)
LN_EPS = 1e-5
PROJ_WIDTH = POOL_WIDTH + RWKV_PAD + ATTN_IN
ROUTE_LANES = 128
WKV_CHUNK = 64
ROW_TILE = 512
WKV_TILE = 256
ATTN_Q_TILE = 512
VMEM_LIMIT = 48 * 1024 * 1024


def _cparams(sem):
    return pltpu.CompilerParams(dimension_semantics=sem, vmem_limit_bytes=VMEM_LIMIT)


def _dot(a, b):
    return jnp.dot(a, b, preferred_element_type=F32)


def _dot_nt(a, b):
    return lax.dot_general(a, b, (((1,), (1,)), ((), ())), preferred_element_type=F32)


def _dot_tn(a, b):
    return lax.dot_general(a, b, (((0,), (0,)), ((), ())), preferred_element_type=F32)


def _split2(x):
    hi = x.astype(BF16)
    lo = (x - hi.astype(F32)).astype(BF16)
    return hi, lo


def _split3(x):
    hi = x.astype(BF16)
    r1 = x - hi.astype(F32)
    mid = r1.astype(BF16)
    lo = (r1 - mid.astype(F32)).astype(BF16)
    return hi, mid, lo


HALF_D = D_MODEL // 2
U32 = jnp.uint32


def _pack_rows(x):
    hi = lax.bitcast_convert_type(x[:, :HALF_D].astype(BF16).astype(F32), U32)
    lo = lax.bitcast_convert_type(x[:, HALF_D:].astype(BF16).astype(F32), U32)
    return hi | (lo >> 16)


def _unpack_rows(u):
    a = lax.bitcast_convert_type(u & jnp.uint32(0xFFFF0000), F32).astype(BF16)
    b = lax.bitcast_convert_type(u << 16, F32).astype(BF16)
    return a, b


def _headsum_sq(x, ones_bf16):
    return _dot((x * x).astype(BF16), ones_bf16)


def _headsum(x, ones_bf16):
    hi, lo = _split2(x)
    return _dot(hi, ones_bf16) + _dot(lo, ones_bf16)


ROPE_QUARTER = HEAD_DIM // 4


def _rope(x, cos, sin_signed):
    n = x.shape[-1]
    lane = lax.broadcasted_iota(jnp.int32, x.shape, 1)
    first = (lane // ROPE_QUARTER) % 2 == 0
    partner = jnp.where(first, pltpu.roll(x, n - ROPE_QUARTER, 1), pltpu.roll(x, ROPE_QUARTER, 1))
    return x * cos + partner * sin_signed


def _layer_norm(z, g, b):
    mu = jnp.mean(z, axis=-1, keepdims=True)
    zc = z - mu
    var = jnp.mean(zc * zc, axis=-1, keepdims=True)
    return zc * lax.rsqrt(var + LN_EPS) * g + b


def _combine_rows(x1, ya, yb, route, g, b):
    g1 = route[:, 2:3]
    g2 = route[:, 3:4]
    a_hi, a_lo = _unpack_rows(ya)
    b_hi, b_lo = _unpack_rows(yb)
    m = jnp.concatenate([g1 * a_hi.astype(F32) + g2 * b_hi.astype(F32),
                         g1 * a_lo.astype(F32) + g2 * b_lo.astype(F32)], axis=1)
    return _layer_norm(DEEPNORM_ALPHA * x1 + m, g, b)


def _proj_kernel(x_ref, w_ref, cos_ref, sin_ref, qg_ref, kg_ref, onesq_ref,
                 pool_ref, rw_ref, q_ref, k_ref, v_ref):
    _proj_rows(x_ref[...], w_ref, cos_ref, sin_ref, qg_ref, kg_ref, onesq_ref,
               pool_ref, rw_ref, q_ref, k_ref, v_ref)


def _combine_proj_kernel(x1_ref, ya_ref, yb_ref, route_ref, lg_ref, lb_ref,
                         w_ref, cos_ref, sin_ref, qg_ref, kg_ref, onesq_ref,
                         x_ref, pool_ref, rw_ref, q_ref, k_ref, v_ref):
    x = _combine_rows(x1_ref[...], ya_ref[...], yb_ref[...], route_ref[...], lg_ref[...], lb_ref[...])
    x_ref[...] = x
    _proj_rows(x, w_ref, cos_ref, sin_ref, qg_ref, kg_ref, onesq_ref, pool_ref, rw_ref, q_ref, k_ref, v_ref)


def _proj_rows(x, w_ref, cos_ref, sin_ref, qg_ref, kg_ref, onesq_ref, pool_ref, rw_ref, q_ref, k_ref, v_ref):
    xb = x.astype(BF16)
    pool_ref[...] = _dot(xb, w_ref[:, 0:POOL_WIDTH])
    rw_ref[...] = _dot(xb, w_ref[:, POOL_WIDTH:POOL_WIDTH + RWKV_PAD])
    at = _dot(xb, w_ref[:, POOL_WIDTH + RWKV_PAD:PROJ_WIDTH])

    q = at[:, 0:ATTN_WIDTH]
    k = at[:, ATTN_WIDTH:ATTN_WIDTH + ATTN_KV_WIDTH]
    v = at[:, ATTN_WIDTH + ATTN_KV_WIDTH:ATTN_IN]
    ones_q = onesq_ref[...]
    ones_k = onesq_ref[0:ATTN_KV_WIDTH, 0:ATTN_KV_WIDTH]
    inv = 1.0 / HEAD_DIM
    qn = q * lax.rsqrt(_headsum_sq(q, ones_q) * inv + QK_EPS) * qg_ref[...]
    kn = k * lax.rsqrt(_headsum_sq(k, ones_k) * inv + QK_EPS) * kg_ref[...]
    cos = cos_ref[...]
    sin = sin_ref[...]
    q_ref[...] = (_rope(qn, cos, sin) * (HEAD_DIM ** -0.5 * math.log2(math.e))).astype(BF16)
    k_ref[...] = _rope(kn, cos[:, 0:ATTN_KV_WIDTH], sin[:, 0:ATTN_KV_WIDTH]).astype(BF16)
    v_ref[...] = v.astype(BF16)


def _proj(x2d, w, cos8, sin8, qg, kg, ones_q, seq, tm, prev=None):
    n = (x2d if prev is None else prev[0]).shape[0]
    nts = seq // tm
    nt = n // tm
    row = lambda width: pl.BlockSpec((tm, width), lambda i: (i, 0))
    const = lambda r, c: pl.BlockSpec((r, c), lambda i: (0, 0))
    proj_in = [const(D_MODEL, PROJ_WIDTH),
               pl.BlockSpec((tm, ATTN_WIDTH), lambda i: (i % nts, 0)),
               pl.BlockSpec((tm, ATTN_WIDTH), lambda i: (i % nts, 0)),
               const(1, ATTN_WIDTH), const(1, ATTN_KV_WIDTH), const(ATTN_WIDTH, ATTN_WIDTH)]
    proj_out = [row(POOL_WIDTH), row(RWKV_PAD), row(ATTN_WIDTH), row(ATTN_KV_WIDTH), row(ATTN_KV_WIDTH)]
    proj_shape = [jax.ShapeDtypeStruct((n, POOL_WIDTH), F32),
                  jax.ShapeDtypeStruct((n, RWKV_PAD), F32),
                  jax.ShapeDtypeStruct((n, ATTN_WIDTH), BF16),
                  jax.ShapeDtypeStruct((n, ATTN_KV_WIDTH), BF16),
                  jax.ShapeDtypeStruct((n, ATTN_KV_WIDTH), BF16)]
    if prev is None:
        return pl.pallas_call(
            _proj_kernel,
            grid=(nt,),
            in_specs=[row(D_MODEL)] + proj_in,
            out_specs=proj_out,
            out_shape=proj_shape,
            compiler_params=_cparams(("parallel",)),
            name="proj",
        )(x2d, w, cos8, sin8, qg, kg, ones_q)
    x1, yab, route, lg, lb = prev
    return pl.pallas_call(
        _combine_proj_kernel,
        grid=(nt,),
        in_specs=[row(D_MODEL), row(HALF_D), pl.BlockSpec((tm, HALF_D), lambda i: (i + nt, 0)),
                  row(ROUTE_LANES), const(1, D_MODEL), const(1, D_MODEL)] + proj_in,
        out_specs=[row(D_MODEL)] + proj_out,
        out_shape=[jax.ShapeDtypeStruct((n, D_MODEL), F32)] + proj_shape,
        compiler_params=_cparams(("parallel",)),
        name="combine_proj",
    )(x1, yab, yab, route, lg, lb, w, cos8, sin8, qg, kg, ones_q)


def _pool_kernel(u_ref, w_ref, scale_ref, o_ref):
    s = u_ref.shape[1]
    u = u_ref[0]
    t = lax.broadcasted_iota(jnp.int32, (s, POOL_WIDTH), 0)

    def down(x, k):
        return jnp.where(t >= k, pltpu.roll(x, k, 0), 0.0)

    def up(x, k):
        return jnp.where(t < s - k, pltpu.roll(x, s - k, 0), 0.0)

    left = [down(u, 1)]
    right = [u]
    for k in (1, 2, 4):
        left.append(left[-1] + down(left[-1], k))
        right.append(right[-1] + up(right[-1], k))
    sums = [l + r for l, r in zip(left, right)]

    grp = lax.broadcasted_iota(jnp.int32, (s, POOL_WIDTH), 1) // POOL_GROUP
    half = jnp.where(grp == 0, 1, jnp.where(grp == 1, 2, jnp.where(grp == 2, 4, 8)))
    cnt = (jnp.minimum(t + half, s) - jnp.maximum(t - half, 0)).astype(F32)
    tot = jnp.where(grp == 0, sums[0], jnp.where(grp == 1, sums[1], jnp.where(grp == 2, sums[2], sums[3])))
    d = tot / cnt - u
    o_ref[0] = _dot(d.astype(BF16), w_ref[...]) * scale_ref[...]


def _pool(u3d, w_bd, scale):
    b, s, _ = u3d.shape
    return pl.pallas_call(
        _pool_kernel,
        grid=(b,),
        in_specs=[pl.BlockSpec((1, s, POOL_WIDTH), lambda i: (i, 0, 0)),
                  pl.BlockSpec((POOL_WIDTH, POOL_WIDTH), lambda i: (0, 0)),
                  pl.BlockSpec((1, POOL_WIDTH), lambda i: (0, 0))],
        out_specs=pl.BlockSpec((1, s, POOL_WIDTH), lambda i: (i, 0, 0)),
        out_shape=jax.ShapeDtypeStruct((b, s, POOL_WIDTH), F32),
        compiler_params=_cparams(("parallel",)),
        name="pool",
    )(u3d, w_bd, scale)


_SHIFT_HALO = 8


def _rwkv_prep_kernel(cur_ref, prev_ref, next_ref, mup_ref, mun_ref, wl_ref, bias_ref,
                      kk_ref, ka_ref, rk_ref, ones_ref,
                      sh_ref, lw_ref, kn_ref, post_ref):
    ts = cur_ref.shape[1]
    ti = pl.program_id(1)
    nt = pl.num_programs(1)
    cur = cur_ref[0]
    row = lax.broadcasted_iota(jnp.int32, cur.shape, 0)
    before = jnp.where(ti > 0, prev_ref[0, _SHIFT_HALO - 1:_SHIFT_HALO, :], 0.0)
    after = jnp.where(ti < nt - 1, next_ref[0, 0:1, :], 0.0)
    prev = jnp.where(row == 0, before, pltpu.roll(cur, 1, 0))
    nxt = jnp.where(row == ts - 1, after, pltpu.roll(cur, ts - 1, 0))
    f = cur + mup_ref[...] * (prev - cur) + mun_ref[...] * (nxt - cur)

    w = RWKV_WIDTH
    r = f[:, 0:w]
    k = f[:, w:2 * w]
    v = f[:, 2 * w:3 * w]
    lora = f[:, 3 * w:4 * w]
    lane = lax.broadcasted_iota(jnp.int32, lora.shape, 1)
    z = jnp.where(lane < 2 * DECAY_LORA, jnp.tanh(lora),
                  jnp.where(lane < 2 * DECAY_LORA + 2 * AAA_LORA, lora,
                            jnp.where(lane < 2 * DECAY_LORA + 2 * AAA_LORA + GATE_LORA,
                                      jax.nn.sigmoid(lora), 0.0)))
    up = _dot(z.astype(BF16), wl_ref[...]) + bias_ref[...]
    ones = ones_ref[...]

    kk0 = k * kk_ref[...]
    nrm = jnp.sqrt(_headsum_sq(kk0, ones))
    kk = kk0 / jnp.maximum(nrm, 1e-12)
    sh_ref[0, :, 0:w] = r.astype(BF16)
    sh_ref[0, :, w:2 * w] = v.astype(BF16)
    sh_ref[0, :, 2 * w:3 * w] = kk.astype(BF16)

    ksum = jnp.zeros_like(k)
    for di in range(2):
        logw = -math.exp(-0.5) * jax.nn.sigmoid(up[:, di * w:(di + 1) * w])
        a = jax.nn.sigmoid(up[:, (2 + di) * w:(3 + di) * w])
        kh = k * (1.0 + (a - 1.0) * ka_ref[...])
        ksum = ksum + kh
        lw_ref[di, 0] = logw
        kn_ref[di, 0, :, 0:w] = kh.astype(BF16)
        kn_ref[di, 0, :, w:2 * w] = (-(a * kk)).astype(BF16)
    bonus = _headsum(r * ksum * rk_ref[...], ones) * v
    post_ref[0, :, 0:w] = bonus
    post_ref[0, :, w:2 * w] = up[:, 4 * w:5 * w]


def _rwkv_prep(rw3d, mup, mun, wl, bias, k_k, k_a, r_k, ones, ts):
    b, s, _ = rw3d.shape
    nt = s // ts
    hb = ts // _SHIFT_HALO
    nhb = s // _SHIFT_HALO
    w = RWKV_WIDTH
    vec = lambda width: pl.BlockSpec((1, width), lambda i, j: (0, 0))
    return pl.pallas_call(
        _rwkv_prep_kernel,
        grid=(b, nt),
        in_specs=[pl.BlockSpec((1, ts, RWKV_PAD), lambda i, j: (i, j, 0)),
                  pl.BlockSpec((1, _SHIFT_HALO, RWKV_PAD),
                               lambda i, j: (i, jnp.maximum(j * hb - 1, 0), 0)),
                  pl.BlockSpec((1, _SHIFT_HALO, RWKV_PAD),
                               lambda i, j: (i, jnp.minimum((j + 1) * hb, nhb - 1), 0)),
                  vec(RWKV_PAD), vec(RWKV_PAD),
                  pl.BlockSpec((w, 5 * w), lambda i, j: (0, 0)),
                  vec(5 * w), vec(w), vec(w), vec(w),
                  pl.BlockSpec((w, w), lambda i, j: (0, 0))],
        out_specs=[pl.BlockSpec((1, ts, 3 * w), lambda i, j: (i, j, 0)),
                   pl.BlockSpec((2, 1, ts, w), lambda i, j: (0, i, j, 0)),
                   pl.BlockSpec((2, 1, ts, 2 * w), lambda i, j: (0, i, j, 0)),
                   pl.BlockSpec((1, ts, 2 * w), lambda i, j: (i, j, 0))],
        out_shape=[jax.ShapeDtypeStruct((b, s, 3 * w), BF16),
                   jax.ShapeDtypeStruct((2, b, s, w), F32),
                   jax.ShapeDtypeStruct((2, b, s, 2 * w), BF16),
                   jax.ShapeDtypeStruct((b, s, 2 * w), F32)],
        compiler_params=_cparams(("parallel", "parallel")),
        name="rwkv_prep",
    )(rw3d, rw3d, rw3d, mup, mun, wl, bias, k_k, k_a, r_k, ones)


def _expand_bd(x_bf16, mask_bd):
    return jnp.where(mask_bd, jnp.concatenate([x_bf16] * RWKV_HEADS, axis=0), jnp.zeros((), BF16))


def _wkv_kernel(shf_ref, shr_ref, lwf_ref, lwr_ref, knf_ref, knr_ref, yf_ref, yr_ref, hf_ref, hr_ref):
    c = WKV_CHUNK
    w = RWKV_WIDTH
    tt = shf_ref.shape[1]
    ncs = tt // c

    @pl.when(pl.program_id(1) == 0)
    def _():
        hf_ref[...] = jnp.zeros_like(hf_ref)
        hr_ref[...] = jnp.zeros_like(hr_ref)

    row_c = lax.broadcasted_iota(jnp.int32, (c, c), 0)
    col_c = lax.broadcasted_iota(jnp.int32, (c, c), 1)
    t_i = lax.broadcasted_iota(jnp.int32, (c, w), 0)
    s_i = lax.broadcasted_iota(jnp.int32, (c, w), 1) % c
    eye_c = (s_i == t_i).astype(F32)
    row_w = lax.broadcasted_iota(jnp.int32, (w, w), 0)
    col_w = lax.broadcasted_iota(jnp.int32, (w, w), 1)
    mask_bd = (row_w // c) == (col_w // HEAD_DIM)
    eye_w = row_w == col_w
    tri_incl = [jnp.where(col_c <= row_c, 1.0, 0.0).astype(BF16),
                jnp.where(col_c >= row_c, 1.0, 0.0).astype(BF16)]
    strict = [s_i < t_i, s_i > t_i]
    incl = [s_i <= t_i, s_i >= t_i]

    def bf(x):
        return x.astype(BF16)

    def bd(x):
        return _expand_bd(bf(x), mask_bd)

    chunks = []
    for ci in range(ncs):
        chunks.append((0, ci * c, shf_ref, lwf_ref, knf_ref))
        chunks.append((1, (ncs - 1 - ci) * c, shr_ref, lwr_ref, knr_ref))

    st = []
    for d, start, sh_ref, lw_ref, kn_ref in chunks:
        rows = pl.ds(start, c)
        st.append(dict(d=d, rows=rows,
                       r=sh_ref[0, rows, 0:w].astype(F32), v_bf=sh_ref[0, rows, w:2 * w],
                       kk=sh_ref[0, rows, 2 * w:3 * w].astype(F32), lw=lw_ref[0, rows, :],
                       kh=kn_ref[0, rows, 0:w].astype(F32), nb=kn_ref[0, rows, w:2 * w].astype(F32)))

    for q in st:
        l_hi, l_mid, l_lo = _split3(q["lw"])
        tri = tri_incl[q["d"]]
        q["cum"] = _dot(tri, l_hi) + _dot(tri, l_mid) + _dot(tri, l_lo)
    for q in st:
        cum, lw = q["cum"], q["lw"]
        tot = jnp.sum(lw, axis=0, keepdims=True)
        e_inv = jnp.exp(-cum)
        e_end = jnp.exp(tot - cum)
        q["tot"] = tot
        q["a_bar"] = q["kk"] * jnp.exp(cum - lw)
        q["r_bar"] = q["r"] * jnp.exp(cum)
        q["b_hat"] = bf(q["nb"] * e_end)
        q["k_hat"] = bf(q["kh"] * e_end)
        q["v_bd"] = _expand_bd(q["v_bf"], mask_bd)
        q["lhs"] = jnp.concatenate([bf(q["a_bar"]), bf(q["r_bar"])], axis=0)
        q["rhs"] = jnp.concatenate([bd(q["nb"] * e_inv), bd(q["kh"] * e_inv)], axis=0)
    for q in st:
        sc = _dot_nt(q["lhs"], q["rhs"])
        sm, im = strict[q["d"]], incl[q["d"]]
        q["x"] = jnp.where(sm, sc[0:c, 0:w], 0.0)
        q["a_ak"] = bf(jnp.where(sm, sc[0:c, w:2 * w], 0.0))
        q["m_rb"] = bf(jnp.where(im, sc[c:2 * c, 0:w], 0.0))
        q["m_rk"] = bf(jnp.where(im, sc[c:2 * c, w:2 * w], 0.0))
        q["tinv"] = eye_c + q["x"]

    for q in st:
        q["x"] = _dot(bf(q["x"]), bd(q["x"]))
    n_lv = int(math.log2(c)) - 1
    for lv in range(n_lv):
        last = lv == n_lv - 1
        for q in st:
            xb = bd(q["x"])
            if last:
                q["tinv"] = q["tinv"] + _dot(bf(q["tinv"]), xb)
            else:
                res = _dot(jnp.concatenate([bf(q["x"]), bf(q["tinv"])], axis=0), xb)
                q["tinv"] = q["tinv"] + res[c:2 * c]
                q["x"] = res[0:c]
    for q in st:
        res = _dot(jnp.concatenate([q["a_ak"], q["m_rk"]], axis=0), q["v_bd"])
        q["akv"] = res[0:c]
        q["mrkv"] = res[c:2 * c]
    for q in st:
        tinv_bf = bf(q["tinv"])
        q["a_pr"] = bf(_dot(tinv_bf, bd(q["a_bar"])))
        q["v_pr"] = bf(_dot(tinv_bf, bd(q["akv"])))
    for q in st:
        q["r_pr"] = bf(q["r_bar"] + _dot(q["m_rb"], _expand_bd(q["a_pr"], mask_bd)))
        q["y_pr"] = _dot(q["m_rb"], _expand_bd(q["v_pr"], mask_bd)) + q["mrkv"]
    for q in st:
        q["p_bd"] = bf(jnp.where(eye_w, jnp.exp(q["tot"]), 0.0)
                       + jnp.where(mask_bd, _dot_tn(q["b_hat"], q["a_pr"]), 0.0))
        q["q_bd"] = jnp.where(mask_bd,
                              _dot_tn(jnp.concatenate([q["b_hat"], q["k_hat"]], axis=0),
                                      jnp.concatenate([q["v_pr"], q["v_bf"]], axis=0)), 0.0)

    h = [hf_ref[...], hr_ref[...]]
    y_refs = [yf_ref, yr_ref]
    for q in st:
        d = q["d"]
        res = _dot(jnp.concatenate([q["r_pr"], q["p_bd"]], axis=0), bf(h[d]))
        y_refs[d][0, q["rows"], :] = res[0:c] + q["y_pr"]
        h[d] = res[c:c + w] + q["q_bd"]
    hf_ref[...] = h[0]
    hr_ref[...] = h[1]


def _wkv(shared, logw, khnb, tt):
    b, s, _ = shared.shape
    nt = s // tt
    w = RWKV_WIDTH
    fwd = lambda i, j: (i, j, 0)
    bwd = lambda i, j: (i, nt - 1 - j, 0)
    return pl.pallas_call(
        _wkv_kernel,
        grid=(b, nt),
        in_specs=[pl.BlockSpec((1, tt, 3 * w), fwd), pl.BlockSpec((1, tt, 3 * w), bwd),
                  pl.BlockSpec((None, 1, tt, w), lambda i, j: (0, i, j, 0)),
                  pl.BlockSpec((None, 1, tt, w), lambda i, j: (1, i, nt - 1 - j, 0)),
                  pl.BlockSpec((None, 1, tt, 2 * w), lambda i, j: (0, i, j, 0)),
                  pl.BlockSpec((None, 1, tt, 2 * w), lambda i, j: (1, i, nt - 1 - j, 0))],
        out_specs=[pl.BlockSpec((1, tt, w), fwd), pl.BlockSpec((1, tt, w), bwd)],
        out_shape=[jax.ShapeDtypeStruct((b, s, w), F32), jax.ShapeDtypeStruct((b, s, w), F32)],
        scratch_shapes=[pltpu.VMEM((w, w), F32), pltpu.VMEM((w, w), F32)],
        compiler_params=_cparams(("parallel", "arbitrary")),
        name="wkv",
    )(shared, shared, logw, logw, khnb, khnb)


def _attn_kernel(q_ref, k_ref, v_ref, o_ref):
    tq = q_ref.shape[0]
    k = k_ref[...]
    v = v_ref[...]
    lane = lax.broadcasted_iota(jnp.int32, (tq, ATTN_KV_WIDTH), 1)
    low = lane < HEAD_DIM
    zero = jnp.zeros((), BF16)
    for j in range(ATTN_HEADS // 2):
        qb = q_ref[:, j * ATTN_KV_WIDTH:(j + 1) * ATTN_KV_WIDTH]
        outs = []
        for half in range(2):
            qm = jnp.where(low if half == 0 else ~low, qb, zero)
            s = _dot_nt(qm, k)
            m = jnp.max(s, axis=-1, keepdims=True)
            p = jnp.exp2(s - m)
            l = jnp.sum(p, axis=-1, keepdims=True)
            outs.append(_dot(p.astype(BF16), v) / l)
        o_ref[:, j * ATTN_KV_WIDTH:(j + 1) * ATTN_KV_WIDTH] = jnp.where(low, outs[0], outs[1]).astype(BF16)


def _attn(q, k, v, seq, tq):
    n = q.shape[0]
    nq = seq // tq
    return pl.pallas_call(
        _attn_kernel,
        grid=(n // seq, nq),
        in_specs=[pl.BlockSpec((tq, ATTN_WIDTH), lambda b, i: (b * nq + i, 0)),
                  pl.BlockSpec((seq, ATTN_KV_WIDTH), lambda b, i: (b, 0)),
                  pl.BlockSpec((seq, ATTN_KV_WIDTH), lambda b, i: (b, 0))],
        out_specs=pl.BlockSpec((tq, ATTN_WIDTH), lambda b, i: (b * nq + i, 0)),
        out_shape=jax.ShapeDtypeStruct((n, ATTN_WIDTH), BF16),
        compiler_params=_cparams(("parallel", "parallel")),
        name="attn",
    )(q, k, v)


def _outproj_kernel(x_ref, pool_ref, wkvf_ref, wkvr_ref, post_ref, attn_ref, wo_ref, gng_ref, gnb_ref,
                    ones_ref, lg_ref, lb_ref, wrh_ref, wrl_ref, br_ref,
                    x1_ref, x1p_ref, route_ref):
    w = RWKV_WIDTH
    ones = ones_ref[...]
    wkv = wkvf_ref[...] + wkvr_ref[...]
    inv = 1.0 / HEAD_DIM
    mu = _headsum(wkv, ones) * inv
    cen = wkv - mu
    var = _headsum_sq(cen, ones) * inv
    yr = cen * lax.rsqrt(var + GN_EPS) * gng_ref[...] + gnb_ref[...] + post_ref[:, 0:w]
    yr = yr * post_ref[:, w:2 * w]
    y = (_dot(pool_ref[...].astype(BF16), wo_ref[0:POOL_WIDTH, :])
         + _dot(yr.astype(BF16), wo_ref[POOL_WIDTH:POOL_WIDTH + w, :])
         + _dot(attn_ref[...], wo_ref[POOL_WIDTH + w:D_MODEL, :]))
    x1 = _layer_norm(DEEPNORM_ALPHA * x_ref[...] + y, lg_ref[...], lb_ref[...])
    x1_ref[...] = x1
    x1p_ref[...] = _pack_rows(x1)

    xh, xl = _split2(x1)
    logits = (_dot(xh, wrh_ref[...]) + _dot(xl, wrh_ref[...]) + _dot(xh, wrl_ref[...])) + br_ref[...]
    lane = lax.broadcasted_iota(jnp.int32, logits.shape, 1)
    lane_f = lane.astype(F32)
    lane_grp_f = ((lane - N_GROUPS) // EXPERTS_PER_GROUP).astype(F32)
    neg = -jnp.inf
    big = jnp.float32(1 << 20)
    gl = jnp.where(lane < N_GROUPS, logits, neg)
    gmax = jnp.max(gl, axis=-1, keepdims=True)
    grp = jnp.min(jnp.where(gl == gmax, lane_f, big), axis=-1, keepdims=True)
    gw = 1.0 / jnp.sum(jnp.exp(gl - gmax), axis=-1, keepdims=True)
    in_grp = (lane >= N_GROUPS) & (lane < N_GROUPS + N_EXPERTS) & (lane_grp_f == grp)
    el = jnp.where(in_grp, logits, neg)
    v1 = jnp.max(el, axis=-1, keepdims=True)
    i1 = jnp.min(jnp.where(el == v1, lane_f, big), axis=-1, keepdims=True)
    el2 = jnp.where(lane_f == i1, neg, el)
    v2 = jnp.max(el2, axis=-1, keepdims=True)
    i2 = jnp.min(jnp.where(el2 == v2, lane_f, big), axis=-1, keepdims=True)
    e21 = jnp.exp(v2 - v1)
    g1 = gw / (1.0 + e21)
    g2 = gw * e21 / (1.0 + e21)
    route = jnp.where(lane == 0, i1 - N_GROUPS,
                      jnp.where(lane == 1, i2 - N_GROUPS,
                                jnp.where(lane == 2, g1, jnp.where(lane == 3, g2, 0.0))))
    route_ref[...] = route


def _outproj(x2d, pool2d, wkvf, wkvr, post2d, attn2d, wo, gng, gnb, ones, lg, lb, wrh, wrl, br, tm):
    n = x2d.shape[0]
    w = RWKV_WIDTH
    row = lambda width: pl.BlockSpec((tm, width), lambda i: (i, 0))
    vec = lambda width: pl.BlockSpec((1, width), lambda i: (0, 0))
    return pl.pallas_call(
        _outproj_kernel,
        grid=(n // tm,),
        in_specs=[row(D_MODEL), row(POOL_WIDTH), row(w), row(w),
                  row(2 * w), row(ATTN_WIDTH),
                  pl.BlockSpec((D_MODEL, D_MODEL), lambda i: (0, 0)),
                  vec(w), vec(w),
                  pl.BlockSpec((w, w), lambda i: (0, 0)),
                  vec(D_MODEL), vec(D_MODEL),
                  pl.BlockSpec((D_MODEL, ROUTE_LANES), lambda i: (0, 0)),
                  pl.BlockSpec((D_MODEL, ROUTE_LANES), lambda i: (0, 0)),
                  vec(ROUTE_LANES)],
        out_specs=[row(D_MODEL), row(HALF_D), row(ROUTE_LANES)],
        out_shape=[jax.ShapeDtypeStruct((n, D_MODEL), F32),
                   jax.ShapeDtypeStruct((n, HALF_D), U32),
                   jax.ShapeDtypeStruct((n, ROUTE_LANES), F32)],
        compiler_params=_cparams(("parallel",)),
        name="outproj",
    )(x2d, pool2d, wkvf, wkvr, post2d, attn2d, wo, gng, gnb, ones, lg, lb, wrh, wrl, br)


def _expert_kernel(bexp_ref, nused_ref, xs_ref, wg_ref, wu_ref, wd_ref, *rest):
    ys_ref, wgb_ref, wub_ref, wdb_ref = rest[-4:]
    i = pl.program_id(0)
    used = i < nused_ref[0]

    @pl.when(used & ((i == 0) | (bexp_ref[i] != bexp_ref[jnp.maximum(i - 1, 0)])))
    def _():
        wgb_ref[...] = wg_ref[0].astype(BF16)
        wub_ref[...] = wu_ref[0].astype(BF16)
        wdb_ref[...] = wd_ref[0].astype(BF16)

    @pl.when(used)
    def _():
        xa, xb = _unpack_rows(xs_ref[...])
        h1 = _dot(xa, wgb_ref[0:HALF_D, :]) + _dot(xb, wgb_ref[HALF_D:D_MODEL, :])
        h2 = _dot(xa, wub_ref[0:HALF_D, :]) + _dot(xb, wub_ref[HALF_D:D_MODEL, :])
        h = (h1 * jax.nn.sigmoid(h1) * h2).astype(BF16)
        ys_ref[...] = _pack_rows(_dot(h, wdb_ref[...]))

    @pl.when(jnp.logical_not(used))
    def _():
        ys_ref[...] = jnp.zeros_like(ys_ref)


def _experts(block_exp, n_used, xs_parts, wg, wu, wd, layer):
    p = sum(xs.shape[0] for xs in xs_parts)
    wmap = lambda i, be, nu: (layer, be[i], 0, 0)
    ys = None
    first = 0
    for xs in xs_parts:
        nb = xs.shape[0] // MOE_BLOCK
        in_specs = [pl.BlockSpec((MOE_BLOCK, HALF_D), lambda i, be, nu: (i, 0)),
                    pl.BlockSpec((None, 1, D_MODEL, EXPERT_HIDDEN), wmap),
                    pl.BlockSpec((None, 1, D_MODEL, EXPERT_HIDDEN), wmap),
                    pl.BlockSpec((None, 1, EXPERT_HIDDEN, D_MODEL), wmap)]
        args = [block_exp[first:first + nb], n_used - first, xs, wg, wu, wd]
        aliases = {}
        if ys is not None:
            in_specs.append(pl.BlockSpec(memory_space=pl.ANY))
            aliases = {len(args): 0}
            args.append(ys)
        grid_spec = pltpu.PrefetchScalarGridSpec(
            num_scalar_prefetch=2,
            grid=(nb,),
            in_specs=in_specs,
            out_specs=pl.BlockSpec((MOE_BLOCK, HALF_D), functools.partial(_block_at, first)),
            scratch_shapes=[pltpu.VMEM((D_MODEL, EXPERT_HIDDEN), BF16),
                            pltpu.VMEM((D_MODEL, EXPERT_HIDDEN), BF16),
                            pltpu.VMEM((EXPERT_HIDDEN, D_MODEL), BF16)],
        )
        ys = pl.pallas_call(
            _expert_kernel,
            grid_spec=grid_spec,
            out_shape=jax.ShapeDtypeStruct((p, HALF_D), U32),
            input_output_aliases=aliases,
            compiler_params=_cparams(("arbitrary",)),
            name="experts",
        )(*args)
        first += nb
    return ys


def _block_at(first, i, be, nu):
    return (i + first, 0)


SC_CORES = 2
SC_SUBCORES = 16
SC_GATHER_ROWS = 64


def _sc_gather(table, idx):
    v, d = table.shape
    b = idx.shape[0]
    nw = SC_CORES * SC_SUBCORES
    ch = SC_GATHER_ROWS
    assert b % (nw * ch * 2) == 0
    b_per_w = b // nw
    nch = b_per_w // ch
    mesh = plsc.VectorSubcoreMesh(core_axis_name="c", subcore_axis_name="s")

    @functools.partial(
        pl.kernel, mesh=mesh,
        out_type=jax.ShapeDtypeStruct((b, d), table.dtype),
        scratch_types=[pltpu.VMEM((b_per_w,), jnp.int32),
                       pltpu.VMEM((2, ch, d), table.dtype),
                       pltpu.SemaphoreType.DMA, pltpu.SemaphoreType.DMA,
                       pltpu.SemaphoreType.DMA, pltpu.SemaphoreType.DMA],
    )
    def gather_kernel(table_hbm, idx_hbm, out_hbm, idx_v, rows_v, gsem0, gsem1, wsem0, wsem1):
        wid = lax.axis_index("s") * SC_CORES + lax.axis_index("c")
        base = wid * b_per_w
        pltpu.sync_copy(idx_hbm.at[pl.ds(base, b_per_w)], idx_v)
        gsem = (gsem0, gsem1)
        wsem = (wsem0, wsem1)

        def gather(jj, slot):
            off = pl.multiple_of(jj * ch, ch)
            return pltpu.make_async_copy(table_hbm.at[idx_v.at[pl.ds(off, ch)]], rows_v.at[slot], gsem[slot])

        def write(jj, slot):
            off = pl.multiple_of(jj * ch, ch)
            return pltpu.make_async_copy(rows_v.at[slot], out_hbm.at[pl.ds(base + off, ch)], wsem[slot])

        gather(0, 0).start()

        @pl.loop(0, nch, step=2)
        def _(j):
            for slot in range(2):
                jj = j + slot

                gather(jj, slot).wait()

                @pl.when(jj >= 1)
                def _():
                    write(jj - 1, 1 - slot).wait()

                @pl.when(jj + 1 < nch)
                def _():
                    gather(jj + 1, 1 - slot).start()

                write(jj, slot).start()

        write(nch - 1, (nch - 1) % 2).wait()

    return gather_kernel(table, idx)


def _gather_rows(table, idx):
    b = idx.shape[0]
    unit = SC_CORES * SC_SUBCORES * SC_GATHER_ROWS * 2
    bp = -(-b // unit) * unit
    if bp == b:
        return _sc_gather(table, idx)
    return _sc_gather(table, jnp.pad(idx, (0, bp - b)))[:b]


def _combine_kernel(x1_ref, ya_ref, yb_ref, route_ref, lg_ref, lb_ref, o_ref):
    o_ref[...] = _combine_rows(x1_ref[...], ya_ref[...], yb_ref[...], route_ref[...], lg_ref[...], lb_ref[...])


def _combine(x1, yab, route, lg, lb, tm):
    n = x1.shape[0]
    nt = n // tm
    row = lambda width: pl.BlockSpec((tm, width), lambda i: (i, 0))
    vec = lambda width: pl.BlockSpec((1, width), lambda i: (0, 0))
    return pl.pallas_call(
        _combine_kernel,
        grid=(nt,),
        in_specs=[row(D_MODEL), row(HALF_D), pl.BlockSpec((tm, HALF_D), lambda i: (i + nt, 0)),
                  row(ROUTE_LANES), vec(D_MODEL), vec(D_MODEL)],
        out_specs=row(D_MODEL),
        out_shape=jax.ShapeDtypeStruct((n, D_MODEL), F32),
        compiler_params=_cparams(("parallel",)),
        name="combine",
    )(x1, yab, yab, route, lg, lb)


def _q_perm():
    order = [h for j in range(ATTN_HEADS // 2) for h in (j, j + ATTN_HEADS // 2)]
    return jnp.concatenate([jnp.arange(h * HEAD_DIM, (h + 1) * HEAD_DIM) for h in order])


def _block_diag(blocks):
    n = len(blocks)
    r, c = blocks[0].shape
    out = jnp.zeros((n * r, n * c), blocks[0].dtype)
    for i, blk in enumerate(blocks):
        out = out.at[i * r:(i + 1) * r, i * c:(i + 1) * c].set(blk)
    return out


def _rope_tables(seq):
    rows = seq // GRID_W
    row_id = jnp.repeat(jnp.arange(rows), GRID_W).astype(F32)
    col_id = jnp.tile(jnp.arange(GRID_W), rows).astype(F32)
    half = HEAD_DIM // 2
    inv_freq = ROPE_THETA ** (-jnp.arange(0, half, 2, dtype=F32) / half)
    ang_r = row_id[:, None] * inv_freq
    ang_c = col_id[:, None] * inv_freq
    ang = jnp.concatenate([ang_r, ang_r, ang_c, ang_c], -1)
    sign = jnp.where((jnp.arange(HEAD_DIM) // 16) % 2 == 0, -1.0, 1.0).astype(F32)
    cos8 = jnp.tile(jnp.cos(ang), (1, ATTN_HEADS))
    sin8 = jnp.tile(jnp.sin(ang) * sign, (1, ATTN_HEADS))
    return cos8, sin8


def _dispatch(route, n):
    m = n * 2
    experts = jnp.arange(N_EXPERTS, dtype=jnp.int32)
    e_flat = jnp.concatenate([route[:, 0], route[:, 1]]).astype(jnp.int32)
    pair = jnp.arange(m, dtype=jnp.int32)
    e_sorted, order = lax.sort((e_flat, pair), num_keys=1, is_stable=True)
    start = jnp.sum(e_sorted[None, :] < experts[:, None], axis=1, dtype=jnp.int32)
    counts = jnp.concatenate([start[1:], jnp.full((1,), m, jnp.int32)]) - start
    padded = (counts + MOE_BLOCK - 1) // MOE_BLOCK * MOE_BLOCK
    ends_p = jnp.sum(jnp.where(experts[None, :] <= experts[:, None], padded[None, :], 0), axis=1, dtype=jnp.int32)
    pstart = ends_p - padded
    delta = pstart - start
    n_blocks = -(-(m + N_EXPERTS * (MOE_BLOCK - 1)) // MOE_BLOCK)
    block_start = jnp.arange(n_blocks, dtype=jnp.int32) * MOE_BLOCK
    block_exp = jnp.minimum(jnp.sum(ends_p[None, :] <= block_start[:, None], axis=1, dtype=jnp.int32),
                            N_EXPERTS - 1)
    lane = jnp.arange(MOE_BLOCK, dtype=jnp.int32)[None, :]
    blk = block_exp[:, None] == experts[None, :]

    def per_block(table):
        return jnp.sum(jnp.where(blk, table[None, :], 0), axis=1, dtype=jnp.int32)

    valid = (block_start - per_block(pstart))[:, None] + lane < per_block(counts)[:, None]
    src = jnp.clip((block_start - per_block(delta))[:, None] + lane, 0, m - 1)
    run = order[src.reshape(-1)].reshape(n_blocks, MOE_BLOCK)
    row_tok = jnp.where(valid, run % n, (block_start[:, None] + lane) % n).reshape(-1)
    n_used = (ends_p[-1] // MOE_BLOCK).astype(jnp.int32).reshape(1)
    dest = pair + jnp.sum(jnp.where(e_sorted[None, :] == experts[:, None], delta[:, None], 0), axis=0,
                          dtype=jnp.int32)
    _, pos = lax.sort((order, dest), num_keys=1)
    return row_tok, pos, block_exp, n_used


def kernel(x, w_in, mu_prev, mu_next, pool_w, pool_scale, rw_w0, rw_w_up, rw_a0, rw_a_up, rw_g_up, rw_k_k, rw_k_a, rw_r_k, rw_gn_g, rw_gn_b, q_norm, k_norm, w_o, ln1_g, ln1_b, router_group, router_group_b, router_expert, router_expert_b, exp_gate, exp_up, exp_down, ln2_g, ln2_b):
    b, s, d = x.shape
    n = b * s
    w = RWKV_WIDTH
    tm = min(ROW_TILE, s)
    qperm = _q_perm()
    cos8, sin8 = _rope_tables(s)
    ones_q = _block_diag([jnp.ones((HEAD_DIM, HEAD_DIM), BF16)] * ATTN_HEADS)
    ones_r = ones_q[0:w, 0:w]
    a_end = POOL_WIDTH
    b_end = POOL_WIDTH + RWKV_IN

    xc = x.reshape(n, d)
    pending = None
    for l in range(DEPTH):
        wq = w_in[l][:, b_end:b_end + ATTN_WIDTH][:, qperm]
        w_proj = jnp.concatenate(
            [w_in[l][:, :a_end], w_in[l][:, a_end:b_end],
             jnp.zeros((d, RWKV_PAD - RWKV_IN), F32), wq, w_in[l][:, b_end + ATTN_WIDTH:]],
            axis=1).astype(BF16)
        pool_bd = _block_diag([pool_w[l, g] for g in range(len(POOL_WINDOWS))]).astype(BF16)
        pad_vec = jnp.zeros((RWKV_PAD - RWKV_IN,), F32)
        mup = jnp.concatenate([mu_prev[l], pad_vec]).reshape(1, RWKV_PAD)
        mun = jnp.concatenate([mu_next[l], pad_vec]).reshape(1, RWKV_PAD)
        w_lora = jnp.zeros((w, 5 * w), F32)
        for di in range(2):
            w_lora = w_lora.at[di * DECAY_LORA:(di + 1) * DECAY_LORA, di * w:(di + 1) * w].set(rw_w_up[l, di])
            o = 2 * DECAY_LORA
            w_lora = w_lora.at[o + di * AAA_LORA:o + (di + 1) * AAA_LORA, (2 + di) * w:(3 + di) * w].set(rw_a_up[l, di])
        o = 2 * DECAY_LORA + 2 * AAA_LORA
        w_lora = w_lora.at[o:o + GATE_LORA, 4 * w:5 * w].set(rw_g_up[l]).astype(BF16)
        lora_bias = jnp.concatenate([rw_w0[l, 0], rw_w0[l, 1], rw_a0[l, 0], rw_a0[l, 1],
                                     jnp.zeros((w,), F32)]).reshape(1, 5 * w)
        qg = jnp.tile(q_norm[l], ATTN_HEADS).reshape(1, ATTN_WIDTH)
        kg = jnp.tile(k_norm[l], 2).reshape(1, ATTN_KV_WIDTH)
        wo_attn = w_o[l][POOL_WIDTH + w:][qperm]
        wo = jnp.concatenate([w_o[l][:POOL_WIDTH + w], wo_attn], axis=0).astype(BF16)
        wr = jnp.concatenate([router_group[l], router_expert[l],
                              jnp.zeros((d, ROUTE_LANES - N_GROUPS - N_EXPERTS), F32)], axis=1)
        wrh = wr.astype(BF16)
        wrl = (wr - wrh.astype(F32)).astype(BF16)
        br = jnp.concatenate([router_group_b[l], router_expert_b[l],
                              jnp.zeros((ROUTE_LANES - N_GROUPS - N_EXPERTS,), F32)]).reshape(1, ROUTE_LANES)

        if pending is None:
            pool_in, rw_in, qh, kh, vh = _proj(xc, w_proj, cos8, sin8, qg, kg, ones_q, s, tm)
        else:
            xc, pool_in, rw_in, qh, kh, vh = _proj(None, w_proj, cos8, sin8, qg, kg, ones_q, s, tm, prev=pending)
        y_pool = _pool(pool_in.reshape(b, s, POOL_WIDTH), pool_bd, pool_scale[l].reshape(1, POOL_WIDTH))
        shared, logw, khnb, post = _rwkv_prep(rw_in.reshape(b, s, RWKV_PAD), mup, mun, w_lora, lora_bias,
                                        rw_k_k[l].reshape(1, w), rw_k_a[l].reshape(1, w),
                                        rw_r_k[l].reshape(1, w), ones_r, tm)
        wkv_f, wkv_r = _wkv(shared, logw, khnb, min(WKV_TILE, s))
        y_attn = _attn(qh, kh, vh, s, min(ATTN_Q_TILE, s))
        x1, x1p, route = _outproj(xc, y_pool.reshape(n, POOL_WIDTH), wkv_f.reshape(n, w), wkv_r.reshape(n, w),
                                  post.reshape(n, 2 * w), y_attn, wo,
                                  rw_gn_g[l].reshape(1, w), rw_gn_b[l].reshape(1, w), ones_r,
                                  ln1_g[l].reshape(1, d), ln1_b[l].reshape(1, d), wrh, wrl, br, tm)

        row_tok, pos, block_exp, n_used = _dispatch(route, n)
        split = block_exp.shape[0] // 2 * MOE_BLOCK
        xs_parts = [_gather_rows(x1p, row_tok[:split]), _gather_rows(x1p, row_tok[split:])]
        ys = _experts(block_exp, n_used, xs_parts, exp_gate, exp_up, exp_down, l)
        yab = _gather_rows(ys, pos)
        pending = (x1, yab, route, ln2_g[l].reshape(1, d), ln2_b[l].reshape(1, d))
    return _combine(*pending, tm).reshape(b, s, d)
```

```python
import functools
import math

import jax
import jax.numpy as jnp
from jax import lax
from jax.experimental import pallas as pl
from jax.experimental.pallas import tpu as pltpu
from jax.experimental.pallas import tpu_sc as plsc

F32 = jnp.float32
BF16 = jnp.bfloat16

D_MODEL = 1024
DEPTH = 4
GRID_W = 64
HEAD_DIM = 64
POOL_WIDTH = 256
POOL_WINDOWS = (2, 4, 8, 16)
POOL_GROUP = 64
RWKV_WIDTH = 256
RWKV_HEADS = 4
DECAY_LORA = 32
AAA_LORA = 32
GATE_LORA = 64
GN_EPS = 64e-5
RWKV_IN = 960
RWKV_PAD = 1024
ATTN_WIDTH = 512
ATTN_HEADS = 8
ATTN_KV_WIDTH = 128
ATTN_IN = ATTN_WIDTH + 2 * ATTN_KV_WIDTH
ROPE_THETA = 10000.0
QK_EPS = 1e-6
N_GROUPS = 4
EXPERTS_PER_GROUP = 8
N_EXPERTS = 32
EXPERT_HIDDEN = 512
MOE_BLOCK = 512
DEEPNORM_ALPHA = float((2 * DEPTH) ** 0.25)
LN_EPS = 1e-5
PROJ_WIDTH = POOL_WIDTH + RWKV_PAD + ATTN_IN
ROUTE_LANES = 128
WKV_CHUNK = 64
ROW_TILE = 512
WKV_TILE = 256
ATTN_Q_TILE = 512
VMEM_LIMIT = 48 * 1024 * 1024


def _cparams(sem):
    return pltpu.CompilerParams(dimension_semantics=sem, vmem_limit_bytes=VMEM_LIMIT)


def _dot(a, b):
    return jnp.dot(a, b, preferred_element_type=F32)


def _dot_nt(a, b):
    return lax.dot_general(a, b, (((1,), (1,)), ((), ())), preferred_element_type=F32)


def _dot_tn(a, b):
    return lax.dot_general(a, b, (((0,), (0,)), ((), ())), preferred_element_type=F32)


def _split2(x):
    hi = x.astype(BF16)
    lo = (x - hi.astype(F32)).astype(BF16)
    return hi, lo


def _split3(x):
    hi = x.astype(BF16)
    r1 = x - hi.astype(F32)
    mid = r1.astype(BF16)
    lo = (r1 - mid.astype(F32)).astype(BF16)
    return hi, mid, lo


HALF_D = D_MODEL // 2
U32 = jnp.uint32


def _pack_rows(x):
    hi = lax.bitcast_convert_type(x[:, :HALF_D].astype(BF16).astype(F32), U32)
    lo = lax.bitcast_convert_type(x[:, HALF_D:].astype(BF16).astype(F32), U32)
    return hi | (lo >> 16)


def _unpack_rows(u):
    a = lax.bitcast_convert_type(u & jnp.uint32(0xFFFF0000), F32).astype(BF16)
    b = lax.bitcast_convert_type(u << 16, F32).astype(BF16)
    return a, b


def _headsum_sq(x, ones_bf16):
    return _dot((x * x).astype(BF16), ones_bf16)


def _headsum(x, ones_bf16):
    hi, lo = _split2(x)
    return _dot(hi, ones_bf16) + _dot(lo, ones_bf16)


ROPE_QUARTER = HEAD_DIM // 4


def _rope(x, cos, sin_signed):
    n = x.shape[-1]
    lane = lax.broadcasted_iota(jnp.int32, x.shape, 1)
    first = (lane // ROPE_QUARTER) % 2 == 0
    partner = jnp.where(first, pltpu.roll(x, n - ROPE_QUARTER, 1), pltpu.roll(x, ROPE_QUARTER, 1))
    return x * cos + partner * sin_signed


def _layer_norm(z, g, b):
    mu = jnp.mean(z, axis=-1, keepdims=True)
    zc = z - mu
    var = jnp.mean(zc * zc, axis=-1, keepdims=True)
    return zc * lax.rsqrt(var + LN_EPS) * g + b


def _combine_rows(x1, ya, yb, route, g, b):
    g1 = route[:, 2:3]
    g2 = route[:, 3:4]
    a_hi, a_lo = _unpack_rows(ya)
    b_hi, b_lo = _unpack_rows(yb)
    m = jnp.concatenate([g1 * a_hi.astype(F32) + g2 * b_hi.astype(F32),
                         g1 * a_lo.astype(F32) + g2 * b_lo.astype(F32)], axis=1)
    return _layer_norm(DEEPNORM_ALPHA * x1 + m, g, b)


def _proj_kernel(x_ref, w_ref, cos_ref, sin_ref, qg_ref, kg_ref, onesq_ref,
                 pool_ref, rw_ref, q_ref, k_ref, v_ref):
    _proj_rows(x_ref[...], w_ref, cos_ref, sin_ref, qg_ref, kg_ref, onesq_ref,
               pool_ref, rw_ref, q_ref, k_ref, v_ref)


def _combine_proj_kernel(x1_ref, ya_ref, yb_ref, route_ref, lg_ref, lb_ref,
                         w_ref, cos_ref, sin_ref, qg_ref, kg_ref, onesq_ref,
                         x_ref, pool_ref, rw_ref, q_ref, k_ref, v_ref):
    x = _combine_rows(x1_ref[...], ya_ref[...], yb_ref[...], route_ref[...], lg_ref[...], lb_ref[...])
    x_ref[...] = x
    _proj_rows(x, w_ref, cos_ref, sin_ref, qg_ref, kg_ref, onesq_ref, pool_ref, rw_ref, q_ref, k_ref, v_ref)


def _proj_rows(x, w_ref, cos_ref, sin_ref, qg_ref, kg_ref, onesq_ref, pool_ref, rw_ref, q_ref, k_ref, v_ref):
    xb = x.astype(BF16)
    pool_ref[...] = _dot(xb, w_ref[:, 0:POOL_WIDTH])
    rw_ref[...] = _dot(xb, w_ref[:, POOL_WIDTH:POOL_WIDTH + RWKV_PAD])
    at = _dot(xb, w_ref[:, POOL_WIDTH + RWKV_PAD:PROJ_WIDTH])

    q = at[:, 0:ATTN_WIDTH]
    k = at[:, ATTN_WIDTH:ATTN_WIDTH + ATTN_KV_WIDTH]
    v = at[:, ATTN_WIDTH + ATTN_KV_WIDTH:ATTN_IN]
    ones_q = onesq_ref[...]
    ones_k = onesq_ref[0:ATTN_KV_WIDTH, 0:ATTN_KV_WIDTH]
    inv = 1.0 / HEAD_DIM
    qn = q * lax.rsqrt(_headsum_sq(q, ones_q) * inv + QK_EPS) * qg_ref[...]
    kn = k * lax.rsqrt(_headsum_sq(k, ones_k) * inv + QK_EPS) * kg_ref[...]
    cos = cos_ref[...]
    sin = sin_ref[...]
    q_ref[...] = (_rope(qn, cos, sin) * (HEAD_DIM ** -0.5 * math.log2(math.e))).astype(BF16)
    k_ref[...] = _rope(kn, cos[:, 0:ATTN_KV_WIDTH], sin[:, 0:ATTN_KV_WIDTH]).astype(BF16)
    v_ref[...] = v.astype(BF16)


def _proj(x2d, w, cos8, sin8, qg, kg, ones_q, seq, tm, prev=None):
    n = (x2d if prev is None else prev[0]).shape[0]
    nts = seq // tm
    nt = n // tm
    row = lambda width: pl.BlockSpec((tm, width), lambda i: (i, 0))
    const = lambda r, c: pl.BlockSpec((r, c), lambda i: (0, 0))
    proj_in = [const(D_MODEL, PROJ_WIDTH),
               pl.BlockSpec((tm, ATTN_WIDTH), lambda i: (i % nts, 0)),
               pl.BlockSpec((tm, ATTN_WIDTH), lambda i: (i % nts, 0)),
               const(1, ATTN_WIDTH), const(1, ATTN_KV_WIDTH), const(ATTN_WIDTH, ATTN_WIDTH)]
    proj_out = [row(POOL_WIDTH), row(RWKV_PAD), row(ATTN_WIDTH), row(ATTN_KV_WIDTH), row(ATTN_KV_WIDTH)]
    proj_shape = [jax.ShapeDtypeStruct((n, POOL_WIDTH), F32),
                  jax.ShapeDtypeStruct((n, RWKV_PAD), F32),
                  jax.ShapeDtypeStruct((n, ATTN_WIDTH), BF16),
                  jax.ShapeDtypeStruct((n, ATTN_KV_WIDTH), BF16),
                  jax.ShapeDtypeStruct((n, ATTN_KV_WIDTH), BF16)]
    if prev is None:
        return pl.pallas_call(
            _proj_kernel,
            grid=(nt,),
            in_specs=[row(D_MODEL)] + proj_in,
            out_specs=proj_out,
            out_shape=proj_shape,
            compiler_params=_cparams(("parallel",)),
            name="proj",
        )(x2d, w, cos8, sin8, qg, kg, ones_q)
    x1, yab, route, lg, lb = prev
    return pl.pallas_call(
        _combine_proj_kernel,
        grid=(nt,),
        in_specs=[row(D_MODEL), row(HALF_D), pl.BlockSpec((tm, HALF_D), lambda i: (i + nt, 0)),
                  row(ROUTE_LANES), const(1, D_MODEL), const(1, D_MODEL)] + proj_in,
        out_specs=[row(D_MODEL)] + proj_out,
        out_shape=[jax.ShapeDtypeStruct((n, D_MODEL), F32)] + proj_shape,
        compiler_params=_cparams(("parallel",)),
        name="combine_proj",
    )(x1, yab, yab, route, lg, lb, w, cos8, sin8, qg, kg, ones_q)


def _pool_kernel(u_ref, w_ref, scale_ref, o_ref):
    s = u_ref.shape[1]
    u = u_ref[0]
    t = lax.broadcasted_iota(jnp.int32, (s, POOL_WIDTH), 0)

    def down(x, k):
        return jnp.where(t >= k, pltpu.roll(x, k, 0), 0.0)

    def up(x, k):
        return jnp.where(t < s - k, pltpu.roll(x, s - k, 0), 0.0)

    left = [down(u, 1)]
    right = [u]
    for k in (1, 2, 4):
        left.append(left[-1] + down(left[-1], k))
        right.append(right[-1] + up(right[-1], k))
    sums = [l + r for l, r in zip(left, right)]

    grp = lax.broadcasted_iota(jnp.int32, (s, POOL_WIDTH), 1) // POOL_GROUP
    half = jnp.where(grp == 0, 1, jnp.where(grp == 1, 2, jnp.where(grp == 2, 4, 8)))
    cnt = (jnp.minimum(t + half, s) - jnp.maximum(t - half, 0)).astype(F32)
    tot = jnp.where(grp == 0, sums[0], jnp.where(grp == 1, sums[1], jnp.where(grp == 2, sums[2], sums[3])))
    d = tot / cnt - u
    o_ref[0] = _dot(d.astype(BF16), w_ref[...]) * scale_ref[...]


def _pool(u3d, w_bd, scale):
    b, s, _ = u3d.shape
    return pl.pallas_call(
        _pool_kernel,
        grid=(b,),
        in_specs=[pl.BlockSpec((1, s, POOL_WIDTH), lambda i: (i, 0, 0)),
                  pl.BlockSpec((POOL_WIDTH, POOL_WIDTH), lambda i: (0, 0)),
                  pl.BlockSpec((1, POOL_WIDTH), lambda i: (0, 0))],
        out_specs=pl.BlockSpec((1, s, POOL_WIDTH), lambda i: (i, 0, 0)),
        out_shape=jax.ShapeDtypeStruct((b, s, POOL_WIDTH), F32),
        compiler_params=_cparams(("parallel",)),
        name="pool",
    )(u3d, w_bd, scale)


_SHIFT_HALO = 8


def _rwkv_prep_kernel(cur_ref, prev_ref, next_ref, mup_ref, mun_ref, wl_ref, bias_ref,
                      kk_ref, ka_ref, rk_ref, ones_ref,
                      sh_ref, lw_ref, kn_ref, post_ref):
    ts = cur_ref.shape[1]
    ti = pl.program_id(1)
    nt = pl.num_programs(1)
    cur = cur_ref[0]
    row = lax.broadcasted_iota(jnp.int32, cur.shape, 0)
    before = jnp.where(ti > 0, prev_ref[0, _SHIFT_HALO - 1:_SHIFT_HALO, :], 0.0)
    after = jnp.where(ti < nt - 1, next_ref[0, 0:1, :], 0.0)
    prev = jnp.where(row == 0, before, pltpu.roll(cur, 1, 0))
    nxt = jnp.where(row == ts - 1, after, pltpu.roll(cur, ts - 1, 0))
    f = cur + mup_ref[...] * (prev - cur) + mun_ref[...] * (nxt - cur)

    w = RWKV_WIDTH
    r = f[:, 0:w]
    k = f[:, w:2 * w]
    v = f[:, 2 * w:3 * w]
    lora = f[:, 3 * w:4 * w]
    lane = lax.broadcasted_iota(jnp.int32, lora.shape, 1)
    z = jnp.where(lane < 2 * DECAY_LORA, jnp.tanh(lora),
                  jnp.where(lane < 2 * DECAY_LORA + 2 * AAA_LORA, lora,
                            jnp.where(lane < 2 * DECAY_LORA + 2 * AAA_LORA + GATE_LORA,
                                      jax.nn.sigmoid(lora), 0.0)))
    up = _dot(z.astype(BF16), wl_ref[...]) + bias_ref[...]
    ones = ones_ref[...]

    kk0 = k * kk_ref[...]
    nrm = jnp.sqrt(_headsum_sq(kk0, ones))
    kk = kk0 / jnp.maximum(nrm, 1e-12)
    sh_ref[0, :, 0:w] = r.astype(BF16)
    sh_ref[0, :, w:2 * w] = v.astype(BF16)
    sh_ref[0, :, 2 * w:3 * w] = kk.astype(BF16)

    ksum = jnp.zeros_like(k)
    for di in range(2):
        logw = -math.exp(-0.5) * jax.nn.sigmoid(up[:, di * w:(di + 1) * w])
        a = jax.nn.sigmoid(up[:, (2 + di) * w:(3 + di) * w])
        kh = k * (1.0 + (a - 1.0) * ka_ref[...])
        ksum = ksum + kh
        lw_ref[di, 0] = logw
        kn_ref[di, 0, :, 0:w] = kh.astype(BF16)
        kn_ref[di, 0, :, w:2 * w] = (-(a * kk)).astype(BF16)
    bonus = _headsum(r * ksum * rk_ref[...], ones) * v
    post_ref[0, :, 0:w] = bonus
    post_ref[0, :, w:2 * w] = up[:, 4 * w:5 * w]


def _rwkv_prep(rw3d, mup, mun, wl, bias, k_k, k_a, r_k, ones, ts):
    b, s, _ = rw3d.shape
    nt = s // ts
    hb = ts // _SHIFT_HALO
    nhb = s // _SHIFT_HALO
    w = RWKV_WIDTH
    vec = lambda width: pl.BlockSpec((1, width), lambda i, j: (0, 0))
    return pl.pallas_call(
        _rwkv_prep_kernel,
        grid=(b, nt),
        in_specs=[pl.BlockSpec((1, ts, RWKV_PAD), lambda i, j: (i, j, 0)),
                  pl.BlockSpec((1, _SHIFT_HALO, RWKV_PAD),
                               lambda i, j: (i, jnp.maximum(j * hb - 1, 0), 0)),
                  pl.BlockSpec((1, _SHIFT_HALO, RWKV_PAD),
                               lambda i, j: (i, jnp.minimum((j + 1) * hb, nhb - 1), 0)),
                  vec(RWKV_PAD), vec(RWKV_PAD),
                  pl.BlockSpec((w, 5 * w), lambda i, j: (0, 0)),
                  vec(5 * w), vec(w), vec(w), vec(w),
                  pl.BlockSpec((w, w), lambda i, j: (0, 0))],
        out_specs=[pl.BlockSpec((1, ts, 3 * w), lambda i, j: (i, j, 0)),
                   pl.BlockSpec((2, 1, ts, w), lambda i, j: (0, i, j, 0)),
                   pl.BlockSpec((2, 1, ts, 2 * w), lambda i, j: (0, i, j, 0)),
                   pl.BlockSpec((1, ts, 2 * w), lambda i, j: (i, j, 0))],
        out_shape=[jax.ShapeDtypeStruct((b, s, 3 * w), BF16),
                   jax.ShapeDtypeStruct((2, b, s, w), F32),
                   jax.ShapeDtypeStruct((2, b, s, 2 * w), BF16),
                   jax.ShapeDtypeStruct((b, s, 2 * w), F32)],
        compiler_params=_cparams(("parallel", "parallel")),
        name="rwkv_prep",
    )(rw3d, rw3d, rw3d, mup, mun, wl, bias, k_k, k_a, r_k, ones)


def _expand_bd(x_bf16, mask_bd):
    return jnp.where(mask_bd, jnp.concatenate([x_bf16] * RWKV_HEADS, axis=0), jnp.zeros((), BF16))


def _wkv_kernel(shf_ref, shr_ref, lwf_ref, lwr_ref, knf_ref, knr_ref, yf_ref, yr_ref, hf_ref, hr_ref):
    c = WKV_CHUNK
    w = RWKV_WIDTH
    tt = shf_ref.shape[1]
    ncs = tt // c

    @pl.when(pl.program_id(1) == 0)
    def _():
        hf_ref[...] = jnp.zeros_like(hf_ref)
        hr_ref[...] = jnp.zeros_like(hr_ref)

    row_c = lax.broadcasted_iota(jnp.int32, (c, c), 0)
    col_c = lax.broadcasted_iota(jnp.int32, (c, c), 1)
    t_i = lax.broadcasted_iota(jnp.int32, (c, w), 0)
    s_i = lax.broadcasted_iota(jnp.int32, (c, w), 1) % c
    eye_c = (s_i == t_i).astype(F32)
    row_w = lax.broadcasted_iota(jnp.int32, (w, w), 0)
    col_w = lax.broadcasted_iota(jnp.int32, (w, w), 1)
    mask_bd = (row_w // c) == (col_w // HEAD_DIM)
    eye_w = row_w == col_w
    tri_incl = [jnp.where(col_c <= row_c, 1.0, 0.0).astype(BF16),
                jnp.where(col_c >= row_c, 1.0, 0.0).astype(BF16)]
    strict = [s_i < t_i, s_i > t_i]
    incl = [s_i <= t_i, s_i >= t_i]

    def bf(x):
        return x.astype(BF16)

    def bd(x):
        return _expand_bd(bf(x), mask_bd)

    chunks = []
    for ci in range(ncs):
        chunks.append((0, ci * c, shf_ref, lwf_ref, knf_ref))
        chunks.append((1, (ncs - 1 - ci) * c, shr_ref, lwr_ref, knr_ref))

    st = []
    for d, start, sh_ref, lw_ref, kn_ref in chunks:
        rows = pl.ds(start, c)
        st.append(dict(d=d, rows=rows,
                       r=sh_ref[0, rows, 0:w].astype(F32), v_bf=sh_ref[0, rows, w:2 * w],
                       kk=sh_ref[0, rows, 2 * w:3 * w].astype(F32), lw=lw_ref[0, rows, :],
                       kh=kn_ref[0, rows, 0:w].astype(F32), nb=kn_ref[0, rows, w:2 * w].astype(F32)))

    for q in st:
        l_hi, l_mid, l_lo = _split3(q["lw"])
        tri = tri_incl[q["d"]]
        q["cum"] = _dot(tri, l_hi) + _dot(tri, l_mid) + _dot(tri, l_lo)
    for q in st:
        cum, lw = q["cum"], q["lw"]
        tot = jnp.sum(lw, axis=0, keepdims=True)
        e_inv = jnp.exp(-cum)
        e_end = jnp.exp(tot - cum)
        q["tot"] = tot
        q["a_bar"] = q["kk"] * jnp.exp(cum - lw)
        q["r_bar"] = q["r"] * jnp.exp(cum)
        q["b_hat"] = bf(q["nb"] * e_end)
        q["k_hat"] = bf(q["kh"] * e_end)
        q["v_bd"] = _expand_bd(q["v_bf"], mask_bd)
        q["lhs"] = jnp.concatenate([bf(q["a_bar"]), bf(q["r_bar"])], axis=0)
        q["rhs"] = jnp.concatenate([bd(q["nb"] * e_inv), bd(q["kh"] * e_inv)], axis=0)
    for q in st:
        sc = _dot_nt(q["lhs"], q["rhs"])
        sm, im = strict[q["d"]], incl[q["d"]]
        q["x"] = jnp.where(sm, sc[0:c, 0:w], 0.0)
        q["a_ak"] = bf(jnp.where(sm, sc[0:c, w:2 * w], 0.0))
        q["m_rb"] = bf(jnp.where(im, sc[c:2 * c, 0:w], 0.0))
        q["m_rk"] = bf(jnp.where(im, sc[c:2 * c, w:2 * w], 0.0))
        q["tinv"] = eye_c + q["x"]

    for q in st:
        q["x"] = _dot(bf(q["x"]), bd(q["x"]))
    n_lv = int(math.log2(c)) - 1
    for lv in range(n_lv):
        last = lv == n_lv - 1
        for q in st:
            xb = bd(q["x"])
            if last:
                q["tinv"] = q["tinv"] + _dot(bf(q["tinv"]), xb)
            else:
                res = _dot(jnp.concatenate([bf(q["x"]), bf(q["tinv"])], axis=0), xb)
                q["tinv"] = q["tinv"] + res[c:2 * c]
                q["x"] = res[0:c]
    for q in st:
        res = _dot(jnp.concatenate([q["a_ak"], q["m_rk"]], axis=0), q["v_bd"])
        q["akv"] = res[0:c]
        q["mrkv"] = res[c:2 * c]
    for q in st:
        tinv_bf = bf(q["tinv"])
        q["a_pr"] = bf(_dot(tinv_bf, bd(q["a_bar"])))
        q["v_pr"] = bf(_dot(tinv_bf, bd(q["akv"])))
    for q in st:
        q["r_pr"] = bf(q["r_bar"] + _dot(q["m_rb"], _expand_bd(q["a_pr"], mask_bd)))
        q["y_pr"] = _dot(q["m_rb"], _expand_bd(q["v_pr"], mask_bd)) + q["mrkv"]
    for q in st:
        q["p_bd"] = bf(jnp.where(eye_w, jnp.exp(q["tot"]), 0.0)
                       + jnp.where(mask_bd, _dot_tn(q["b_hat"], q["a_pr"]), 0.0))
        q["q_bd"] = jnp.where(mask_bd,
                              _dot_tn(jnp.concatenate([q["b_hat"], q["k_hat"]], axis=0),
                                      jnp.concatenate([q["v_pr"], q["v_bf"]], axis=0)), 0.0)

    h = [hf_ref[...], hr_ref[...]]
    y_refs = [yf_ref, yr_ref]
    for q in st:
        d = q["d"]
        res = _dot(jnp.concatenate([q["r_pr"], q["p_bd"]], axis=0), bf(h[d]))
        y_refs[d][0, q["rows"], :] = res[0:c] + q["y_pr"]
        h[d] = res[c:c + w] + q["q_bd"]
    hf_ref[...] = h[0]
    hr_ref[...] = h[1]


def _wkv(shared, logw, khnb, tt):
    b, s, _ = shared.shape
    nt = s // tt
    w = RWKV_WIDTH
    fwd = lambda i, j: (i, j, 0)
    bwd = lambda i, j: (i, nt - 1 - j, 0)
    return pl.pallas_call(
        _wkv_kernel,
        grid=(b, nt),
        in_specs=[pl.BlockSpec((1, tt, 3 * w), fwd), pl.BlockSpec((1, tt, 3 * w), bwd),
                  pl.BlockSpec((None, 1, tt, w), lambda i, j: (0, i, j, 0)),
                  pl.BlockSpec((None, 1, tt, w), lambda i, j: (1, i, nt - 1 - j, 0)),
                  pl.BlockSpec((None, 1, tt, 2 * w), lambda i, j: (0, i, j, 0)),
                  pl.BlockSpec((None, 1, tt, 2 * w), lambda i, j: (1, i, nt - 1 - j, 0))],
        out_specs=[pl.BlockSpec((1, tt, w), fwd), pl.BlockSpec((1, tt, w), bwd)],
        out_shape=[jax.ShapeDtypeStruct((b, s, w), F32), jax.ShapeDtypeStruct((b, s, w), F32)],
        scratch_shapes=[pltpu.VMEM((w, w), F32), pltpu.VMEM((w, w), F32)],
        compiler_params=_cparams(("parallel", "arbitrary")),
        name="wkv",
    )(shared, shared, logw, logw, khnb, khnb)


def _attn_kernel(q_ref, k_ref, v_ref, o_ref):
    tq = q_ref.shape[0]
    k = k_ref[...]
    v = v_ref[...]
    lane = lax.broadcasted_iota(jnp.int32, (tq, ATTN_KV_WIDTH), 1)
    low = lane < HEAD_DIM
    zero = jnp.zeros((), BF16)
    v_lane = lax.broadcasted_iota(jnp.int32, v.shape, 1)
    one = jnp.ones((), BF16)
    v_ext = [jnp.where(v_lane < HEAD_DIM, v, one), jnp.where(v_lane >= HEAD_DIM, v, one)]
    for j in range(ATTN_HEADS // 2):
        qb = q_ref[:, j * ATTN_KV_WIDTH:(j + 1) * ATTN_KV_WIDTH]
        outs = []
        for half in range(2):
            qm = jnp.where(low if half == 0 else ~low, qb, zero)
            s = _dot_nt(qm, k)
            m = jnp.max(s, axis=-1, keepdims=True)
            p = jnp.exp2(s - m)
            acc = _dot(p.astype(BF16), v_ext[half])
            l = acc[:, HEAD_DIM:HEAD_DIM + 1] if half == 0 else acc[:, 0:1]
            outs.append(acc / l)
        o_ref[:, j * ATTN_KV_WIDTH:(j + 1) * ATTN_KV_WIDTH] = jnp.where(low, outs[0], outs[1]).astype(BF16)


def _attn(q, k, v, seq, tq):
    n = q.shape[0]
    nq = seq // tq
    return pl.pallas_call(
        _attn_kernel,
        grid=(n // seq, nq),
        in_specs=[pl.BlockSpec((tq, ATTN_WIDTH), lambda b, i: (b * nq + i, 0)),
                  pl.BlockSpec((seq, ATTN_KV_WIDTH), lambda b, i: (b, 0)),
                  pl.BlockSpec((seq, ATTN_KV_WIDTH), lambda b, i: (b, 0))],
        out_specs=pl.BlockSpec((tq, ATTN_WIDTH), lambda b, i: (b * nq + i, 0)),
        out_shape=jax.ShapeDtypeStruct((n, ATTN_WIDTH), BF16),
        compiler_params=_cparams(("parallel", "parallel")),
        name="attn",
    )(q, k, v)


def _outproj_kernel(x_ref, pool_ref, wkvf_ref, wkvr_ref, post_ref, attn_ref, wo_ref, gng_ref, gnb_ref,
                    ones_ref, lg_ref, lb_ref, wrh_ref, wrl_ref, br_ref,
                    x1_ref, x1p_ref, route_ref):
    w = RWKV_WIDTH
    ones = ones_ref[...]
    wkv = wkvf_ref[...] + wkvr_ref[...]
    inv = 1.0 / HEAD_DIM
    mu = _headsum(wkv, ones) * inv
    cen = wkv - mu
    var = _headsum_sq(cen, ones) * inv
    yr = cen * lax.rsqrt(var + GN_EPS) * gng_ref[...] + gnb_ref[...] + post_ref[:, 0:w]
    yr = yr * post_ref[:, w:2 * w]
    y = (_dot(pool_ref[...].astype(BF16), wo_ref[0:POOL_WIDTH, :])
         + _dot(yr.astype(BF16), wo_ref[POOL_WIDTH:POOL_WIDTH + w, :])
         + _dot(attn_ref[...], wo_ref[POOL_WIDTH + w:D_MODEL, :]))
    x1 = _layer_norm(DEEPNORM_ALPHA * x_ref[...] + y, lg_ref[...], lb_ref[...])
    x1_ref[...] = x1
    x1p_ref[...] = _pack_rows(x1)

    xh, xl = _split2(x1)
    logits = (_dot(xh, wrh_ref[...]) + _dot(xl, wrh_ref[...]) + _dot(xh, wrl_ref[...])) + br_ref[...]
    lane = lax.broadcasted_iota(jnp.int32, logits.shape, 1)
    lane_f = lane.astype(F32)
    lane_grp_f = ((lane - N_GROUPS) // EXPERTS_PER_GROUP).astype(F32)
    neg = -jnp.inf
    big = jnp.float32(1 << 20)
    gl = jnp.where(lane < N_GROUPS, logits, neg)
    gmax = jnp.max(gl, axis=-1, keepdims=True)
    grp = jnp.min(jnp.where(gl == gmax, lane_f, big), axis=-1, keepdims=True)
    gw = 1.0 / jnp.sum(jnp.exp(gl - gmax), axis=-1, keepdims=True)
    in_grp = (lane >= N_GROUPS) & (lane < N_GROUPS + N_EXPERTS) & (lane_grp_f == grp)
    el = jnp.where(in_grp, logits, neg)
    v1 = jnp.max(el, axis=-1, keepdims=True)
    i1 = jnp.min(jnp.where(el == v1, lane_f, big), axis=-1, keepdims=True)
    el2 = jnp.where(lane_f == i1, neg, el)
    v2 = jnp.max(el2, axis=-1, keepdims=True)
    i2 = jnp.min(jnp.where(el2 == v2, lane_f, big), axis=-1, keepdims=True)
    e21 = jnp.exp(v2 - v1)
    g1 = gw / (1.0 + e21)
    g2 = gw * e21 / (1.0 + e21)
    route = jnp.where(lane == 0, i1 - N_GROUPS,
                      jnp.where(lane == 1, i2 - N_GROUPS,
                                jnp.where(lane == 2, g1, jnp.where(lane == 3, g2, 0.0))))
    route_ref[...] = route


def _outproj(x2d, pool2d, wkvf, wkvr, post2d, attn2d, wo, gng, gnb, ones, lg, lb, wrh, wrl, br, tm):
    n = x2d.shape[0]
    w = RWKV_WIDTH
    row = lambda width: pl.BlockSpec((tm, width), lambda i: (i, 0))
    vec = lambda width: pl.BlockSpec((1, width), lambda i: (0, 0))
    return pl.pallas_call(
        _outproj_kernel,
        grid=(n // tm,),
        in_specs=[row(D_MODEL), row(POOL_WIDTH), row(w), row(w),
                  row(2 * w), row(ATTN_WIDTH),
                  pl.BlockSpec((D_MODEL, D_MODEL), lambda i: (0, 0)),
                  vec(w), vec(w),
                  pl.BlockSpec((w, w), lambda i: (0, 0)),
                  vec(D_MODEL), vec(D_MODEL),
                  pl.BlockSpec((D_MODEL, ROUTE_LANES), lambda i: (0, 0)),
                  pl.BlockSpec((D_MODEL, ROUTE_LANES), lambda i: (0, 0)),
                  vec(ROUTE_LANES)],
        out_specs=[row(D_MODEL), row(HALF_D), row(ROUTE_LANES)],
        out_shape=[jax.ShapeDtypeStruct((n, D_MODEL), F32),
                   jax.ShapeDtypeStruct((n, HALF_D), U32),
                   jax.ShapeDtypeStruct((n, ROUTE_LANES), F32)],
        compiler_params=_cparams(("parallel",)),
        name="outproj",
    )(x2d, pool2d, wkvf, wkvr, post2d, attn2d, wo, gng, gnb, ones, lg, lb, wrh, wrl, br)


def _expert_kernel(bexp_ref, nused_ref, xs_ref, wg_ref, wu_ref, wd_ref, *rest):
    ys_ref, wgb_ref, wub_ref, wdb_ref = rest[-4:]
    i = pl.program_id(0)
    used = i < nused_ref[0]

    @pl.when(used & ((i == 0) | (bexp_ref[i] != bexp_ref[jnp.maximum(i - 1, 0)])))
    def _():
        wgb_ref[...] = wg_ref[0].astype(BF16)
        wub_ref[...] = wu_ref[0].astype(BF16)
        wdb_ref[...] = wd_ref[0].astype(BF16)

    @pl.when(used)
    def _():
        xa, xb = _unpack_rows(xs_ref[...])
        h1 = _dot(xa, wgb_ref[0:HALF_D, :]) + _dot(xb, wgb_ref[HALF_D:D_MODEL, :])
        h2 = _dot(xa, wub_ref[0:HALF_D, :]) + _dot(xb, wub_ref[HALF_D:D_MODEL, :])
        h = (h1 * jax.nn.sigmoid(h1) * h2).astype(BF16)
        ys_ref[...] = _pack_rows(_dot(h, wdb_ref[...]))

    @pl.when(jnp.logical_not(used))
    def _():
        ys_ref[...] = jnp.zeros_like(ys_ref)


def _experts(block_exp, n_used, xs_parts, wg, wu, wd, layer):
    p = sum(xs.shape[0] for xs in xs_parts)
    wmap = lambda i, be, nu: (layer, be[i], 0, 0)
    ys = None
    first = 0
    for xs in xs_parts:
        nb = xs.shape[0] // MOE_BLOCK
        in_specs = [pl.BlockSpec((MOE_BLOCK, HALF_D), lambda i, be, nu: (i, 0)),
                    pl.BlockSpec((None, 1, D_MODEL, EXPERT_HIDDEN), wmap),
                    pl.BlockSpec((None, 1, D_MODEL, EXPERT_HIDDEN), wmap),
                    pl.BlockSpec((None, 1, EXPERT_HIDDEN, D_MODEL), wmap)]
        args = [block_exp[first:first + nb], n_used - first, xs, wg, wu, wd]
        aliases = {}
        if ys is not None:
            in_specs.append(pl.BlockSpec(memory_space=pl.ANY))
            aliases = {len(args): 0}
            args.append(ys)
        grid_spec = pltpu.PrefetchScalarGridSpec(
            num_scalar_prefetch=2,
            grid=(nb,),
            in_specs=in_specs,
            out_specs=pl.BlockSpec((MOE_BLOCK, HALF_D), functools.partial(_block_at, first)),
            scratch_shapes=[pltpu.VMEM((D_MODEL, EXPERT_HIDDEN), BF16),
                            pltpu.VMEM((D_MODEL, EXPERT_HIDDEN), BF16),
                            pltpu.VMEM((EXPERT_HIDDEN, D_MODEL), BF16)],
        )
        ys = pl.pallas_call(
            _expert_kernel,
            grid_spec=grid_spec,
            out_shape=jax.ShapeDtypeStruct((p, HALF_D), U32),
            input_output_aliases=aliases,
            compiler_params=_cparams(("arbitrary",)),
            name="experts",
        )(*args)
        first += nb
    return ys


def _block_at(first, i, be, nu):
    return (i + first, 0)


SC_CORES = 2
SC_SUBCORES = 16
SC_GATHER_ROWS = 64


def _sc_gather(table, idx):
    v, d = table.shape
    b = idx.shape[0]
    nw = SC_CORES * SC_SUBCORES
    ch = SC_GATHER_ROWS
    assert b % (nw * ch * 2) == 0
    b_per_w = b // nw
    nch = b_per_w // ch
    mesh = plsc.VectorSubcoreMesh(core_axis_name="c", subcore_axis_name="s")

    @functools.partial(
        pl.kernel, mesh=mesh,
        out_type=jax.ShapeDtypeStruct((b, d), table.dtype),
        scratch_types=[pltpu.VMEM((b_per_w,), jnp.int32),
                       pltpu.VMEM((2, ch, d), table.dtype),
                       pltpu.SemaphoreType.DMA, pltpu.SemaphoreType.DMA,
                       pltpu.SemaphoreType.DMA, pltpu.SemaphoreType.DMA],
    )
    def gather_kernel(table_hbm, idx_hbm, out_hbm, idx_v, rows_v, gsem0, gsem1, wsem0, wsem1):
        wid = lax.axis_index("s") * SC_CORES + lax.axis_index("c")
        base = wid * b_per_w
        pltpu.sync_copy(idx_hbm.at[pl.ds(base, b_per_w)], idx_v)
        gsem = (gsem0, gsem1)
        wsem = (wsem0, wsem1)

        def gather(jj, slot):
            off = pl.multiple_of(jj * ch, ch)
            return pltpu.make_async_copy(table_hbm.at[idx_v.at[pl.ds(off, ch)]], rows_v.at[slot], gsem[slot])

        def write(jj, slot):
            off = pl.multiple_of(jj * ch, ch)
            return pltpu.make_async_copy(rows_v.at[slot], out_hbm.at[pl.ds(base + off, ch)], wsem[slot])

        gather(0, 0).start()

        @pl.loop(0, nch, step=2)
        def _(j):
            for slot in range(2):
                jj = j + slot

                gather(jj, slot).wait()

                @pl.when(jj >= 1)
                def _():
                    write(jj - 1, 1 - slot).wait()

                @pl.when(jj + 1 < nch)
                def _():
                    gather(jj + 1, 1 - slot).start()

                write(jj, slot).start()

        write(nch - 1, (nch - 1) % 2).wait()

    return gather_kernel(table, idx)


def _gather_rows(table, idx):
    b = idx.shape[0]
    unit = SC_CORES * SC_SUBCORES * SC_GATHER_ROWS * 2
    bp = -(-b // unit) * unit
    if bp == b:
        return _sc_gather(table, idx)
    return _sc_gather(table, jnp.pad(idx, (0, bp - b)))[:b]


def _combine_kernel(x1_ref, ya_ref, yb_ref, route_ref, lg_ref, lb_ref, o_ref):
    o_ref[...] = _combine_rows(x1_ref[...], ya_ref[...], yb_ref[...], route_ref[...], lg_ref[...], lb_ref[...])


def _combine(x1, yab, route, lg, lb, tm):
    n = x1.shape[0]
    nt = n // tm
    row = lambda width: pl.BlockSpec((tm, width), lambda i: (i, 0))
    vec = lambda width: pl.BlockSpec((1, width), lambda i: (0, 0))
    return pl.pallas_call(
        _combine_kernel,
        grid=(nt,),
        in_specs=[row(D_MODEL), row(HALF_D), pl.BlockSpec((tm, HALF_D), lambda i: (i + nt, 0)),
                  row(ROUTE_LANES), vec(D_MODEL), vec(D_MODEL)],
        out_specs=row(D_MODEL),
        out_shape=jax.ShapeDtypeStruct((n, D_MODEL), F32),
        compiler_params=_cparams(("parallel",)),
        name="combine",
    )(x1, yab, yab, route, lg, lb)


def _q_perm():
    order = [h for j in range(ATTN_HEADS // 2) for h in (j, j + ATTN_HEADS // 2)]
    return jnp.concatenate([jnp.arange(h * HEAD_DIM, (h + 1) * HEAD_DIM) for h in order])


def _block_diag(blocks):
    n = len(blocks)
    r, c = blocks[0].shape
    out = jnp.zeros((n * r, n * c), blocks[0].dtype)
    for i, blk in enumerate(blocks):
        out = out.at[i * r:(i + 1) * r, i * c:(i + 1) * c].set(blk)
    return out


def _rope_tables(seq):
    rows = seq // GRID_W
    row_id = jnp.repeat(jnp.arange(rows), GRID_W).astype(F32)
    col_id = jnp.tile(jnp.arange(GRID_W), rows).astype(F32)
    half = HEAD_DIM // 2
    inv_freq = ROPE_THETA ** (-jnp.arange(0, half, 2, dtype=F32) / half)
    ang_r = row_id[:, None] * inv_freq
    ang_c = col_id[:, None] * inv_freq
    ang = jnp.concatenate([ang_r, ang_r, ang_c, ang_c], -1)
    sign = jnp.where((jnp.arange(HEAD_DIM) // 16) % 2 == 0, -1.0, 1.0).astype(F32)
    cos8 = jnp.tile(jnp.cos(ang), (1, ATTN_HEADS))
    sin8 = jnp.tile(jnp.sin(ang) * sign, (1, ATTN_HEADS))
    return cos8, sin8


def _dispatch(route, n):
    m = n * 2
    experts = jnp.arange(N_EXPERTS, dtype=jnp.int32)
    e_flat = jnp.concatenate([route[:, 0], route[:, 1]]).astype(jnp.int32)
    pair = jnp.arange(m, dtype=jnp.int32)
    e_sorted, order = lax.sort((e_flat, pair), num_keys=1, is_stable=True)
    start = jnp.sum(e_sorted[None, :] < experts[:, None], axis=1, dtype=jnp.int32)
    counts = jnp.concatenate([start[1:], jnp.full((1,), m, jnp.int32)]) - start
    padded = (counts + MOE_BLOCK - 1) // MOE_BLOCK * MOE_BLOCK
    ends_p = jnp.sum(jnp.where(experts[None, :] <= experts[:, None], padded[None, :], 0), axis=1, dtype=jnp.int32)
    pstart = ends_p - padded
    delta = pstart - start
    n_blocks = -(-(m + N_EXPERTS * (MOE_BLOCK - 1)) // MOE_BLOCK)
    block_start = jnp.arange(n_blocks, dtype=jnp.int32) * MOE_BLOCK
    block_exp = jnp.minimum(jnp.sum(ends_p[None, :] <= block_start[:, None], axis=1, dtype=jnp.int32),
                            N_EXPERTS - 1)
    lane = jnp.arange(MOE_BLOCK, dtype=jnp.int32)[None, :]
    blk = block_exp[:, None] == experts[None, :]

    def per_block(table):
        return jnp.sum(jnp.where(blk, table[None, :], 0), axis=1, dtype=jnp.int32)

    valid = (block_start - per_block(pstart))[:, None] + lane < per_block(counts)[:, None]
    src = jnp.clip((block_start - per_block(delta))[:, None] + lane, 0, m - 1)
    run = order[src.reshape(-1)].reshape(n_blocks, MOE_BLOCK)
    row_tok = jnp.where(valid, run % n, (block_start[:, None] + lane) % n).reshape(-1)
    n_used = (ends_p[-1] // MOE_BLOCK).astype(jnp.int32).reshape(1)
    dest = pair + jnp.sum(jnp.where(e_sorted[None, :] == experts[:, None], delta[:, None], 0), axis=0,
                          dtype=jnp.int32)
    _, pos = lax.sort((order, dest), num_keys=1)
    return row_tok, pos, block_exp, n_used


def kernel(x, w_in, mu_prev, mu_next, pool_w, pool_scale, rw_w0, rw_w_up, rw_a0, rw_a_up, rw_g_up, rw_k_k, rw_k_a, rw_r_k, rw_gn_g, rw_gn_b, q_norm, k_norm, w_o, ln1_g, ln1_b, router_group, router_group_b, router_expert, router_expert_b, exp_gate, exp_up, exp_down, ln2_g, ln2_b):
    b, s, d = x.shape
    n = b * s
    w = RWKV_WIDTH
    tm = min(ROW_TILE, s)
    qperm = _q_perm()
    cos8, sin8 = _rope_tables(s)
    ones_q = _block_diag([jnp.ones((HEAD_DIM, HEAD_DIM), BF16)] * ATTN_HEADS)
    ones_r = ones_q[0:w, 0:w]
    a_end = POOL_WIDTH
    b_end = POOL_WIDTH + RWKV_IN

    xc = x.reshape(n, d)
    pending = None
    for l in range(DEPTH):
        wq = w_in[l][:, b_end:b_end + ATTN_WIDTH][:, qperm]
        w_proj = jnp.concatenate(
            [w_in[l][:, :a_end], w_in[l][:, a_end:b_end],
             jnp.zeros((d, RWKV_PAD - RWKV_IN), F32), wq, w_in[l][:, b_end + ATTN_WIDTH:]],
            axis=1).astype(BF16)
        pool_bd = _block_diag([pool_w[l, g] for g in range(len(POOL_WINDOWS))]).astype(BF16)
        pad_vec = jnp.zeros((RWKV_PAD - RWKV_IN,), F32)
        mup = jnp.concatenate([mu_prev[l], pad_vec]).reshape(1, RWKV_PAD)
        mun = jnp.concatenate([mu_next[l], pad_vec]).reshape(1, RWKV_PAD)
        w_lora = jnp.zeros((w, 5 * w), F32)
        for di in range(2):
            w_lora = w_lora.at[di * DECAY_LORA:(di + 1) * DECAY_LORA, di * w:(di + 1) * w].set(rw_w_up[l, di])
            o = 2 * DECAY_LORA
            w_lora = w_lora.at[o + di * AAA_LORA:o + (di + 1) * AAA_LORA, (2 + di) * w:(3 + di) * w].set(rw_a_up[l, di])
        o = 2 * DECAY_LORA + 2 * AAA_LORA
        w_lora = w_lora.at[o:o + GATE_LORA, 4 * w:5 * w].set(rw_g_up[l]).astype(BF16)
        lora_bias = jnp.concatenate([rw_w0[l, 0], rw_w0[l, 1], rw_a0[l, 0], rw_a0[l, 1],
                                     jnp.zeros((w,), F32)]).reshape(1, 5 * w)
        qg = jnp.tile(q_norm[l], ATTN_HEADS).reshape(1, ATTN_WIDTH)
        kg = jnp.tile(k_norm[l], 2).reshape(1, ATTN_KV_WIDTH)
        wo_attn = w_o[l][POOL_WIDTH + w:][qperm]
        wo = jnp.concatenate([w_o[l][:POOL_WIDTH + w], wo_attn], axis=0).astype(BF16)
        wr = jnp.concatenate([router_group[l], router_expert[l],
                              jnp.zeros((d, ROUTE_LANES - N_GROUPS - N_EXPERTS), F32)], axis=1)
        wrh = wr.astype(BF16)
        wrl = (wr - wrh.astype(F32)).astype(BF16)
        br = jnp.concatenate([router_group_b[l], router_expert_b[l],
                              jnp.zeros((ROUTE_LANES - N_GROUPS - N_EXPERTS,), F32)]).reshape(1, ROUTE_LANES)

        if pending is None:
            pool_in, rw_in, qh, kh, vh = _proj(xc, w_proj, cos8, sin8, qg, kg, ones_q, s, tm)
        else:
            xc, pool_in, rw_in, qh, kh, vh = _proj(None, w_proj, cos8, sin8, qg, kg, ones_q, s, tm, prev=pending)
        y_pool = _pool(pool_in.reshape(b, s, POOL_WIDTH), pool_bd, pool_scale[l].reshape(1, POOL_WIDTH))
        shared, logw, khnb, post = _rwkv_prep(rw_in.reshape(b, s, RWKV_PAD), mup, mun, w_lora, lora_bias,
                                        rw_k_k[l].reshape(1, w), rw_k_a[l].reshape(1, w),
                                        rw_r_k[l].reshape(1, w), ones_r, tm)
        wkv_f, wkv_r = _wkv(shared, logw, khnb, min(WKV_TILE, s))
        y_attn = _attn(qh, kh, vh, s, min(ATTN_Q_TILE, s))
        x1, x1p, route = _outproj(xc, y_pool.reshape(n, POOL_WIDTH), wkv_f.reshape(n, w), wkv_r.reshape(n, w),
                                  post.reshape(n, 2 * w), y_attn, wo,
                                  rw_gn_g[l].reshape(1, w), rw_gn_b[l].reshape(1, w), ones_r,
                                  ln1_g[l].reshape(1, d), ln1_b[l].reshape(1, d), wrh, wrl, br, tm)

        row_tok, pos, block_exp, n_used = _dispatch(route, n)
        split = block_exp.shape[0] // 2 * MOE_BLOCK
        xs_parts = [_gather_rows(x1p, row_tok[:split]), _gather_rows(x1p, row_tok[split:])]
        ys = _experts(block_exp, n_used, xs_parts, exp_gate, exp_up, exp_down, l)
        yab = _gather_rows(ys, pos)
        pending = (x1, yab, route, ln2_g[l].reshape(1, d), ln2_b[l].reshape(1, d))
    return _combine(*pending, tm).reshape(b, s, d)
```

```python
import functools
import math

import jax
import jax.numpy as jnp
from jax import lax
from jax.experimental import pallas as pl
from jax.experimental.pallas import tpu as pltpu
from jax.experimental.pallas import tpu_sc as plsc

F32 = jnp.float32
BF16 = jnp.bfloat16

D_MODEL = 1024
DEPTH = 4
GRID_W = 64
HEAD_DIM = 64
POOL_WIDTH = 256
POOL_WINDOWS = (2, 4, 8, 16)
POOL_GROUP = 64
RWKV_WIDTH = 256
RWKV_HEADS = 4
DECAY_LORA = 32
AAA_LORA = 32
GATE_LORA = 64
GN_EPS = 64e-5
RWKV_IN = 960
RWKV_PAD = 1024
ATTN_WIDTH = 512
ATTN_HEADS = 8
ATTN_KV_WIDTH = 128
ATTN_IN = ATTN_WIDTH + 2 * ATTN_KV_WIDTH
ROPE_THETA = 10000.0
QK_EPS = 1e-6
N_GROUPS = 4
EXPERTS_PER_GROUP = 8
N_EXPERTS = 32
EXPERT_HIDDEN = 512
MOE_BLOCK = 512
DEEPNORM_ALPHA = float((2 * DEPTH) ** 0.25)
LN_EPS = 1e-5
PROJ_WIDTH = POOL_WIDTH + RWKV_PAD + ATTN_IN
ROUTE_LANES = 128
WKV_CHUNK = 64
ROW_TILE = 512
WKV_TILE = 256
ATTN_Q_TILE = 512
VMEM_LIMIT = 48 * 1024 * 1024


def _cparams(sem):
    return pltpu.CompilerParams(dimension_semantics=sem, vmem_limit_bytes=VMEM_LIMIT)


def _dot(a, b):
    return jnp.dot(a, b, preferred_element_type=F32)


def _dot_nt(a, b):
    return lax.dot_general(a, b, (((1,), (1,)), ((), ())), preferred_element_type=F32)


def _dot_tn(a, b):
    return lax.dot_general(a, b, (((0,), (0,)), ((), ())), preferred_element_type=F32)


def _split2(x):
    hi = x.astype(BF16)
    lo = (x - hi.astype(F32)).astype(BF16)
    return hi, lo


def _split3(x):
    hi = x.astype(BF16)
    r1 = x - hi.astype(F32)
    mid = r1.astype(BF16)
    lo = (r1 - mid.astype(F32)).astype(BF16)
    return hi, mid, lo


HALF_D = D_MODEL // 2
U32 = jnp.uint32


def _pack_rows(x):
    hi = lax.bitcast_convert_type(x[:, :HALF_D].astype(BF16).astype(F32), U32)
    lo = lax.bitcast_convert_type(x[:, HALF_D:].astype(BF16).astype(F32), U32)
    return hi | (lo >> 16)


def _unpack_rows(u):
    a = lax.bitcast_convert_type(u & jnp.uint32(0xFFFF0000), F32).astype(BF16)
    b = lax.bitcast_convert_type(u << 16, F32).astype(BF16)
    return a, b


def _headsum_sq(x, ones_bf16):
    return _dot((x * x).astype(BF16), ones_bf16)


def _headsum(x, ones_bf16):
    hi, lo = _split2(x)
    return _dot(hi, ones_bf16) + _dot(lo, ones_bf16)


ROPE_QUARTER = HEAD_DIM // 4


def _rope(x, cos, sin_signed):
    n = x.shape[-1]
    lane = lax.broadcasted_iota(jnp.int32, x.shape, 1)
    first = (lane // ROPE_QUARTER) % 2 == 0
    partner = jnp.where(first, pltpu.roll(x, n - ROPE_QUARTER, 1), pltpu.roll(x, ROPE_QUARTER, 1))
    return x * cos + partner * sin_signed


def _layer_norm(z, g, b):
    mu = jnp.mean(z, axis=-1, keepdims=True)
    zc = z - mu
    var = jnp.mean(zc * zc, axis=-1, keepdims=True)
    return zc * lax.rsqrt(var + LN_EPS) * g + b


def _combine_rows(x1, ya, yb, route, g, b):
    g1 = route[:, 2:3]
    g2 = route[:, 3:4]
    a_hi, a_lo = _unpack_rows(ya)
    b_hi, b_lo = _unpack_rows(yb)
    m = jnp.concatenate([g1 * a_hi.astype(F32) + g2 * b_hi.astype(F32),
                         g1 * a_lo.astype(F32) + g2 * b_lo.astype(F32)], axis=1)
    return _layer_norm(DEEPNORM_ALPHA * x1 + m, g, b)


def _proj_kernel(x_ref, w_ref, cos_ref, sin_ref, qg_ref, kg_ref, onesq_ref,
                 pool_ref, rw_ref, q_ref, k_ref, v_ref):
    _proj_rows(x_ref[...], w_ref, cos_ref, sin_ref, qg_ref, kg_ref, onesq_ref,
               pool_ref, rw_ref, q_ref, k_ref, v_ref)


def _combine_proj_kernel(x1_ref, ya_ref, yb_ref, route_ref, lg_ref, lb_ref,
                         w_ref, cos_ref, sin_ref, qg_ref, kg_ref, onesq_ref,
                         x_ref, pool_ref, rw_ref, q_ref, k_ref, v_ref):
    x = _combine_rows(x1_ref[...], ya_ref[...], yb_ref[...], route_ref[...], lg_ref[...], lb_ref[...])
    x_ref[...] = x
    _proj_rows(x, w_ref, cos_ref, sin_ref, qg_ref, kg_ref, onesq_ref, pool_ref, rw_ref, q_ref, k_ref, v_ref)


def _proj_rows(x, w_ref, cos_ref, sin_ref, qg_ref, kg_ref, onesq_ref, pool_ref, rw_ref, q_ref, k_ref, v_ref):
    xb = x.astype(BF16)
    pool_ref[...] = _dot(xb, w_ref[:, 0:POOL_WIDTH])
    rw_ref[...] = _dot(xb, w_ref[:, POOL_WIDTH:POOL_WIDTH + RWKV_PAD])
    at = _dot(xb, w_ref[:, POOL_WIDTH + RWKV_PAD:PROJ_WIDTH])

    q = at[:, 0:ATTN_WIDTH]
    k = at[:, ATTN_WIDTH:ATTN_WIDTH + ATTN_KV_WIDTH]
    v = at[:, ATTN_WIDTH + ATTN_KV_WIDTH:ATTN_IN]
    ones_q = onesq_ref[...]
    ones_k = onesq_ref[0:ATTN_KV_WIDTH, 0:ATTN_KV_WIDTH]
    inv = 1.0 / HEAD_DIM
    qn = q * lax.rsqrt(_headsum_sq(q, ones_q) * inv + QK_EPS) * qg_ref[...]
    kn = k * lax.rsqrt(_headsum_sq(k, ones_k) * inv + QK_EPS) * kg_ref[...]
    cos = cos_ref[...]
    sin = sin_ref[...]
    q_ref[...] = (_rope(qn, cos, sin) * (HEAD_DIM ** -0.5 * math.log2(math.e))).astype(BF16)
    k_ref[...] = _rope(kn, cos[:, 0:ATTN_KV_WIDTH], sin[:, 0:ATTN_KV_WIDTH]).astype(BF16)
    v_ref[...] = v.astype(BF16)


def _proj(x2d, w, cos8, sin8, qg, kg, ones_q, seq, tm, prev=None):
    n = (x2d if prev is None else prev[0]).shape[0]
    nts = seq // tm
    nt = n // tm
    row = lambda width: pl.BlockSpec((tm, width), lambda i: (i, 0))
    const = lambda r, c: pl.BlockSpec((r, c), lambda i: (0, 0))
    proj_in = [const(D_MODEL, PROJ_WIDTH),
               pl.BlockSpec((tm, ATTN_WIDTH), lambda i: (i % nts, 0)),
               pl.BlockSpec((tm, ATTN_WIDTH), lambda i: (i % nts, 0)),
               const(1, ATTN_WIDTH), const(1, ATTN_KV_WIDTH), const(ATTN_WIDTH, ATTN_WIDTH)]
    proj_out = [row(POOL_WIDTH), row(RWKV_PAD), row(ATTN_WIDTH), row(ATTN_KV_WIDTH), row(ATTN_KV_WIDTH)]
    proj_shape = [jax.ShapeDtypeStruct((n, POOL_WIDTH), F32),
                  jax.ShapeDtypeStruct((n, RWKV_PAD), F32),
                  jax.ShapeDtypeStruct((n, ATTN_WIDTH), BF16),
                  jax.ShapeDtypeStruct((n, ATTN_KV_WIDTH), BF16),
                  jax.ShapeDtypeStruct((n, ATTN_KV_WIDTH), BF16)]
    if prev is None:
        return pl.pallas_call(
            _proj_kernel,
            grid=(nt,),
            in_specs=[row(D_MODEL)] + proj_in,
            out_specs=proj_out,
            out_shape=proj_shape,
            compiler_params=_cparams(("parallel",)),
            name="proj",
        )(x2d, w, cos8, sin8, qg, kg, ones_q)
    x1, yab, route, lg, lb = prev
    return pl.pallas_call(
        _combine_proj_kernel,
        grid=(nt,),
        in_specs=[row(D_MODEL), row(HALF_D), pl.BlockSpec((tm, HALF_D), lambda i: (i + nt, 0)),
                  row(ROUTE_LANES), const(1, D_MODEL), const(1, D_MODEL)] + proj_in,
        out_specs=[row(D_MODEL)] + proj_out,
        out_shape=[jax.ShapeDtypeStruct((n, D_MODEL), F32)] + proj_shape,
        compiler_params=_cparams(("parallel",)),
        name="combine_proj",
    )(x1, yab, yab, route, lg, lb, w, cos8, sin8, qg, kg, ones_q)


def _pool_kernel(u_ref, w_ref, scale_ref, o_ref):
    s = u_ref.shape[1]
    u = u_ref[0]
    t = lax.broadcasted_iota(jnp.int32, (s, POOL_WIDTH), 0)

    def down(x, k):
        return jnp.where(t >= k, pltpu.roll(x, k, 0), 0.0)

    def up(x, k):
        return jnp.where(t < s - k, pltpu.roll(x, s - k, 0), 0.0)

    left = [down(u, 1)]
    right = [u]
    for k in (1, 2, 4):
        left.append(left[-1] + down(left[-1], k))
        right.append(right[-1] + up(right[-1], k))
    sums = [l + r for l, r in zip(left, right)]

    grp = lax.broadcasted_iota(jnp.int32, (s, POOL_WIDTH), 1) // POOL_GROUP
    half = jnp.where(grp == 0, 1, jnp.where(grp == 1, 2, jnp.where(grp == 2, 4, 8)))
    cnt = (jnp.minimum(t + half, s) - jnp.maximum(t - half, 0)).astype(F32)
    tot = jnp.where(grp == 0, sums[0], jnp.where(grp == 1, sums[1], jnp.where(grp == 2, sums[2], sums[3])))
    d = tot / cnt - u
    o_ref[0] = _dot(d.astype(BF16), w_ref[...]) * scale_ref[...]


def _pool(u3d, w_bd, scale):
    b, s, _ = u3d.shape
    return pl.pallas_call(
        _pool_kernel,
        grid=(b,),
        in_specs=[pl.BlockSpec((1, s, POOL_WIDTH), lambda i: (i, 0, 0)),
                  pl.BlockSpec((POOL_WIDTH, POOL_WIDTH), lambda i: (0, 0)),
                  pl.BlockSpec((1, POOL_WIDTH), lambda i: (0, 0))],
        out_specs=pl.BlockSpec((1, s, POOL_WIDTH), lambda i: (i, 0, 0)),
        out_shape=jax.ShapeDtypeStruct((b, s, POOL_WIDTH), F32),
        compiler_params=_cparams(("parallel",)),
        name="pool",
    )(u3d, w_bd, scale)


_SHIFT_HALO = 8


def _rwkv_prep_kernel(cur_ref, prev_ref, next_ref, mup_ref, mun_ref, wl_ref, bias_ref,
                      kk_ref, ka_ref, rk_ref, ones_ref,
                      sh_ref, lw_ref, kn_ref, post_ref):
    ts = cur_ref.shape[1]
    ti = pl.program_id(1)
    nt = pl.num_programs(1)
    cur = cur_ref[0]
    row = lax.broadcasted_iota(jnp.int32, cur.shape, 0)
    before = jnp.where(ti > 0, prev_ref[0, _SHIFT_HALO - 1:_SHIFT_HALO, :], 0.0)
    after = jnp.where(ti < nt - 1, next_ref[0, 0:1, :], 0.0)
    prev = jnp.where(row == 0, before, pltpu.roll(cur, 1, 0))
    nxt = jnp.where(row == ts - 1, after, pltpu.roll(cur, ts - 1, 0))
    f = cur + mup_ref[...] * (prev - cur) + mun_ref[...] * (nxt - cur)

    w = RWKV_WIDTH
    r = f[:, 0:w]
    k = f[:, w:2 * w]
    v = f[:, 2 * w:3 * w]
    lora = f[:, 3 * w:4 * w]
    lane = lax.broadcasted_iota(jnp.int32, lora.shape, 1)
    z = jnp.where(lane < 2 * DECAY_LORA, jnp.tanh(lora),
                  jnp.where(lane < 2 * DECAY_LORA + 2 * AAA_LORA, lora,
                            jnp.where(lane < 2 * DECAY_LORA + 2 * AAA_LORA + GATE_LORA,
                                      jax.nn.sigmoid(lora), 0.0)))
    up = _dot(z.astype(BF16), wl_ref[...]) + bias_ref[...]
    ones = ones_ref[...]

    kk0 = k * kk_ref[...]
    nrm = jnp.sqrt(_headsum_sq(kk0, ones))
    kk = kk0 / jnp.maximum(nrm, 1e-12)
    sh_ref[0, :, 0:w] = r.astype(BF16)
    sh_ref[0, :, w:2 * w] = v.astype(BF16)
    sh_ref[0, :, 2 * w:3 * w] = kk.astype(BF16)

    ksum = jnp.zeros_like(k)
    for di in range(2):
        logw = -math.exp(-0.5) * jax.nn.sigmoid(up[:, di * w:(di + 1) * w])
        a = jax.nn.sigmoid(up[:, (2 + di) * w:(3 + di) * w])
        kh = k * (1.0 + (a - 1.0) * ka_ref[...])
        ksum = ksum + kh
        lw_ref[di, 0] = logw
        kn_ref[di, 0, :, 0:w] = kh.astype(BF16)
        kn_ref[di, 0, :, w:2 * w] = (-(a * kk)).astype(BF16)
    bonus = _headsum(r * ksum * rk_ref[...], ones) * v
    post_ref[0, :, 0:w] = bonus
    post_ref[0, :, w:2 * w] = up[:, 4 * w:5 * w]


def _rwkv_prep(rw3d, mup, mun, wl, bias, k_k, k_a, r_k, ones, ts):
    b, s, _ = rw3d.shape
    nt = s // ts
    hb = ts // _SHIFT_HALO
    nhb = s // _SHIFT_HALO
    w = RWKV_WIDTH
    vec = lambda width: pl.BlockSpec((1, width), lambda i, j: (0, 0))
    return pl.pallas_call(
        _rwkv_prep_kernel,
        grid=(b, nt),
        in_specs=[pl.BlockSpec((1, ts, RWKV_PAD), lambda i, j: (i, j, 0)),
                  pl.BlockSpec((1, _SHIFT_HALO, RWKV_PAD),
                               lambda i, j: (i, jnp.maximum(j * hb - 1, 0), 0)),
                  pl.BlockSpec((1, _SHIFT_HALO, RWKV_PAD),
                               lambda i, j: (i, jnp.minimum((j + 1) * hb, nhb - 1), 0)),
                  vec(RWKV_PAD), vec(RWKV_PAD),
                  pl.BlockSpec((w, 5 * w), lambda i, j: (0, 0)),
                  vec(5 * w), vec(w), vec(w), vec(w),
                  pl.BlockSpec((w, w), lambda i, j: (0, 0))],
        out_specs=[pl.BlockSpec((1, ts, 3 * w), lambda i, j: (i, j, 0)),
                   pl.BlockSpec((2, 1, ts, w), lambda i, j: (0, i, j, 0)),
                   pl.BlockSpec((2, 1, ts, 2 * w), lambda i, j: (0, i, j, 0)),
                   pl.BlockSpec((1, ts, 2 * w), lambda i, j: (i, j, 0))],
        out_shape=[jax.ShapeDtypeStruct((b, s, 3 * w), BF16),
                   jax.ShapeDtypeStruct((2, b, s, w), F32),
                   jax.ShapeDtypeStruct((2, b, s, 2 * w), BF16),
                   jax.ShapeDtypeStruct((b, s, 2 * w), F32)],
        compiler_params=_cparams(("parallel", "parallel")),
        name="rwkv_prep",
    )(rw3d, rw3d, rw3d, mup, mun, wl, bias, k_k, k_a, r_k, ones)


def _expand_bd(x_bf16, mask_bd):
    return jnp.where(mask_bd, jnp.concatenate([x_bf16] * RWKV_HEADS, axis=0), jnp.zeros((), BF16))


def _wkv_kernel(shf_ref, shr_ref, lwf_ref, lwr_ref, knf_ref, knr_ref, yf_ref, yr_ref, hf_ref, hr_ref):
    c = WKV_CHUNK
    w = RWKV_WIDTH
    tt = shf_ref.shape[1]
    ncs = tt // c

    @pl.when(pl.program_id(1) == 0)
    def _():
        hf_ref[...] = jnp.zeros_like(hf_ref)
        hr_ref[...] = jnp.zeros_like(hr_ref)

    row_c = lax.broadcasted_iota(jnp.int32, (c, c), 0)
    col_c = lax.broadcasted_iota(jnp.int32, (c, c), 1)
    t_i = lax.broadcasted_iota(jnp.int32, (c, w), 0)
    s_i = lax.broadcasted_iota(jnp.int32, (c, w), 1) % c
    eye_c = (s_i == t_i).astype(F32)
    row_w = lax.broadcasted_iota(jnp.int32, (w, w), 0)
    col_w = lax.broadcasted_iota(jnp.int32, (w, w), 1)
    mask_bd = (row_w // c) == (col_w // HEAD_DIM)
    eye_w = row_w == col_w
    tri_incl = [jnp.where(col_c <= row_c, 1.0, 0.0).astype(BF16),
                jnp.where(col_c >= row_c, 1.0, 0.0).astype(BF16)]
    strict = [s_i < t_i, s_i > t_i]
    incl = [s_i <= t_i, s_i >= t_i]

    def bf(x):
        return x.astype(BF16)

    def bd(x):
        return _expand_bd(bf(x), mask_bd)

    chunks = []
    for ci in range(ncs):
        chunks.append((0, ci * c, shf_ref, lwf_ref, knf_ref))
        chunks.append((1, (ncs - 1 - ci) * c, shr_ref, lwr_ref, knr_ref))

    st = []
    for d, start, sh_ref, lw_ref, kn_ref in chunks:
        rows = pl.ds(start, c)
        st.append(dict(d=d, rows=rows,
                       r=sh_ref[0, rows, 0:w].astype(F32), v_bf=sh_ref[0, rows, w:2 * w],
                       kk=sh_ref[0, rows, 2 * w:3 * w].astype(F32), lw=lw_ref[0, rows, :],
                       kh=kn_ref[0, rows, 0:w].astype(F32), nb=kn_ref[0, rows, w:2 * w].astype(F32)))

    for q in st:
        l_hi, l_mid, l_lo = _split3(q["lw"])
        tri = tri_incl[q["d"]]
        q["cum"] = _dot(tri, l_hi) + _dot(tri, l_mid) + _dot(tri, l_lo)
    for q in st:
        cum, lw = q["cum"], q["lw"]
        tot = jnp.sum(lw, axis=0, keepdims=True)
        e_inv = jnp.exp(-cum)
        e_end = jnp.exp(tot - cum)
        q["tot"] = tot
        q["a_bar"] = q["kk"] * jnp.exp(cum - lw)
        q["r_bar"] = q["r"] * jnp.exp(cum)
        q["b_hat"] = bf(q["nb"] * e_end)
        q["k_hat"] = bf(q["kh"] * e_end)
        q["v_bd"] = _expand_bd(q["v_bf"], mask_bd)
        q["lhs"] = jnp.concatenate([bf(q["a_bar"]), bf(q["r_bar"])], axis=0)
        q["rhs"] = jnp.concatenate([bd(q["nb"] * e_inv), bd(q["kh"] * e_inv)], axis=0)
    for q in st:
        sc = _dot_nt(q["lhs"], q["rhs"])
        sm, im = strict[q["d"]], incl[q["d"]]
        q["x"] = jnp.where(sm, sc[0:c, 0:w], 0.0)
        q["a_ak"] = bf(jnp.where(sm, sc[0:c, w:2 * w], 0.0))
        q["m_rb"] = bf(jnp.where(im, sc[c:2 * c, 0:w], 0.0))
        q["m_rk"] = bf(jnp.where(im, sc[c:2 * c, w:2 * w], 0.0))
        q["tinv"] = eye_c + q["x"]

    for q in st:
        q["x"] = _dot(bf(q["x"]), bd(q["x"]))
    n_lv = int(math.log2(c)) - 1
    for lv in range(n_lv):
        last = lv == n_lv - 1
        for q in st:
            xb = bd(q["x"])
            if last:
                q["tinv"] = q["tinv"] + _dot(bf(q["tinv"]), xb)
            else:
                res = _dot(jnp.concatenate([bf(q["x"]), bf(q["tinv"])], axis=0), xb)
                q["tinv"] = q["tinv"] + res[c:2 * c]
                q["x"] = res[0:c]
    for q in st:
        res = _dot(jnp.concatenate([q["a_ak"], q["m_rk"]], axis=0), q["v_bd"])
        q["akv"] = res[0:c]
        q["mrkv"] = res[c:2 * c]
    for q in st:
        tinv_bf = bf(q["tinv"])
        q["a_pr"] = bf(_dot(tinv_bf, bd(q["a_bar"])))
        q["v_pr"] = bf(_dot(tinv_bf, bd(q["akv"])))
    for q in st:
        q["r_pr"] = bf(q["r_bar"] + _dot(q["m_rb"], _expand_bd(q["a_pr"], mask_bd)))
        q["y_pr"] = _dot(q["m_rb"], _expand_bd(q["v_pr"], mask_bd)) + q["mrkv"]
    for q in st:
        q["p_bd"] = bf(jnp.where(eye_w, jnp.exp(q["tot"]), 0.0)
                       + jnp.where(mask_bd, _dot_tn(q["b_hat"], q["a_pr"]), 0.0))
        q["q_bd"] = jnp.where(mask_bd,
                              _dot_tn(jnp.concatenate([q["b_hat"], q["k_hat"]], axis=0),
                                      jnp.concatenate([q["v_pr"], q["v_bf"]], axis=0)), 0.0)

    h = [hf_ref[...], hr_ref[...]]
    y_refs = [yf_ref, yr_ref]
    for q in st:
        d = q["d"]
        res = _dot(jnp.concatenate([q["r_pr"], q["p_bd"]], axis=0), bf(h[d]))
        y_refs[d][0, q["rows"], :] = res[0:c] + q["y_pr"]
        h[d] = res[c:c + w] + q["q_bd"]
    hf_ref[...] = h[0]
    hr_ref[...] = h[1]


def _wkv(shared, logw, khnb, tt):
    b, s, _ = shared.shape
    nt = s // tt
    w = RWKV_WIDTH
    fwd = lambda i, j: (i, j, 0)
    bwd = lambda i, j: (i, nt - 1 - j, 0)
    return pl.pallas_call(
        _wkv_kernel,
        grid=(b, nt),
        in_specs=[pl.BlockSpec((1, tt, 3 * w), fwd), pl.BlockSpec((1, tt, 3 * w), bwd),
                  pl.BlockSpec((None, 1, tt, w), lambda i, j: (0, i, j, 0)),
                  pl.BlockSpec((None, 1, tt, w), lambda i, j: (1, i, nt - 1 - j, 0)),
                  pl.BlockSpec((None, 1, tt, 2 * w), lambda i, j: (0, i, j, 0)),
                  pl.BlockSpec((None, 1, tt, 2 * w), lambda i, j: (1, i, nt - 1 - j, 0))],
        out_specs=[pl.BlockSpec((1, tt, w), fwd), pl.BlockSpec((1, tt, w), bwd)],
        out_shape=[jax.ShapeDtypeStruct((b, s, w), F32), jax.ShapeDtypeStruct((b, s, w), F32)],
        scratch_shapes=[pltpu.VMEM((w, w), F32), pltpu.VMEM((w, w), F32)],
        compiler_params=_cparams(("parallel", "arbitrary")),
        name="wkv",
    )(shared, shared, logw, logw, khnb, khnb)


def _attn_kernel(q_ref, k_ref, v_ref, o_ref):
    tq = q_ref.shape[0]
    k = k_ref[...]
    v = v_ref[...]
    lane = lax.broadcasted_iota(jnp.int32, (tq, ATTN_KV_WIDTH), 1)
    low = lane < HEAD_DIM
    zero = jnp.zeros((), BF16)
    v_lane = lax.broadcasted_iota(jnp.int32, v.shape, 1)
    one = jnp.ones((), BF16)
    v_ext = [jnp.where(v_lane < HEAD_DIM, v, one), jnp.where(v_lane >= HEAD_DIM, v, one)]
    for j in range(ATTN_HEADS // 2):
        qb = q_ref[:, j * ATTN_KV_WIDTH:(j + 1) * ATTN_KV_WIDTH]
        outs = []
        s2 = _dot_nt(jnp.concatenate([jnp.where(low, qb, zero), jnp.where(low, zero, qb)], axis=0), k)
        for half in range(2):
            s = s2[half * tq:(half + 1) * tq]
            m = jnp.max(s, axis=-1, keepdims=True)
            p = jnp.exp2(s - m)
            acc = _dot(p.astype(BF16), v_ext[half])
            l = acc[:, HEAD_DIM:HEAD_DIM + 1] if half == 0 else acc[:, 0:1]
            outs.append(acc / l)
        o_ref[:, j * ATTN_KV_WIDTH:(j + 1) * ATTN_KV_WIDTH] = jnp.where(low, outs[0], outs[1]).astype(BF16)


def _attn(q, k, v, seq, tq):
    n = q.shape[0]
    nq = seq // tq
    return pl.pallas_call(
        _attn_kernel,
        grid=(n // seq, nq),
        in_specs=[pl.BlockSpec((tq, ATTN_WIDTH), lambda b, i: (b * nq + i, 0)),
                  pl.BlockSpec((seq, ATTN_KV_WIDTH), lambda b, i: (b, 0)),
                  pl.BlockSpec((seq, ATTN_KV_WIDTH), lambda b, i: (b, 0))],
        out_specs=pl.BlockSpec((tq, ATTN_WIDTH), lambda b, i: (b * nq + i, 0)),
        out_shape=jax.ShapeDtypeStruct((n, ATTN_WIDTH), BF16),
        compiler_params=_cparams(("parallel", "parallel")),
        name="attn",
    )(q, k, v)


def _outproj_kernel(x_ref, pool_ref, wkvf_ref, wkvr_ref, post_ref, attn_ref, wo_ref, gng_ref, gnb_ref,
                    ones_ref, lg_ref, lb_ref, wrh_ref, wrl_ref, br_ref,
                    x1_ref, x1p_ref, route_ref):
    w = RWKV_WIDTH
    ones = ones_ref[...]
    wkv = wkvf_ref[...] + wkvr_ref[...]
    inv = 1.0 / HEAD_DIM
    mu = _headsum(wkv, ones) * inv
    cen = wkv - mu
    var = _headsum_sq(cen, ones) * inv
    yr = cen * lax.rsqrt(var + GN_EPS) * gng_ref[...] + gnb_ref[...] + post_ref[:, 0:w]
    yr = yr * post_ref[:, w:2 * w]
    y = (_dot(pool_ref[...].astype(BF16), wo_ref[0:POOL_WIDTH, :])
         + _dot(yr.astype(BF16), wo_ref[POOL_WIDTH:POOL_WIDTH + w, :])
         + _dot(attn_ref[...], wo_ref[POOL_WIDTH + w:D_MODEL, :]))
    x1 = _layer_norm(DEEPNORM_ALPHA * x_ref[...] + y, lg_ref[...], lb_ref[...])
    x1_ref[...] = x1
    x1p_ref[...] = _pack_rows(x1)

    xh, xl = _split2(x1)
    logits = (_dot(xh, wrh_ref[...]) + _dot(xl, wrh_ref[...]) + _dot(xh, wrl_ref[...])) + br_ref[...]
    lane = lax.broadcasted_iota(jnp.int32, logits.shape, 1)
    lane_f = lane.astype(F32)
    lane_grp_f = ((lane - N_GROUPS) // EXPERTS_PER_GROUP).astype(F32)
    neg = -jnp.inf
    big = jnp.float32(1 << 20)
    gl = jnp.where(lane < N_GROUPS, logits, neg)
    gmax = jnp.max(gl, axis=-1, keepdims=True)
    grp = jnp.min(jnp.where(gl == gmax, lane_f, big), axis=-1, keepdims=True)
    gw = 1.0 / jnp.sum(jnp.exp(gl - gmax), axis=-1, keepdims=True)
    in_grp = (lane >= N_GROUPS) & (lane < N_GROUPS + N_EXPERTS) & (lane_grp_f == grp)
    el = jnp.where(in_grp, logits, neg)
    v1 = jnp.max(el, axis=-1, keepdims=True)
    i1 = jnp.min(jnp.where(el == v1, lane_f, big), axis=-1, keepdims=True)
    el2 = jnp.where(lane_f == i1, neg, el)
    v2 = jnp.max(el2, axis=-1, keepdims=True)
    i2 = jnp.min(jnp.where(el2 == v2, lane_f, big), axis=-1, keepdims=True)
    e21 = jnp.exp(v2 - v1)
    g1 = gw / (1.0 + e21)
    g2 = gw * e21 / (1.0 + e21)
    route = jnp.where(lane == 0, i1 - N_GROUPS,
                      jnp.where(lane == 1, i2 - N_GROUPS,
                                jnp.where(lane == 2, g1, jnp.where(lane == 3, g2, 0.0))))
    route_ref[...] = route


def _outproj(x2d, pool2d, wkvf, wkvr, post2d, attn2d, wo, gng, gnb, ones, lg, lb, wrh, wrl, br, tm):
    n = x2d.shape[0]
    w = RWKV_WIDTH
    row = lambda width: pl.BlockSpec((tm, width), lambda i: (i, 0))
    vec = lambda width: pl.BlockSpec((1, width), lambda i: (0, 0))
    return pl.pallas_call(
        _outproj_kernel,
        grid=(n // tm,),
        in_specs=[row(D_MODEL), row(POOL_WIDTH), row(w), row(w),
                  row(2 * w), row(ATTN_WIDTH),
                  pl.BlockSpec((D_MODEL, D_MODEL), lambda i: (0, 0)),
                  vec(w), vec(w),
                  pl.BlockSpec((w, w), lambda i: (0, 0)),
                  vec(D_MODEL), vec(D_MODEL),
                  pl.BlockSpec((D_MODEL, ROUTE_LANES), lambda i: (0, 0)),
                  pl.BlockSpec((D_MODEL, ROUTE_LANES), lambda i: (0, 0)),
                  vec(ROUTE_LANES)],
        out_specs=[row(D_MODEL), row(HALF_D), row(ROUTE_LANES)],
        out_shape=[jax.ShapeDtypeStruct((n, D_MODEL), F32),
                   jax.ShapeDtypeStruct((n, HALF_D), U32),
                   jax.ShapeDtypeStruct((n, ROUTE_LANES), F32)],
        compiler_params=_cparams(("parallel",)),
        name="outproj",
    )(x2d, pool2d, wkvf, wkvr, post2d, attn2d, wo, gng, gnb, ones, lg, lb, wrh, wrl, br)


def _expert_kernel(bexp_ref, nused_ref, xs_ref, wg_ref, wu_ref, wd_ref, *rest):
    ys_ref, wgb_ref, wub_ref, wdb_ref = rest[-4:]
    i = pl.program_id(0)
    used = i < nused_ref[0]

    @pl.when(used & ((i == 0) | (bexp_ref[i] != bexp_ref[jnp.maximum(i - 1, 0)])))
    def _():
        wgb_ref[...] = wg_ref[0].astype(BF16)
        wub_ref[...] = wu_ref[0].astype(BF16)
        wdb_ref[...] = wd_ref[0].astype(BF16)

    @pl.when(used)
    def _():
        xa, xb = _unpack_rows(xs_ref[...])
        h1 = _dot(xa, wgb_ref[0:HALF_D, :]) + _dot(xb, wgb_ref[HALF_D:D_MODEL, :])
        h2 = _dot(xa, wub_ref[0:HALF_D, :]) + _dot(xb, wub_ref[HALF_D:D_MODEL, :])
        h = (h1 * jax.nn.sigmoid(h1) * h2).astype(BF16)
        ys_ref[...] = _pack_rows(_dot(h, wdb_ref[...]))

    @pl.when(jnp.logical_not(used))
    def _():
        ys_ref[...] = jnp.zeros_like(ys_ref)


def _experts(block_exp, n_used, xs_parts, wg, wu, wd, layer):
    p = sum(xs.shape[0] for xs in xs_parts)
    wmap = lambda i, be, nu: (layer, be[i], 0, 0)
    ys = None
    first = 0
    for xs in xs_parts:
        nb = xs.shape[0] // MOE_BLOCK
        in_specs = [pl.BlockSpec((MOE_BLOCK, HALF_D), lambda i, be, nu: (i, 0)),
                    pl.BlockSpec((None, 1, D_MODEL, EXPERT_HIDDEN), wmap),
                    pl.BlockSpec((None, 1, D_MODEL, EXPERT_HIDDEN), wmap),
                    pl.BlockSpec((None, 1, EXPERT_HIDDEN, D_MODEL), wmap)]
        args = [block_exp[first:first + nb], n_used - first, xs, wg, wu, wd]
        aliases = {}
        if ys is not None:
            in_specs.append(pl.BlockSpec(memory_space=pl.ANY))
            aliases = {len(args): 0}
            args.append(ys)
        grid_spec = pltpu.PrefetchScalarGridSpec(
            num_scalar_prefetch=2,
            grid=(nb,),
            in_specs=in_specs,
            out_specs=pl.BlockSpec((MOE_BLOCK, HALF_D), functools.partial(_block_at, first)),
            scratch_shapes=[pltpu.VMEM((D_MODEL, EXPERT_HIDDEN), BF16),
                            pltpu.VMEM((D_MODEL, EXPERT_HIDDEN), BF16),
                            pltpu.VMEM((EXPERT_HIDDEN, D_MODEL), BF16)],
        )
        ys = pl.pallas_call(
            _expert_kernel,
            grid_spec=grid_spec,
            out_shape=jax.ShapeDtypeStruct((p, HALF_D), U32),
            input_output_aliases=aliases,
            compiler_params=_cparams(("arbitrary",)),
            name="experts",
        )(*args)
        first += nb
    return ys


def _block_at(first, i, be, nu):
    return (i + first, 0)


SC_CORES = 2
SC_SUBCORES = 16
SC_GATHER_ROWS = 64


def _sc_gather(table, idx):
    v, d = table.shape
    b = idx.shape[0]
    nw = SC_CORES * SC_SUBCORES
    ch = SC_GATHER_ROWS
    assert b % (nw * ch * 2) == 0
    b_per_w = b // nw
    nch = b_per_w // ch
    mesh = plsc.VectorSubcoreMesh(core_axis_name="c", subcore_axis_name="s")

    @functools.partial(
        pl.kernel, mesh=mesh,
        out_type=jax.ShapeDtypeStruct((b, d), table.dtype),
        scratch_types=[pltpu.VMEM((b_per_w,), jnp.int32),
                       pltpu.VMEM((2, ch, d), table.dtype),
                       pltpu.SemaphoreType.DMA, pltpu.SemaphoreType.DMA,
                       pltpu.SemaphoreType.DMA, pltpu.SemaphoreType.DMA],
    )
    def gather_kernel(table_hbm, idx_hbm, out_hbm, idx_v, rows_v, gsem0, gsem1, wsem0, wsem1):
        wid = lax.axis_index("s") * SC_CORES + lax.axis_index("c")
        base = wid * b_per_w
        pltpu.sync_copy(idx_hbm.at[pl.ds(base, b_per_w)], idx_v)
        gsem = (gsem0, gsem1)
        wsem = (wsem0, wsem1)

        def gather(jj, slot):
            off = pl.multiple_of(jj * ch, ch)
            return pltpu.make_async_copy(table_hbm.at[idx_v.at[pl.ds(off, ch)]], rows_v.at[slot], gsem[slot])

        def write(jj, slot):
            off = pl.multiple_of(jj * ch, ch)
            return pltpu.make_async_copy(rows_v.at[slot], out_hbm.at[pl.ds(base + off, ch)], wsem[slot])

        gather(0, 0).start()

        @pl.loop(0, nch, step=2)
        def _(j):
            for slot in range(2):
                jj = j + slot

                gather(jj, slot).wait()

                @pl.when(jj >= 1)
                def _():
                    write(jj - 1, 1 - slot).wait()

                @pl.when(jj + 1 < nch)
                def _():
                    gather(jj + 1, 1 - slot).start()

                write(jj, slot).start()

        write(nch - 1, (nch - 1) % 2).wait()

    return gather_kernel(table, idx)


def _gather_rows(table, idx):
    b = idx.shape[0]
    unit = SC_CORES * SC_SUBCORES * SC_GATHER_ROWS * 2
    bp = -(-b // unit) * unit
    if bp == b:
        return _sc_gather(table, idx)
    return _sc_gather(table, jnp.pad(idx, (0, bp - b)))[:b]


def _combine_kernel(x1_ref, ya_ref, yb_ref, route_ref, lg_ref, lb_ref, o_ref):
    o_ref[...] = _combine_rows(x1_ref[...], ya_ref[...], yb_ref[...], route_ref[...], lg_ref[...], lb_ref[...])


def _combine(x1, yab, route, lg, lb, tm):
    n = x1.shape[0]
    nt = n // tm
    row = lambda width: pl.BlockSpec((tm, width), lambda i: (i, 0))
    vec = lambda width: pl.BlockSpec((1, width), lambda i: (0, 0))
    return pl.pallas_call(
        _combine_kernel,
        grid=(nt,),
        in_specs=[row(D_MODEL), row(HALF_D), pl.BlockSpec((tm, HALF_D), lambda i: (i + nt, 0)),
                  row(ROUTE_LANES), vec(D_MODEL), vec(D_MODEL)],
        out_specs=row(D_MODEL),
        out_shape=jax.ShapeDtypeStruct((n, D_MODEL), F32),
        compiler_params=_cparams(("parallel",)),
        name="combine",
    )(x1, yab, yab, route, lg, lb)


def _q_perm():
    order = [h for j in range(ATTN_HEADS // 2) for h in (j, j + ATTN_HEADS // 2)]
    return jnp.concatenate([jnp.arange(h * HEAD_DIM, (h + 1) * HEAD_DIM) for h in order])


def _block_diag(blocks):
    n = len(blocks)
    r, c = blocks[0].shape
    out = jnp.zeros((n * r, n * c), blocks[0].dtype)
    for i, blk in enumerate(blocks):
        out = out.at[i * r:(i + 1) * r, i * c:(i + 1) * c].set(blk)
    return out


def _rope_tables(seq):
    rows = seq // GRID_W
    row_id = jnp.repeat(jnp.arange(rows), GRID_W).astype(F32)
    col_id = jnp.tile(jnp.arange(GRID_W), rows).astype(F32)
    half = HEAD_DIM // 2
    inv_freq = ROPE_THETA ** (-jnp.arange(0, half, 2, dtype=F32) / half)
    ang_r = row_id[:, None] * inv_freq
    ang_c = col_id[:, None] * inv_freq
    ang = jnp.concatenate([ang_r, ang_r, ang_c, ang_c], -1)
    sign = jnp.where((jnp.arange(HEAD_DIM) // 16) % 2 == 0, -1.0, 1.0).astype(F32)
    cos8 = jnp.tile(jnp.cos(ang), (1, ATTN_HEADS))
    sin8 = jnp.tile(jnp.sin(ang) * sign, (1, ATTN_HEADS))
    return cos8, sin8


def _dispatch(route, n):
    m = n * 2
    experts = jnp.arange(N_EXPERTS, dtype=jnp.int32)
    e_flat = jnp.concatenate([route[:, 0], route[:, 1]]).astype(jnp.int32)
    pair = jnp.arange(m, dtype=jnp.int32)
    e_sorted, order = lax.sort((e_flat, pair), num_keys=1, is_stable=True)
    start = jnp.sum(e_sorted[None, :] < experts[:, None], axis=1, dtype=jnp.int32)
    counts = jnp.concatenate([start[1:], jnp.full((1,), m, jnp.int32)]) - start
    padded = (counts + MOE_BLOCK - 1) // MOE_BLOCK * MOE_BLOCK
    ends_p = jnp.sum(jnp.where(experts[None, :] <= experts[:, None], padded[None, :], 0), axis=1, dtype=jnp.int32)
    pstart = ends_p - padded
    delta = pstart - start
    n_blocks = -(-(m + N_EXPERTS * (MOE_BLOCK - 1)) // MOE_BLOCK)
    block_start = jnp.arange(n_blocks, dtype=jnp.int32) * MOE_BLOCK
    block_exp = jnp.minimum(jnp.sum(ends_p[None, :] <= block_start[:, None], axis=1, dtype=jnp.int32),
                            N_EXPERTS - 1)
    lane = jnp.arange(MOE_BLOCK, dtype=jnp.int32)[None, :]
    blk = block_exp[:, None] == experts[None, :]

    def per_block(table):
        return jnp.sum(jnp.where(blk, table[None, :], 0), axis=1, dtype=jnp.int32)

    valid = (block_start - per_block(pstart))[:, None] + lane < per_block(counts)[:, None]
    src = jnp.clip((block_start - per_block(delta))[:, None] + lane, 0, m - 1)
    run = order[src.reshape(-1)].reshape(n_blocks, MOE_BLOCK)
    row_tok = jnp.where(valid, run % n, (block_start[:, None] + lane) % n).reshape(-1)
    n_used = (ends_p[-1] // MOE_BLOCK).astype(jnp.int32).reshape(1)
    dest = pair + jnp.sum(jnp.where(e_sorted[None, :] == experts[:, None], delta[:, None], 0), axis=0,
                          dtype=jnp.int32)
    _, pos = lax.sort((order, dest), num_keys=1)
    return row_tok, pos, block_exp, n_used


def kernel(x, w_in, mu_prev, mu_next, pool_w, pool_scale, rw_w0, rw_w_up, rw_a0, rw_a_up, rw_g_up, rw_k_k, rw_k_a, rw_r_k, rw_gn_g, rw_gn_b, q_norm, k_norm, w_o, ln1_g, ln1_b, router_group, router_group_b, router_expert, router_expert_b, exp_gate, exp_up, exp_down, ln2_g, ln2_b):
    b, s, d = x.shape
    n = b * s
    w = RWKV_WIDTH
    tm = min(ROW_TILE, s)
    qperm = _q_perm()
    cos8, sin8 = _rope_tables(s)
    ones_q = _block_diag([jnp.ones((HEAD_DIM, HEAD_DIM), BF16)] * ATTN_HEADS)
    ones_r = ones_q[0:w, 0:w]
    a_end = POOL_WIDTH
    b_end = POOL_WIDTH + RWKV_IN

    xc = x.reshape(n, d)
    pending = None
    for l in range(DEPTH):
        wq = w_in[l][:, b_end:b_end + ATTN_WIDTH][:, qperm]
        w_proj = jnp.concatenate(
            [w_in[l][:, :a_end], w_in[l][:, a_end:b_end],
             jnp.zeros((d, RWKV_PAD - RWKV_IN), F32), wq, w_in[l][:, b_end + ATTN_WIDTH:]],
            axis=1).astype(BF16)
        pool_bd = _block_diag([pool_w[l, g] for g in range(len(POOL_WINDOWS))]).astype(BF16)
        pad_vec = jnp.zeros((RWKV_PAD - RWKV_IN,), F32)
        mup = jnp.concatenate([mu_prev[l], pad_vec]).reshape(1, RWKV_PAD)
        mun = jnp.concatenate([mu_next[l], pad_vec]).reshape(1, RWKV_PAD)
        w_lora = jnp.zeros((w, 5 * w), F32)
        for di in range(2):
            w_lora = w_lora.at[di * DECAY_LORA:(di + 1) * DECAY_LORA, di * w:(di + 1) * w].set(rw_w_up[l, di])
            o = 2 * DECAY_LORA
            w_lora = w_lora.at[o + di * AAA_LORA:o + (di + 1) * AAA_LORA, (2 + di) * w:(3 + di) * w].set(rw_a_up[l, di])
        o = 2 * DECAY_LORA + 2 * AAA_LORA
        w_lora = w_lora.at[o:o + GATE_LORA, 4 * w:5 * w].set(rw_g_up[l]).astype(BF16)
        lora_bias = jnp.concatenate([rw_w0[l, 0], rw_w0[l, 1], rw_a0[l, 0], rw_a0[l, 1],
                                     jnp.zeros((w,), F32)]).reshape(1, 5 * w)
        qg = jnp.tile(q_norm[l], ATTN_HEADS).reshape(1, ATTN_WIDTH)
        kg = jnp.tile(k_norm[l], 2).reshape(1, ATTN_KV_WIDTH)
        wo_attn = w_o[l][POOL_WIDTH + w:][qperm]
        wo = jnp.concatenate([w_o[l][:POOL_WIDTH + w], wo_attn], axis=0).astype(BF16)
        wr = jnp.concatenate([router_group[l], router_expert[l],
                              jnp.zeros((d, ROUTE_LANES - N_GROUPS - N_EXPERTS), F32)], axis=1)
        wrh = wr.astype(BF16)
        wrl = (wr - wrh.astype(F32)).astype(BF16)
        br = jnp.concatenate([router_group_b[l], router_expert_b[l],
                              jnp.zeros((ROUTE_LANES - N_GROUPS - N_EXPERTS,), F32)]).reshape(1, ROUTE_LANES)

        if pending is None:
            pool_in, rw_in, qh, kh, vh = _proj(xc, w_proj, cos8, sin8, qg, kg, ones_q, s, tm)
        else:
            xc, pool_in, rw_in, qh, kh, vh = _proj(None, w_proj, cos8, sin8, qg, kg, ones_q, s, tm, prev=pending)
        y_pool = _pool(pool_in.reshape(b, s, POOL_WIDTH), pool_bd, pool_scale[l].reshape(1, POOL_WIDTH))
        shared, logw, khnb, post = _rwkv_prep(rw_in.reshape(b, s, RWKV_PAD), mup, mun, w_lora, lora_bias,
                                        rw_k_k[l].reshape(1, w), rw_k_a[l].reshape(1, w),
                                        rw_r_k[l].reshape(1, w), ones_r, tm)
        wkv_f, wkv_r = _wkv(shared, logw, khnb, min(WKV_TILE, s))
        y_attn = _attn(qh, kh, vh, s, min(ATTN_Q_TILE, s))
        x1, x1p, route = _outproj(xc, y_pool.reshape(n, POOL_WIDTH), wkv_f.reshape(n, w), wkv_r.reshape(n, w),
                                  post.reshape(n, 2 * w), y_attn, wo,
                                  rw_gn_g[l].reshape(1, w), rw_gn_b[l].reshape(1, w), ones_r,
                                  ln1_g[l].reshape(1, d), ln1_b[l].reshape(1, d), wrh, wrl, br, tm)

        row_tok, pos, block_exp, n_used = _dispatch(route, n)
        split = block_exp.shape[0] // 2 * MOE_BLOCK
        xs_parts = [_gather_rows(x1p, row_tok[:split]), _gather_rows(x1p, row_tok[split:])]
        ys = _experts(block_exp, n_used, xs_parts, exp_gate, exp_up, exp_down, l)
        yab = _gather_rows(ys, pos)
        pending = (x1, yab, route, ln2_g[l].reshape(1, d), ln2_b[l].reshape(1, d))
    return _combine(*pending, tm).reshape(b, s, d)
```
